```python
import functools
import jax, jax.numpy as jnp
from jax import lax
import numpy as np

D_MODEL = 1024
BATCH = 16
SEQ = 4096
DEPTH = 1
DEC_BATCH = 32
DEC_SEQ = 64
PAST_LEN = 2048

CHUNK = 64
HEAD_DIM = 64
A_Q_HEADS = 8
A_KV_HEADS = 2
A_GROUP = A_Q_HEADS // A_KV_HEADS
WINDOW_A = 128
A_PREV_CHUNKS = WINDOW_A // CHUNK
ROT_DIM = HEAD_DIM // 4
ROPE_THETA = 500000.0
B_HEADS = 8
B_PREV_CHUNKS = 8
WINDOW_B = B_PREV_CHUNKS * CHUNK
REL_BACK = 4 * CHUNK
REL_FWD = CHUNK - 1
N_REL = REL_BACK + REL_FWD + 1
A_Q_W = A_Q_HEADS * HEAD_DIM
A_KV_W = A_KV_HEADS * HEAD_DIM
B_W = B_HEADS * HEAD_DIM
IN_SPLITS = (A_Q_W, A_Q_W + A_KV_W, A_Q_W + 2 * A_KV_W, A_Q_W + 2 * A_KV_W + B_W, A_Q_W + 2 * A_KV_W + 2 * B_W, A_Q_W + 2 * A_KV_W + 3 * B_W, A_Q_W + 2 * A_KV_W + 3 * B_W + D_MODEL)
IN_WIDTH = A_Q_W + 2 * A_KV_W + 3 * B_W + 2 * D_MODEL
N_EXPERTS = 32
TOP_K = 4
D_FF = D_MODEL
SWIGLU_LIMIT = 7.0
SWIGLU_ALPHA = 1.702
ROUTE_BLOCK = 256
RMS_EPS = 1e-6
NEG_INF = -1e30

kernel_name = 'hybrid_stream_swa_sink_chunkband_moe_step'


def rms_norm(x, g):
    xf = x.astype(jnp.float32)
    y = xf * lax.rsqrt(jnp.mean(xf * xf, axis=-1, keepdims=True) + RMS_EPS)
    return (y * g.astype(jnp.float32)).astype(x.dtype)


def rope_partial(x, pos):
    inv_freq = ROPE_THETA ** (-jnp.arange(0, ROT_DIM, 2, dtype=jnp.float32) / ROT_DIM)
    ang = pos.astype(jnp.float32)[:, None] * inv_freq[None, :]
    cos = jnp.cos(ang)[None, :, None, :]
    sin = jnp.sin(ang)[None, :, None, :]
    xr = x[..., :ROT_DIM].astype(jnp.float32)
    x1, x2 = xr[..., :ROT_DIM // 2], xr[..., ROT_DIM // 2:]
    rot = jnp.concatenate([x1 * cos - x2 * sin, x2 * cos + x1 * sin], axis=-1)
    return jnp.concatenate([rot.astype(x.dtype), x[..., ROT_DIM:]], axis=-1)


def adaln_mods(c, w_ada, b_ada):
    mod = jax.nn.silu(c) @ w_ada + b_ada
    return jnp.split(mod[:, None, :], 6, axis=-1)


def mixer_inputs(h, w_in, q_norm_a, k_norm_a, q_norm_b, k_norm_b, pos):
    bsz, length, _ = h.shape
    qa, ka, va, qb, kb, vb, ga, gb = jnp.split(h @ w_in, IN_SPLITS, axis=-1)
    qa = rope_partial(rms_norm(qa.reshape(bsz, length, A_Q_HEADS, HEAD_DIM), q_norm_a), pos)
    ka = rope_partial(rms_norm(ka.reshape(bsz, length, A_KV_HEADS, HEAD_DIM), k_norm_a), pos)
    va = va.reshape(bsz, length, A_KV_HEADS, HEAD_DIM)
    qb = rms_norm(qb.reshape(bsz, length, B_HEADS, HEAD_DIM), q_norm_b)
    kb = rms_norm(kb.reshape(bsz, length, B_HEADS, HEAD_DIM), k_norm_b)
    vb = vb.reshape(bsz, length, B_HEADS, HEAD_DIM)
    qa = qa.reshape(bsz, length, A_KV_HEADS, A_GROUP, HEAD_DIM)
    qb = qb[:, :, :, None, :]
    return qa, ka, va, qb, kb, vb, ga, gb


def attend(q, k, v, valid, bias, sink):
    s = jnp.einsum('bqkgd,bjkd->bkgqj', q, k).astype(jnp.float32) * (HEAD_DIM ** -0.5)
    if bias is not None:
        s = s + bias.astype(jnp.float32)
    if valid is not None:
        s = jnp.where(valid, s, NEG_INF)
    if sink is None:
        p = jax.nn.softmax(s, axis=-1)
    else:
        sk = sink.astype(jnp.float32)[None, :, :, None, None]
        m = jnp.maximum(jnp.max(s, axis=-1, keepdims=True), sk)
        e = jnp.exp(s - m)
        p = e / (jnp.sum(e, axis=-1, keepdims=True) + jnp.exp(sk - m))
    return jnp.einsum('bkgqj,bjkd->bqkgd', p.astype(v.dtype), v)


def rel_bias(table, q_pos, k_pos):
    idx = jnp.clip(q_pos[:, None] - k_pos[None, :], -REL_FWD, REL_BACK) + REL_FWD
    return table[:, idx][:, None]


def chunk_band_attention(q, k, v, n_prev, bias, sink):
    bsz, length, n_kv, grp, dh = q.shape
    n_chunks = length // CHUNK
    pad = n_prev * CHUNK
    band = pad + CHUNK
    kp = jnp.pad(k, ((0, 0), (pad, 0), (0, 0), (0, 0)))
    vp = jnp.pad(v, ((0, 0), (pad, 0), (0, 0), (0, 0)))
    k_rel = jnp.arange(band) - pad

    def one_chunk(ci):
        start = ci * CHUNK
        qc = lax.dynamic_slice_in_dim(q, start, CHUNK, axis=1)
        kc = lax.dynamic_slice_in_dim(kp, start, band, axis=1)
        vc = lax.dynamic_slice_in_dim(vp, start, band, axis=1)
        return attend(qc, kc, vc, (start + k_rel) >= 0, bias, sink)

    out = lax.map(one_chunk, jnp.arange(n_chunks))
    return jnp.transpose(out, (1, 0, 2, 3, 4, 5)).reshape(bsz, length, n_kv, grp, dh)


def attn_prompt(qa, ka, va, qb, kb, vb, sinks, rel_table):
    length = qa.shape[1]
    bias = rel_bias(rel_table, jnp.arange(CHUNK), jnp.arange(WINDOW_B + CHUNK) - WINDOW_B)
    oa = chunk_band_attention(qa, ka, va, A_PREV_CHUNKS, None, sinks)
    ob = chunk_band_attention(qb, kb, vb, B_PREV_CHUNKS, bias, None)
    wa, wb = min(WINDOW_A, length), min(WINDOW_B, length)
    state = (ka[:, length - wa:], va[:, length - wa:], kb[:, length - wb:], vb[:, length - wb:])
    return oa, ob, state


def attn_sample(cache_ak, cache_av, cache_bk, cache_bv, qa, ka, va, qb, kb, vb, sinks, rel_table):
    length = qa.shape[1]
    wb = cache_bk.shape[1]
    oa = attend(qa, jnp.concatenate([cache_ak, ka], axis=1), jnp.concatenate([cache_av, va], axis=1), None, None, sinks)
    bias = rel_bias(rel_table, jnp.arange(length), jnp.concatenate([jnp.arange(wb) - wb, jnp.arange(length)]))
    ob = attend(qb, jnp.concatenate([cache_bk, kb], axis=1), jnp.concatenate([cache_bv, vb], axis=1), None, bias, None)
    return oa, ob, (ka, va, kb, vb)


def branch_merge(oa, ob, ga, gb, w_branch_a, w_branch_b, w_out):
    bsz, length = oa.shape[:2]
    ya = oa.reshape(bsz, length, A_Q_W) @ w_branch_a
    yb = ob.reshape(bsz, length, B_W) @ w_branch_b
    return (jax.nn.sigmoid(ga) * ya + jax.nn.sigmoid(gb) * yb) @ w_out


def moe(h, w_router, b_router, w_gate_up, b_gate_up, w_down, b_down):
    bsz, length, d = h.shape
    t = h.reshape(-1, d)
    n_tok = t.shape[0]
    logits = (t @ w_router + b_router).astype(jnp.float32)
    top_v, top_i = lax.top_k(logits, TOP_K)
    gates = jax.nn.softmax(top_v, axis=-1)
    n_assign = n_tok * TOP_K
    e_flat = top_i.reshape(-1).astype(jnp.int32)
    tok_flat = jnp.arange(n_assign, dtype=jnp.int32) // TOP_K
    order = jnp.argsort(e_flat)
    e_sorted = e_flat[order]
    counts = jnp.bincount(e_flat, length=N_EXPERTS).astype(jnp.int32)
    starts = jnp.cumsum(counts) - counts
    padded = (counts + ROUTE_BLOCK - 1) // ROUTE_BLOCK * ROUTE_BLOCK
    pends = jnp.cumsum(padded)
    pstarts = pends - padded
    dest = pstarts[e_sorted] + (jnp.arange(n_assign, dtype=jnp.int32) - starts[e_sorted])
    n_blocks = -(-n_assign // ROUTE_BLOCK) + N_EXPERTS
    n_rows = n_blocks * ROUTE_BLOCK
    tok_buf = jnp.full((n_rows,), n_tok, jnp.int32).at[dest].set(tok_flat[order])
    w_buf = jnp.zeros((n_rows,), jnp.float32).at[dest].set(gates.reshape(-1)[order])
    blk_expert = jnp.minimum(jnp.searchsorted(pends, jnp.arange(n_blocks, dtype=jnp.int32) * ROUTE_BLOCK, side='right'), N_EXPERTS - 1)
    t_pad = jnp.concatenate([t, jnp.zeros((1, d), t.dtype)], axis=0)

    def expert_block(args):
        idx, e = args
        gu = t_pad[idx] @ w_gate_up[e] + b_gate_up[e]
        gate = jnp.minimum(gu[:, :D_FF], SWIGLU_LIMIT)
        up = jnp.clip(gu[:, D_FF:], -SWIGLU_LIMIT, SWIGLU_LIMIT)
        glu = gate * jax.nn.sigmoid(gate * SWIGLU_ALPHA)
        return ((up + 1.0) * glu) @ w_down[e] + b_down[e]

    yb = lax.map(expert_block, (tok_buf.reshape(n_blocks, ROUTE_BLOCK), blk_expert))
    contrib = (yb.reshape(n_rows, d) * w_buf[:, None]).astype(h.dtype)
    out = jnp.zeros((n_tok + 1, d), h.dtype).at[tok_buf].add(contrib)
    return out[:n_tok].reshape(bsz, length, d)


def trunk_layer(x, c, pos, attn_fn, params):
    (g_attn, g_mlp, w_ada, b_ada, w_in, q_norm_a, k_norm_a, q_norm_b, k_norm_b, sinks_a, rel_table_b,
     w_branch_a, w_branch_b, w_out, w_router, b_router, w_gate_up, b_gate_up, w_down, b_down) = params
    sh_a, sc_a, gt_a, sh_m, sc_m, gt_m = adaln_mods(c, w_ada, b_ada)
    h = rms_norm(x, g_attn) * (1.0 + sc_a) + sh_a
    qa, ka, va, qb, kb, vb, ga, gb = mixer_inputs(h, w_in, q_norm_a, k_norm_a, q_norm_b, k_norm_b, pos)
    oa, ob, state = attn_fn(qa, ka, va, qb, kb, vb, sinks_a.reshape(A_KV_HEADS, A_GROUP), rel_table_b)
    x = x + gt_a * branch_merge(oa, ob, ga, gb, w_branch_a, w_branch_b, w_out)
    h = rms_norm(x, g_mlp) * (1.0 + sc_m) + sh_m
    x = x + gt_m * moe(h, w_router, b_router, w_gate_up, b_gate_up, w_down, b_down)
    return x, state


def setup_inputs(seed: int = 0) -> dict:
    key = jax.random.key(seed)
    ks = jax.random.split(key, 28)

    def nrm(k, shape, scale):
        return jax.random.normal(k, shape, jnp.float32) * scale

    wa_c = min(WINDOW_A, PAST_LEN)
    wb_c = min(WINDOW_B, PAST_LEN)
    L = DEPTH
    return {
        'x_prompt': nrm(ks[0], (BATCH, SEQ, D_MODEL), 1.0),
        'x_sample': nrm(ks[1], (DEC_BATCH, DEC_SEQ, D_MODEL), 1.0),
        'c_prompt': nrm(ks[2], (BATCH, D_MODEL), 1.0),
        'c_sample': nrm(ks[3], (DEC_BATCH, D_MODEL), 1.0),
        'cache_a_k': nrm(ks[4], (L, DEC_BATCH, wa_c, A_KV_HEADS, HEAD_DIM), 1.0),
        'cache_a_v': nrm(ks[5], (L, DEC_BATCH, wa_c, A_KV_HEADS, HEAD_DIM), 1.0),
        'cache_b_k': nrm(ks[6], (L, DEC_BATCH, wb_c, B_HEADS, HEAD_DIM), 1.0),
        'cache_b_v': nrm(ks[7], (L, DEC_BATCH, wb_c, B_HEADS, HEAD_DIM), 1.0),
        'g_attn': 1.0 + nrm(ks[8], (L, D_MODEL), 0.01),
        'g_mlp': 1.0 + nrm(ks[9], (L, D_MODEL), 0.01),
        'w_ada': nrm(ks[10], (L, D_MODEL, 6 * D_MODEL), 0.5 * D_MODEL ** -0.5),
        'b_ada': nrm(ks[11], (L, 6 * D_MODEL), 0.01),
        'w_in': nrm(ks[12], (L, D_MODEL, IN_WIDTH), D_MODEL ** -0.5),
        'q_norm_a': 1.0 + nrm(ks[13], (L, HEAD_DIM), 0.01),
        'k_norm_a': 1.0 + nrm(ks[14], (L, HEAD_DIM), 0.01),
        'q_norm_b': 1.0 + nrm(ks[15], (L, HEAD_DIM), 0.01),
        'k_norm_b': 1.0 + nrm(ks[16], (L, HEAD_DIM), 0.01),
        'sinks_a': nrm(ks[17], (L, A_Q_HEADS), 1.0),
        'rel_table_b': nrm(ks[18], (L, B_HEADS, N_REL), 0.5),
        'w_branch_a': nrm(ks[19], (L, A_Q_W, D_MODEL), A_Q_W ** -0.5),
        'w_branch_b': nrm(ks[20], (L, B_W, D_MODEL), B_W ** -0.5),
        'w_out': nrm(ks[21], (L, D_MODEL, D_MODEL), D_MODEL ** -0.5),
        'w_router': nrm(ks[22], (L, D_MODEL, N_EXPERTS), D_MODEL ** -0.5),
        'b_router': nrm(ks[23], (L, N_EXPERTS), 0.01),
        'w_gate_up': nrm(ks[24], (L, N_EXPERTS, D_MODEL, 2 * D_FF), D_MODEL ** -0.5),
        'b_gate_up': nrm(ks[25], (L, N_EXPERTS, 2 * D_FF), 0.01),
        'w_down': nrm(ks[26], (L, N_EXPERTS, D_FF, D_MODEL), D_FF ** -0.5),
        'b_down': nrm(ks[27], (L, N_EXPERTS, D_MODEL), 0.01),
    }


def reference(x_prompt, x_sample, c_prompt, c_sample, cache_a_k, cache_a_v, cache_b_k, cache_b_v,
              g_attn, g_mlp, w_ada, b_ada, w_in, q_norm_a, k_norm_a, q_norm_b, k_norm_b, sinks_a,
              rel_table_b, w_branch_a, w_branch_b, w_out, w_router, b_router, w_gate_up, b_gate_up,
              w_down, b_down):
    pos_prompt = jnp.arange(x_prompt.shape[1], dtype=jnp.int32)
    pos_sample = PAST_LEN + jnp.arange(x_sample.shape[1], dtype=jnp.int32)
    y_prompt, y_sample = x_prompt, x_sample
    states_p, states_s = [], []
    for l in range(DEPTH):
        params = (g_attn[l], g_mlp[l], w_ada[l], b_ada[l], w_in[l], q_norm_a[l], k_norm_a[l],
                  q_norm_b[l], k_norm_b[l], sinks_a[l], rel_table_b[l], w_branch_a[l], w_branch_b[l],
                  w_out[l], w_router[l], b_router[l], w_gate_up[l], b_gate_up[l], w_down[l], b_down[l])
        y_prompt, st_p = trunk_layer(y_prompt, c_prompt, pos_prompt, attn_prompt, params)
        sample_attn = functools.partial(attn_sample, cache_a_k[l], cache_a_v[l], cache_b_k[l], cache_b_v[l])
        y_sample, st_s = trunk_layer(y_sample, c_sample, pos_sample, sample_attn, params)
        states_p.append(st_p)
        states_s.append(st_s)
    new_a_k_prompt = jnp.stack([s[0] for s in states_p])
    new_a_v_prompt = jnp.stack([s[1] for s in states_p])
    new_b_k_prompt = jnp.stack([s[2] for s in states_p])
    new_b_v_prompt = jnp.stack([s[3] for s in states_p])
    new_a_k_sample = jnp.stack([s[0] for s in states_s])
    new_a_v_sample = jnp.stack([s[1] for s in states_s])
    new_b_k_sample = jnp.stack([s[2] for s in states_s])
    new_b_v_sample = jnp.stack([s[3] for s in states_s])
    return (y_prompt, y_sample, new_a_k_prompt, new_a_v_prompt, new_b_k_prompt, new_b_v_prompt, new_a_k_sample, new_a_v_sample, new_b_k_sample, new_b_v_sample)
```

```python
import functools

import jax
import jax.numpy as jnp
from jax import lax
from jax.experimental import pallas as pl
from jax.experimental.pallas import tpu as pltpu

F32 = jnp.float32
BF16 = jnp.bfloat16
I32 = jnp.int32

D_MODEL = 1024
CHUNK = 64
HEAD_DIM = 64
LANES = 128
SUBLANES = 8
A_Q_HEADS = 8
A_KV_HEADS = 2
A_GROUP = A_Q_HEADS // A_KV_HEADS
B_HEADS = 8
A_Q_W = A_Q_HEADS * HEAD_DIM
A_KV_W = A_KV_HEADS * HEAD_DIM
B_W = B_HEADS * HEAD_DIM
QKV_W = A_Q_W + 2 * A_KV_W + 3 * B_W
WINDOW_A = 128
WINDOW_B = 512
BAND_A = WINDOW_A + CHUNK
BAND_B = WINDOW_B + CHUNK
PAD = WINDOW_B
ROT_DIM = HEAD_DIM // 4
ROPE_THETA = 500000.0
REL_BACK = 4 * CHUNK
REL_FWD = CHUNK - 1
N_EXPERTS = 32
TOP_K = 4
D_FF = D_MODEL
SWIGLU_LIMIT = 7.0
SWIGLU_ALPHA = 1.702
ROUTE_BLOCK = 256
RMS_EPS = 1e-6
NEG_INF = -1e30
ROW_TILES = D_MODEL // LANES
VMEM_LIMIT = 56 * 1024 * 1024

A_HEAD_ORDER = (0, 4, 1, 5, 2, 6, 3, 7)


def _cparams(*sem):
    return pltpu.CompilerParams(dimension_semantics=sem, vmem_limit_bytes=VMEM_LIMIT)


def _adaln_kernel(c_ref, w_ref, b_ref, o_ref):
    c = c_ref[...]
    a = c * jax.nn.sigmoid(c)
    o_ref[...] = jnp.dot(a, w_ref[...], preferred_element_type=F32,
                         precision=lax.Precision.HIGHEST) + b_ref[...]


def _adaln(c, w_ada, b_ada):
    n_rows = c.shape[0]
    n_out = w_ada.shape[1]
    tn = 1024
    return pl.pallas_call(
        _adaln_kernel,
        grid=(n_out // tn,),
        in_specs=[pl.BlockSpec((n_rows, D_MODEL), lambda j: (0, 0)),
                  pl.BlockSpec((D_MODEL, tn), lambda j: (0, j)),
                  pl.BlockSpec((1, tn), lambda j: (0, j))],
        out_specs=pl.BlockSpec((n_rows, tn), lambda j: (0, j)),
        out_shape=jax.ShapeDtypeStruct((n_rows, n_out), F32),
        compiler_params=_cparams("arbitrary"),
        name="adaln",
    )(c, w_ada, b_ada.reshape(1, n_out))


def _modulated_norm(x, g, scale, shift):
    ms = jnp.mean(x * x, axis=-1, keepdims=True)
    return (x * lax.rsqrt(ms + RMS_EPS) * g) * (1.0 + scale) + shift


def _head_norm(parts, gain, seg):
    m = parts[0].shape[0]
    sq = jnp.concatenate([p * p for p in parts], axis=0).astype(BF16)
    ms = jnp.dot(sq, seg, preferred_element_type=F32)
    r = lax.rsqrt(ms + RMS_EPS)
    return [parts[j] * r[j * m:(j + 1) * m] * gain for j in range(len(parts))]


def _rope(p, cos, sin_lo, sin_hi):
    return p * cos + pltpu.roll(p, LANES - ROT_DIM // 2, 1) * sin_lo + pltpu.roll(p, ROT_DIM // 2, 1) * sin_hi


def _stage1_kernel(x_ref, sh_ref, sc_ref, g_ref, w_ref, gqa_ref, gka_ref, gqb_ref, gkb_ref, seg_ref,
                   cos_ref, sinlo_ref, sinhi_ref,
                   qa_ref, ka_ref, va_ref, qb_ref, kb_ref, vb_ref, ska_ref, sva_ref, skb_ref, svb_ref,
                   *, bb, tl, pad_tiles, n_tiles, sa_rows, sb_rows):
    i = pl.program_id(1)
    m = bb * tl

    if pad_tiles:
        @pl.when(i < pad_tiles)
        def _():
            ka_ref[...] = jnp.zeros_like(ka_ref)
            va_ref[...] = jnp.zeros_like(va_ref)
            kb_ref[...] = jnp.zeros_like(kb_ref)
            vb_ref[...] = jnp.zeros_like(vb_ref)

    @pl.when(i >= pad_tiles)
    def _():
        h = _modulated_norm(x_ref[...], g_ref[...], sc_ref[...], sh_ref[...])
        hb = h.reshape(m, D_MODEL).astype(BF16)
        qkv = jnp.dot(hb, w_ref[...], preferred_element_type=F32)
        seg = seg_ref[...]

        def cols(lo, width):
            return [qkv[:, lo + LANES * j: lo + LANES * (j + 1)] for j in range(width // LANES)]

        cos = jnp.concatenate([cos_ref[...]] * bb, axis=0)
        sin_lo = jnp.concatenate([sinlo_ref[...]] * bb, axis=0)
        sin_hi = jnp.concatenate([sinhi_ref[...]] * bb, axis=0)
        scale = HEAD_DIM ** -0.5

        qa = [_rope(p, cos, sin_lo, sin_hi) * scale for p in _head_norm(cols(0, A_Q_W), gqa_ref[...], seg)]
        ka = [_rope(p, cos, sin_lo, sin_hi) for p in _head_norm(cols(A_Q_W, A_KV_W), gka_ref[...], seg)]
        va = cols(A_Q_W + A_KV_W, A_KV_W)
        off_b = A_Q_W + 2 * A_KV_W
        qb = [p * scale for p in _head_norm(cols(off_b, B_W), gqb_ref[...], seg)]
        kb = _head_norm(cols(off_b + B_W, B_W), gkb_ref[...], seg)
        vb = cols(off_b + 2 * B_W, B_W)

        def put(ref, parts):
            for j, p in enumerate(parts):
                ref[:, :, LANES * j:LANES * (j + 1)] = p.reshape(bb, tl, LANES).astype(ref.dtype)

        put(qa_ref, qa)
        put(ka_ref, ka)
        put(va_ref, va)
        put(qb_ref, qb)
        put(kb_ref, kb)
        put(vb_ref, vb)

        @pl.when(i == n_tiles - 1)
        def _():
            def put_state(ref, parts, rows):
                for j, p in enumerate(parts):
                    ref[:, :, LANES * j:LANES * (j + 1)] = p.reshape(bb, tl, LANES)[:, tl - rows:, :]

            put_state(ska_ref, ka, sa_rows)
            put_state(sva_ref, va, sa_rows)
            put_state(skb_ref, kb, sb_rows)
            put_state(svb_ref, vb, sb_rows)


def _rope_tables(pos):
    inv_freq = ROPE_THETA ** (-jnp.arange(0, ROT_DIM, 2, dtype=F32) / ROT_DIM)
    ang = pos.astype(F32)[:, None] * inv_freq[None, :]
    half = ROT_DIM // 2
    lane = jnp.arange(LANES) % HEAD_DIM
    ang_l = ang[:, lane % half]
    rot = lane < ROT_DIM
    cos = jnp.where(rot[None, :], jnp.cos(ang_l), 1.0)
    sin_lo = jnp.where((lane < half)[None, :], -jnp.sin(ang_l), 0.0)
    sin_hi = jnp.where(((lane >= half) & rot)[None, :], jnp.sin(ang_l), 0.0)
    return cos, sin_lo, sin_hi


def _stage1(x, shift, scale, g_attn, w_qkv, gains, pos, *, bb, tl, pad):
    bsz, length, _ = x.shape
    n_tiles = length // tl
    pad_tiles = pad // tl
    sa_rows, sb_rows = min(WINDOW_A, length), min(WINDOW_B, length)
    assert bsz % bb == 0 and length % tl == 0 and pad % tl == 0 and sb_rows <= tl
    cos, sin_lo, sin_hi = _rope_tables(pos)
    seg = (jnp.arange(LANES)[:, None] // HEAD_DIM == jnp.arange(LANES)[None, :] // HEAD_DIM)
    seg = (seg.astype(F32) / HEAD_DIM).astype(BF16)
    gq_a, gk_a, gq_b, gk_b = [jnp.tile(g, LANES // HEAD_DIM).reshape(1, LANES) for g in gains]

    def xmap(b, i):
        return (b, jnp.maximum(i - pad_tiles, 0), 0)

    def const2(b, i):
        return (0, 0)

    def tabmap(b, i):
        return (jnp.maximum(i - pad_tiles, 0), 0)

    def modmap(b, i):
        return (b, 0, 0)

    def kvmap(b, i):
        return (b, i, 0)

    lp = length + pad
    kern = functools.partial(_stage1_kernel, bb=bb, tl=tl, pad_tiles=pad_tiles,
                             n_tiles=n_tiles + pad_tiles, sa_rows=sa_rows, sb_rows=sb_rows)
    out_shapes = (
        jax.ShapeDtypeStruct((bsz, length, A_Q_W), BF16),
        jax.ShapeDtypeStruct((bsz, lp, A_KV_W), BF16),
        jax.ShapeDtypeStruct((bsz, lp, A_KV_W), BF16),
        jax.ShapeDtypeStruct((bsz, length, B_W), BF16),
        jax.ShapeDtypeStruct((bsz, lp, B_W), BF16),
        jax.ShapeDtypeStruct((bsz, lp, B_W), BF16),
        jax.ShapeDtypeStruct((bsz, sa_rows, A_KV_W), F32),
        jax.ShapeDtypeStruct((bsz, sa_rows, A_KV_W), F32),
        jax.ShapeDtypeStruct((bsz, sb_rows, B_W), F32),
        jax.ShapeDtypeStruct((bsz, sb_rows, B_W), F32),
    )
    out_specs = (
        pl.BlockSpec((bb, tl, A_Q_W), xmap),
        pl.BlockSpec((bb, tl, A_KV_W), kvmap),
        pl.BlockSpec((bb, tl, A_KV_W), kvmap),
        pl.BlockSpec((bb, tl, B_W), xmap),
        pl.BlockSpec((bb, tl, B_W), kvmap),
        pl.BlockSpec((bb, tl, B_W), kvmap),
        pl.BlockSpec((bb, sa_rows, A_KV_W), modmap),
        pl.BlockSpec((bb, sa_rows, A_KV_W), modmap),
        pl.BlockSpec((bb, sb_rows, B_W), modmap),
        pl.BlockSpec((bb, sb_rows, B_W), modmap),
    )
    in_specs = [
        pl.BlockSpec((bb, tl, D_MODEL), xmap),
        pl.BlockSpec((bb, 1, D_MODEL), modmap),
        pl.BlockSpec((bb, 1, D_MODEL), modmap),
        pl.BlockSpec((1, D_MODEL), const2),
        pl.BlockSpec((D_MODEL, QKV_W), const2),
        pl.BlockSpec((1, LANES), const2),
        pl.BlockSpec((1, LANES), const2),
        pl.BlockSpec((1, LANES), const2),
        pl.BlockSpec((1, LANES), const2),
        pl.BlockSpec((LANES, LANES), const2),
        pl.BlockSpec((tl, LANES), tabmap),
        pl.BlockSpec((tl, LANES), tabmap),
        pl.BlockSpec((tl, LANES), tabmap),
    ]
    return pl.pallas_call(
        kern,
        grid=(bsz // bb, n_tiles + pad_tiles),
        in_specs=in_specs,
        out_specs=out_specs,
        out_shape=out_shapes,
        compiler_params=_cparams("arbitrary", "arbitrary"),
        name="stage1",
    )(x, shift, scale, g_attn.reshape(1, D_MODEL), w_qkv, gq_a, gk_a, gq_b, gk_b, seg, cos, sin_lo, sin_hi)


def _nt_dot(a, b):
    return lax.dot_general(a, b, (((1,), (1,)), ((), ())), preferred_element_type=F32)


def _attn_kernel(sink_ref, qa_ref, ka_ref, va_ref, qb_ref, kb_ref, vb_ref, bias_ref, oa_ref, ob_ref,
                 *, n_c, mask_prefix):
    qi = pl.program_id(1)
    lane = lax.broadcasted_iota(I32, (CHUNK, LANES), 1)
    low = lane < HEAD_DIM
    half_mask = (jnp.where(low, 1.0, 0.0).astype(BF16), jnp.where(low, 0.0, 1.0).astype(BF16))
    key_b = lax.broadcasted_iota(I32, (CHUNK, BAND_B), 1)
    key_a = lax.broadcasted_iota(I32, (A_GROUP * CHUNK, BAND_A), 1)
    row_a = lax.broadcasted_iota(I32, (A_GROUP * CHUNK, 1), 0) // CHUNK

    sink_cols = []
    for p in range(A_KV_HEADS):
        col = jnp.zeros((A_GROUP * CHUNK, 1), F32)
        for j in range(A_GROUP):
            col = jnp.where(row_a == j, sink_ref[p * A_GROUP + j], col)
        sink_cols.append(col)

    for c in range(n_c):
        ci = qi * n_c + c
        row0 = pl.multiple_of(ci * CHUNK, CHUNK)
        rows = slice(c * CHUNK, (c + 1) * CHUNK)
        first_valid = PAD - ci * CHUNK

        for j in range(B_W // LANES):
            cg = slice(LANES * j, LANES * (j + 1))
            q = qb_ref[0, rows, cg]
            k = kb_ref[0, pl.ds(row0, BAND_B), cg]
            v = vb_ref[0, pl.ds(row0, BAND_B), cg]
            outs = []
            for p in range(2):
                s = _nt_dot(q * half_mask[p], k) + bias_ref[2 * j + p]
                if mask_prefix:
                    s = jnp.where(key_b >= first_valid, s, NEG_INF)
                mx = jnp.max(s, axis=-1, keepdims=True)
                e = jnp.exp(s - mx)
                den = jnp.sum(e, axis=-1, keepdims=True)
                o = jnp.dot(e.astype(BF16), v, preferred_element_type=F32)
                outs.append(o * (1.0 / den))
            ob_ref[0, rows, cg] = jnp.where(low, outs[0], outs[1]).astype(ob_ref.dtype)

        row_a0 = pl.multiple_of(row0 + (WINDOW_B - WINDOW_A), CHUNK)
        k = ka_ref[0, pl.ds(row_a0, BAND_A), :]
        v = va_ref[0, pl.ds(row_a0, BAND_A), :]
        outs = []
        for p in range(A_KV_HEADS):
            q = jnp.concatenate(
                [qa_ref[0, rows, LANES * j:LANES * (j + 1)] * half_mask[p] for j in range(A_GROUP)],
                axis=0)
            s = _nt_dot(q, k)
            if mask_prefix:
                s = jnp.where(key_a >= first_valid - (WINDOW_B - WINDOW_A), s, NEG_INF)
            sk = sink_cols[p]
            mx = jnp.maximum(jnp.max(s, axis=-1, keepdims=True), sk)
            e = jnp.exp(s - mx)
            den = jnp.sum(e, axis=-1, keepdims=True) + jnp.exp(sk - mx)
            o = jnp.dot(e.astype(BF16), v, preferred_element_type=F32)
            outs.append(o * (1.0 / den))
        for j in range(A_GROUP):
            rj = slice(j * CHUNK, (j + 1) * CHUNK)
            oa_ref[0, rows, LANES * j:LANES * (j + 1)] = jnp.where(low, outs[0][rj], outs[1][rj]).astype(oa_ref.dtype)


def _attention(qa, ka, va, qb, kb, vb, sinks, bias, *, n_c, mask_prefix):
    bsz, length, _ = qa.shape
    lp = ka.shape[1]
    qt = n_c * CHUNK
    assert length % qt == 0 and lp == length + PAD

    def qmap(b, i):
        return (b, i, 0)

    def kmap(b, i):
        return (b, 0, 0)

    kern = functools.partial(_attn_kernel, n_c=n_c, mask_prefix=mask_prefix)
    return pl.pallas_call(
        kern,
        grid=(bsz, length // qt),
        in_specs=[
            pl.BlockSpec(memory_space=pltpu.SMEM),
            pl.BlockSpec((1, qt, A_Q_W), qmap),
            pl.BlockSpec((1, lp, A_KV_W), kmap),
            pl.BlockSpec((1, lp, A_KV_W), kmap),
            pl.BlockSpec((1, qt, B_W), qmap),
            pl.BlockSpec((1, lp, B_W), kmap),
            pl.BlockSpec((1, lp, B_W), kmap),
            pl.BlockSpec((B_HEADS, CHUNK, BAND_B), lambda b, i: (0, 0, 0)),
        ],
        out_specs=(pl.BlockSpec((1, qt, A_Q_W), qmap), pl.BlockSpec((1, qt, B_W), qmap)),
        out_shape=(jax.ShapeDtypeStruct((bsz, length, A_Q_W), BF16),
                   jax.ShapeDtypeStruct((bsz, length, B_W), BF16)),
        compiler_params=_cparams("arbitrary", "arbitrary"),
        name="attention",
    )(sinks, qa, ka, va, qb, kb, vb, bias)


def _merge_kernel(x_ref, oa_ref, ob_ref, sha_ref, sca_ref, gta_ref, shm_ref, scm_ref, ga_ref, gm_ref,
                  wg_ref, wa_ref, wb_ref, wo_ref, wr_ref, br_ref,
                  x1_ref, h2_ref, ri_ref, rw_ref, cnt_ref, carry_ref, *, bb, tl):
    m = bb * tl
    first = jnp.logical_and(pl.program_id(0) == 0, pl.program_id(1) == 0)

    @pl.when(first)
    def _():
        carry_ref[...] = jnp.zeros_like(carry_ref)

    x = x_ref[...]
    h = _modulated_norm(x, ga_ref[...], sca_ref[...], sha_ref[...])
    hb = h.reshape(m, D_MODEL).astype(BF16)
    gates = jnp.dot(hb, wg_ref[...], preferred_element_type=F32)
    ya = jnp.dot(oa_ref[...].reshape(m, A_Q_W), wa_ref[...], preferred_element_type=F32)
    yb = jnp.dot(ob_ref[...].reshape(m, B_W), wb_ref[...], preferred_element_type=F32)
    mix = jax.nn.sigmoid(gates[:, :D_MODEL]) * ya + jax.nn.sigmoid(gates[:, D_MODEL:]) * yb
    z = jnp.dot(mix.astype(BF16), wo_ref[...], preferred_element_type=F32)
    x1 = x + gta_ref[...] * z.reshape(bb, tl, D_MODEL)
    x1_ref[...] = x1

    h2 = _modulated_norm(x1, gm_ref[...], scm_ref[...], shm_ref[...]).reshape(m, D_MODEL)
    for j in range(ROW_TILES):
        h2_ref[pl.ds(j, m, stride=ROW_TILES), :] = h2[:, LANES * j:LANES * (j + 1)]

    logits = jnp.dot(h2, wr_ref[...], preferred_element_type=F32,
                     precision=lax.Precision.HIGHEST) + br_ref[...]
    lane = lax.broadcasted_iota(I32, (m, LANES), 1)
    lane_f = lane.astype(F32)
    work = logits
    top_v, top_i = [], []
    sel_f = jnp.zeros((m, LANES), F32)
    for _ in range(TOP_K):
        mx = jnp.max(work, axis=-1, keepdims=True)
        idx = jnp.min(jnp.where(work == mx, lane_f, float(LANES)), axis=-1, keepdims=True)
        hit = lane_f == idx
        top_v.append(mx)
        top_i.append(idx)
        sel_f = jnp.where(hit, 1.0, sel_f)
        work = jnp.where(hit, -jnp.inf, work)
    ex = [jnp.exp(v - top_v[0]) for v in top_v]
    inv = 1.0 / (ex[0] + ex[1] + ex[2] + ex[3])

    r_i = lax.broadcasted_iota(I32, (m, m), 0)
    c_i = lax.broadcasted_iota(I32, (m, m), 1)
    before = jnp.where(r_i > c_i, 1.0, 0.0).astype(BF16)
    pos = jnp.dot(before, sel_f.astype(BF16), preferred_element_type=F32) + carry_ref[...]
    carry = carry_ref[...] + jnp.sum(sel_f, axis=0, keepdims=True)
    carry_ref[...] = carry
    cnt_ref[...] = carry

    ri_f = jnp.zeros((m, LANES), F32)
    rw = jnp.zeros((m, LANES), F32)
    for k in range(TOP_K):
        rank = jnp.sum(jnp.where(lane_f == top_i[k], pos, 0.0), axis=-1, keepdims=True)
        ri_f = jnp.where(lane == k, top_i[k], ri_f)
        ri_f = jnp.where(lane == TOP_K + k, rank, ri_f)
        rw = jnp.where(lane == k, ex[k] * inv, rw)
    ri = ri_f.astype(I32)
    ri_ref[...] = ri.reshape(bb, tl, LANES)
    rw_ref[...] = rw.reshape(bb, tl, LANES)


def _merge(x, oa, ob, mods, g_attn, g_mlp, wg, wa, wb, wo, wr, br, *, bb, tl):
    bsz, length, _ = x.shape
    n_tok = bsz * length
    m = bb * tl
    assert bb == 1 or tl == length
    sh_a, sc_a, gt_a, sh_m, sc_m = mods

    def xmap(b, i):
        return (b, i, 0)

    def modmap(b, i):
        return (b, 0, 0)

    def const2(b, i):
        return (0, 0)

    n_l = length // tl

    def flatmap(b, i):
        return (b * n_l + i, 0)

    modspec = pl.BlockSpec((bb, 1, D_MODEL), modmap)
    kern = functools.partial(_merge_kernel, bb=bb, tl=tl)
    return pl.pallas_call(
        kern,
        grid=(bsz // bb, n_l),
        in_specs=[
            pl.BlockSpec((bb, tl, D_MODEL), xmap),
            pl.BlockSpec((bb, tl, A_Q_W), xmap),
            pl.BlockSpec((bb, tl, B_W), xmap),
            modspec, modspec, modspec, modspec, modspec,
            pl.BlockSpec((1, D_MODEL), const2),
            pl.BlockSpec((1, D_MODEL), const2),
            pl.BlockSpec((D_MODEL, 2 * D_MODEL), const2),
            pl.BlockSpec((A_Q_W, D_MODEL), const2),
            pl.BlockSpec((B_W, D_MODEL), const2),
            pl.BlockSpec((D_MODEL, D_MODEL), const2),
            pl.BlockSpec((D_MODEL, LANES), const2),
            pl.BlockSpec((1, LANES), const2),
        ],
        out_specs=(
            pl.BlockSpec((bb, tl, D_MODEL), xmap),
            pl.BlockSpec((m * ROW_TILES, LANES), flatmap),
            pl.BlockSpec((bb, tl, LANES), xmap),
            pl.BlockSpec((bb, tl, LANES), xmap),
            pl.BlockSpec((1, LANES), const2),
        ),
        out_shape=(
            jax.ShapeDtypeStruct((bsz, length, D_MODEL), F32),
            jax.ShapeDtypeStruct((n_tok * ROW_TILES, LANES), F32),
            jax.ShapeDtypeStruct((bsz, length, LANES), I32),
            jax.ShapeDtypeStruct((bsz, length, LANES), F32),
            jax.ShapeDtypeStruct((1, LANES), F32),
        ),
        scratch_shapes=[pltpu.VMEM((1, LANES), F32)],
        compiler_params=_cparams("arbitrary", "arbitrary"),
        name="merge",
    )(x, oa, ob, sh_a, sc_a, gt_a, sh_m, sc_m, g_attn.reshape(1, D_MODEL), g_mlp.reshape(1, D_MODEL),
      wg, wa, wb, wo, wr, br)


def _row_copy(src, src_row, dst, dst_row, sem):
    s = pl.multiple_of(src_row * ROW_TILES, ROW_TILES)
    d = pl.multiple_of(dst_row * ROW_TILES, ROW_TILES)
    return pltpu.make_async_copy(src.at[pl.ds(s, ROW_TILES), :], dst.at[pl.ds(d, ROW_TILES), :], sem)


def _dispatch_kernel(meta_ref, dest_ref, h2_ref, xs_ref, zero_ref, sem, zsem, *, tm, n_steps):
    step = pl.program_id(0)
    n_assign = tm * TOP_K

    def issue(i, carry):
        _row_copy(h2_ref, i // TOP_K, xs_ref, dest_ref[0, 0, i], sem).start()
        return carry

    lax.fori_loop(0, n_assign, issue, 0)

    zero_ref[...] = jnp.zeros_like(zero_ref)
    for q in range(pl.cdiv(N_EXPERTS, n_steps)):
        e = step + q * n_steps

        @pl.when(e < N_EXPERTS)
        def _():
            lo = meta_ref[0, e]
            hi = meta_ref[1, e]

            def zissue(r, carry):
                _row_copy(zero_ref, 0, xs_ref, r, zsem).start()
                return carry

            lax.fori_loop(lo, hi, zissue, 0)

            def zwait(r, carry):
                _row_copy(zero_ref, 0, xs_ref, r, zsem).wait()
                return carry

            lax.fori_loop(lo, hi, zwait, 0)

    def wait(i, carry):
        _row_copy(h2_ref, 0, xs_ref, 0, sem).wait()
        return carry

    lax.fori_loop(0, n_assign, wait, 0)


def _dispatch(h2_rows, dest, pad_lo, pad_hi, n_rows, *, tm):
    n_tok = h2_rows.shape[0] // ROW_TILES
    n_steps = n_tok // tm
    assert n_tok % tm == 0
    meta = jnp.stack([pad_lo, pad_hi]).astype(I32)
    dest3 = dest.reshape(n_steps, 1, tm * TOP_K)
    kern = functools.partial(_dispatch_kernel, tm=tm, n_steps=n_steps)
    return pl.pallas_call(
        kern,
        grid_spec=pltpu.PrefetchScalarGridSpec(
            num_scalar_prefetch=1,
            grid=(n_steps,),
            in_specs=[
                pl.BlockSpec((1, 1, tm * TOP_K), lambda s, meta: (s, 0, 0), memory_space=pltpu.SMEM),
                pl.BlockSpec((tm * ROW_TILES, LANES), lambda s, meta: (s, 0)),
            ],
            out_specs=pl.BlockSpec(memory_space=pl.ANY),
            scratch_shapes=[pltpu.VMEM((ROW_TILES, LANES), F32),
                            pltpu.SemaphoreType.DMA(()),
                            pltpu.SemaphoreType.DMA(())],
        ),
        out_shape=jax.ShapeDtypeStruct((n_rows * ROW_TILES, LANES), F32),
        compiler_params=_cparams("arbitrary"),
        name="dispatch",
    )(meta, dest3, h2_rows)


def _expert_kernel(blk_ref, used_ref, xs_ref, wgu_ref, bgu_ref, wd_ref, bd_ref, y_ref, wgu_bf, wd_bf):
    r = pl.program_id(0)
    e = blk_ref[r]
    prev = blk_ref[jnp.maximum(r - 1, 0)]
    active = r < used_ref[0]

    @pl.when(jnp.logical_and(active, jnp.logical_or(r == 0, e != prev)))
    def _():
        wgu_bf[...] = wgu_ref[0].astype(BF16)
        wd_bf[...] = wd_ref[0].astype(BF16)

    @pl.when(active)
    def _():
        x = jnp.concatenate(
            [xs_ref[pl.ds(j, ROUTE_BLOCK, stride=ROW_TILES), :] for j in range(ROW_TILES)], axis=1).astype(BF16)
        gu = jnp.dot(x, wgu_bf[...], preferred_element_type=F32) + bgu_ref[0]
        gate = jnp.minimum(gu[:, :D_FF], SWIGLU_LIMIT)
        up = jnp.clip(gu[:, D_FF:], -SWIGLU_LIMIT, SWIGLU_LIMIT)
        glu = gate * jax.nn.sigmoid(gate * SWIGLU_ALPHA)
        act = ((up + 1.0) * glu).astype(BF16)
        y = jnp.dot(act, wd_bf[...], preferred_element_type=F32) + bd_ref[0]
        for j in range(ROW_TILES):
            y_ref[pl.ds(j, ROUTE_BLOCK, stride=ROW_TILES), :] = y[:, LANES * j:LANES * (j + 1)]


def _experts(xs_rows, blk_expert, n_used, w_gate_up, b_gate_up, w_down, b_down):
    n_blocks = blk_expert.shape[0]
    rb = ROUTE_BLOCK * ROW_TILES

    def rowmap(r, blk, used):
        return (jnp.minimum(r, used[0] - 1), 0)

    def wmap(r, blk, used):
        return (blk[jnp.minimum(r, used[0] - 1)], 0, 0)

    return pl.pallas_call(
        _expert_kernel,
        grid_spec=pltpu.PrefetchScalarGridSpec(
            num_scalar_prefetch=2,
            grid=(n_blocks,),
            in_specs=[
                pl.BlockSpec((rb, LANES), rowmap),
                pl.BlockSpec((1, D_MODEL, 2 * D_FF), wmap),
                pl.BlockSpec((1, 1, 2 * D_FF), wmap),
                pl.BlockSpec((1, D_FF, D_MODEL), wmap),
                pl.BlockSpec((1, 1, D_MODEL), wmap),
            ],
            out_specs=pl.BlockSpec((rb, LANES), rowmap),
            scratch_shapes=[pltpu.VMEM((D_MODEL, 2 * D_FF), BF16), pltpu.VMEM((D_FF, D_MODEL), BF16)],
        ),
        out_shape=jax.ShapeDtypeStruct(xs_rows.shape, F32),
        compiler_params=_cparams("arbitrary"),
        name="experts",
    )(blk_expert, n_used, xs_rows, w_gate_up, b_gate_up.reshape(N_EXPERTS, 1, 2 * D_FF),
      w_down, b_down.reshape(N_EXPERTS, 1, D_MODEL))


def _combine_kernel(dest_ref, x1_ref, gt_ref, rw_ref, y_ref, o_ref, buf_ref, sem, *, bb, tl):
    m = bb * tl
    n_assign = m * TOP_K

    def issue(i, carry):
        _row_copy(y_ref, dest_ref[0, 0, i], buf_ref, (i % TOP_K) * m + i // TOP_K, sem).start()
        return carry

    lax.fori_loop(0, n_assign, issue, 0)

    def wait(i, carry):
        _row_copy(y_ref, 0, buf_ref, 0, sem).wait()
        return carry

    lax.fori_loop(0, n_assign, wait, 0)

    rw = rw_ref[...].reshape(m, LANES)
    w = [rw[:, k:k + 1] for k in range(TOP_K)]
    gt = gt_ref[...]
    for j in range(ROW_TILES):
        acc = jnp.zeros((m, LANES), F32)
        for k in range(TOP_K):
            acc = acc + w[k] * buf_ref[pl.ds(k * m * ROW_TILES + j, m, stride=ROW_TILES), :]
        cg = slice(LANES * j, LANES * (j + 1))
        o_ref[:, :, cg] = x1_ref[:, :, cg] + gt[:, :, cg] * acc.reshape(bb, tl, LANES)


def _combine(x1, gt_m, rw, y_rows, dest, *, bb, tl):
    bsz, length, _ = x1.shape
    m = bb * tl
    assert bb == 1 or tl == length
    n_l = length // tl
    n_steps = (bsz // bb) * n_l
    dest3 = dest.reshape(n_steps, 1, m * TOP_K)

    def xmap(b, i):
        return (b, i, 0)

    kern = functools.partial(_combine_kernel, bb=bb, tl=tl)
    return pl.pallas_call(
        kern,
        grid=(bsz // bb, n_l),
        in_specs=[
            pl.BlockSpec((1, 1, m * TOP_K), lambda b, i: (b * n_l + i, 0, 0), memory_space=pltpu.SMEM),
            pl.BlockSpec((bb, tl, D_MODEL), xmap),
            pl.BlockSpec((bb, 1, D_MODEL), lambda b, i: (b, 0, 0)),
            pl.BlockSpec((bb, tl, LANES), xmap),
            pl.BlockSpec(memory_space=pl.ANY),
        ],
        out_specs=pl.BlockSpec((bb, tl, D_MODEL), xmap),
        out_shape=jax.ShapeDtypeStruct(x1.shape, F32),
        scratch_shapes=[pltpu.VMEM((TOP_K * m * ROW_TILES, LANES), F32), pltpu.SemaphoreType.DMA(())],
        compiler_params=_cparams("arbitrary", "arbitrary"),
        name="combine",
    )(dest3, x1, gt_m, rw, y_rows)


def _route_plan(ri, counts_f, n_tok):
    idx = ri[..., :TOP_K].reshape(n_tok, TOP_K)
    rank = ri[..., TOP_K:2 * TOP_K].reshape(n_tok, TOP_K)
    counts = counts_f[0, :N_EXPERTS].astype(I32)
    padded = (counts + ROUTE_BLOCK - 1) // ROUTE_BLOCK * ROUTE_BLOCK
    pends = jnp.cumsum(padded)
    pstarts = pends - padded
    dest = pstarts[idx] + rank
    n_blocks = n_tok * TOP_K // ROUTE_BLOCK + N_EXPERTS
    blk_expert = jnp.minimum(
        jnp.searchsorted(pends, jnp.arange(n_blocks, dtype=I32) * ROUTE_BLOCK, side='right'), N_EXPERTS - 1)
    n_used = (pends[-1] // ROUTE_BLOCK).reshape(1)
    return dest.astype(I32), blk_expert.astype(I32), n_used.astype(I32), pstarts + counts, pends, n_blocks


def _layer(x, mods, pos, prefix, weights, *, bb, tl, n_c, tm_dispatch, tl_combine, bb_combine):
    (g_attn, g_mlp, w_qkv, gains, sinks, bias, wg, wa, wb, wo, wr, br,
     w_gate_up, b_gate_up, w_down, b_down) = weights
    bsz, length, _ = x.shape
    n_tok = bsz * length
    sh_a, sc_a, gt_a, sh_m, sc_m, gt_m = mods
    pad = 0 if prefix is not None else PAD
    qa, ka, va, qb, kb, vb, ska, sva, skb, svb = _stage1(
        x, sh_a, sc_a, g_attn, w_qkv, gains, pos, bb=bb, tl=tl, pad=pad)
    if prefix is not None:
        ka, va, kb, vb = [jnp.concatenate([p, t], axis=1) for p, t in zip(prefix, (ka, va, kb, vb))]
    oa, ob = _attention(qa, ka, va, qb, kb, vb, sinks, bias, n_c=n_c, mask_prefix=prefix is None)
    x1, h2_rows, ri, rw, counts_f = _merge(x, oa, ob, (sh_a, sc_a, gt_a, sh_m, sc_m), g_attn, g_mlp,
                                           wg, wa, wb, wo, wr, br, bb=bb, tl=tl)
    dest, blk_expert, n_used, pad_lo, pad_hi, n_blocks = _route_plan(ri, counts_f, n_tok)
    xs_rows = _dispatch(h2_rows, dest, pad_lo, pad_hi, n_blocks * ROUTE_BLOCK, tm=tm_dispatch)
    y_rows = _experts(xs_rows, blk_expert, n_used, w_gate_up, b_gate_up, w_down, b_down)
    y = _combine(x1, gt_m, rw, y_rows, dest, bb=bb_combine, tl=tl_combine)
    return y, (ska, sva, skb, svb)


def _rel_bias_table(rel_table):
    q_pos = jnp.arange(CHUNK)
    k_pos = jnp.arange(BAND_B) - WINDOW_B
    idx = jnp.clip(q_pos[:, None] - k_pos[None, :], -REL_FWD, REL_BACK) + REL_FWD
    return rel_table[:, idx].astype(F32)


def _prep_weights(g_attn, g_mlp, w_in, q_norm_a, k_norm_a, q_norm_b, k_norm_b, sinks_a, rel_table_b,
                  w_branch_a, w_branch_b, w_out, w_router, b_router, w_gate_up, b_gate_up, w_down, b_down):
    order = jnp.asarray(A_HEAD_ORDER)
    w_qa = w_in[:, :A_Q_W].reshape(D_MODEL, A_Q_HEADS, HEAD_DIM)[:, order].reshape(D_MODEL, A_Q_W)
    w_qkv = jnp.concatenate([w_qa, w_in[:, A_Q_W:QKV_W]], axis=1).astype(BF16)
    wg = w_in[:, QKV_W:].astype(BF16)
    wa = w_branch_a.reshape(A_Q_HEADS, HEAD_DIM, D_MODEL)[order].reshape(A_Q_W, D_MODEL).astype(BF16)
    wb = w_branch_b.astype(BF16)
    wo = w_out.astype(BF16)
    wr = jnp.zeros((D_MODEL, LANES), F32).at[:, :N_EXPERTS].set(w_router)
    br = jnp.full((1, LANES), NEG_INF, F32).at[0, :N_EXPERTS].set(b_router)
    bias = _rel_bias_table(rel_table_b)
    gains = (q_norm_a, k_norm_a, q_norm_b, k_norm_b)
    return (g_attn, g_mlp, w_qkv, gains, sinks_a.astype(F32), bias, wg, wa, wb, wo, wr, br,
            w_gate_up, b_gate_up, w_down, b_down)


def _forward(x_prompt, x_sample, c_prompt, c_sample, cache_a_k, cache_a_v, cache_b_k, cache_b_v, params,
             past_len, cfg_prompt, cfg_sample):
    (g_attn, g_mlp, w_ada, b_ada, w_in, q_norm_a, k_norm_a, q_norm_b, k_norm_b, sinks_a, rel_table_b,
     w_branch_a, w_branch_b, w_out, w_router, b_router, w_gate_up, b_gate_up, w_down, b_down) = params
    depth = g_attn.shape[0]
    bp, lp_, _ = x_prompt.shape
    bs, ls, _ = x_sample.shape
    pos_p = jnp.arange(lp_, dtype=I32)
    pos_s = past_len + jnp.arange(ls, dtype=I32)
    yp, ys = x_prompt, x_sample
    st_p, st_s = [], []
    for l in range(depth):
        weights = _prep_weights(g_attn[l], g_mlp[l], w_in[l], q_norm_a[l], k_norm_a[l], q_norm_b[l],
                                k_norm_b[l], sinks_a[l], rel_table_b[l], w_branch_a[l], w_branch_b[l],
                                w_out[l], w_router[l], b_router[l], w_gate_up[l], b_gate_up[l],
                                w_down[l], b_down[l])
        mod = _adaln(jnp.concatenate([c_prompt, c_sample], axis=0), w_ada[l], b_ada[l])
        mod = mod.reshape(bp + bs, 6, 1, D_MODEL)
        mods_p = [mod[:bp, k] for k in range(6)]
        mods_s = [mod[bp:, k] for k in range(6)]
        yp, sp = _layer(yp, mods_p, pos_p, None, weights, **cfg_prompt)
        wa_c = cache_a_k.shape[2]
        wb_c = cache_b_k.shape[2]
        assert wb_c == WINDOW_B and wa_c == WINDOW_A
        zeros_a = jnp.zeros((bs, PAD - wa_c, A_KV_W), BF16)
        prefix = (
            jnp.concatenate([zeros_a, cache_a_k[l].reshape(bs, wa_c, A_KV_W).astype(BF16)], axis=1),
            jnp.concatenate([zeros_a, cache_a_v[l].reshape(bs, wa_c, A_KV_W).astype(BF16)], axis=1),
            cache_b_k[l].reshape(bs, wb_c, B_W).astype(BF16),
            cache_b_v[l].reshape(bs, wb_c, B_W).astype(BF16),
        )
        ys, ss = _layer(ys, mods_s, pos_s, prefix, weights, **cfg_sample)
        st_p.append(sp)
        st_s.append(ss)

    def stack(states, k, heads):
        return jnp.stack([s[k].reshape(s[k].shape[0], s[k].shape[1], heads, HEAD_DIM) for s in states])

    return (yp, ys,
            stack(st_p, 0, A_KV_HEADS), stack(st_p, 1, A_KV_HEADS), stack(st_p, 2, B_HEADS), stack(st_p, 3, B_HEADS),
            stack(st_s, 0, A_KV_HEADS), stack(st_s, 1, A_KV_HEADS), stack(st_s, 2, B_HEADS), stack(st_s, 3, B_HEADS))


PAST_LEN = 2048
CFG_PROMPT = dict(bb=1, tl=512, n_c=4, tm_dispatch=512, tl_combine=256, bb_combine=1)
CFG_SAMPLE = dict(bb=8, tl=64, n_c=1, tm_dispatch=64, tl_combine=64, bb_combine=4)


def kernel(x_prompt, x_sample, c_prompt, c_sample, cache_a_k, cache_a_v, cache_b_k, cache_b_v, g_attn, g_mlp,
           w_ada, b_ada, w_in, q_norm_a, k_norm_a, q_norm_b, k_norm_b, sinks_a, rel_table_b, w_branch_a,
           w_branch_b, w_out, w_router, b_router, w_gate_up, b_gate_up, w_down, b_down):
    params = (g_attn, g_mlp, w_ada, b_ada, w_in, q_norm_a, k_norm_a, q_norm_b, k_norm_b, sinks_a, rel_table_b,
              w_branch_a, w_branch_b, w_out, w_router, b_router, w_gate_up, b_gate_up, w_down, b_down)
    return _forward(x_prompt, x_sample, c_prompt, c_sample, cache_a_k, cache_a_v, cache_b_k, cache_b_v,
                    params, PAST_LEN, CFG_PROMPT, CFG_SAMPLE)
```

```python
import functools

import jax
import jax.numpy as jnp
from jax import lax
from jax.experimental import pallas as pl
from jax.experimental.pallas import tpu as pltpu
from jax.experimental.pallas import tpu_sc as plsc

F32 = jnp.float32
BF16 = jnp.bfloat16
I32 = jnp.int32

D_MODEL = 1024
CHUNK = 64
HEAD_DIM = 64
LANES = 128
SUBLANES = 8
A_Q_HEADS = 8
A_KV_HEADS = 2
A_GROUP = A_Q_HEADS // A_KV_HEADS
B_HEADS = 8
A_Q_W = A_Q_HEADS * HEAD_DIM
A_KV_W = A_KV_HEADS * HEAD_DIM
B_W = B_HEADS * HEAD_DIM
QKV_W = A_Q_W + 2 * A_KV_W + 3 * B_W
WINDOW_A = 128
WINDOW_B = 512
BAND_A = WINDOW_A + CHUNK
BAND_B = WINDOW_B + CHUNK
PAD = WINDOW_B
ROT_DIM = HEAD_DIM // 4
ROPE_THETA = 500000.0
REL_BACK = 4 * CHUNK
REL_FWD = CHUNK - 1
N_EXPERTS = 32
TOP_K = 4
D_FF = D_MODEL
SWIGLU_LIMIT = 7.0
SWIGLU_ALPHA = 1.702
ROUTE_BLOCK = 256
RMS_EPS = 1e-6
NEG_INF = -1e30
ROW_TILES = D_MODEL // LANES
VMEM_LIMIT = 56 * 1024 * 1024
SC_CORES = 2
SC_SUBCORES = 16
SC_WORKERS = SC_CORES * SC_SUBCORES
SC_ROWS = 32
TRASH_ROWS = 8

A_HEAD_ORDER = (0, 4, 1, 5, 2, 6, 3, 7)


def _cparams(*sem):
    return pltpu.CompilerParams(dimension_semantics=sem, vmem_limit_bytes=VMEM_LIMIT)


def _adaln_kernel(c_ref, w_ref, b_ref, o_ref):
    c = c_ref[...]
    a = c * jax.nn.sigmoid(c)
    o_ref[...] = jnp.dot(a, w_ref[...], preferred_element_type=F32,
                         precision=lax.Precision.HIGHEST) + b_ref[...]


def _adaln(c, w_ada, b_ada):
    n_rows = c.shape[0]
    n_out = w_ada.shape[1]
    tn = 1024
    return pl.pallas_call(
        _adaln_kernel,
        grid=(n_out // tn,),
        in_specs=[pl.BlockSpec((n_rows, D_MODEL), lambda j: (0, 0)),
                  pl.BlockSpec((D_MODEL, tn), lambda j: (0, j)),
                  pl.BlockSpec((1, tn), lambda j: (0, j))],
        out_specs=pl.BlockSpec((n_rows, tn), lambda j: (0, j)),
        out_shape=jax.ShapeDtypeStruct((n_rows, n_out), F32),
        compiler_params=_cparams("arbitrary"),
        name="adaln",
    )(c, w_ada, b_ada.reshape(1, n_out))


def _modulated_norm(x, g, scale, shift):
    ms = jnp.mean(x * x, axis=-1, keepdims=True)
    return (x * lax.rsqrt(ms + RMS_EPS) * g) * (1.0 + scale) + shift


def _head_norm(parts, gain, seg):
    m = parts[0].shape[0]
    sq = jnp.concatenate([p * p for p in parts], axis=0).astype(BF16)
    ms = jnp.dot(sq, seg, preferred_element_type=F32)
    r = lax.rsqrt(ms + RMS_EPS)
    return [parts[j] * r[j * m:(j + 1) * m] * gain for j in range(len(parts))]


def _rope(p, cos, sin_lo, sin_hi):
    return p * cos + pltpu.roll(p, LANES - ROT_DIM // 2, 1) * sin_lo + pltpu.roll(p, ROT_DIM // 2, 1) * sin_hi


def _stage1_kernel(x_ref, sh_ref, sc_ref, g_ref, w_ref, gqa_ref, gka_ref, gqb_ref, gkb_ref, seg_ref,
                   cos_ref, sinlo_ref, sinhi_ref,
                   qa_ref, ka_ref, va_ref, qb_ref, kb_ref, vb_ref, ska_ref, sva_ref, skb_ref, svb_ref,
                   *, bb, tl, pad_tiles, n_tiles, sa_rows, sb_rows):
    i = pl.program_id(1)
    m = bb * tl

    if pad_tiles:
        @pl.when(i < pad_tiles)
        def _():
            ka_ref[...] = jnp.zeros_like(ka_ref)
            va_ref[...] = jnp.zeros_like(va_ref)
            kb_ref[...] = jnp.zeros_like(kb_ref)
            vb_ref[...] = jnp.zeros_like(vb_ref)

    @pl.when(i >= pad_tiles)
    def _():
        h = _modulated_norm(x_ref[...], g_ref[...], sc_ref[...], sh_ref[...])
        hb = h.reshape(m, D_MODEL).astype(BF16)
        qkv = jnp.dot(hb, w_ref[...], preferred_element_type=F32)
        seg = seg_ref[...]

        def cols(lo, width):
            return [qkv[:, lo + LANES * j: lo + LANES * (j + 1)] for j in range(width // LANES)]

        cos = jnp.concatenate([cos_ref[...]] * bb, axis=0)
        sin_lo = jnp.concatenate([sinlo_ref[...]] * bb, axis=0)
        sin_hi = jnp.concatenate([sinhi_ref[...]] * bb, axis=0)
        scale = HEAD_DIM ** -0.5

        qa = [_rope(p, cos, sin_lo, sin_hi) * scale for p in _head_norm(cols(0, A_Q_W), gqa_ref[...], seg)]
        ka = [_rope(p, cos, sin_lo, sin_hi) for p in _head_norm(cols(A_Q_W, A_KV_W), gka_ref[...], seg)]
        va = cols(A_Q_W + A_KV_W, A_KV_W)
        off_b = A_Q_W + 2 * A_KV_W
        qb = [p * scale for p in _head_norm(cols(off_b, B_W), gqb_ref[...], seg)]
        kb = _head_norm(cols(off_b + B_W, B_W), gkb_ref[...], seg)
        vb = cols(off_b + 2 * B_W, B_W)

        def put(ref, parts):
            for j, p in enumerate(parts):
                ref[:, :, LANES * j:LANES * (j + 1)] = p.reshape(bb, tl, LANES).astype(ref.dtype)

        put(qa_ref, qa)
        put(ka_ref, ka)
        put(va_ref, va)
        put(qb_ref, qb)
        put(kb_ref, kb)
        put(vb_ref, vb)

        @pl.when(i == n_tiles - 1)
        def _():
            def put_state(ref, parts, rows):
                for j, p in enumerate(parts):
                    ref[:, :, LANES * j:LANES * (j + 1)] = p.reshape(bb, tl, LANES)[:, tl - rows:, :]

            put_state(ska_ref, ka, sa_rows)
            put_state(sva_ref, va, sa_rows)
            put_state(skb_ref, kb, sb_rows)
            put_state(svb_ref, vb, sb_rows)


def _rope_tables(pos):
    inv_freq = ROPE_THETA ** (-jnp.arange(0, ROT_DIM, 2, dtype=F32) / ROT_DIM)
    ang = pos.astype(F32)[:, None] * inv_freq[None, :]
    half = ROT_DIM // 2
    lane = jnp.arange(LANES) % HEAD_DIM
    ang_l = ang[:, lane % half]
    rot = lane < ROT_DIM
    cos = jnp.where(rot[None, :], jnp.cos(ang_l), 1.0)
    sin_lo = jnp.where((lane < half)[None, :], -jnp.sin(ang_l), 0.0)
    sin_hi = jnp.where(((lane >= half) & rot)[None, :], jnp.sin(ang_l), 0.0)
    return cos, sin_lo, sin_hi


def _stage1(x, shift, scale, g_attn, w_qkv, gains, pos, *, bb, tl, pad):
    bsz, length, _ = x.shape
    n_tiles = length // tl
    pad_tiles = pad // tl
    sa_rows, sb_rows = min(WINDOW_A, length), min(WINDOW_B, length)
    assert bsz % bb == 0 and length % tl == 0 and pad % tl == 0 and sb_rows <= tl
    cos, sin_lo, sin_hi = _rope_tables(pos)
    seg = (jnp.arange(LANES)[:, None] // HEAD_DIM == jnp.arange(LANES)[None, :] // HEAD_DIM)
    seg = (seg.astype(F32) / HEAD_DIM).astype(BF16)
    gq_a, gk_a, gq_b, gk_b = [jnp.tile(g, LANES // HEAD_DIM).reshape(1, LANES) for g in gains]

    def xmap(b, i):
        return (b, jnp.maximum(i - pad_tiles, 0), 0)

    def const2(b, i):
        return (0, 0)

    def tabmap(b, i):
        return (jnp.maximum(i - pad_tiles, 0), 0)

    def modmap(b, i):
        return (b, 0, 0)

    def kvmap(b, i):
        return (b, i, 0)

    lp = length + pad
    kern = functools.partial(_stage1_kernel, bb=bb, tl=tl, pad_tiles=pad_tiles,
                             n_tiles=n_tiles + pad_tiles, sa_rows=sa_rows, sb_rows=sb_rows)
    out_shapes = (
        jax.ShapeDtypeStruct((bsz, length, A_Q_W), BF16),
        jax.ShapeDtypeStruct((bsz, lp, A_KV_W), BF16),
        jax.ShapeDtypeStruct((bsz, lp, A_KV_W), BF16),
        jax.ShapeDtypeStruct((bsz, length, B_W), BF16),
        jax.ShapeDtypeStruct((bsz, lp, B_W), BF16),
        jax.ShapeDtypeStruct((bsz, lp, B_W), BF16),
        jax.ShapeDtypeStruct((bsz, sa_rows, A_KV_W), F32),
        jax.ShapeDtypeStruct((bsz, sa_rows, A_KV_W), F32),
        jax.ShapeDtypeStruct((bsz, sb_rows, B_W), F32),
        jax.ShapeDtypeStruct((bsz, sb_rows, B_W), F32),
    )
    out_specs = (
        pl.BlockSpec((bb, tl, A_Q_W), xmap),
        pl.BlockSpec((bb, tl, A_KV_W), kvmap),
        pl.BlockSpec((bb, tl, A_KV_W), kvmap),
        pl.BlockSpec((bb, tl, B_W), xmap),
        pl.BlockSpec((bb, tl, B_W), kvmap),
        pl.BlockSpec((bb, tl, B_W), kvmap),
        pl.BlockSpec((bb, sa_rows, A_KV_W), modmap),
        pl.BlockSpec((bb, sa_rows, A_KV_W), modmap),
        pl.BlockSpec((bb, sb_rows, B_W), modmap),
        pl.BlockSpec((bb, sb_rows, B_W), modmap),
    )
    in_specs = [
        pl.BlockSpec((bb, tl, D_MODEL), xmap),
        pl.BlockSpec((bb, 1, D_MODEL), modmap),
        pl.BlockSpec((bb, 1, D_MODEL), modmap),
        pl.BlockSpec((1, D_MODEL), const2),
        pl.BlockSpec((D_MODEL, QKV_W), const2),
        pl.BlockSpec((1, LANES), const2),
        pl.BlockSpec((1, LANES), const2),
        pl.BlockSpec((1, LANES), const2),
        pl.BlockSpec((1, LANES), const2),
        pl.BlockSpec((LANES, LANES), const2),
        pl.BlockSpec((tl, LANES), tabmap),
        pl.BlockSpec((tl, LANES), tabmap),
        pl.BlockSpec((tl, LANES), tabmap),
    ]
    return pl.pallas_call(
        kern,
        grid=(bsz // bb, n_tiles + pad_tiles),
        in_specs=in_specs,
        out_specs=out_specs,
        out_shape=out_shapes,
        compiler_params=_cparams("arbitrary", "arbitrary"),
        name="stage1",
    )(x, shift, scale, g_attn.reshape(1, D_MODEL), w_qkv, gq_a, gk_a, gq_b, gk_b, seg, cos, sin_lo, sin_hi)


def _nt_dot(a, b):
    return lax.dot_general(a, b, (((1,), (1,)), ((), ())), preferred_element_type=F32)


def _attn_kernel(sink_ref, qa_ref, ka_ref, va_ref, qb_ref, kb_ref, vb_ref, bias_ref, oa_ref, ob_ref,
                 *, n_c, mask_prefix):
    qi = pl.program_id(1)
    lane = lax.broadcasted_iota(I32, (CHUNK, LANES), 1)
    low = lane < HEAD_DIM
    half_mask = (jnp.where(low, 1.0, 0.0).astype(BF16), jnp.where(low, 0.0, 1.0).astype(BF16))
    key_b = lax.broadcasted_iota(I32, (CHUNK, BAND_B), 1)
    key_a = lax.broadcasted_iota(I32, (A_GROUP * CHUNK, BAND_A), 1)
    row_a = lax.broadcasted_iota(I32, (A_GROUP * CHUNK, 1), 0) // CHUNK

    sink_cols = []
    for p in range(A_KV_HEADS):
        col = jnp.zeros((A_GROUP * CHUNK, 1), F32)
        for j in range(A_GROUP):
            col = jnp.where(row_a == j, sink_ref[p * A_GROUP + j], col)
        sink_cols.append(col)

    for c in range(n_c):
        ci = qi * n_c + c
        row0 = pl.multiple_of(ci * CHUNK, CHUNK)
        rows = slice(c * CHUNK, (c + 1) * CHUNK)
        first_valid = PAD - ci * CHUNK

        for j in range(B_W // LANES):
            cg = slice(LANES * j, LANES * (j + 1))
            q = qb_ref[0, rows, cg]
            k = kb_ref[0, pl.ds(row0, BAND_B), cg]
            v = vb_ref[0, pl.ds(row0, BAND_B), cg]
            outs = []
            for p in range(2):
                s = _nt_dot(q * half_mask[p], k) + bias_ref[2 * j + p]
                if mask_prefix:
                    s = jnp.where(key_b >= first_valid, s, NEG_INF)
                mx = jnp.max(s, axis=-1, keepdims=True)
                e = jnp.exp(s - mx)
                den = jnp.sum(e, axis=-1, keepdims=True)
                o = jnp.dot(e.astype(BF16), v, preferred_element_type=F32)
                outs.append(o * (1.0 / den))
            ob_ref[0, rows, cg] = jnp.where(low, outs[0], outs[1]).astype(ob_ref.dtype)

        row_a0 = pl.multiple_of(row0 + (WINDOW_B - WINDOW_A), CHUNK)
        k = ka_ref[0, pl.ds(row_a0, BAND_A), :]
        v = va_ref[0, pl.ds(row_a0, BAND_A), :]
        outs = []
        for p in range(A_KV_HEADS):
            q = jnp.concatenate(
                [qa_ref[0, rows, LANES * j:LANES * (j + 1)] * half_mask[p] for j in range(A_GROUP)],
                axis=0)
            s = _nt_dot(q, k)
            if mask_prefix:
                s = jnp.where(key_a >= first_valid - (WINDOW_B - WINDOW_A), s, NEG_INF)
            sk = sink_cols[p]
            mx = jnp.maximum(jnp.max(s, axis=-1, keepdims=True), sk)
            e = jnp.exp(s - mx)
            den = jnp.sum(e, axis=-1, keepdims=True) + jnp.exp(sk - mx)
            o = jnp.dot(e.astype(BF16), v, preferred_element_type=F32)
            outs.append(o * (1.0 / den))
        for j in range(A_GROUP):
            rj = slice(j * CHUNK, (j + 1) * CHUNK)
            oa_ref[0, rows, LANES * j:LANES * (j + 1)] = jnp.where(low, outs[0][rj], outs[1][rj]).astype(oa_ref.dtype)


def _attention(qa, ka, va, qb, kb, vb, sinks, bias, *, n_c, mask_prefix):
    bsz, length, _ = qa.shape
    lp = ka.shape[1]
    qt = n_c * CHUNK
    assert length % qt == 0 and lp == length + PAD

    def qmap(b, i):
        return (b, i, 0)

    def kmap(b, i):
        return (b, 0, 0)

    kern = functools.partial(_attn_kernel, n_c=n_c, mask_prefix=mask_prefix)
    return pl.pallas_call(
        kern,
        grid=(bsz, length // qt),
        in_specs=[
            pl.BlockSpec(memory_space=pltpu.SMEM),
            pl.BlockSpec((1, qt, A_Q_W), qmap),
            pl.BlockSpec((1, lp, A_KV_W), kmap),
            pl.BlockSpec((1, lp, A_KV_W), kmap),
            pl.BlockSpec((1, qt, B_W), qmap),
            pl.BlockSpec((1, lp, B_W), kmap),
            pl.BlockSpec((1, lp, B_W), kmap),
            pl.BlockSpec((B_HEADS, CHUNK, BAND_B), lambda b, i: (0, 0, 0)),
        ],
        out_specs=(pl.BlockSpec((1, qt, A_Q_W), qmap), pl.BlockSpec((1, qt, B_W), qmap)),
        out_shape=(jax.ShapeDtypeStruct((bsz, length, A_Q_W), BF16),
                   jax.ShapeDtypeStruct((bsz, length, B_W), BF16)),
        compiler_params=_cparams("arbitrary", "arbitrary"),
        name="attention",
    )(sinks, qa, ka, va, qb, kb, vb, bias)


def _merge_kernel(x_ref, oa_ref, ob_ref, sha_ref, sca_ref, gta_ref, shm_ref, scm_ref, ga_ref, gm_ref,
                  wg_ref, wa_ref, wb_ref, wo_ref, wr_ref, br_ref,
                  x1_ref, h2_ref, ri_ref, rw_ref, cnt_ref, carry_ref, *, bb, tl):
    m = bb * tl
    first = jnp.logical_and(pl.program_id(0) == 0, pl.program_id(1) == 0)

    @pl.when(first)
    def _():
        carry_ref[...] = jnp.zeros_like(carry_ref)

    x = x_ref[...]
    h = _modulated_norm(x, ga_ref[...], sca_ref[...], sha_ref[...])
    hb = h.reshape(m, D_MODEL).astype(BF16)
    gates = jnp.dot(hb, wg_ref[...], preferred_element_type=F32)
    ya = jnp.dot(oa_ref[...].reshape(m, A_Q_W), wa_ref[...], preferred_element_type=F32)
    yb = jnp.dot(ob_ref[...].reshape(m, B_W), wb_ref[...], preferred_element_type=F32)
    mix = jax.nn.sigmoid(gates[:, :D_MODEL]) * ya + jax.nn.sigmoid(gates[:, D_MODEL:]) * yb
    z = jnp.dot(mix.astype(BF16), wo_ref[...], preferred_element_type=F32)
    x1 = x + gta_ref[...] * z.reshape(bb, tl, D_MODEL)
    x1_ref[...] = x1

    h2 = _modulated_norm(x1, gm_ref[...], scm_ref[...], shm_ref[...]).reshape(m, D_MODEL)
    for j in range(ROW_TILES):
        h2_ref[pl.ds(j, m, stride=ROW_TILES), :] = h2[:, LANES * j:LANES * (j + 1)]

    logits = jnp.dot(h2, wr_ref[...], preferred_element_type=F32,
                     precision=lax.Precision.HIGHEST) + br_ref[...]
    lane = lax.broadcasted_iota(I32, (m, LANES), 1)
    lane_f = lane.astype(F32)
    work = logits
    top_v, top_i = [], []
    sel_f = jnp.zeros((m, LANES), F32)
    for _ in range(TOP_K):
        mx = jnp.max(work, axis=-1, keepdims=True)
        idx = jnp.min(jnp.where(work == mx, lane_f, float(LANES)), axis=-1, keepdims=True)
        hit = lane_f == idx
        top_v.append(mx)
        top_i.append(idx)
        sel_f = jnp.where(hit, 1.0, sel_f)
        work = jnp.where(hit, -jnp.inf, work)
    ex = [jnp.exp(v - top_v[0]) for v in top_v]
    inv = 1.0 / (ex[0] + ex[1] + ex[2] + ex[3])

    r_i = lax.broadcasted_iota(I32, (m, m), 0)
    c_i = lax.broadcasted_iota(I32, (m, m), 1)
    before = jnp.where(r_i > c_i, 1.0, 0.0).astype(BF16)
    pos = jnp.dot(before, sel_f.astype(BF16), preferred_element_type=F32) + carry_ref[...]
    carry = carry_ref[...] + jnp.sum(sel_f, axis=0, keepdims=True)
    carry_ref[...] = carry
    cnt_ref[...] = carry

    ri_f = jnp.zeros((m, LANES), F32)
    rw = jnp.zeros((m, LANES), F32)
    for k in range(TOP_K):
        rank = jnp.sum(jnp.where(lane_f == top_i[k], pos, 0.0), axis=-1, keepdims=True)
        ri_f = jnp.where(lane == k, top_i[k], ri_f)
        ri_f = jnp.where(lane == TOP_K + k, rank, ri_f)
        rw = jnp.where(lane == k, ex[k] * inv, rw)
    ri = ri_f.astype(I32)
    ri_ref[...] = ri.reshape(bb, tl, LANES)
    rw_ref[...] = rw.reshape(bb, tl, LANES)


def _merge(x, oa, ob, mods, g_attn, g_mlp, wg, wa, wb, wo, wr, br, *, bb, tl):
    bsz, length, _ = x.shape
    n_tok = bsz * length
    m = bb * tl
    assert bb == 1 or tl == length
    sh_a, sc_a, gt_a, sh_m, sc_m = mods

    def xmap(b, i):
        return (b, i, 0)

    def modmap(b, i):
        return (b, 0, 0)

    def const2(b, i):
        return (0, 0)

    n_l = length // tl

    def flatmap(b, i):
        return (b * n_l + i, 0)

    modspec = pl.BlockSpec((bb, 1, D_MODEL), modmap)
    kern = functools.partial(_merge_kernel, bb=bb, tl=tl)
    return pl.pallas_call(
        kern,
        grid=(bsz // bb, n_l),
        in_specs=[
            pl.BlockSpec((bb, tl, D_MODEL), xmap),
            pl.BlockSpec((bb, tl, A_Q_W), xmap),
            pl.BlockSpec((bb, tl, B_W), xmap),
            modspec, modspec, modspec, modspec, modspec,
            pl.BlockSpec((1, D_MODEL), const2),
            pl.BlockSpec((1, D_MODEL), const2),
            pl.BlockSpec((D_MODEL, 2 * D_MODEL), const2),
            pl.BlockSpec((A_Q_W, D_MODEL), const2),
            pl.BlockSpec((B_W, D_MODEL), const2),
            pl.BlockSpec((D_MODEL, D_MODEL), const2),
            pl.BlockSpec((D_MODEL, LANES), const2),
            pl.BlockSpec((1, LANES), const2),
        ],
        out_specs=(
            pl.BlockSpec((bb, tl, D_MODEL), xmap),
            pl.BlockSpec((m * ROW_TILES, LANES), flatmap),
            pl.BlockSpec((bb, tl, LANES), xmap),
            pl.BlockSpec((bb, tl, LANES), xmap),
            pl.BlockSpec((1, LANES), const2),
        ),
        out_shape=(
            jax.ShapeDtypeStruct((bsz, length, D_MODEL), F32),
            jax.ShapeDtypeStruct((n_tok * ROW_TILES, LANES), F32),
            jax.ShapeDtypeStruct((bsz, length, LANES), I32),
            jax.ShapeDtypeStruct((bsz, length, LANES), F32),
            jax.ShapeDtypeStruct((1, LANES), F32),
        ),
        scratch_shapes=[pltpu.VMEM((1, LANES), F32)],
        compiler_params=_cparams("arbitrary", "arbitrary"),
        name="merge",
    )(x, oa, ob, sh_a, sc_a, gt_a, sh_m, sc_m, g_attn.reshape(1, D_MODEL), g_mlp.reshape(1, D_MODEL),
      wg, wa, wb, wo, wr, br)


def _sc_mesh():
    return plsc.VectorSubcoreMesh(core_axis_name="c", subcore_axis_name="s",
                                  num_cores=SC_CORES, num_subcores=SC_SUBCORES)


def _sc_worker():
    return lax.axis_index("c") * SC_SUBCORES + lax.axis_index("s")


def _sc_params():
    return pltpu.CompilerParams(use_tc_tiling_on_sc=True)


def _dispatch(h2_rows, idx_flat, pad_flat, n_rows):
    n_tok = h2_rows.shape[0]
    per_w = n_tok // SC_WORKERS
    n_chunks = per_w // SC_ROWS
    n_idx = n_chunks * TOP_K * SC_ROWS
    n_pad = pad_flat.shape[0] // SC_WORKERS
    pad_chunks = n_pad // SC_ROWS
    assert per_w % (2 * SC_ROWS) == 0 and n_pad % SC_ROWS == 0
    zeros = jnp.zeros((SC_ROWS, ROW_TILES, LANES), F32)

    def body(h2_hbm, idx_hbm, pad_hbm, zeros_hbm, xs_hbm, idx_v, pad_v, buf0, buf1, ls0, ls1, ss0, ss1):
        wid = _sc_worker()
        bufs, lsems, ssems = (buf0, buf1), (ls0, ls1), (ss0, ss1)
        pltpu.sync_copy(idx_hbm.at[pl.ds(pl.multiple_of(wid * n_idx, SC_ROWS), n_idx)], idx_v)
        pltpu.sync_copy(pad_hbm.at[pl.ds(pl.multiple_of(wid * n_pad, SC_ROWS), n_pad)], pad_v)

        pltpu.sync_copy(zeros_hbm, buf0)
        for c in range(pad_chunks):
            pltpu.async_copy(buf0, xs_hbm.at[pad_v.at[pl.ds(c * SC_ROWS, SC_ROWS)]], ss0)
        for c in range(pad_chunks):
            pltpu.make_async_copy(buf0, xs_hbm.at[pad_v.at[pl.ds(c * SC_ROWS, SC_ROWS)]], ss0).wait()

        base = wid * per_w

        def load(g, b):
            src = h2_hbm.at[pl.ds(pl.multiple_of(base + g * SC_ROWS, SC_ROWS), SC_ROWS)]
            return pltpu.make_async_copy(src, bufs[b], lsems[b])

        def scatter(g, k, b):
            off = pl.multiple_of((g * TOP_K + k) * SC_ROWS, SC_ROWS)
            return pltpu.make_async_copy(bufs[b], xs_hbm.at[idx_v.at[pl.ds(off, SC_ROWS)]], ssems[b])

        load(0, 0).start()

        @pl.loop(0, n_chunks, step=2)
        def _(g0):
            for b in range(2):
                g = g0 + b
                load(g, b).wait()

                @pl.when(g >= 1)
                def _():
                    for k in range(TOP_K):
                        scatter(g - 1, k, 1 - b).wait()

                @pl.when(g + 1 < n_chunks)
                def _():
                    load(g + 1, 1 - b).start()

                for k in range(TOP_K):
                    scatter(g, k, b).start()

        for k in range(TOP_K):
            scatter(n_chunks - 1, k, (n_chunks - 1) % 2).wait()

    call = pl.kernel(
        body,
        out_type=jax.ShapeDtypeStruct((n_rows + TRASH_ROWS, ROW_TILES, LANES), F32),
        mesh=_sc_mesh(),
        scratch_types=[pltpu.VMEM((n_idx,), I32), pltpu.VMEM((n_pad,), I32),
                       pltpu.VMEM((SC_ROWS, ROW_TILES, LANES), F32), pltpu.VMEM((SC_ROWS, ROW_TILES, LANES), F32),
                       pltpu.SemaphoreType.DMA, pltpu.SemaphoreType.DMA,
                       pltpu.SemaphoreType.DMA, pltpu.SemaphoreType.DMA],
        compiler_params=_sc_params(),
        name="dispatch",
    )
    return call(h2_rows, idx_flat, pad_flat, zeros)


def _gather_back(y_rows, idx_flat, n_tok):
    per_w = n_tok // SC_WORKERS
    n_chunks = per_w // SC_ROWS
    n_items = n_chunks * TOP_K
    n_idx = n_items * SC_ROWS
    assert per_w % SC_ROWS == 0 and n_items % 2 == 0

    def body(y_hbm, idx_hbm, yg_hbm, idx_v, buf0, buf1, gs0, gs1, ws0, ws1):
        wid = _sc_worker()
        bufs, gsems, wsems = (buf0, buf1), (gs0, gs1), (ws0, ws1)
        pltpu.sync_copy(idx_hbm.at[pl.ds(pl.multiple_of(wid * n_idx, SC_ROWS), n_idx)], idx_v)
        base = wid * per_w

        def gather(i, b):
            off = pl.multiple_of(i * SC_ROWS, SC_ROWS)
            return pltpu.make_async_copy(y_hbm.at[idx_v.at[pl.ds(off, SC_ROWS)]], bufs[b], gsems[b])

        def write(i, b):
            g, k = i // TOP_K, i % TOP_K
            dst = yg_hbm.at[k, pl.ds(pl.multiple_of(base + g * SC_ROWS, SC_ROWS), SC_ROWS)]
            return pltpu.make_async_copy(bufs[b], dst, wsems[b])

        gather(0, 0).start()

        @pl.loop(0, n_items, step=2)
        def _(i0):
            for b in range(2):
                i = i0 + b
                gather(i, b).wait()

                @pl.when(i >= 1)
                def _():
                    write(i - 1, 1 - b).wait()

                @pl.when(i + 1 < n_items)
                def _():
                    gather(i + 1, 1 - b).start()

                write(i, b).start()

        write(n_items - 1, (n_items - 1) % 2).wait()

    call = pl.kernel(
        body,
        out_type=jax.ShapeDtypeStruct((TOP_K, n_tok, ROW_TILES, LANES), F32),
        mesh=_sc_mesh(),
        scratch_types=[pltpu.VMEM((n_idx,), I32),
                       pltpu.VMEM((SC_ROWS, ROW_TILES, LANES), F32), pltpu.VMEM((SC_ROWS, ROW_TILES, LANES), F32),
                       pltpu.SemaphoreType.DMA, pltpu.SemaphoreType.DMA,
                       pltpu.SemaphoreType.DMA, pltpu.SemaphoreType.DMA],
        compiler_params=_sc_params(),
        name="gather_back",
    )
    return call(y_rows, idx_flat)


def _expert_kernel(blk_ref, used_ref, xs_ref, wgu_ref, bgu_ref, wd_ref, bd_ref, y_ref, wgu_bf, wd_bf):
    r = pl.program_id(0)
    e = blk_ref[r]
    prev = blk_ref[jnp.maximum(r - 1, 0)]
    active = r < used_ref[0]

    @pl.when(jnp.logical_and(active, jnp.logical_or(r == 0, e != prev)))
    def _():
        wgu_bf[...] = wgu_ref[0].astype(BF16)
        wd_bf[...] = wd_ref[0].astype(BF16)

    @pl.when(active)
    def _():
        x = jnp.concatenate(
            [xs_ref[pl.ds(j, ROUTE_BLOCK, stride=ROW_TILES), :] for j in range(ROW_TILES)], axis=1).astype(BF16)
        gu = jnp.dot(x, wgu_bf[...], preferred_element_type=F32) + bgu_ref[0]
        gate = jnp.minimum(gu[:, :D_FF], SWIGLU_LIMIT)
        up = jnp.clip(gu[:, D_FF:], -SWIGLU_LIMIT, SWIGLU_LIMIT)
        glu = gate * jax.nn.sigmoid(gate * SWIGLU_ALPHA)
        act = ((up + 1.0) * glu).astype(BF16)
        y = jnp.dot(act, wd_bf[...], preferred_element_type=F32) + bd_ref[0]
        for j in range(ROW_TILES):
            y_ref[pl.ds(j, ROUTE_BLOCK, stride=ROW_TILES), :] = y[:, LANES * j:LANES * (j + 1)]


def _experts(xs_rows, blk_expert, n_used, w_gate_up, b_gate_up, w_down, b_down):
    n_blocks = blk_expert.shape[0]
    rb = ROUTE_BLOCK * ROW_TILES

    def rowmap(r, blk, used):
        return (jnp.minimum(r, used[0] - 1), 0)

    def wmap(r, blk, used):
        return (blk[jnp.minimum(r, used[0] - 1)], 0, 0)

    return pl.pallas_call(
        _expert_kernel,
        grid_spec=pltpu.PrefetchScalarGridSpec(
            num_scalar_prefetch=2,
            grid=(n_blocks,),
            in_specs=[
                pl.BlockSpec((rb, LANES), rowmap),
                pl.BlockSpec((1, D_MODEL, 2 * D_FF), wmap),
                pl.BlockSpec((1, 1, 2 * D_FF), wmap),
                pl.BlockSpec((1, D_FF, D_MODEL), wmap),
                pl.BlockSpec((1, 1, D_MODEL), wmap),
            ],
            out_specs=pl.BlockSpec((rb, LANES), rowmap),
            scratch_shapes=[pltpu.VMEM((D_MODEL, 2 * D_FF), BF16), pltpu.VMEM((D_FF, D_MODEL), BF16)],
        ),
        out_shape=jax.ShapeDtypeStruct(xs_rows.shape, F32),
        compiler_params=_cparams("arbitrary"),
        name="experts",
    )(blk_expert, n_used, xs_rows, w_gate_up, b_gate_up.reshape(N_EXPERTS, 1, 2 * D_FF),
      w_down, b_down.reshape(N_EXPERTS, 1, D_MODEL))


def _combine_kernel(x1_ref, gt_ref, rw_ref, yg_ref, o_ref, *, bb, tl):
    m = bb * tl
    rw = rw_ref[...].reshape(m, LANES)
    w = [rw[:, k:k + 1] for k in range(TOP_K)]
    gt = gt_ref[...]
    for j in range(ROW_TILES):
        acc = jnp.zeros((m, LANES), F32)
        for k in range(TOP_K):
            acc = acc + w[k] * yg_ref[k, pl.ds(j, m, stride=ROW_TILES), :]
        cg = slice(LANES * j, LANES * (j + 1))
        o_ref[:, :, cg] = x1_ref[:, :, cg] + gt[:, :, cg] * acc.reshape(bb, tl, LANES)


def _combine(x1, gt_m, rw, yg, *, bb, tl):
    bsz, length, _ = x1.shape
    m = bb * tl
    assert bb == 1 or tl == length
    n_l = length // tl

    def xmap(b, i):
        return (b, i, 0)

    kern = functools.partial(_combine_kernel, bb=bb, tl=tl)
    return pl.pallas_call(
        kern,
        grid=(bsz // bb, n_l),
        in_specs=[
            pl.BlockSpec((bb, tl, D_MODEL), xmap),
            pl.BlockSpec((bb, 1, D_MODEL), lambda b, i: (b, 0, 0)),
            pl.BlockSpec((bb, tl, LANES), xmap),
            pl.BlockSpec((TOP_K, m * ROW_TILES, LANES), lambda b, i: (0, b * n_l + i, 0)),
        ],
        out_specs=pl.BlockSpec((bb, tl, D_MODEL), xmap),
        out_shape=jax.ShapeDtypeStruct(x1.shape, F32),
        compiler_params=_cparams("arbitrary", "arbitrary"),
        name="combine",
    )(x1, gt_m, rw, yg)


def _route_plan(ri, counts_f, n_tok):
    idx = ri[..., :TOP_K].reshape(n_tok, TOP_K)
    rank = ri[..., TOP_K:2 * TOP_K].reshape(n_tok, TOP_K)
    counts = counts_f[0, :N_EXPERTS].astype(I32)
    padded = (counts + ROUTE_BLOCK - 1) // ROUTE_BLOCK * ROUTE_BLOCK
    pends = jnp.cumsum(padded)
    pstarts = pends - padded
    dest = pstarts[idx] + rank
    n_blocks = n_tok * TOP_K // ROUTE_BLOCK + N_EXPERTS
    blk_expert = jnp.minimum(
        jnp.searchsorted(pends, jnp.arange(n_blocks, dtype=I32) * ROUTE_BLOCK, side='right'), N_EXPERTS - 1)
    n_used = (pends[-1] // ROUTE_BLOCK).reshape(1)
    n_rows = n_blocks * ROUTE_BLOCK
    per_w = n_tok // SC_WORKERS
    idx_flat = dest.astype(I32).reshape(SC_WORKERS, per_w // SC_ROWS, SC_ROWS, TOP_K)
    idx_flat = idx_flat.transpose(0, 1, 3, 2).reshape(-1)
    r = jnp.arange(ROUTE_BLOCK, dtype=I32)[None, :]
    pad_flat = jnp.where(r < (padded - counts)[:, None], (pstarts + counts)[:, None] + r, n_rows)
    return idx_flat, pad_flat.astype(I32).reshape(-1), blk_expert.astype(I32), n_used.astype(I32), n_rows


def _layer(x, mods, pos, prefix, weights, *, bb, tl, n_c, tl_combine, bb_combine):
    (g_attn, g_mlp, w_qkv, gains, sinks, bias, wg, wa, wb, wo, wr, br,
     w_gate_up, b_gate_up, w_down, b_down) = weights
    bsz, length, _ = x.shape
    n_tok = bsz * length
    sh_a, sc_a, gt_a, sh_m, sc_m, gt_m = mods
    pad = 0 if prefix is not None else PAD
    qa, ka, va, qb, kb, vb, ska, sva, skb, svb = _stage1(
        x, sh_a, sc_a, g_attn, w_qkv, gains, pos, bb=bb, tl=tl, pad=pad)
    if prefix is not None:
        ka, va, kb, vb = [jnp.concatenate([p, t], axis=1) for p, t in zip(prefix, (ka, va, kb, vb))]
    oa, ob = _attention(qa, ka, va, qb, kb, vb, sinks, bias, n_c=n_c, mask_prefix=prefix is None)
    x1, h2_rows, ri, rw, counts_f = _merge(x, oa, ob, (sh_a, sc_a, gt_a, sh_m, sc_m), g_attn, g_mlp,
                                           wg, wa, wb, wo, wr, br, bb=bb, tl=tl)
    idx_flat, pad_flat, blk_expert, n_used, n_rows = _route_plan(ri, counts_f, n_tok)
    xs = _dispatch(h2_rows.reshape(n_tok, ROW_TILES, LANES), idx_flat, pad_flat, n_rows)
    y_rows = _experts(xs.reshape(-1, LANES), blk_expert, n_used, w_gate_up, b_gate_up, w_down, b_down)
    yg = _gather_back(y_rows.reshape(-1, ROW_TILES, LANES), idx_flat, n_tok)
    y = _combine(x1, gt_m, rw, yg.reshape(TOP_K, n_tok * ROW_TILES, LANES), bb=bb_combine, tl=tl_combine)
    return y, (ska, sva, skb, svb)


def _rel_bias_table(rel_table):
    q_pos = jnp.arange(CHUNK)
    k_pos = jnp.arange(BAND_B) - WINDOW_B
    idx = jnp.clip(q_pos[:, None] - k_pos[None, :], -REL_FWD, REL_BACK) + REL_FWD
    return rel_table[:, idx].astype(F32)


def _prep_weights(g_attn, g_mlp, w_in, q_norm_a, k_norm_a, q_norm_b, k_norm_b, sinks_a, rel_table_b,
                  w_branch_a, w_branch_b, w_out, w_router, b_router, w_gate_up, b_gate_up, w_down, b_down):
    order = jnp.asarray(A_HEAD_ORDER)
    w_qa = w_in[:, :A_Q_W].reshape(D_MODEL, A_Q_HEADS, HEAD_DIM)[:, order].reshape(D_MODEL, A_Q_W)
    w_qkv = jnp.concatenate([w_qa, w_in[:, A_Q_W:QKV_W]], axis=1).astype(BF16)
    wg = w_in[:, QKV_W:].astype(BF16)
    wa = w_branch_a.reshape(A_Q_HEADS, HEAD_DIM, D_MODEL)[order].reshape(A_Q_W, D_MODEL).astype(BF16)
    wb = w_branch_b.astype(BF16)
    wo = w_out.astype(BF16)
    wr = jnp.zeros((D_MODEL, LANES), F32).at[:, :N_EXPERTS].set(w_router)
    br = jnp.full((1, LANES), NEG_INF, F32).at[0, :N_EXPERTS].set(b_router)
    bias = _rel_bias_table(rel_table_b)
    gains = (q_norm_a, k_norm_a, q_norm_b, k_norm_b)
    return (g_attn, g_mlp, w_qkv, gains, sinks_a.astype(F32), bias, wg, wa, wb, wo, wr, br,
            w_gate_up, b_gate_up, w_down, b_down)


def _forward(x_prompt, x_sample, c_prompt, c_sample, cache_a_k, cache_a_v, cache_b_k, cache_b_v, params,
             past_len, cfg_prompt, cfg_sample):
    (g_attn, g_mlp, w_ada, b_ada, w_in, q_norm_a, k_norm_a, q_norm_b, k_norm_b, sinks_a, rel_table_b,
     w_branch_a, w_branch_b, w_out, w_router, b_router, w_gate_up, b_gate_up, w_down, b_down) = params
    depth = g_attn.shape[0]
    bp, lp_, _ = x_prompt.shape
    bs, ls, _ = x_sample.shape
    pos_p = jnp.arange(lp_, dtype=I32)
    pos_s = past_len + jnp.arange(ls, dtype=I32)
    yp, ys = x_prompt, x_sample
    st_p, st_s = [], []
    for l in range(depth):
        weights = _prep_weights(g_attn[l], g_mlp[l], w_in[l], q_norm_a[l], k_norm_a[l], q_norm_b[l],
                                k_norm_b[l], sinks_a[l], rel_table_b[l], w_branch_a[l], w_branch_b[l],
                                w_out[l], w_router[l], b_router[l], w_gate_up[l], b_gate_up[l],
                                w_down[l], b_down[l])
        mod = _adaln(jnp.concatenate([c_prompt, c_sample], axis=0), w_ada[l], b_ada[l])
        mod = mod.reshape(bp + bs, 6, 1, D_MODEL)
        mods_p = [mod[:bp, k] for k in range(6)]
        mods_s = [mod[bp:, k] for k in range(6)]
        yp, sp = _layer(yp, mods_p, pos_p, None, weights, **cfg_prompt)
        wa_c = cache_a_k.shape[2]
        wb_c = cache_b_k.shape[2]
        assert wb_c == WINDOW_B and wa_c == WINDOW_A
        zeros_a = jnp.zeros((bs, PAD - wa_c, A_KV_W), BF16)
        prefix = (
            jnp.concatenate([zeros_a, cache_a_k[l].reshape(bs, wa_c, A_KV_W).astype(BF16)], axis=1),
            jnp.concatenate([zeros_a, cache_a_v[l].reshape(bs, wa_c, A_KV_W).astype(BF16)], axis=1),
            cache_b_k[l].reshape(bs, wb_c, B_W).astype(BF16),
            cache_b_v[l].reshape(bs, wb_c, B_W).astype(BF16),
        )
        ys, ss = _layer(ys, mods_s, pos_s, prefix, weights, **cfg_sample)
        st_p.append(sp)
        st_s.append(ss)

    def stack(states, k, heads):
        return jnp.stack([s[k].reshape(s[k].shape[0], s[k].shape[1], heads, HEAD_DIM) for s in states])

    return (yp, ys,
            stack(st_p, 0, A_KV_HEADS), stack(st_p, 1, A_KV_HEADS), stack(st_p, 2, B_HEADS), stack(st_p, 3, B_HEADS),
            stack(st_s, 0, A_KV_HEADS), stack(st_s, 1, A_KV_HEADS), stack(st_s, 2, B_HEADS), stack(st_s, 3, B_HEADS))


PAST_LEN = 2048
CFG_PROMPT = dict(bb=1, tl=512, n_c=4, tl_combine=512, bb_combine=1)
CFG_SAMPLE = dict(bb=8, tl=64, n_c=1, tl_combine=64, bb_combine=8)


def kernel(x_prompt, x_sample, c_prompt, c_sample, cache_a_k, cache_a_v, cache_b_k, cache_b_v, g_attn, g_mlp,
           w_ada, b_ada, w_in, q_norm_a, k_norm_a, q_norm_b, k_norm_b, sinks_a, rel_table_b, w_branch_a,
           w_branch_b, w_out, w_router, b_router, w_gate_up, b_gate_up, w_down, b_down):
    params = (g_attn, g_mlp, w_ada, b_ada, w_in, q_norm_a, k_norm_a, q_norm_b, k_norm_b, sinks_a, rel_table_b,
              w_branch_a, w_branch_b, w_out, w_router, b_router, w_gate_up, b_gate_up, w_down, b_down)
    return _forward(x_prompt, x_sample, c_prompt, c_sample, cache_a_k, cache_a_v, cache_b_k, cache_b_v,
                    params, PAST_LEN, CFG_PROMPT, CFG_SAMPLE)
```

```python
import functools

import jax
import jax.numpy as jnp
from jax import lax
from jax.experimental import pallas as pl
from jax.experimental.pallas import tpu as pltpu
from jax.experimental.pallas import tpu_sc as plsc

F32 = jnp.float32
BF16 = jnp.bfloat16
I32 = jnp.int32

D_MODEL = 1024
CHUNK = 64
HEAD_DIM = 64
LANES = 128
SUBLANES = 8
A_Q_HEADS = 8
A_KV_HEADS = 2
A_GROUP = A_Q_HEADS // A_KV_HEADS
B_HEADS = 8
A_Q_W = A_Q_HEADS * HEAD_DIM
A_KV_W = A_KV_HEADS * HEAD_DIM
B_W = B_HEADS * HEAD_DIM
QKV_W = A_Q_W + 2 * A_KV_W + 3 * B_W
WINDOW_A = 128
WINDOW_B = 512
BAND_A = WINDOW_A + CHUNK
BAND_B = WINDOW_B + CHUNK
PAD = WINDOW_B
ROT_DIM = HEAD_DIM // 4
ROPE_THETA = 500000.0
REL_BACK = 4 * CHUNK
REL_FWD = CHUNK - 1
N_EXPERTS = 32
TOP_K = 4
D_FF = D_MODEL
SWIGLU_LIMIT = 7.0
SWIGLU_ALPHA = 1.702
ROUTE_BLOCK = 256
RMS_EPS = 1e-6
NEG_INF = -1e30
ROW_TILES = D_MODEL // LANES
VMEM_LIMIT = 56 * 1024 * 1024
SC_CORES = 2
SC_SUBCORES = 16
SC_WORKERS = SC_CORES * SC_SUBCORES
SC_ROWS = 32
TRASH_ROWS = 8

A_HEAD_ORDER = (0, 4, 1, 5, 2, 6, 3, 7)


def _cparams(*sem):
    return pltpu.CompilerParams(dimension_semantics=sem, vmem_limit_bytes=VMEM_LIMIT)


def _adaln_kernel(c_ref, w_ref, b_ref, o_ref):
    c = c_ref[...]
    a = c * jax.nn.sigmoid(c)
    o_ref[...] = jnp.dot(a, w_ref[...], preferred_element_type=F32,
                         precision=lax.Precision.HIGHEST) + b_ref[...]


def _adaln(c, w_ada, b_ada):
    n_rows = c.shape[0]
    n_out = w_ada.shape[1]
    tn = 1024
    return pl.pallas_call(
        _adaln_kernel,
        grid=(n_out // tn,),
        in_specs=[pl.BlockSpec((n_rows, D_MODEL), lambda j: (0, 0)),
                  pl.BlockSpec((D_MODEL, tn), lambda j: (0, j)),
                  pl.BlockSpec((1, tn), lambda j: (0, j))],
        out_specs=pl.BlockSpec((n_rows, tn), lambda j: (0, j)),
        out_shape=jax.ShapeDtypeStruct((n_rows, n_out), F32),
        compiler_params=_cparams("arbitrary"),
        name="adaln",
    )(c, w_ada, b_ada.reshape(1, n_out))


def _modulated_norm(x, g, scale, shift):
    ms = jnp.mean(x * x, axis=-1, keepdims=True)
    return (x * lax.rsqrt(ms + RMS_EPS) * g) * (1.0 + scale) + shift


def _head_norm(parts, gain, seg):
    m = parts[0].shape[0]
    sq = jnp.concatenate([p * p for p in parts], axis=0).astype(BF16)
    ms = jnp.dot(sq, seg, preferred_element_type=F32)
    r = lax.rsqrt(ms + RMS_EPS)
    return [parts[j] * r[j * m:(j + 1) * m] * gain for j in range(len(parts))]


def _rope(p, cos, sin_lo, sin_hi):
    return p * cos + pltpu.roll(p, LANES - ROT_DIM // 2, 1) * sin_lo + pltpu.roll(p, ROT_DIM // 2, 1) * sin_hi


def _stage1_kernel(x_ref, sh_ref, sc_ref, g_ref, w_ref, gqa_ref, gka_ref, gqb_ref, gkb_ref, seg_ref,
                   cos_ref, sinlo_ref, sinhi_ref,
                   qa_ref, ka_ref, va_ref, qb_ref, kb_ref, vb_ref, ska_ref, sva_ref, skb_ref, svb_ref,
                   *, bb, tl, pad_tiles, n_tiles, sa_rows, sb_rows):
    i = pl.program_id(1)
    m = bb * tl

    if pad_tiles:
        @pl.when(i < pad_tiles)
        def _():
            ka_ref[...] = jnp.zeros_like(ka_ref)
            va_ref[...] = jnp.zeros_like(va_ref)
            kb_ref[...] = jnp.zeros_like(kb_ref)
            vb_ref[...] = jnp.zeros_like(vb_ref)

    @pl.when(i >= pad_tiles)
    def _():
        h = _modulated_norm(x_ref[...], g_ref[...], sc_ref[...], sh_ref[...])
        hb = h.reshape(m, D_MODEL).astype(BF16)
        qkv = jnp.dot(hb, w_ref[...], preferred_element_type=F32)
        seg = seg_ref[...]

        def cols(lo, width):
            return [qkv[:, lo + LANES * j: lo + LANES * (j + 1)] for j in range(width // LANES)]

        cos = jnp.concatenate([cos_ref[...]] * bb, axis=0)
        sin_lo = jnp.concatenate([sinlo_ref[...]] * bb, axis=0)
        sin_hi = jnp.concatenate([sinhi_ref[...]] * bb, axis=0)
        scale = HEAD_DIM ** -0.5

        qa = [_rope(p, cos, sin_lo, sin_hi) * scale for p in _head_norm(cols(0, A_Q_W), gqa_ref[...], seg)]
        ka = [_rope(p, cos, sin_lo, sin_hi) for p in _head_norm(cols(A_Q_W, A_KV_W), gka_ref[...], seg)]
        va = cols(A_Q_W + A_KV_W, A_KV_W)
        off_b = A_Q_W + 2 * A_KV_W
        qb = [p * scale for p in _head_norm(cols(off_b, B_W), gqb_ref[...], seg)]
        kb = _head_norm(cols(off_b + B_W, B_W), gkb_ref[...], seg)
        vb = cols(off_b + 2 * B_W, B_W)

        def put(ref, parts):
            for j, p in enumerate(parts):
                ref[:, :, LANES * j:LANES * (j + 1)] = p.reshape(bb, tl, LANES).astype(ref.dtype)

        put(qa_ref, qa)
        put(ka_ref, ka)
        put(va_ref, va)
        put(qb_ref, qb)
        put(kb_ref, kb)
        put(vb_ref, vb)

        @pl.when(i == n_tiles - 1)
        def _():
            def put_state(ref, parts, rows):
                for j, p in enumerate(parts):
                    ref[:, :, LANES * j:LANES * (j + 1)] = p.reshape(bb, tl, LANES)[:, tl - rows:, :]

            put_state(ska_ref, ka, sa_rows)
            put_state(sva_ref, va, sa_rows)
            put_state(skb_ref, kb, sb_rows)
            put_state(svb_ref, vb, sb_rows)


def _rope_tables(pos):
    inv_freq = ROPE_THETA ** (-jnp.arange(0, ROT_DIM, 2, dtype=F32) / ROT_DIM)
    ang = pos.astype(F32)[:, None] * inv_freq[None, :]
    half = ROT_DIM // 2
    lane = jnp.arange(LANES) % HEAD_DIM
    ang_l = ang[:, lane % half]
    rot = lane < ROT_DIM
    cos = jnp.where(rot[None, :], jnp.cos(ang_l), 1.0)
    sin_lo = jnp.where((lane < half)[None, :], -jnp.sin(ang_l), 0.0)
    sin_hi = jnp.where(((lane >= half) & rot)[None, :], jnp.sin(ang_l), 0.0)
    return cos, sin_lo, sin_hi


def _stage1(x, shift, scale, g_attn, w_qkv, gains, pos, *, bb, tl, pad):
    bsz, length, _ = x.shape
    n_tiles = length // tl
    pad_tiles = pad // tl
    sa_rows, sb_rows = min(WINDOW_A, length), min(WINDOW_B, length)
    assert bsz % bb == 0 and length % tl == 0 and pad % tl == 0 and sb_rows <= tl
    cos, sin_lo, sin_hi = _rope_tables(pos)
    seg = (jnp.arange(LANES)[:, None] // HEAD_DIM == jnp.arange(LANES)[None, :] // HEAD_DIM)
    seg = (seg.astype(F32) / HEAD_DIM).astype(BF16)
    gq_a, gk_a, gq_b, gk_b = [jnp.tile(g, LANES // HEAD_DIM).reshape(1, LANES) for g in gains]

    def xmap(b, i):
        return (b, jnp.maximum(i - pad_tiles, 0), 0)

    def const2(b, i):
        return (0, 0)

    def tabmap(b, i):
        return (jnp.maximum(i - pad_tiles, 0), 0)

    def modmap(b, i):
        return (b, 0, 0)

    def kvmap(b, i):
        return (b, i, 0)

    lp = length + pad
    kern = functools.partial(_stage1_kernel, bb=bb, tl=tl, pad_tiles=pad_tiles,
                             n_tiles=n_tiles + pad_tiles, sa_rows=sa_rows, sb_rows=sb_rows)
    out_shapes = (
        jax.ShapeDtypeStruct((bsz, length, A_Q_W), BF16),
        jax.ShapeDtypeStruct((bsz, lp, A_KV_W), BF16),
        jax.ShapeDtypeStruct((bsz, lp, A_KV_W), BF16),
        jax.ShapeDtypeStruct((bsz, length, B_W), BF16),
        jax.ShapeDtypeStruct((bsz, lp, B_W), BF16),
        jax.ShapeDtypeStruct((bsz, lp, B_W), BF16),
        jax.ShapeDtypeStruct((bsz, sa_rows, A_KV_W), F32),
        jax.ShapeDtypeStruct((bsz, sa_rows, A_KV_W), F32),
        jax.ShapeDtypeStruct((bsz, sb_rows, B_W), F32),
        jax.ShapeDtypeStruct((bsz, sb_rows, B_W), F32),
    )
    out_specs = (
        pl.BlockSpec((bb, tl, A_Q_W), xmap),
        pl.BlockSpec((bb, tl, A_KV_W), kvmap),
        pl.BlockSpec((bb, tl, A_KV_W), kvmap),
        pl.BlockSpec((bb, tl, B_W), xmap),
        pl.BlockSpec((bb, tl, B_W), kvmap),
        pl.BlockSpec((bb, tl, B_W), kvmap),
        pl.BlockSpec((bb, sa_rows, A_KV_W), modmap),
        pl.BlockSpec((bb, sa_rows, A_KV_W), modmap),
        pl.BlockSpec((bb, sb_rows, B_W), modmap),
        pl.BlockSpec((bb, sb_rows, B_W), modmap),
    )
    in_specs = [
        pl.BlockSpec((bb, tl, D_MODEL), xmap),
        pl.BlockSpec((bb, 1, D_MODEL), modmap),
        pl.BlockSpec((bb, 1, D_MODEL), modmap),
        pl.BlockSpec((1, D_MODEL), const2),
        pl.BlockSpec((D_MODEL, QKV_W), const2),
        pl.BlockSpec((1, LANES), const2),
        pl.BlockSpec((1, LANES), const2),
        pl.BlockSpec((1, LANES), const2),
        pl.BlockSpec((1, LANES), const2),
        pl.BlockSpec((LANES, LANES), const2),
        pl.BlockSpec((tl, LANES), tabmap),
        pl.BlockSpec((tl, LANES), tabmap),
        pl.BlockSpec((tl, LANES), tabmap),
    ]
    return pl.pallas_call(
        kern,
        grid=(bsz // bb, n_tiles + pad_tiles),
        in_specs=in_specs,
        out_specs=out_specs,
        out_shape=out_shapes,
        compiler_params=_cparams("arbitrary", "arbitrary"),
        name="stage1",
    )(x, shift, scale, g_attn.reshape(1, D_MODEL), w_qkv, gq_a, gk_a, gq_b, gk_b, seg, cos, sin_lo, sin_hi)


def _nt_dot(a, b):
    return lax.dot_general(a, b, (((1,), (1,)), ((), ())), preferred_element_type=F32)


def _attn_kernel(sink_ref, qa_ref, ka_ref, va_ref, qb_ref, kb_ref, vb_ref, bias_ref, oa_ref, ob_ref,
                 *, n_c, mask_prefix):
    qi = pl.program_id(1)
    lane = lax.broadcasted_iota(I32, (CHUNK, LANES), 1)
    low = lane < HEAD_DIM
    half_mask = (jnp.where(low, 1.0, 0.0).astype(BF16), jnp.where(low, 0.0, 1.0).astype(BF16))
    n_pairs = B_W // LANES
    rows_b = n_pairs * 2 * CHUNK
    rows_a = A_Q_HEADS * CHUNK
    key_b = lax.broadcasted_iota(I32, (rows_b, BAND_B), 1)
    key_a = lax.broadcasted_iota(I32, (rows_a, BAND_A), 1)
    row_a = lax.broadcasted_iota(I32, (rows_a, 1), 0) // CHUNK

    sink_col = jnp.zeros((rows_a, 1), F32)
    for r in range(A_Q_HEADS):
        sink_col = jnp.where(row_a == r, sink_ref[r], sink_col)

    for c in range(n_c):
        ci = qi * n_c + c
        row0 = pl.multiple_of(ci * CHUNK, CHUNK)
        rows = slice(c * CHUNK, (c + 1) * CHUNK)
        first_valid = PAD - ci * CHUNK

        s_parts = []
        for j in range(n_pairs):
            cg = slice(LANES * j, LANES * (j + 1))
            q = qb_ref[0, rows, cg]
            qs = jnp.concatenate([q * half_mask[0], q * half_mask[1]], axis=0)
            k = kb_ref[0, pl.ds(row0, BAND_B), cg]
            s_parts.append(_nt_dot(qs, k) + bias_ref[j])
        s = jnp.concatenate(s_parts, axis=0)
        if mask_prefix:
            s = jnp.where(key_b >= first_valid, s, NEG_INF)
        mx = jnp.max(s, axis=-1, keepdims=True)
        e = jnp.exp(s - mx)
        inv = 1.0 / jnp.sum(e, axis=-1, keepdims=True)
        p_b = e.astype(BF16)
        for j in range(n_pairs):
            cg = slice(LANES * j, LANES * (j + 1))
            v = vb_ref[0, pl.ds(row0, BAND_B), cg]
            rj = slice(2 * CHUNK * j, 2 * CHUNK * (j + 1))
            o = jnp.dot(p_b[rj], v, preferred_element_type=F32) * inv[rj]
            ob_ref[0, rows, cg] = jnp.where(low, o[:CHUNK], o[CHUNK:]).astype(ob_ref.dtype)

        row_a0 = pl.multiple_of(row0 + (WINDOW_B - WINDOW_A), CHUNK)
        k = ka_ref[0, pl.ds(row_a0, BAND_A), :]
        v = va_ref[0, pl.ds(row_a0, BAND_A), :]
        q = jnp.concatenate(
            [qa_ref[0, rows, LANES * j:LANES * (j + 1)] * half_mask[p]
             for p in range(A_KV_HEADS) for j in range(A_GROUP)], axis=0)
        s = _nt_dot(q, k)
        if mask_prefix:
            s = jnp.where(key_a >= first_valid - (WINDOW_B - WINDOW_A), s, NEG_INF)
        mx = jnp.maximum(jnp.max(s, axis=-1, keepdims=True), sink_col)
        e = jnp.exp(s - mx)
        inv = 1.0 / (jnp.sum(e, axis=-1, keepdims=True) + jnp.exp(sink_col - mx))
        o = jnp.dot(e.astype(BF16), v, preferred_element_type=F32) * inv
        for j in range(A_GROUP):
            lo = o[j * CHUNK:(j + 1) * CHUNK]
            hi = o[(A_GROUP + j) * CHUNK:(A_GROUP + j + 1) * CHUNK]
            oa_ref[0, rows, LANES * j:LANES * (j + 1)] = jnp.where(low, lo, hi).astype(oa_ref.dtype)


def _attention(qa, ka, va, qb, kb, vb, sinks, bias, *, n_c, mask_prefix):
    bsz, length, _ = qa.shape
    lp = ka.shape[1]
    qt = n_c * CHUNK
    assert length % qt == 0 and lp == length + PAD

    def qmap(b, i):
        return (b, i, 0)

    def kmap(b, i):
        return (b, 0, 0)

    kern = functools.partial(_attn_kernel, n_c=n_c, mask_prefix=mask_prefix)
    return pl.pallas_call(
        kern,
        grid=(bsz, length // qt),
        in_specs=[
            pl.BlockSpec(memory_space=pltpu.SMEM),
            pl.BlockSpec((1, qt, A_Q_W), qmap),
            pl.BlockSpec((1, lp, A_KV_W), kmap),
            pl.BlockSpec((1, lp, A_KV_W), kmap),
            pl.BlockSpec((1, qt, B_W), qmap),
            pl.BlockSpec((1, lp, B_W), kmap),
            pl.BlockSpec((1, lp, B_W), kmap),
            pl.BlockSpec((B_HEADS // 2, 2 * CHUNK, BAND_B), lambda b, i: (0, 0, 0)),
        ],
        out_specs=(pl.BlockSpec((1, qt, A_Q_W), qmap), pl.BlockSpec((1, qt, B_W), qmap)),
        out_shape=(jax.ShapeDtypeStruct((bsz, length, A_Q_W), BF16),
                   jax.ShapeDtypeStruct((bsz, length, B_W), BF16)),
        compiler_params=_cparams("arbitrary", "arbitrary"),
        name="attention",
    )(sinks, qa, ka, va, qb, kb, vb, bias)


def _merge_kernel(x_ref, oa_ref, ob_ref, sha_ref, sca_ref, gta_ref, shm_ref, scm_ref, ga_ref, gm_ref,
                  wg_ref, wa_ref, wb_ref, wo_ref, wr_ref, br_ref,
                  x1_ref, h2_ref, ri_ref, rw_ref, cnt_ref, carry_ref, *, bb, tl):
    m = bb * tl
    first = jnp.logical_and(pl.program_id(0) == 0, pl.program_id(1) == 0)

    @pl.when(first)
    def _():
        carry_ref[...] = jnp.zeros_like(carry_ref)

    x = x_ref[...]
    h = _modulated_norm(x, ga_ref[...], sca_ref[...], sha_ref[...])
    hb = h.reshape(m, D_MODEL).astype(BF16)
    gates = jnp.dot(hb, wg_ref[...], preferred_element_type=F32)
    ya = jnp.dot(oa_ref[...].reshape(m, A_Q_W), wa_ref[...], preferred_element_type=F32)
    yb = jnp.dot(ob_ref[...].reshape(m, B_W), wb_ref[...], preferred_element_type=F32)
    mix = jax.nn.sigmoid(gates[:, :D_MODEL]) * ya + jax.nn.sigmoid(gates[:, D_MODEL:]) * yb
    z = jnp.dot(mix.astype(BF16), wo_ref[...], preferred_element_type=F32)
    x1 = x + gta_ref[...] * z.reshape(bb, tl, D_MODEL)
    x1_ref[...] = x1

    h2 = _modulated_norm(x1, gm_ref[...], scm_ref[...], shm_ref[...]).reshape(m, D_MODEL)
    for j in range(ROW_TILES):
        h2_ref[pl.ds(j, m, stride=ROW_TILES), :] = h2[:, LANES * j:LANES * (j + 1)]

    logits = jnp.dot(h2, wr_ref[...], preferred_element_type=F32,
                     precision=lax.Precision.HIGHEST) + br_ref[...]
    lane = lax.broadcasted_iota(I32, (m, LANES), 1)
    lane_f = lane.astype(F32)
    work = logits
    top_v, top_i = [], []
    sel_f = jnp.zeros((m, LANES), F32)
    for _ in range(TOP_K):
        mx = jnp.max(work, axis=-1, keepdims=True)
        idx = jnp.min(jnp.where(work == mx, lane_f, float(LANES)), axis=-1, keepdims=True)
        hit = lane_f == idx
        top_v.append(mx)
        top_i.append(idx)
        sel_f = jnp.where(hit, 1.0, sel_f)
        work = jnp.where(hit, -jnp.inf, work)
    ex = [jnp.exp(v - top_v[0]) for v in top_v]
    inv = 1.0 / (ex[0] + ex[1] + ex[2] + ex[3])

    r_i = lax.broadcasted_iota(I32, (m, m), 0)
    c_i = lax.broadcasted_iota(I32, (m, m), 1)
    before = jnp.where(r_i > c_i, 1.0, 0.0).astype(BF16)
    pos = jnp.dot(before, sel_f.astype(BF16), preferred_element_type=F32) + carry_ref[...]
    carry = carry_ref[...] + jnp.sum(sel_f, axis=0, keepdims=True)
    carry_ref[...] = carry
    cnt_ref[...] = carry

    ri_f = jnp.zeros((m, LANES), F32)
    rw = jnp.zeros((m, LANES), F32)
    for k in range(TOP_K):
        rank = jnp.sum(jnp.where(lane_f == top_i[k], pos, 0.0), axis=-1, keepdims=True)
        ri_f = jnp.where(lane == k, top_i[k], ri_f)
        ri_f = jnp.where(lane == TOP_K + k, rank, ri_f)
        rw = jnp.where(lane == k, ex[k] * inv, rw)
    ri = ri_f.astype(I32)
    ri_ref[...] = ri.reshape(bb, tl, LANES)
    rw_ref[...] = rw.reshape(bb, tl, LANES)


def _merge(x, oa, ob, mods, g_attn, g_mlp, wg, wa, wb, wo, wr, br, *, bb, tl):
    bsz, length, _ = x.shape
    n_tok = bsz * length
    m = bb * tl
    assert bb == 1 or tl == length
    sh_a, sc_a, gt_a, sh_m, sc_m = mods

    def xmap(b, i):
        return (b, i, 0)

    def modmap(b, i):
        return (b, 0, 0)

    def const2(b, i):
        return (0, 0)

    n_l = length // tl

    def flatmap(b, i):
        return (b * n_l + i, 0)

    modspec = pl.BlockSpec((bb, 1, D_MODEL), modmap)
    kern = functools.partial(_merge_kernel, bb=bb, tl=tl)
    return pl.pallas_call(
        kern,
        grid=(bsz // bb, n_l),
        in_specs=[
            pl.BlockSpec((bb, tl, D_MODEL), xmap),
            pl.BlockSpec((bb, tl, A_Q_W), xmap),
            pl.BlockSpec((bb, tl, B_W), xmap),
            modspec, modspec, modspec, modspec, modspec,
            pl.BlockSpec((1, D_MODEL), const2),
            pl.BlockSpec((1, D_MODEL), const2),
            pl.BlockSpec((D_MODEL, 2 * D_MODEL), const2),
            pl.BlockSpec((A_Q_W, D_MODEL), const2),
            pl.BlockSpec((B_W, D_MODEL), const2),
            pl.BlockSpec((D_MODEL, D_MODEL), const2),
            pl.BlockSpec((D_MODEL, LANES), const2),
            pl.BlockSpec((1, LANES), const2),
        ],
        out_specs=(
            pl.BlockSpec((bb, tl, D_MODEL), xmap),
            pl.BlockSpec((m * ROW_TILES, LANES), flatmap),
            pl.BlockSpec((bb, tl, LANES), xmap),
            pl.BlockSpec((bb, tl, LANES), xmap),
            pl.BlockSpec((1, LANES), const2),
        ),
        out_shape=(
            jax.ShapeDtypeStruct((bsz, length, D_MODEL), F32),
            jax.ShapeDtypeStruct((n_tok * ROW_TILES, LANES), F32),
            jax.ShapeDtypeStruct((bsz, length, LANES), I32),
            jax.ShapeDtypeStruct((bsz, length, LANES), F32),
            jax.ShapeDtypeStruct((1, LANES), F32),
        ),
        scratch_shapes=[pltpu.VMEM((1, LANES), F32)],
        compiler_params=_cparams("arbitrary", "arbitrary"),
        name="merge",
    )(x, oa, ob, sh_a, sc_a, gt_a, sh_m, sc_m, g_attn.reshape(1, D_MODEL), g_mlp.reshape(1, D_MODEL),
      wg, wa, wb, wo, wr, br)


def _sc_mesh():
    return plsc.VectorSubcoreMesh(core_axis_name="c", subcore_axis_name="s",
                                  num_cores=SC_CORES, num_subcores=SC_SUBCORES)


def _sc_worker():
    return lax.axis_index("c") * SC_SUBCORES + lax.axis_index("s")


def _sc_params():
    return pltpu.CompilerParams(use_tc_tiling_on_sc=True)


def _dispatch(h2_rows, idx_flat, pad_flat, n_rows):
    n_tok = h2_rows.shape[0]
    per_w = n_tok // SC_WORKERS
    n_chunks = per_w // SC_ROWS
    n_idx = n_chunks * TOP_K * SC_ROWS
    n_pad = pad_flat.shape[0] // SC_WORKERS
    pad_chunks = n_pad // SC_ROWS
    assert per_w % (2 * SC_ROWS) == 0 and n_pad % SC_ROWS == 0
    zeros = jnp.zeros((SC_ROWS, ROW_TILES, LANES), F32)

    def body(h2_hbm, idx_hbm, pad_hbm, zeros_hbm, xs_hbm, idx_v, pad_v, buf0, buf1, ls0, ls1, ss0, ss1):
        wid = _sc_worker()
        bufs, lsems, ssems = (buf0, buf1), (ls0, ls1), (ss0, ss1)
        pltpu.sync_copy(idx_hbm.at[pl.ds(pl.multiple_of(wid * n_idx, SC_ROWS), n_idx)], idx_v)
        pltpu.sync_copy(pad_hbm.at[pl.ds(pl.multiple_of(wid * n_pad, SC_ROWS), n_pad)], pad_v)

        pltpu.sync_copy(zeros_hbm, buf0)
        for c in range(pad_chunks):
            pltpu.async_copy(buf0, xs_hbm.at[pad_v.at[pl.ds(c * SC_ROWS, SC_ROWS)]], ss0)
        for c in range(pad_chunks):
            pltpu.make_async_copy(buf0, xs_hbm.at[pad_v.at[pl.ds(c * SC_ROWS, SC_ROWS)]], ss0).wait()

        base = wid * per_w

        def load(g, b):
            src = h2_hbm.at[pl.ds(pl.multiple_of(base + g * SC_ROWS, SC_ROWS), SC_ROWS)]
            return pltpu.make_async_copy(src, bufs[b], lsems[b])

        def scatter(g, k, b):
            off = pl.multiple_of((g * TOP_K + k) * SC_ROWS, SC_ROWS)
            return pltpu.make_async_copy(bufs[b], xs_hbm.at[idx_v.at[pl.ds(off, SC_ROWS)]], ssems[b])

        load(0, 0).start()

        @pl.loop(0, n_chunks, step=2)
        def _(g0):
            for b in range(2):
                g = g0 + b
                load(g, b).wait()

                @pl.when(g >= 1)
                def _():
                    for k in range(TOP_K):
                        scatter(g - 1, k, 1 - b).wait()

                @pl.when(g + 1 < n_chunks)
                def _():
                    load(g + 1, 1 - b).start()

                for k in range(TOP_K):
                    scatter(g, k, b).start()

        for k in range(TOP_K):
            scatter(n_chunks - 1, k, (n_chunks - 1) % 2).wait()

    call = pl.kernel(
        body,
        out_type=jax.ShapeDtypeStruct((n_rows + TRASH_ROWS, ROW_TILES, LANES), F32),
        mesh=_sc_mesh(),
        scratch_types=[pltpu.VMEM((n_idx,), I32), pltpu.VMEM((n_pad,), I32),
                       pltpu.VMEM((SC_ROWS, ROW_TILES, LANES), F32), pltpu.VMEM((SC_ROWS, ROW_TILES, LANES), F32),
                       pltpu.SemaphoreType.DMA, pltpu.SemaphoreType.DMA,
                       pltpu.SemaphoreType.DMA, pltpu.SemaphoreType.DMA],
        compiler_params=_sc_params(),
        name="dispatch",
    )
    return call(h2_rows, idx_flat, pad_flat, zeros)


def _gather_back(y_rows, idx_flat, n_tok):
    per_w = n_tok // SC_WORKERS
    n_chunks = per_w // SC_ROWS
    n_items = n_chunks * TOP_K
    n_idx = n_items * SC_ROWS
    assert per_w % SC_ROWS == 0 and n_items % 2 == 0

    def body(y_hbm, idx_hbm, yg_hbm, idx_v, buf0, buf1, gs0, gs1, ws0, ws1):
        wid = _sc_worker()
        bufs, gsems, wsems = (buf0, buf1), (gs0, gs1), (ws0, ws1)
        pltpu.sync_copy(idx_hbm.at[pl.ds(pl.multiple_of(wid * n_idx, SC_ROWS), n_idx)], idx_v)
        base = wid * per_w

        def gather(i, b):
            off = pl.multiple_of(i * SC_ROWS, SC_ROWS)
            return pltpu.make_async_copy(y_hbm.at[idx_v.at[pl.ds(off, SC_ROWS)]], bufs[b], gsems[b])

        def write(i, b):
            g, k = i // TOP_K, i % TOP_K
            dst = yg_hbm.at[k, pl.ds(pl.multiple_of(base + g * SC_ROWS, SC_ROWS), SC_ROWS)]
            return pltpu.make_async_copy(bufs[b], dst, wsems[b])

        gather(0, 0).start()

        @pl.loop(0, n_items, step=2)
        def _(i0):
            for b in range(2):
                i = i0 + b
                gather(i, b).wait()

                @pl.when(i >= 1)
                def _():
                    write(i - 1, 1 - b).wait()

                @pl.when(i + 1 < n_items)
                def _():
                    gather(i + 1, 1 - b).start()

                write(i, b).start()

        write(n_items - 1, (n_items - 1) % 2).wait()

    call = pl.kernel(
        body,
        out_type=jax.ShapeDtypeStruct((TOP_K, n_tok, ROW_TILES, LANES), F32),
        mesh=_sc_mesh(),
        scratch_types=[pltpu.VMEM((n_idx,), I32),
                       pltpu.VMEM((SC_ROWS, ROW_TILES, LANES), F32), pltpu.VMEM((SC_ROWS, ROW_TILES, LANES), F32),
                       pltpu.SemaphoreType.DMA, pltpu.SemaphoreType.DMA,
                       pltpu.SemaphoreType.DMA, pltpu.SemaphoreType.DMA],
        compiler_params=_sc_params(),
        name="gather_back",
    )
    return call(y_rows, idx_flat)


def _expert_kernel(blk_ref, used_ref, xs_ref, wgu_ref, bgu_ref, wd_ref, bd_ref, y_ref, wgu_bf, wd_bf):
    r = pl.program_id(0)
    e = blk_ref[r]
    prev = blk_ref[jnp.maximum(r - 1, 0)]
    active = r < used_ref[0]

    @pl.when(jnp.logical_and(active, jnp.logical_or(r == 0, e != prev)))
    def _():
        wgu_bf[...] = wgu_ref[0].astype(BF16)
        wd_bf[...] = wd_ref[0].astype(BF16)

    @pl.when(active)
    def _():
        x = jnp.concatenate(
            [xs_ref[pl.ds(j, ROUTE_BLOCK, stride=ROW_TILES), :] for j in range(ROW_TILES)], axis=1).astype(BF16)
        gu = jnp.dot(x, wgu_bf[...], preferred_element_type=F32) + bgu_ref[0]
        gate = jnp.minimum(gu[:, :D_FF], SWIGLU_LIMIT)
        up = jnp.clip(gu[:, D_FF:], -SWIGLU_LIMIT, SWIGLU_LIMIT)
        glu = gate * jax.nn.sigmoid(gate * SWIGLU_ALPHA)
        act = ((up + 1.0) * glu).astype(BF16)
        y = jnp.dot(act, wd_bf[...], preferred_element_type=F32) + bd_ref[0]
        for j in range(ROW_TILES):
            y_ref[pl.ds(j, ROUTE_BLOCK, stride=ROW_TILES), :] = y[:, LANES * j:LANES * (j + 1)]


def _experts(xs_rows, blk_expert, n_used, w_gate_up, b_gate_up, w_down, b_down):
    n_blocks = blk_expert.shape[0]
    rb = ROUTE_BLOCK * ROW_TILES

    def rowmap(r, blk, used):
        return (jnp.minimum(r, used[0] - 1), 0)

    def wmap(r, blk, used):
        return (blk[jnp.minimum(r, used[0] - 1)], 0, 0)

    return pl.pallas_call(
        _expert_kernel,
        grid_spec=pltpu.PrefetchScalarGridSpec(
            num_scalar_prefetch=2,
            grid=(n_blocks,),
            in_specs=[
                pl.BlockSpec((rb, LANES), rowmap),
                pl.BlockSpec((1, D_MODEL, 2 * D_FF), wmap),
                pl.BlockSpec((1, 1, 2 * D_FF), wmap),
                pl.BlockSpec((1, D_FF, D_MODEL), wmap),
                pl.BlockSpec((1, 1, D_MODEL), wmap),
            ],
            out_specs=pl.BlockSpec((rb, LANES), rowmap),
            scratch_shapes=[pltpu.VMEM((D_MODEL, 2 * D_FF), BF16), pltpu.VMEM((D_FF, D_MODEL), BF16)],
        ),
        out_shape=jax.ShapeDtypeStruct(xs_rows.shape, F32),
        compiler_params=_cparams("arbitrary"),
        name="experts",
    )(blk_expert, n_used, xs_rows, w_gate_up, b_gate_up.reshape(N_EXPERTS, 1, 2 * D_FF),
      w_down, b_down.reshape(N_EXPERTS, 1, D_MODEL))


def _combine_kernel(x1_ref, gt_ref, rw_ref, yg_ref, o_ref, *, bb, tl):
    m = bb * tl
    rw = rw_ref[...].reshape(m, LANES)
    w = [rw[:, k:k + 1] for k in range(TOP_K)]
    gt = gt_ref[...]
    for j in range(ROW_TILES):
        acc = jnp.zeros((m, LANES), F32)
        for k in range(TOP_K):
            acc = acc + w[k] * yg_ref[k, pl.ds(j, m, stride=ROW_TILES), :]
        cg = slice(LANES * j, LANES * (j + 1))
        o_ref[:, :, cg] = x1_ref[:, :, cg] + gt[:, :, cg] * acc.reshape(bb, tl, LANES)


def _combine(x1, gt_m, rw, yg, *, bb, tl):
    bsz, length, _ = x1.shape
    m = bb * tl
    assert bb == 1 or tl == length
    n_l = length // tl

    def xmap(b, i):
        return (b, i, 0)

    kern = functools.partial(_combine_kernel, bb=bb, tl=tl)
    return pl.pallas_call(
        kern,
        grid=(bsz // bb, n_l),
        in_specs=[
            pl.BlockSpec((bb, tl, D_MODEL), xmap),
            pl.BlockSpec((bb, 1, D_MODEL), lambda b, i: (b, 0, 0)),
            pl.BlockSpec((bb, tl, LANES), xmap),
            pl.BlockSpec((TOP_K, m * ROW_TILES, LANES), lambda b, i: (0, b * n_l + i, 0)),
        ],
        out_specs=pl.BlockSpec((bb, tl, D_MODEL), xmap),
        out_shape=jax.ShapeDtypeStruct(x1.shape, F32),
        compiler_params=_cparams("arbitrary", "arbitrary"),
        name="combine",
    )(x1, gt_m, rw, yg)


def _route_plan(ri, counts_f, n_tok):
    idx = ri[..., :TOP_K].reshape(n_tok, TOP_K)
    rank = ri[..., TOP_K:2 * TOP_K].reshape(n_tok, TOP_K)
    counts = counts_f[0, :N_EXPERTS].astype(I32)
    padded = (counts + ROUTE_BLOCK - 1) // ROUTE_BLOCK * ROUTE_BLOCK
    pends = jnp.cumsum(padded)
    pstarts = pends - padded
    dest = pstarts[idx] + rank
    n_blocks = n_tok * TOP_K // ROUTE_BLOCK + N_EXPERTS
    blk_start = jnp.arange(n_blocks, dtype=I32) * ROUTE_BLOCK
    blk_expert = jnp.minimum(jnp.sum((pends[None, :] <= blk_start[:, None]).astype(I32), axis=1), N_EXPERTS - 1)
    n_used = (pends[-1] // ROUTE_BLOCK).reshape(1)
    n_rows = n_blocks * ROUTE_BLOCK
    per_w = n_tok // SC_WORKERS
    idx_flat = dest.astype(I32).reshape(SC_WORKERS, per_w // SC_ROWS, SC_ROWS, TOP_K)
    idx_flat = idx_flat.transpose(0, 1, 3, 2).reshape(-1)
    r = jnp.arange(ROUTE_BLOCK, dtype=I32)[None, :]
    pad_flat = jnp.where(r < (padded - counts)[:, None], (pstarts + counts)[:, None] + r, n_rows)
    return idx_flat, pad_flat.astype(I32).reshape(-1), blk_expert.astype(I32), n_used.astype(I32), n_rows


def _layer(x, mods, pos, prefix, weights, *, bb, tl, n_c, tl_combine, bb_combine):
    (g_attn, g_mlp, w_qkv, gains, sinks, bias, wg, wa, wb, wo, wr, br,
     w_gate_up, b_gate_up, w_down, b_down) = weights
    bsz, length, _ = x.shape
    n_tok = bsz * length
    sh_a, sc_a, gt_a, sh_m, sc_m, gt_m = mods
    pad = 0 if prefix is not None else PAD
    qa, ka, va, qb, kb, vb, ska, sva, skb, svb = _stage1(
        x, sh_a, sc_a, g_attn, w_qkv, gains, pos, bb=bb, tl=tl, pad=pad)
    if prefix is not None:
        ka, va, kb, vb = [jnp.concatenate([p, t], axis=1) for p, t in zip(prefix, (ka, va, kb, vb))]
    oa, ob = _attention(qa, ka, va, qb, kb, vb, sinks, bias, n_c=n_c, mask_prefix=prefix is None)
    x1, h2_rows, ri, rw, counts_f = _merge(x, oa, ob, (sh_a, sc_a, gt_a, sh_m, sc_m), g_attn, g_mlp,
                                           wg, wa, wb, wo, wr, br, bb=bb, tl=tl)
    idx_flat, pad_flat, blk_expert, n_used, n_rows = _route_plan(ri, counts_f, n_tok)
    xs = _dispatch(h2_rows.reshape(n_tok, ROW_TILES, LANES), idx_flat, pad_flat, n_rows)
    y_rows = _experts(xs.reshape(-1, LANES), blk_expert, n_used, w_gate_up, b_gate_up, w_down, b_down)
    yg = _gather_back(y_rows.reshape(-1, ROW_TILES, LANES), idx_flat, n_tok)
    y = _combine(x1, gt_m, rw, yg.reshape(TOP_K, n_tok * ROW_TILES, LANES), bb=bb_combine, tl=tl_combine)
    return y, (ska, sva, skb, svb)


def _rel_bias_table(rel_table):
    q_pos = jnp.arange(CHUNK)
    k_pos = jnp.arange(BAND_B) - WINDOW_B
    idx = jnp.clip(q_pos[:, None] - k_pos[None, :], -REL_FWD, REL_BACK) + REL_FWD
    return rel_table[:, idx].astype(F32).reshape(B_HEADS // 2, 2 * CHUNK, BAND_B)


def _prep_weights(g_attn, g_mlp, w_in, q_norm_a, k_norm_a, q_norm_b, k_norm_b, sinks_a, rel_table_b,
                  w_branch_a, w_branch_b, w_out, w_router, b_router, w_gate_up, b_gate_up, w_down, b_down):
    order = jnp.asarray(A_HEAD_ORDER)
    w_qa = w_in[:, :A_Q_W].reshape(D_MODEL, A_Q_HEADS, HEAD_DIM)[:, order].reshape(D_MODEL, A_Q_W)
    w_qkv = jnp.concatenate([w_qa, w_in[:, A_Q_W:QKV_W]], axis=1).astype(BF16)
    wg = w_in[:, QKV_W:].astype(BF16)
    wa = w_branch_a.reshape(A_Q_HEADS, HEAD_DIM, D_MODEL)[order].reshape(A_Q_W, D_MODEL).astype(BF16)
    wb = w_branch_b.astype(BF16)
    wo = w_out.astype(BF16)
    wr = jnp.zeros((D_MODEL, LANES), F32).at[:, :N_EXPERTS].set(w_router)
    br = jnp.full((1, LANES), NEG_INF, F32).at[0, :N_EXPERTS].set(b_router)
    bias = _rel_bias_table(rel_table_b)
    gains = (q_norm_a, k_norm_a, q_norm_b, k_norm_b)
    return (g_attn, g_mlp, w_qkv, gains, sinks_a.astype(F32), bias, wg, wa, wb, wo, wr, br,
            w_gate_up, b_gate_up, w_down, b_down)


def _forward(x_prompt, x_sample, c_prompt, c_sample, cache_a_k, cache_a_v, cache_b_k, cache_b_v, params,
             past_len, cfg_prompt, cfg_sample):
    (g_attn, g_mlp, w_ada, b_ada, w_in, q_norm_a, k_norm_a, q_norm_b, k_norm_b, sinks_a, rel_table_b,
     w_branch_a, w_branch_b, w_out, w_router, b_router, w_gate_up, b_gate_up, w_down, b_down) = params
    depth = g_attn.shape[0]
    bp, lp_, _ = x_prompt.shape
    bs, ls, _ = x_sample.shape
    pos_p = jnp.arange(lp_, dtype=I32)
    pos_s = past_len + jnp.arange(ls, dtype=I32)
    yp, ys = x_prompt, x_sample
    st_p, st_s = [], []
    for l in range(depth):
        weights = _prep_weights(g_attn[l], g_mlp[l], w_in[l], q_norm_a[l], k_norm_a[l], q_norm_b[l],
                                k_norm_b[l], sinks_a[l], rel_table_b[l], w_branch_a[l], w_branch_b[l],
                                w_out[l], w_router[l], b_router[l], w_gate_up[l], b_gate_up[l],
                                w_down[l], b_down[l])
        mod = _adaln(jnp.concatenate([c_prompt, c_sample], axis=0), w_ada[l], b_ada[l])
        mod = mod.reshape(bp + bs, 6, 1, D_MODEL)
        mods_p = [mod[:bp, k] for k in range(6)]
        mods_s = [mod[bp:, k] for k in range(6)]
        yp, sp = _layer(yp, mods_p, pos_p, None, weights, **cfg_prompt)
        wa_c = cache_a_k.shape[2]
        wb_c = cache_b_k.shape[2]
        assert wb_c == WINDOW_B and wa_c == WINDOW_A
        zeros_a = jnp.zeros((bs, PAD - wa_c, A_KV_W), BF16)
        prefix = (
            jnp.concatenate([zeros_a, cache_a_k[l].reshape(bs, wa_c, A_KV_W).astype(BF16)], axis=1),
            jnp.concatenate([zeros_a, cache_a_v[l].reshape(bs, wa_c, A_KV_W).astype(BF16)], axis=1),
            cache_b_k[l].reshape(bs, wb_c, B_W).astype(BF16),
            cache_b_v[l].reshape(bs, wb_c, B_W).astype(BF16),
        )
        ys, ss = _layer(ys, mods_s, pos_s, prefix, weights, **cfg_sample)
        st_p.append(sp)
        st_s.append(ss)

    def stack(states, k, heads):
        return jnp.stack([s[k].reshape(s[k].shape[0], s[k].shape[1], heads, HEAD_DIM) for s in states])

    return (yp, ys,
            stack(st_p, 0, A_KV_HEADS), stack(st_p, 1, A_KV_HEADS), stack(st_p, 2, B_HEADS), stack(st_p, 3, B_HEADS),
            stack(st_s, 0, A_KV_HEADS), stack(st_s, 1, A_KV_HEADS), stack(st_s, 2, B_HEADS), stack(st_s, 3, B_HEADS))


PAST_LEN = 2048
CFG_PROMPT = dict(bb=1, tl=512, n_c=4, tl_combine=512, bb_combine=1)
CFG_SAMPLE = dict(bb=8, tl=64, n_c=1, tl_combine=64, bb_combine=8)


def kernel(x_prompt, x_sample, c_prompt, c_sample, cache_a_k, cache_a_v, cache_b_k, cache_b_v, g_attn, g_mlp,
           w_ada, b_ada, w_in, q_norm_a, k_norm_a, q_norm_b, k_norm_b, sinks_a, rel_table_b, w_branch_a,
           w_branch_b, w_out, w_router, b_router, w_gate_up, b_gate_up, w_down, b_down):
    params = (g_attn, g_mlp, w_ada, b_ada, w_in, q_norm_a, k_norm_a, q_norm_b, k_norm_b, sinks_a, rel_table_b,
              w_branch_a, w_branch_b, w_out, w_router, b_router, w_gate_up, b_gate_up, w_down, b_down)
    return _forward(x_prompt, x_sample, c_prompt, c_sample, cache_a_k, cache_a_v, cache_b_k, cache_b_v,
                    params, PAST_LEN, CFG_PROMPT, CFG_SAMPLE)
```

```python
import functools

import jax
import jax.numpy as jnp
from jax import lax
from jax.experimental import pallas as pl
from jax.experimental.pallas import tpu as pltpu
from jax.experimental.pallas import tpu_sc as plsc

F32 = jnp.float32
BF16 = jnp.bfloat16
I32 = jnp.int32

D_MODEL = 1024
CHUNK = 64
HEAD_DIM = 64
LANES = 128
SUBLANES = 8
A_Q_HEADS = 8
A_KV_HEADS = 2
A_GROUP = A_Q_HEADS // A_KV_HEADS
B_HEADS = 8
A_Q_W = A_Q_HEADS * HEAD_DIM
A_KV_W = A_KV_HEADS * HEAD_DIM
B_W = B_HEADS * HEAD_DIM
QKV_W = A_Q_W + 2 * A_KV_W + 3 * B_W
WINDOW_A = 128
WINDOW_B = 512
BAND_A = WINDOW_A + CHUNK
BAND_B = WINDOW_B + CHUNK
PAD = WINDOW_B
ROT_DIM = HEAD_DIM // 4
ROPE_THETA = 500000.0
REL_BACK = 4 * CHUNK
REL_FWD = CHUNK - 1
N_EXPERTS = 32
TOP_K = 4
D_FF = D_MODEL
SWIGLU_LIMIT = 7.0
SWIGLU_ALPHA = 1.702
ROUTE_BLOCK = 512
FF_TILE = 256
RMS_EPS = 1e-6
NEG_INF = -1e30
ROW_TILES = D_MODEL // LANES
VMEM_LIMIT = 56 * 1024 * 1024
SC_CORES = 2
SC_SUBCORES = 16
SC_WORKERS = SC_CORES * SC_SUBCORES
SC_ROWS = 32
TRASH_ROWS = 8

A_HEAD_ORDER = (0, 4, 1, 5, 2, 6, 3, 7)


def _cparams(*sem):
    return pltpu.CompilerParams(dimension_semantics=sem, vmem_limit_bytes=VMEM_LIMIT)


def _adaln_kernel(c_ref, w_ref, b_ref, o_ref):
    c = c_ref[...]
    a = c * jax.nn.sigmoid(c)
    o_ref[...] = jnp.dot(a, w_ref[...], preferred_element_type=F32,
                         precision=lax.Precision.HIGHEST) + b_ref[...]


def _adaln(c, w_ada, b_ada):
    n_rows = c.shape[0]
    n_out = w_ada.shape[1]
    tn = 1024
    return pl.pallas_call(
        _adaln_kernel,
        grid=(n_out // tn,),
        in_specs=[pl.BlockSpec((n_rows, D_MODEL), lambda j: (0, 0)),
                  pl.BlockSpec((D_MODEL, tn), lambda j: (0, j)),
                  pl.BlockSpec((1, tn), lambda j: (0, j))],
        out_specs=pl.BlockSpec((n_rows, tn), lambda j: (0, j)),
        out_shape=jax.ShapeDtypeStruct((n_rows, n_out), F32),
        compiler_params=_cparams("arbitrary"),
        name="adaln",
    )(c, w_ada, b_ada.reshape(1, n_out))


def _modulated_norm(x, g, scale, shift):
    ms = jnp.mean(x * x, axis=-1, keepdims=True)
    return (x * lax.rsqrt(ms + RMS_EPS) * g) * (1.0 + scale) + shift


def _head_norm(parts, gain, seg):
    m = parts[0].shape[0]
    sq = jnp.concatenate([p * p for p in parts], axis=0).astype(BF16)
    ms = jnp.dot(sq, seg, preferred_element_type=F32)
    r = lax.rsqrt(ms + RMS_EPS)
    return [parts[j] * r[j * m:(j + 1) * m] * gain for j in range(len(parts))]


def _rope(p, cos, sin_lo, sin_hi):
    return p * cos + pltpu.roll(p, LANES - ROT_DIM // 2, 1) * sin_lo + pltpu.roll(p, ROT_DIM // 2, 1) * sin_hi


def _stage1_kernel(x_ref, sh_ref, sc_ref, g_ref, w_ref, gqa_ref, gka_ref, gqb_ref, gkb_ref, seg_ref,
                   cos_ref, sinlo_ref, sinhi_ref,
                   qa_ref, ka_ref, va_ref, qb_ref, kb_ref, vb_ref, ska_ref, sva_ref, skb_ref, svb_ref,
                   *, bb, tl, pad_tiles, n_tiles, sa_rows, sb_rows):
    i = pl.program_id(1)
    m = bb * tl

    if pad_tiles:
        @pl.when(i < pad_tiles)
        def _():
            ka_ref[...] = jnp.zeros_like(ka_ref)
            va_ref[...] = jnp.zeros_like(va_ref)
            kb_ref[...] = jnp.zeros_like(kb_ref)
            vb_ref[...] = jnp.zeros_like(vb_ref)

    @pl.when(i >= pad_tiles)
    def _():
        h = _modulated_norm(x_ref[...], g_ref[...], sc_ref[...], sh_ref[...])
        hb = h.reshape(m, D_MODEL).astype(BF16)
        qkv = jnp.dot(hb, w_ref[...], preferred_element_type=F32)
        seg = seg_ref[...]

        def cols(lo, width):
            return [qkv[:, lo + LANES * j: lo + LANES * (j + 1)] for j in range(width // LANES)]

        cos = jnp.concatenate([cos_ref[...]] * bb, axis=0)
        sin_lo = jnp.concatenate([sinlo_ref[...]] * bb, axis=0)
        sin_hi = jnp.concatenate([sinhi_ref[...]] * bb, axis=0)
        scale = HEAD_DIM ** -0.5

        qa = [_rope(p, cos, sin_lo, sin_hi) * scale for p in _head_norm(cols(0, A_Q_W), gqa_ref[...], seg)]
        ka = [_rope(p, cos, sin_lo, sin_hi) for p in _head_norm(cols(A_Q_W, A_KV_W), gka_ref[...], seg)]
        va = cols(A_Q_W + A_KV_W, A_KV_W)
        off_b = A_Q_W + 2 * A_KV_W
        qb = [p * scale for p in _head_norm(cols(off_b, B_W), gqb_ref[...], seg)]
        kb = _head_norm(cols(off_b + B_W, B_W), gkb_ref[...], seg)
        vb = cols(off_b + 2 * B_W, B_W)

        def put(ref, parts):
            for j, p in enumerate(parts):
                ref[:, :, LANES * j:LANES * (j + 1)] = p.reshape(bb, tl, LANES).astype(ref.dtype)

        put(qa_ref, qa)
        put(ka_ref, ka)
        put(va_ref, va)
        put(qb_ref, qb)
        put(kb_ref, kb)
        put(vb_ref, vb)

        @pl.when(i == n_tiles - 1)
        def _():
            def put_state(ref, parts, rows):
                for j, p in enumerate(parts):
                    ref[:, :, LANES * j:LANES * (j + 1)] = p.reshape(bb, tl, LANES)[:, tl - rows:, :]

            put_state(ska_ref, ka, sa_rows)
            put_state(sva_ref, va, sa_rows)
            put_state(skb_ref, kb, sb_rows)
            put_state(svb_ref, vb, sb_rows)


def _rope_tables(pos):
    inv_freq = ROPE_THETA ** (-jnp.arange(0, ROT_DIM, 2, dtype=F32) / ROT_DIM)
    ang = pos.astype(F32)[:, None] * inv_freq[None, :]
    half = ROT_DIM // 2
    lane = jnp.arange(LANES) % HEAD_DIM
    ang_l = ang[:, lane % half]
    rot = lane < ROT_DIM
    cos = jnp.where(rot[None, :], jnp.cos(ang_l), 1.0)
    sin_lo = jnp.where((lane < half)[None, :], -jnp.sin(ang_l), 0.0)
    sin_hi = jnp.where(((lane >= half) & rot)[None, :], jnp.sin(ang_l), 0.0)
    return cos, sin_lo, sin_hi


def _stage1(x, shift, scale, g_attn, w_qkv, gains, pos, *, bb, tl, pad):
    bsz, length, _ = x.shape
    n_tiles = length // tl
    pad_tiles = pad // tl
    sa_rows, sb_rows = min(WINDOW_A, length), min(WINDOW_B, length)
    assert bsz % bb == 0 and length % tl == 0 and pad % tl == 0 and sb_rows <= tl
    cos, sin_lo, sin_hi = _rope_tables(pos)
    seg = (jnp.arange(LANES)[:, None] // HEAD_DIM == jnp.arange(LANES)[None, :] // HEAD_DIM)
    seg = (seg.astype(F32) / HEAD_DIM).astype(BF16)
    gq_a, gk_a, gq_b, gk_b = [jnp.tile(g, LANES // HEAD_DIM).reshape(1, LANES) for g in gains]

    def xmap(b, i):
        return (b, jnp.maximum(i - pad_tiles, 0), 0)

    def const2(b, i):
        return (0, 0)

    def tabmap(b, i):
        return (jnp.maximum(i - pad_tiles, 0), 0)

    def modmap(b, i):
        return (b, 0, 0)

    def kvmap(b, i):
        return (b, i, 0)

    lp = length + pad
    kern = functools.partial(_stage1_kernel, bb=bb, tl=tl, pad_tiles=pad_tiles,
                             n_tiles=n_tiles + pad_tiles, sa_rows=sa_rows, sb_rows=sb_rows)
    out_shapes = (
        jax.ShapeDtypeStruct((bsz, length, A_Q_W), BF16),
        jax.ShapeDtypeStruct((bsz, lp, A_KV_W), BF16),
        jax.ShapeDtypeStruct((bsz, lp, A_KV_W), BF16),
        jax.ShapeDtypeStruct((bsz, length, B_W), BF16),
        jax.ShapeDtypeStruct((bsz, lp, B_W), BF16),
        jax.ShapeDtypeStruct((bsz, lp, B_W), BF16),
        jax.ShapeDtypeStruct((bsz, sa_rows, A_KV_W), F32),
        jax.ShapeDtypeStruct((bsz, sa_rows, A_KV_W), F32),
        jax.ShapeDtypeStruct((bsz, sb_rows, B_W), F32),
        jax.ShapeDtypeStruct((bsz, sb_rows, B_W), F32),
    )
    out_specs = (
        pl.BlockSpec((bb, tl, A_Q_W), xmap),
        pl.BlockSpec((bb, tl, A_KV_W), kvmap),
        pl.BlockSpec((bb, tl, A_KV_W), kvmap),
        pl.BlockSpec((bb, tl, B_W), xmap),
        pl.BlockSpec((bb, tl, B_W), kvmap),
        pl.BlockSpec((bb, tl, B_W), kvmap),
        pl.BlockSpec((bb, sa_rows, A_KV_W), modmap),
        pl.BlockSpec((bb, sa_rows, A_KV_W), modmap),
        pl.BlockSpec((bb, sb_rows, B_W), modmap),
        pl.BlockSpec((bb, sb_rows, B_W), modmap),
    )
    in_specs = [
        pl.BlockSpec((bb, tl, D_MODEL), xmap),
        pl.BlockSpec((bb, 1, D_MODEL), modmap),
        pl.BlockSpec((bb, 1, D_MODEL), modmap),
        pl.BlockSpec((1, D_MODEL), const2),
        pl.BlockSpec((D_MODEL, QKV_W), const2),
        pl.BlockSpec((1, LANES), const2),
        pl.BlockSpec((1, LANES), const2),
        pl.BlockSpec((1, LANES), const2),
        pl.BlockSpec((1, LANES), const2),
        pl.BlockSpec((LANES, LANES), const2),
        pl.BlockSpec((tl, LANES), tabmap),
        pl.BlockSpec((tl, LANES), tabmap),
        pl.BlockSpec((tl, LANES), tabmap),
    ]
    return pl.pallas_call(
        kern,
        grid=(bsz // bb, n_tiles + pad_tiles),
        in_specs=in_specs,
        out_specs=out_specs,
        out_shape=out_shapes,
        compiler_params=_cparams("arbitrary", "arbitrary"),
        name="stage1",
    )(x, shift, scale, g_attn.reshape(1, D_MODEL), w_qkv, gq_a, gk_a, gq_b, gk_b, seg, cos, sin_lo, sin_hi)


def _nt_dot(a, b):
    return lax.dot_general(a, b, (((1,), (1,)), ((), ())), preferred_element_type=F32)


def _attn_kernel(sink_ref, qa_ref, ka_ref, va_ref, qb_ref, kb_ref, vb_ref, bias_ref, oa_ref, ob_ref,
                 *, n_c, mask_prefix):
    qi = pl.program_id(1)
    lane = lax.broadcasted_iota(I32, (CHUNK, LANES), 1)
    low = lane < HEAD_DIM
    half_mask = (jnp.where(low, 1.0, 0.0).astype(BF16), jnp.where(low, 0.0, 1.0).astype(BF16))
    n_pairs = B_W // LANES
    rows_b = n_pairs * 2 * CHUNK
    rows_a = A_Q_HEADS * CHUNK
    key_b = lax.broadcasted_iota(I32, (rows_b, BAND_B), 1)
    key_a = lax.broadcasted_iota(I32, (rows_a, BAND_A), 1)
    row_a = lax.broadcasted_iota(I32, (rows_a, 1), 0) // CHUNK

    sink_col = jnp.zeros((rows_a, 1), F32)
    for r in range(A_Q_HEADS):
        sink_col = jnp.where(row_a == r, sink_ref[r], sink_col)

    for c in range(n_c):
        ci = qi * n_c + c
        row0 = pl.multiple_of(ci * CHUNK, CHUNK)
        rows = slice(c * CHUNK, (c + 1) * CHUNK)
        first_valid = PAD - ci * CHUNK

        s_parts = []
        for j in range(n_pairs):
            cg = slice(LANES * j, LANES * (j + 1))
            q = qb_ref[0, rows, cg]
            qs = jnp.concatenate([q * half_mask[0], q * half_mask[1]], axis=0)
            k = kb_ref[0, pl.ds(row0, BAND_B), cg]
            s_parts.append(_nt_dot(qs, k) + bias_ref[j])
        s = jnp.concatenate(s_parts, axis=0)
        if mask_prefix:
            s = jnp.where(key_b >= first_valid, s, NEG_INF)
        mx = jnp.max(s, axis=-1, keepdims=True)
        e = jnp.exp(s - mx)
        inv = 1.0 / jnp.sum(e, axis=-1, keepdims=True)
        p_b = e.astype(BF16)
        for j in range(n_pairs):
            cg = slice(LANES * j, LANES * (j + 1))
            v = vb_ref[0, pl.ds(row0, BAND_B), cg]
            rj = slice(2 * CHUNK * j, 2 * CHUNK * (j + 1))
            o = jnp.dot(p_b[rj], v, preferred_element_type=F32) * inv[rj]
            ob_ref[0, rows, cg] = jnp.where(low, o[:CHUNK], o[CHUNK:]).astype(ob_ref.dtype)

        row_a0 = pl.multiple_of(row0 + (WINDOW_B - WINDOW_A), CHUNK)
        k = ka_ref[0, pl.ds(row_a0, BAND_A), :]
        v = va_ref[0, pl.ds(row_a0, BAND_A), :]
        q = jnp.concatenate(
            [qa_ref[0, rows, LANES * j:LANES * (j + 1)] * half_mask[p]
             for p in range(A_KV_HEADS) for j in range(A_GROUP)], axis=0)
        s = _nt_dot(q, k)
        if mask_prefix:
            s = jnp.where(key_a >= first_valid - (WINDOW_B - WINDOW_A), s, NEG_INF)
        mx = jnp.maximum(jnp.max(s, axis=-1, keepdims=True), sink_col)
        e = jnp.exp(s - mx)
        inv = 1.0 / (jnp.sum(e, axis=-1, keepdims=True) + jnp.exp(sink_col - mx))
        o = jnp.dot(e.astype(BF16), v, preferred_element_type=F32) * inv
        for j in range(A_GROUP):
            lo = o[j * CHUNK:(j + 1) * CHUNK]
            hi = o[(A_GROUP + j) * CHUNK:(A_GROUP + j + 1) * CHUNK]
            oa_ref[0, rows, LANES * j:LANES * (j + 1)] = jnp.where(low, lo, hi).astype(oa_ref.dtype)


def _attention(qa, ka, va, qb, kb, vb, sinks, bias, *, n_c, mask_prefix):
    bsz, length, _ = qa.shape
    lp = ka.shape[1]
    qt = n_c * CHUNK
    assert length % qt == 0 and lp == length + PAD

    def qmap(b, i):
        return (b, i, 0)

    def kmap(b, i):
        return (b, 0, 0)

    kern = functools.partial(_attn_kernel, n_c=n_c, mask_prefix=mask_prefix)
    return pl.pallas_call(
        kern,
        grid=(bsz, length // qt),
        in_specs=[
            pl.BlockSpec(memory_space=pltpu.SMEM),
            pl.BlockSpec((1, qt, A_Q_W), qmap),
            pl.BlockSpec((1, lp, A_KV_W), kmap),
            pl.BlockSpec((1, lp, A_KV_W), kmap),
            pl.BlockSpec((1, qt, B_W), qmap),
            pl.BlockSpec((1, lp, B_W), kmap),
            pl.BlockSpec((1, lp, B_W), kmap),
            pl.BlockSpec((B_HEADS // 2, 2 * CHUNK, BAND_B), lambda b, i: (0, 0, 0)),
        ],
        out_specs=(pl.BlockSpec((1, qt, A_Q_W), qmap), pl.BlockSpec((1, qt, B_W), qmap)),
        out_shape=(jax.ShapeDtypeStruct((bsz, length, A_Q_W), BF16),
                   jax.ShapeDtypeStruct((bsz, length, B_W), BF16)),
        compiler_params=_cparams("arbitrary", "arbitrary"),
        name="attention",
    )(sinks, qa, ka, va, qb, kb, vb, bias)


def _merge_kernel(x_ref, oa_ref, ob_ref, sha_ref, sca_ref, gta_ref, shm_ref, scm_ref, ga_ref, gm_ref,
                  wg_ref, wa_ref, wb_ref, wo_ref, wrh_ref, wrl_ref, br_ref,
                  x1_ref, h2_ref, ri_ref, rw_ref, cnt_ref, carry_ref, logit_ref, *, bb, tl):
    m = bb * tl
    step = pl.program_id(0)

    @pl.when(step == 0)
    def _():
        carry_ref[...] = jnp.zeros_like(carry_ref)
        logit_ref[...] = jnp.zeros_like(logit_ref)

    logits = logit_ref[...]
    lane = lax.broadcasted_iota(I32, (m, LANES), 1)
    lane_f = lane.astype(F32)
    work = logits
    top_v, top_i = [], []
    sel_f = jnp.zeros((m, LANES), F32)
    for _ in range(TOP_K):
        mx = jnp.max(work, axis=-1, keepdims=True)
        idx = jnp.min(jnp.where(work == mx, lane_f, float(LANES)), axis=-1, keepdims=True)
        hit = lane_f == idx
        top_v.append(mx)
        top_i.append(idx)
        sel_f = jnp.where(hit, 1.0, sel_f)
        work = jnp.where(hit, -jnp.inf, work)
    ex = [jnp.exp(v - top_v[0]) for v in top_v]
    inv = 1.0 / (ex[0] + ex[1] + ex[2] + ex[3])

    r_i = lax.broadcasted_iota(I32, (m, m), 0)
    c_i = lax.broadcasted_iota(I32, (m, m), 1)
    before = jnp.where(r_i > c_i, 1.0, 0.0).astype(BF16)
    pos = jnp.dot(before, sel_f.astype(BF16), preferred_element_type=F32) + carry_ref[...]
    live = jnp.where(step > 0, 1.0, 0.0)
    carry = carry_ref[...] + live * jnp.sum(sel_f, axis=0, keepdims=True)
    carry_ref[...] = carry
    cnt_ref[...] = carry

    ri_f = jnp.zeros((m, LANES), F32)
    rw = jnp.zeros((m, LANES), F32)
    for k in range(TOP_K):
        rank = jnp.sum(jnp.where(lane_f == top_i[k], pos, 0.0), axis=-1, keepdims=True)
        ri_f = jnp.where(lane == k, top_i[k], ri_f)
        ri_f = jnp.where(lane == TOP_K + k, rank, ri_f)
        rw = jnp.where(lane == k, ex[k] * inv, rw)
    ri_ref[...] = ri_f.astype(I32).reshape(bb, tl, LANES)
    rw_ref[...] = rw.reshape(bb, tl, LANES)

    x = x_ref[...]
    h = _modulated_norm(x, ga_ref[...], sca_ref[...], sha_ref[...])
    hb = h.reshape(m, D_MODEL).astype(BF16)
    gates = jnp.dot(hb, wg_ref[...], preferred_element_type=F32)
    ya = jnp.dot(oa_ref[...].reshape(m, A_Q_W), wa_ref[...], preferred_element_type=F32)
    yb = jnp.dot(ob_ref[...].reshape(m, B_W), wb_ref[...], preferred_element_type=F32)
    mix = jax.nn.sigmoid(gates[:, :D_MODEL]) * ya + jax.nn.sigmoid(gates[:, D_MODEL:]) * yb
    z = jnp.dot(mix.astype(BF16), wo_ref[...], preferred_element_type=F32)
    x1 = x + gta_ref[...] * z.reshape(bb, tl, D_MODEL)
    x1_ref[...] = x1

    h2 = _modulated_norm(x1, gm_ref[...], scm_ref[...], shm_ref[...]).reshape(m, D_MODEL)
    for j in range(ROW_TILES):
        h2_ref[pl.ds(j, m, stride=ROW_TILES), :] = h2[:, LANES * j:LANES * (j + 1)]

    h2_hi = h2.astype(BF16)
    h2_lo = (h2 - h2_hi.astype(F32)).astype(BF16)
    wrh = wrh_ref[...]
    logit_ref[...] = (jnp.dot(h2_hi, wrh, preferred_element_type=F32)
                      + jnp.dot(h2_lo, wrh, preferred_element_type=F32)
                      + jnp.dot(h2_hi, wrl_ref[...], preferred_element_type=F32)) + br_ref[...]


def _merge(x, oa, ob, mods, g_attn, g_mlp, wg, wa, wb, wo, wr, br, *, bb, tl):
    bsz, length, _ = x.shape
    n_tok = bsz * length
    m = bb * tl
    assert bb == 1 or tl == length
    sh_a, sc_a, gt_a, sh_m, sc_m = mods
    n_l = length // tl
    n_tiles = (bsz // bb) * n_l
    wr_hi = wr.astype(BF16)
    wr_lo = (wr - wr_hi.astype(F32)).astype(BF16)

    def cur(s):
        return jnp.minimum(s, n_tiles - 1)

    def prev(s):
        return jnp.maximum(s - 1, 0)

    def xmap(s):
        return (cur(s) // n_l, cur(s) % n_l, 0)

    def modmap(s):
        return (cur(s) // n_l, 0, 0)

    def const2(s):
        return (0, 0)

    def flatmap(s):
        return (cur(s), 0)

    def routemap(s):
        return (prev(s) // n_l, prev(s) % n_l, 0)

    modspec = pl.BlockSpec((bb, 1, D_MODEL), modmap)
    kern = functools.partial(_merge_kernel, bb=bb, tl=tl)
    return pl.pallas_call(
        kern,
        grid=(n_tiles + 1,),
        in_specs=[
            pl.BlockSpec((bb, tl, D_MODEL), xmap),
            pl.BlockSpec((bb, tl, A_Q_W), xmap),
            pl.BlockSpec((bb, tl, B_W), xmap),
            modspec, modspec, modspec, modspec, modspec,
            pl.BlockSpec((1, D_MODEL), const2),
            pl.BlockSpec((1, D_MODEL), const2),
            pl.BlockSpec((D_MODEL, 2 * D_MODEL), const2),
            pl.BlockSpec((A_Q_W, D_MODEL), const2),
            pl.BlockSpec((B_W, D_MODEL), const2),
            pl.BlockSpec((D_MODEL, D_MODEL), const2),
            pl.BlockSpec((D_MODEL, LANES), const2),
            pl.BlockSpec((D_MODEL, LANES), const2),
            pl.BlockSpec((1, LANES), const2),
        ],
        out_specs=(
            pl.BlockSpec((bb, tl, D_MODEL), xmap),
            pl.BlockSpec((m * ROW_TILES, LANES), flatmap),
            pl.BlockSpec((bb, tl, LANES), routemap),
            pl.BlockSpec((bb, tl, LANES), routemap),
            pl.BlockSpec((1, LANES), const2),
        ),
        out_shape=(
            jax.ShapeDtypeStruct((bsz, length, D_MODEL), F32),
            jax.ShapeDtypeStruct((n_tok * ROW_TILES, LANES), F32),
            jax.ShapeDtypeStruct((bsz, length, LANES), I32),
            jax.ShapeDtypeStruct((bsz, length, LANES), F32),
            jax.ShapeDtypeStruct((1, LANES), F32),
        ),
        scratch_shapes=[pltpu.VMEM((1, LANES), F32), pltpu.VMEM((m, LANES), F32)],
        compiler_params=_cparams("arbitrary"),
        name="merge",
    )(x, oa, ob, sh_a, sc_a, gt_a, sh_m, sc_m, g_attn.reshape(1, D_MODEL), g_mlp.reshape(1, D_MODEL),
      wg, wa, wb, wo, wr_hi, wr_lo, br)


def _sc_mesh():
    return plsc.VectorSubcoreMesh(core_axis_name="c", subcore_axis_name="s",
                                  num_cores=SC_CORES, num_subcores=SC_SUBCORES)


def _sc_worker():
    return lax.axis_index("c") * SC_SUBCORES + lax.axis_index("s")


def _sc_params():
    return pltpu.CompilerParams(use_tc_tiling_on_sc=True)


def _dispatch(h2_rows, idx_flat, pad_flat, n_rows):
    n_tok = h2_rows.shape[0]
    per_w = n_tok // SC_WORKERS
    n_chunks = per_w // SC_ROWS
    n_idx = n_chunks * TOP_K * SC_ROWS
    n_pad = pad_flat.shape[0] // SC_WORKERS
    pad_chunks = n_pad // SC_ROWS
    assert per_w % (2 * SC_ROWS) == 0 and n_pad % SC_ROWS == 0
    zeros = jnp.zeros((SC_ROWS, ROW_TILES, LANES), F32)

    def body(h2_hbm, idx_hbm, pad_hbm, zeros_hbm, xs_hbm, idx_v, pad_v, buf0, buf1, ls0, ls1, ss0, ss1):
        wid = _sc_worker()
        bufs, lsems, ssems = (buf0, buf1), (ls0, ls1), (ss0, ss1)
        pltpu.sync_copy(idx_hbm.at[pl.ds(pl.multiple_of(wid * n_idx, SC_ROWS), n_idx)], idx_v)
        pltpu.sync_copy(pad_hbm.at[pl.ds(pl.multiple_of(wid * n_pad, SC_ROWS), n_pad)], pad_v)

        pltpu.sync_copy(zeros_hbm, buf0)
        for c in range(pad_chunks):
            pltpu.async_copy(buf0, xs_hbm.at[pad_v.at[pl.ds(c * SC_ROWS, SC_ROWS)]], ss0)
        for c in range(pad_chunks):
            pltpu.make_async_copy(buf0, xs_hbm.at[pad_v.at[pl.ds(c * SC_ROWS, SC_ROWS)]], ss0).wait()

        base = wid * per_w

        def load(g, b):
            src = h2_hbm.at[pl.ds(pl.multiple_of(base + g * SC_ROWS, SC_ROWS), SC_ROWS)]
            return pltpu.make_async_copy(src, bufs[b], lsems[b])

        def scatter(g, k, b):
            off = pl.multiple_of((g * TOP_K + k) * SC_ROWS, SC_ROWS)
            return pltpu.make_async_copy(bufs[b], xs_hbm.at[idx_v.at[pl.ds(off, SC_ROWS)]], ssems[b])

        load(0, 0).start()

        @pl.loop(0, n_chunks, step=2)
        def _(g0):
            for b in range(2):
                g = g0 + b
                load(g, b).wait()

                @pl.when(g >= 1)
                def _():
                    for k in range(TOP_K):
                        scatter(g - 1, k, 1 - b).wait()

                @pl.when(g + 1 < n_chunks)
                def _():
                    load(g + 1, 1 - b).start()

                for k in range(TOP_K):
                    scatter(g, k, b).start()

        for k in range(TOP_K):
            scatter(n_chunks - 1, k, (n_chunks - 1) % 2).wait()

    call = pl.kernel(
        body,
        out_type=jax.ShapeDtypeStruct((n_rows + TRASH_ROWS, ROW_TILES, LANES), F32),
        mesh=_sc_mesh(),
        scratch_types=[pltpu.VMEM((n_idx,), I32), pltpu.VMEM((n_pad,), I32),
                       pltpu.VMEM((SC_ROWS, ROW_TILES, LANES), F32), pltpu.VMEM((SC_ROWS, ROW_TILES, LANES), F32),
                       pltpu.SemaphoreType.DMA, pltpu.SemaphoreType.DMA,
                       pltpu.SemaphoreType.DMA, pltpu.SemaphoreType.DMA],
        compiler_params=_sc_params(),
        name="dispatch",
    )
    return call(h2_rows, idx_flat, pad_flat, zeros)


def _gather_back(y_rows, idx_flat, n_tok):
    per_w = n_tok // SC_WORKERS
    n_chunks = per_w // SC_ROWS
    n_items = n_chunks * TOP_K
    n_idx = n_items * SC_ROWS
    assert per_w % SC_ROWS == 0 and n_items % 2 == 0

    def body(y_hbm, idx_hbm, yg_hbm, idx_v, buf0, buf1, gs0, gs1, ws0, ws1):
        wid = _sc_worker()
        bufs, gsems, wsems = (buf0, buf1), (gs0, gs1), (ws0, ws1)
        pltpu.sync_copy(idx_hbm.at[pl.ds(pl.multiple_of(wid * n_idx, SC_ROWS), n_idx)], idx_v)
        base = wid * per_w

        def gather(i, b):
            off = pl.multiple_of(i * SC_ROWS, SC_ROWS)
            return pltpu.make_async_copy(y_hbm.at[idx_v.at[pl.ds(off, SC_ROWS)]], bufs[b], gsems[b])

        def write(i, b):
            g, k = i // TOP_K, i % TOP_K
            dst = yg_hbm.at[k, pl.ds(pl.multiple_of(base + g * SC_ROWS, SC_ROWS), SC_ROWS)]
            return pltpu.make_async_copy(bufs[b], dst, wsems[b])

        gather(0, 0).start()

        @pl.loop(0, n_items, step=2)
        def _(i0):
            for b in range(2):
                i = i0 + b
                gather(i, b).wait()

                @pl.when(i >= 1)
                def _():
                    write(i - 1, 1 - b).wait()

                @pl.when(i + 1 < n_items)
                def _():
                    gather(i + 1, 1 - b).start()

                write(i, b).start()

        write(n_items - 1, (n_items - 1) % 2).wait()

    call = pl.kernel(
        body,
        out_type=jax.ShapeDtypeStruct((TOP_K, n_tok, ROW_TILES, LANES), F32),
        mesh=_sc_mesh(),
        scratch_types=[pltpu.VMEM((n_idx,), I32),
                       pltpu.VMEM((SC_ROWS, ROW_TILES, LANES), F32), pltpu.VMEM((SC_ROWS, ROW_TILES, LANES), F32),
                       pltpu.SemaphoreType.DMA, pltpu.SemaphoreType.DMA,
                       pltpu.SemaphoreType.DMA, pltpu.SemaphoreType.DMA],
        compiler_params=_sc_params(),
        name="gather_back",
    )
    return call(y_rows, idx_flat)


def _expert_kernel(blk_ref, used_ref, xs_ref, wgu_ref, bgu_ref, wd_ref, bd_ref, y_ref, wgu_bf, wd_bf):
    r = pl.program_id(0)
    e = blk_ref[r]
    prev = blk_ref[jnp.maximum(r - 1, 0)]
    active = r < used_ref[0]

    @pl.when(jnp.logical_and(active, jnp.logical_or(r == 0, e != prev)))
    def _():
        wgu_bf[...] = wgu_ref[0].astype(BF16)
        wd_bf[...] = wd_ref[0].astype(BF16)

    @pl.when(active)
    def _():
        x = jnp.concatenate(
            [xs_ref[pl.ds(j, ROUTE_BLOCK, stride=ROW_TILES), :] for j in range(ROW_TILES)], axis=1).astype(BF16)
        acts = []
        for j in range(D_FF // FF_TILE):
            gc = slice(FF_TILE * j, FF_TILE * (j + 1))
            uc = slice(D_FF + FF_TILE * j, D_FF + FF_TILE * (j + 1))
            g = jnp.dot(x, wgu_bf[:, gc], preferred_element_type=F32) + bgu_ref[0, :, gc]
            u = jnp.dot(x, wgu_bf[:, uc], preferred_element_type=F32) + bgu_ref[0, :, uc]
            gate = jnp.minimum(g, SWIGLU_LIMIT)
            up = jnp.clip(u, -SWIGLU_LIMIT, SWIGLU_LIMIT)
            glu = gate * jax.nn.sigmoid(gate * SWIGLU_ALPHA)
            acts.append(((up + 1.0) * glu).astype(BF16))
        act = jnp.concatenate(acts, axis=1)
        y = jnp.dot(act, wd_bf[...], preferred_element_type=F32) + bd_ref[0]
        for j in range(ROW_TILES):
            y_ref[pl.ds(j, ROUTE_BLOCK, stride=ROW_TILES), :] = y[:, LANES * j:LANES * (j + 1)]


def _experts(xs_rows, blk_expert, n_used, w_gate_up, b_gate_up, w_down, b_down):
    n_blocks = blk_expert.shape[0]
    rb = ROUTE_BLOCK * ROW_TILES

    def rowmap(r, blk, used):
        return (jnp.minimum(r, used[0] - 1), 0)

    def wmap(r, blk, used):
        return (blk[jnp.minimum(r, used[0] - 1)], 0, 0)

    return pl.pallas_call(
        _expert_kernel,
        grid_spec=pltpu.PrefetchScalarGridSpec(
            num_scalar_prefetch=2,
            grid=(n_blocks,),
            in_specs=[
                pl.BlockSpec((rb, LANES), rowmap),
                pl.BlockSpec((1, D_MODEL, 2 * D_FF), wmap),
                pl.BlockSpec((1, 1, 2 * D_FF), wmap),
                pl.BlockSpec((1, D_FF, D_MODEL), wmap),
                pl.BlockSpec((1, 1, D_MODEL), wmap),
            ],
            out_specs=pl.BlockSpec((rb, LANES), rowmap),
            scratch_shapes=[pltpu.VMEM((D_MODEL, 2 * D_FF), BF16), pltpu.VMEM((D_FF, D_MODEL), BF16)],
        ),
        out_shape=jax.ShapeDtypeStruct(xs_rows.shape, F32),
        compiler_params=_cparams("arbitrary"),
        name="experts",
    )(blk_expert, n_used, xs_rows, w_gate_up, b_gate_up.reshape(N_EXPERTS, 1, 2 * D_FF),
      w_down, b_down.reshape(N_EXPERTS, 1, D_MODEL))


def _combine_kernel(x1_ref, gt_ref, rw_ref, yg_ref, o_ref, *, bb, tl):
    m = bb * tl
    rw = rw_ref[...].reshape(m, LANES)
    w = [rw[:, k:k + 1] for k in range(TOP_K)]
    gt = gt_ref[...]
    for j in range(ROW_TILES):
        acc = jnp.zeros((m, LANES), F32)
        for k in range(TOP_K):
            acc = acc + w[k] * yg_ref[k, pl.ds(j, m, stride=ROW_TILES), :]
        cg = slice(LANES * j, LANES * (j + 1))
        o_ref[:, :, cg] = x1_ref[:, :, cg] + gt[:, :, cg] * acc.reshape(bb, tl, LANES)


def _combine(x1, gt_m, rw, yg, *, bb, tl):
    bsz, length, _ = x1.shape
    m = bb * tl
    assert bb == 1 or tl == length
    n_l = length // tl

    def xmap(b, i):
        return (b, i, 0)

    kern = functools.partial(_combine_kernel, bb=bb, tl=tl)
    return pl.pallas_call(
        kern,
        grid=(bsz // bb, n_l),
        in_specs=[
            pl.BlockSpec((bb, tl, D_MODEL), xmap),
            pl.BlockSpec((bb, 1, D_MODEL), lambda b, i: (b, 0, 0)),
            pl.BlockSpec((bb, tl, LANES), xmap),
            pl.BlockSpec((TOP_K, m * ROW_TILES, LANES), lambda b, i: (0, b * n_l + i, 0)),
        ],
        out_specs=pl.BlockSpec((bb, tl, D_MODEL), xmap),
        out_shape=jax.ShapeDtypeStruct(x1.shape, F32),
        compiler_params=_cparams("arbitrary", "arbitrary"),
        name="combine",
    )(x1, gt_m, rw, yg)


def _route_plan(ri, counts_f, n_tok):
    idx = ri[..., :TOP_K].reshape(n_tok, TOP_K)
    rank = ri[..., TOP_K:2 * TOP_K].reshape(n_tok, TOP_K)
    counts = counts_f[0, :N_EXPERTS].astype(I32)
    padded = (counts + ROUTE_BLOCK - 1) // ROUTE_BLOCK * ROUTE_BLOCK
    pends = jnp.cumsum(padded)
    pstarts = pends - padded
    dest = pstarts[idx] + rank
    n_blocks = n_tok * TOP_K // ROUTE_BLOCK + N_EXPERTS
    blk_start = jnp.arange(n_blocks, dtype=I32) * ROUTE_BLOCK
    blk_expert = jnp.minimum(jnp.sum((pends[None, :] <= blk_start[:, None]).astype(I32), axis=1), N_EXPERTS - 1)
    n_used = (pends[-1] // ROUTE_BLOCK).reshape(1)
    n_rows = n_blocks * ROUTE_BLOCK
    per_w = n_tok // SC_WORKERS
    idx_flat = dest.astype(I32).reshape(SC_WORKERS, per_w // SC_ROWS, SC_ROWS, TOP_K)
    idx_flat = idx_flat.transpose(0, 1, 3, 2).reshape(-1)
    r = jnp.arange(ROUTE_BLOCK, dtype=I32)[None, :]
    pad_flat = jnp.where(r < (padded - counts)[:, None], (pstarts + counts)[:, None] + r, n_rows)
    return idx_flat, pad_flat.astype(I32).reshape(-1), blk_expert.astype(I32), n_used.astype(I32), n_rows


def _layer(x, mods, pos, prefix, weights, *, bb, tl, n_c, tl_combine, bb_combine):
    (g_attn, g_mlp, w_qkv, gains, sinks, bias, wg, wa, wb, wo, wr, br,
     w_gate_up, b_gate_up, w_down, b_down) = weights
    bsz, length, _ = x.shape
    n_tok = bsz * length
    sh_a, sc_a, gt_a, sh_m, sc_m, gt_m = mods
    pad = 0 if prefix is not None else PAD
    qa, ka, va, qb, kb, vb, ska, sva, skb, svb = _stage1(
        x, sh_a, sc_a, g_attn, w_qkv, gains, pos, bb=bb, tl=tl, pad=pad)
    if prefix is not None:
        ka, va, kb, vb = [jnp.concatenate([p, t], axis=1) for p, t in zip(prefix, (ka, va, kb, vb))]
    oa, ob = _attention(qa, ka, va, qb, kb, vb, sinks, bias, n_c=n_c, mask_prefix=prefix is None)
    x1, h2_rows, ri, rw, counts_f = _merge(x, oa, ob, (sh_a, sc_a, gt_a, sh_m, sc_m), g_attn, g_mlp,
                                           wg, wa, wb, wo, wr, br, bb=bb, tl=tl)
    idx_flat, pad_flat, blk_expert, n_used, n_rows = _route_plan(ri, counts_f, n_tok)
    xs = _dispatch(h2_rows.reshape(n_tok, ROW_TILES, LANES), idx_flat, pad_flat, n_rows)
    y_rows = _experts(xs.reshape(-1, LANES), blk_expert, n_used, w_gate_up, b_gate_up, w_down, b_down)
    yg = _gather_back(y_rows.reshape(-1, ROW_TILES, LANES), idx_flat, n_tok)
    y = _combine(x1, gt_m, rw, yg.reshape(TOP_K, n_tok * ROW_TILES, LANES), bb=bb_combine, tl=tl_combine)
    return y, (ska, sva, skb, svb)


def _rel_bias_table(rel_table):
    q_pos = jnp.arange(CHUNK)
    k_pos = jnp.arange(BAND_B) - WINDOW_B
    idx = jnp.clip(q_pos[:, None] - k_pos[None, :], -REL_FWD, REL_BACK) + REL_FWD
    return rel_table[:, idx].astype(F32).reshape(B_HEADS // 2, 2 * CHUNK, BAND_B)


def _prep_weights(g_attn, g_mlp, w_in, q_norm_a, k_norm_a, q_norm_b, k_norm_b, sinks_a, rel_table_b,
                  w_branch_a, w_branch_b, w_out, w_router, b_router, w_gate_up, b_gate_up, w_down, b_down):
    order = jnp.asarray(A_HEAD_ORDER)
    w_qa = w_in[:, :A_Q_W].reshape(D_MODEL, A_Q_HEADS, HEAD_DIM)[:, order].reshape(D_MODEL, A_Q_W)
    w_qkv = jnp.concatenate([w_qa, w_in[:, A_Q_W:QKV_W]], axis=1).astype(BF16)
    wg = w_in[:, QKV_W:].astype(BF16)
    wa = w_branch_a.reshape(A_Q_HEADS, HEAD_DIM, D_MODEL)[order].reshape(A_Q_W, D_MODEL).astype(BF16)
    wb = w_branch_b.astype(BF16)
    wo = w_out.astype(BF16)
    wr = jnp.zeros((D_MODEL, LANES), F32).at[:, :N_EXPERTS].set(w_router)
    br = jnp.full((1, LANES), NEG_INF, F32).at[0, :N_EXPERTS].set(b_router)
    bias = _rel_bias_table(rel_table_b)
    gains = (q_norm_a, k_norm_a, q_norm_b, k_norm_b)
    return (g_attn, g_mlp, w_qkv, gains, sinks_a.astype(F32), bias, wg, wa, wb, wo, wr, br,
            w_gate_up, b_gate_up, w_down, b_down)


def _forward(x_prompt, x_sample, c_prompt, c_sample, cache_a_k, cache_a_v, cache_b_k, cache_b_v, params,
             past_len, cfg_prompt, cfg_sample):
    (g_attn, g_mlp, w_ada, b_ada, w_in, q_norm_a, k_norm_a, q_norm_b, k_norm_b, sinks_a, rel_table_b,
     w_branch_a, w_branch_b, w_out, w_router, b_router, w_gate_up, b_gate_up, w_down, b_down) = params
    depth = g_attn.shape[0]
    bp, lp_, _ = x_prompt.shape
    bs, ls, _ = x_sample.shape
    pos_p = jnp.arange(lp_, dtype=I32)
    pos_s = past_len + jnp.arange(ls, dtype=I32)
    yp, ys = x_prompt, x_sample
    st_p, st_s = [], []
    for l in range(depth):
        weights = _prep_weights(g_attn[l], g_mlp[l], w_in[l], q_norm_a[l], k_norm_a[l], q_norm_b[l],
                                k_norm_b[l], sinks_a[l], rel_table_b[l], w_branch_a[l], w_branch_b[l],
                                w_out[l], w_router[l], b_router[l], w_gate_up[l], b_gate_up[l],
                                w_down[l], b_down[l])
        mod = _adaln(jnp.concatenate([c_prompt, c_sample], axis=0), w_ada[l], b_ada[l])
        mod = mod.reshape(bp + bs, 6, 1, D_MODEL)
        mods_p = [mod[:bp, k] for k in range(6)]
        mods_s = [mod[bp:, k] for k in range(6)]
        yp, sp = _layer(yp, mods_p, pos_p, None, weights, **cfg_prompt)
        wa_c = cache_a_k.shape[2]
        wb_c = cache_b_k.shape[2]
        assert wb_c == WINDOW_B and wa_c == WINDOW_A
        zeros_a = jnp.zeros((bs, PAD - wa_c, A_KV_W), BF16)
        prefix = (
            jnp.concatenate([zeros_a, cache_a_k[l].reshape(bs, wa_c, A_KV_W).astype(BF16)], axis=1),
            jnp.concatenate([zeros_a, cache_a_v[l].reshape(bs, wa_c, A_KV_W).astype(BF16)], axis=1),
            cache_b_k[l].reshape(bs, wb_c, B_W).astype(BF16),
            cache_b_v[l].reshape(bs, wb_c, B_W).astype(BF16),
        )
        ys, ss = _layer(ys, mods_s, pos_s, prefix, weights, **cfg_sample)
        st_p.append(sp)
        st_s.append(ss)

    def stack(states, k, heads):
        return jnp.stack([s[k].reshape(s[k].shape[0], s[k].shape[1], heads, HEAD_DIM) for s in states])

    return (yp, ys,
            stack(st_p, 0, A_KV_HEADS), stack(st_p, 1, A_KV_HEADS), stack(st_p, 2, B_HEADS), stack(st_p, 3, B_HEADS),
            stack(st_s, 0, A_KV_HEADS), stack(st_s, 1, A_KV_HEADS), stack(st_s, 2, B_HEADS), stack(st_s, 3, B_HEADS))


PAST_LEN = 2048
CFG_PROMPT = dict(bb=1, tl=512, n_c=4, tl_combine=512, bb_combine=1)
CFG_SAMPLE = dict(bb=8, tl=64, n_c=1, tl_combine=64, bb_combine=8)


def kernel(x_prompt, x_sample, c_prompt, c_sample, cache_a_k, cache_a_v, cache_b_k, cache_b_v, g_attn, g_mlp,
           w_ada, b_ada, w_in, q_norm_a, k_norm_a, q_norm_b, k_norm_b, sinks_a, rel_table_b, w_branch_a,
           w_branch_b, w_out, w_router, b_router, w_gate_up, b_gate_up, w_down, b_down):
    params = (g_attn, g_mlp, w_ada, b_ada, w_in, q_norm_a, k_norm_a, q_norm_b, k_norm_b, sinks_a, rel_table_b,
              w_branch_a, w_branch_b, w_out, w_router, b_router, w_gate_up, b_gate_up, w_down, b_down)
    return _forward(x_prompt, x_sample, c_prompt, c_sample, cache_a_k, cache_a_v, cache_b_k, cache_b_v,
                    params, PAST_LEN, CFG_PROMPT, CFG_SAMPLE)
```

```python
import functools

import jax
import jax.numpy as jnp
from jax import lax
from jax.experimental import pallas as pl
from jax.experimental.pallas import tpu as pltpu
from jax.experimental.pallas import tpu_sc as plsc

F32 = jnp.float32
BF16 = jnp.bfloat16
I32 = jnp.int32

D_MODEL = 1024
CHUNK = 64
HEAD_DIM = 64
LANES = 128
SUBLANES = 8
A_Q_HEADS = 8
A_KV_HEADS = 2
A_GROUP = A_Q_HEADS // A_KV_HEADS
B_HEADS = 8
A_Q_W = A_Q_HEADS * HEAD_DIM
A_KV_W = A_KV_HEADS * HEAD_DIM
B_W = B_HEADS * HEAD_DIM
QKV_W = A_Q_W + 2 * A_KV_W + 3 * B_W
WINDOW_A = 128
WINDOW_B = 512
BAND_A = WINDOW_A + CHUNK
BAND_B = WINDOW_B + CHUNK
PAD = WINDOW_B
ROT_DIM = HEAD_DIM // 4
ROPE_THETA = 500000.0
REL_BACK = 4 * CHUNK
REL_FWD = CHUNK - 1
N_EXPERTS = 32
TOP_K = 4
D_FF = D_MODEL
SWIGLU_LIMIT = 7.0
SWIGLU_ALPHA = 1.702
ROUTE_BLOCK = 512
FF_TILE = 256
RMS_EPS = 1e-6
NEG_INF = -1e30
ROW_TILES = D_MODEL // LANES
VMEM_LIMIT = 56 * 1024 * 1024
SC_CORES = 2
SC_SUBCORES = 16
SC_WORKERS = SC_CORES * SC_SUBCORES
SC_ROWS = 32
TRASH_ROWS = 8

A_HEAD_ORDER = (0, 4, 1, 5, 2, 6, 3, 7)


def _cparams(*sem):
    return pltpu.CompilerParams(dimension_semantics=sem, vmem_limit_bytes=VMEM_LIMIT)


def _adaln_kernel(c_ref, w_ref, b_ref, o_ref):
    c = c_ref[...]
    a = c * jax.nn.sigmoid(c)
    o_ref[...] = jnp.dot(a, w_ref[...], preferred_element_type=F32,
                         precision=lax.Precision.HIGHEST) + b_ref[...]


def _adaln(c, w_ada, b_ada):
    n_rows = c.shape[0]
    n_out = w_ada.shape[1]
    tn = 1024
    return pl.pallas_call(
        _adaln_kernel,
        grid=(n_out // tn,),
        in_specs=[pl.BlockSpec((n_rows, D_MODEL), lambda j: (0, 0)),
                  pl.BlockSpec((D_MODEL, tn), lambda j: (0, j)),
                  pl.BlockSpec((1, tn), lambda j: (0, j))],
        out_specs=pl.BlockSpec((n_rows, tn), lambda j: (0, j)),
        out_shape=jax.ShapeDtypeStruct((n_rows, n_out), F32),
        compiler_params=_cparams("arbitrary"),
        name="adaln",
    )(c, w_ada, b_ada.reshape(1, n_out))


def _modulated_norm(x, g, scale, shift):
    ms = jnp.mean(x * x, axis=-1, keepdims=True)
    return (x * lax.rsqrt(ms + RMS_EPS) * g) * (1.0 + scale) + shift


def _head_norm(parts, gain, seg):
    m = parts[0].shape[0]
    sq = jnp.concatenate([p * p for p in parts], axis=0).astype(BF16)
    ms = jnp.dot(sq, seg, preferred_element_type=F32)
    r = lax.rsqrt(ms + RMS_EPS)
    return [parts[j] * r[j * m:(j + 1) * m] * gain for j in range(len(parts))]


def _rope(p, cos, sin_lo, sin_hi):
    return p * cos + pltpu.roll(p, LANES - ROT_DIM // 2, 1) * sin_lo + pltpu.roll(p, ROT_DIM // 2, 1) * sin_hi


def _stage1_kernel(x_ref, sh_ref, sc_ref, g_ref, w_ref, gqa_ref, gka_ref, gqb_ref, gkb_ref, seg_ref,
                   cos_ref, sinlo_ref, sinhi_ref,
                   qa_ref, ka_ref, va_ref, qb_ref, kb_ref, vb_ref, ska_ref, sva_ref, skb_ref, svb_ref,
                   *, bb, tl, pad_tiles, n_tiles, sa_rows, sb_rows):
    i = pl.program_id(1)
    m = bb * tl

    if pad_tiles:
        @pl.when(i < pad_tiles)
        def _():
            ka_ref[...] = jnp.zeros_like(ka_ref)
            va_ref[...] = jnp.zeros_like(va_ref)
            kb_ref[...] = jnp.zeros_like(kb_ref)
            vb_ref[...] = jnp.zeros_like(vb_ref)

    @pl.when(i >= pad_tiles)
    def _():
        h = _modulated_norm(x_ref[...], g_ref[...], sc_ref[...], sh_ref[...])
        hb = h.reshape(m, D_MODEL).astype(BF16)
        qkv = jnp.dot(hb, w_ref[...], preferred_element_type=F32)
        seg = seg_ref[...]

        def cols(lo, width):
            return [qkv[:, lo + LANES * j: lo + LANES * (j + 1)] for j in range(width // LANES)]

        cos = jnp.concatenate([cos_ref[...]] * bb, axis=0)
        sin_lo = jnp.concatenate([sinlo_ref[...]] * bb, axis=0)
        sin_hi = jnp.concatenate([sinhi_ref[...]] * bb, axis=0)
        scale = HEAD_DIM ** -0.5

        qa = [_rope(p, cos, sin_lo, sin_hi) * scale for p in _head_norm(cols(0, A_Q_W), gqa_ref[...], seg)]
        ka = [_rope(p, cos, sin_lo, sin_hi) for p in _head_norm(cols(A_Q_W, A_KV_W), gka_ref[...], seg)]
        va = cols(A_Q_W + A_KV_W, A_KV_W)
        off_b = A_Q_W + 2 * A_KV_W
        qb = [p * scale for p in _head_norm(cols(off_b, B_W), gqb_ref[...], seg)]
        kb = _head_norm(cols(off_b + B_W, B_W), gkb_ref[...], seg)
        vb = cols(off_b + 2 * B_W, B_W)

        def put(ref, parts):
            for j, p in enumerate(parts):
                ref[:, :, LANES * j:LANES * (j + 1)] = p.reshape(bb, tl, LANES).astype(ref.dtype)

        put(qa_ref, qa)
        put(ka_ref, ka)
        put(va_ref, va)
        put(qb_ref, qb)
        put(kb_ref, kb)
        put(vb_ref, vb)

        @pl.when(i == n_tiles - 1)
        def _():
            def put_state(ref, parts, rows):
                for j, p in enumerate(parts):
                    ref[:, :, LANES * j:LANES * (j + 1)] = p.reshape(bb, tl, LANES)[:, tl - rows:, :]

            put_state(ska_ref, ka, sa_rows)
            put_state(sva_ref, va, sa_rows)
            put_state(skb_ref, kb, sb_rows)
            put_state(svb_ref, vb, sb_rows)


def _rope_tables(pos):
    inv_freq = ROPE_THETA ** (-jnp.arange(0, ROT_DIM, 2, dtype=F32) / ROT_DIM)
    ang = pos.astype(F32)[:, None] * inv_freq[None, :]
    half = ROT_DIM // 2
    lane = jnp.arange(LANES) % HEAD_DIM
    ang_l = ang[:, lane % half]
    rot = lane < ROT_DIM
    cos = jnp.where(rot[None, :], jnp.cos(ang_l), 1.0)
    sin_lo = jnp.where((lane < half)[None, :], -jnp.sin(ang_l), 0.0)
    sin_hi = jnp.where(((lane >= half) & rot)[None, :], jnp.sin(ang_l), 0.0)
    return cos, sin_lo, sin_hi


def _stage1(x, shift, scale, g_attn, w_qkv, gains, pos, *, bb, tl, pad):
    bsz, length, _ = x.shape
    n_tiles = length // tl
    pad_tiles = pad // tl
    sa_rows, sb_rows = min(WINDOW_A, length), min(WINDOW_B, length)
    assert bsz % bb == 0 and length % tl == 0 and pad % tl == 0 and sb_rows <= tl
    cos, sin_lo, sin_hi = _rope_tables(pos)
    seg = (jnp.arange(LANES)[:, None] // HEAD_DIM == jnp.arange(LANES)[None, :] // HEAD_DIM)
    seg = (seg.astype(F32) / HEAD_DIM).astype(BF16)
    gq_a, gk_a, gq_b, gk_b = [jnp.tile(g, LANES // HEAD_DIM).reshape(1, LANES) for g in gains]

    def xmap(b, i):
        return (b, jnp.maximum(i - pad_tiles, 0), 0)

    def const2(b, i):
        return (0, 0)

    def tabmap(b, i):
        return (jnp.maximum(i - pad_tiles, 0), 0)

    def modmap(b, i):
        return (b, 0, 0)

    def kvmap(b, i):
        return (b, i, 0)

    lp = length + pad
    kern = functools.partial(_stage1_kernel, bb=bb, tl=tl, pad_tiles=pad_tiles,
                             n_tiles=n_tiles + pad_tiles, sa_rows=sa_rows, sb_rows=sb_rows)
    out_shapes = (
        jax.ShapeDtypeStruct((bsz, length, A_Q_W), BF16),
        jax.ShapeDtypeStruct((bsz, lp, A_KV_W), BF16),
        jax.ShapeDtypeStruct((bsz, lp, A_KV_W), BF16),
        jax.ShapeDtypeStruct((bsz, length, B_W), BF16),
        jax.ShapeDtypeStruct((bsz, lp, B_W), BF16),
        jax.ShapeDtypeStruct((bsz, lp, B_W), BF16),
        jax.ShapeDtypeStruct((bsz, sa_rows, A_KV_W), F32),
        jax.ShapeDtypeStruct((bsz, sa_rows, A_KV_W), F32),
        jax.ShapeDtypeStruct((bsz, sb_rows, B_W), F32),
        jax.ShapeDtypeStruct((bsz, sb_rows, B_W), F32),
    )
    out_specs = (
        pl.BlockSpec((bb, tl, A_Q_W), xmap),
        pl.BlockSpec((bb, tl, A_KV_W), kvmap),
        pl.BlockSpec((bb, tl, A_KV_W), kvmap),
        pl.BlockSpec((bb, tl, B_W), xmap),
        pl.BlockSpec((bb, tl, B_W), kvmap),
        pl.BlockSpec((bb, tl, B_W), kvmap),
        pl.BlockSpec((bb, sa_rows, A_KV_W), modmap),
        pl.BlockSpec((bb, sa_rows, A_KV_W), modmap),
        pl.BlockSpec((bb, sb_rows, B_W), modmap),
        pl.BlockSpec((bb, sb_rows, B_W), modmap),
    )
    in_specs = [
        pl.BlockSpec((bb, tl, D_MODEL), xmap),
        pl.BlockSpec((bb, 1, D_MODEL), modmap),
        pl.BlockSpec((bb, 1, D_MODEL), modmap),
        pl.BlockSpec((1, D_MODEL), const2),
        pl.BlockSpec((D_MODEL, QKV_W), const2),
        pl.BlockSpec((1, LANES), const2),
        pl.BlockSpec((1, LANES), const2),
        pl.BlockSpec((1, LANES), const2),
        pl.BlockSpec((1, LANES), const2),
        pl.BlockSpec((LANES, LANES), const2),
        pl.BlockSpec((tl, LANES), tabmap),
        pl.BlockSpec((tl, LANES), tabmap),
        pl.BlockSpec((tl, LANES), tabmap),
    ]
    return pl.pallas_call(
        kern,
        grid=(bsz // bb, n_tiles + pad_tiles),
        in_specs=in_specs,
        out_specs=out_specs,
        out_shape=out_shapes,
        compiler_params=_cparams("arbitrary", "arbitrary"),
        name="stage1",
    )(x, shift, scale, g_attn.reshape(1, D_MODEL), w_qkv, gq_a, gk_a, gq_b, gk_b, seg, cos, sin_lo, sin_hi)


def _nt_dot(a, b):
    return lax.dot_general(a, b, (((1,), (1,)), ((), ())), preferred_element_type=F32)


def _attn_kernel(sink_ref, qa_ref, ka_ref, va_ref, qb_ref, kb_ref, vb_ref, bias_ref, oa_ref, ob_ref,
                 *, n_c, mask_prefix):
    qi = pl.program_id(1)
    lane = lax.broadcasted_iota(I32, (CHUNK, LANES), 1)
    low = lane < HEAD_DIM
    half_mask = (jnp.where(low, 1.0, 0.0).astype(BF16), jnp.where(low, 0.0, 1.0).astype(BF16))
    n_pairs = B_W // LANES
    rows_b = n_pairs * 2 * CHUNK
    rows_a = A_Q_HEADS * CHUNK
    key_b = lax.broadcasted_iota(I32, (rows_b, BAND_B), 1)
    key_a = lax.broadcasted_iota(I32, (rows_a, BAND_A), 1)
    row_a = lax.broadcasted_iota(I32, (rows_a, 1), 0) // CHUNK

    sink_col = jnp.zeros((rows_a, 1), F32)
    for r in range(A_Q_HEADS):
        sink_col = jnp.where(row_a == r, sink_ref[r], sink_col)

    for c in range(n_c):
        ci = qi * n_c + c
        row0 = pl.multiple_of(ci * CHUNK, CHUNK)
        rows = slice(c * CHUNK, (c + 1) * CHUNK)
        first_valid = PAD - ci * CHUNK

        s_parts = []
        for j in range(n_pairs):
            cg = slice(LANES * j, LANES * (j + 1))
            q = qb_ref[0, rows, cg]
            qs = jnp.concatenate([q * half_mask[0], q * half_mask[1]], axis=0)
            k = kb_ref[0, pl.ds(row0, BAND_B), cg]
            s_parts.append(_nt_dot(qs, k) + bias_ref[j])
        s = jnp.concatenate(s_parts, axis=0)
        if mask_prefix:
            s = jnp.where(key_b >= first_valid, s, NEG_INF)
        mx = jnp.max(s, axis=-1, keepdims=True)
        e = jnp.exp(s - mx)
        inv = 1.0 / jnp.sum(e, axis=-1, keepdims=True)
        p_b = e.astype(BF16)
        for j in range(n_pairs):
            cg = slice(LANES * j, LANES * (j + 1))
            v = vb_ref[0, pl.ds(row0, BAND_B), cg]
            rj = slice(2 * CHUNK * j, 2 * CHUNK * (j + 1))
            o = jnp.dot(p_b[rj], v, preferred_element_type=F32) * inv[rj]
            ob_ref[0, rows, cg] = jnp.where(low, o[:CHUNK], o[CHUNK:]).astype(ob_ref.dtype)

        row_a0 = pl.multiple_of(row0 + (WINDOW_B - WINDOW_A), CHUNK)
        k = ka_ref[0, pl.ds(row_a0, BAND_A), :]
        v = va_ref[0, pl.ds(row_a0, BAND_A), :]
        q = jnp.concatenate(
            [qa_ref[0, rows, LANES * j:LANES * (j + 1)] * half_mask[p]
             for p in range(A_KV_HEADS) for j in range(A_GROUP)], axis=0)
        s = _nt_dot(q, k)
        if mask_prefix:
            s = jnp.where(key_a >= first_valid - (WINDOW_B - WINDOW_A), s, NEG_INF)
        mx = jnp.maximum(jnp.max(s, axis=-1, keepdims=True), sink_col)
        e = jnp.exp(s - mx)
        inv = 1.0 / (jnp.sum(e, axis=-1, keepdims=True) + jnp.exp(sink_col - mx))
        o = jnp.dot(e.astype(BF16), v, preferred_element_type=F32) * inv
        for j in range(A_GROUP):
            lo = o[j * CHUNK:(j + 1) * CHUNK]
            hi = o[(A_GROUP + j) * CHUNK:(A_GROUP + j + 1) * CHUNK]
            oa_ref[0, rows, LANES * j:LANES * (j + 1)] = jnp.where(low, lo, hi).astype(oa_ref.dtype)


def _attention(qa, ka, va, qb, kb, vb, sinks, bias, *, n_c, mask_prefix):
    bsz, length, _ = qa.shape
    lp = ka.shape[1]
    qt = n_c * CHUNK
    assert length % qt == 0 and lp == length + PAD

    def qmap(b, i):
        return (b, i, 0)

    def kmap(b, i):
        return (b, 0, 0)

    kern = functools.partial(_attn_kernel, n_c=n_c, mask_prefix=mask_prefix)
    return pl.pallas_call(
        kern,
        grid=(bsz, length // qt),
        in_specs=[
            pl.BlockSpec(memory_space=pltpu.SMEM),
            pl.BlockSpec((1, qt, A_Q_W), qmap),
            pl.BlockSpec((1, lp, A_KV_W), kmap),
            pl.BlockSpec((1, lp, A_KV_W), kmap),
            pl.BlockSpec((1, qt, B_W), qmap),
            pl.BlockSpec((1, lp, B_W), kmap),
            pl.BlockSpec((1, lp, B_W), kmap),
            pl.BlockSpec((B_HEADS // 2, 2 * CHUNK, BAND_B), lambda b, i: (0, 0, 0)),
        ],
        out_specs=(pl.BlockSpec((1, qt, A_Q_W), qmap), pl.BlockSpec((1, qt, B_W), qmap)),
        out_shape=(jax.ShapeDtypeStruct((bsz, length, A_Q_W), BF16),
                   jax.ShapeDtypeStruct((bsz, length, B_W), BF16)),
        compiler_params=_cparams("arbitrary", "arbitrary"),
        name="attention",
    )(sinks, qa, ka, va, qb, kb, vb, bias)


def _merge_kernel(x_ref, oa_ref, ob_ref, sha_ref, sca_ref, gta_ref, shm_ref, scm_ref, ga_ref, gm_ref,
                  wg_ref, wa_ref, wb_ref, wo_ref, wrh_ref, wrl_ref, br_ref,
                  x1_ref, h2_ref, ri_ref, rw_ref, cnt_ref, carry_ref, logit_ref, *, bb, tl, seg_tiles):
    m = bb * tl
    step = pl.program_id(0)

    @pl.when(step == 0)
    def _():
        carry_ref[...] = jnp.zeros_like(carry_ref)
        logit_ref[...] = jnp.zeros_like(logit_ref)

    logits = logit_ref[...]
    lane = lax.broadcasted_iota(I32, (m, LANES), 1)
    lane_f = lane.astype(F32)
    work = logits
    top_v, top_i = [], []
    sel_f = jnp.zeros((m, LANES), F32)
    for _ in range(TOP_K):
        mx = jnp.max(work, axis=-1, keepdims=True)
        idx = jnp.min(jnp.where(work == mx, lane_f, float(LANES)), axis=-1, keepdims=True)
        hit = lane_f == idx
        top_v.append(mx)
        top_i.append(idx)
        sel_f = jnp.where(hit, 1.0, sel_f)
        work = jnp.where(hit, -jnp.inf, work)
    ex = [jnp.exp(v - top_v[0]) for v in top_v]
    inv = 1.0 / (ex[0] + ex[1] + ex[2] + ex[3])

    r_i = lax.broadcasted_iota(I32, (m, m), 0)
    c_i = lax.broadcasted_iota(I32, (m, m), 1)
    before = jnp.where(r_i > c_i, 1.0, 0.0).astype(BF16)
    keep = jnp.where(jnp.maximum(step - 1, 0) % seg_tiles == 0, 0.0, 1.0)
    carry_in = carry_ref[...] * keep
    pos = jnp.dot(before, sel_f.astype(BF16), preferred_element_type=F32) + carry_in
    live = jnp.where(step > 0, 1.0, 0.0)
    carry = carry_in + live * jnp.sum(sel_f, axis=0, keepdims=True)
    carry_ref[...] = carry
    cnt_ref[0] = carry

    ri_f = jnp.zeros((m, LANES), F32)
    rw = jnp.zeros((m, LANES), F32)
    for k in range(TOP_K):
        rank = jnp.sum(jnp.where(lane_f == top_i[k], pos, 0.0), axis=-1, keepdims=True)
        ri_f = jnp.where(lane == k, top_i[k], ri_f)
        ri_f = jnp.where(lane == TOP_K + k, rank, ri_f)
        rw = jnp.where(lane == k, ex[k] * inv, rw)
    ri_ref[...] = ri_f.astype(I32).reshape(bb, tl, LANES)
    rw_ref[...] = rw.reshape(bb, tl, LANES)

    x = x_ref[...]
    h = _modulated_norm(x, ga_ref[...], sca_ref[...], sha_ref[...])
    hb = h.reshape(m, D_MODEL).astype(BF16)
    gates = jnp.dot(hb, wg_ref[...], preferred_element_type=F32)
    ya = jnp.dot(oa_ref[...].reshape(m, A_Q_W), wa_ref[...], preferred_element_type=F32)
    yb = jnp.dot(ob_ref[...].reshape(m, B_W), wb_ref[...], preferred_element_type=F32)
    mix = jax.nn.sigmoid(gates[:, :D_MODEL]) * ya + jax.nn.sigmoid(gates[:, D_MODEL:]) * yb
    z = jnp.dot(mix.astype(BF16), wo_ref[...], preferred_element_type=F32)
    x1 = x + gta_ref[...] * z.reshape(bb, tl, D_MODEL)
    x1_ref[...] = x1

    h2 = _modulated_norm(x1, gm_ref[...], scm_ref[...], shm_ref[...]).reshape(m, D_MODEL)
    for j in range(ROW_TILES):
        h2_ref[pl.ds(j, m, stride=ROW_TILES), :] = h2[:, LANES * j:LANES * (j + 1)]

    h2_hi = h2.astype(BF16)
    h2_lo = (h2 - h2_hi.astype(F32)).astype(BF16)
    wrh = wrh_ref[...]
    logit_ref[...] = (jnp.dot(h2_hi, wrh, preferred_element_type=F32)
                      + jnp.dot(h2_lo, wrh, preferred_element_type=F32)
                      + jnp.dot(h2_hi, wrl_ref[...], preferred_element_type=F32)) + br_ref[...]


def _merge(x, oa, ob, mods, g_attn, g_mlp, wg, wa, wb, wo, wr, br, *, bb, tl, n_seg):
    bsz, length, _ = x.shape
    n_tok = bsz * length
    m = bb * tl
    assert bb == 1 or tl == length
    sh_a, sc_a, gt_a, sh_m, sc_m = mods
    n_l = length // tl
    n_tiles = (bsz // bb) * n_l
    assert n_tiles % n_seg == 0
    seg_tiles = n_tiles // n_seg
    wr_hi = wr.astype(BF16)
    wr_lo = (wr - wr_hi.astype(F32)).astype(BF16)

    def cur(s):
        return jnp.minimum(s, n_tiles - 1)

    def prev(s):
        return jnp.maximum(s - 1, 0)

    def xmap(s):
        return (cur(s) // n_l, cur(s) % n_l, 0)

    def modmap(s):
        return (cur(s) // n_l, 0, 0)

    def const2(s):
        return (0, 0)

    def flatmap(s):
        return (cur(s), 0)

    def routemap(s):
        return (prev(s) // n_l, prev(s) % n_l, 0)

    modspec = pl.BlockSpec((bb, 1, D_MODEL), modmap)
    kern = functools.partial(_merge_kernel, bb=bb, tl=tl, seg_tiles=seg_tiles)
    return pl.pallas_call(
        kern,
        grid=(n_tiles + 1,),
        in_specs=[
            pl.BlockSpec((bb, tl, D_MODEL), xmap),
            pl.BlockSpec((bb, tl, A_Q_W), xmap),
            pl.BlockSpec((bb, tl, B_W), xmap),
            modspec, modspec, modspec, modspec, modspec,
            pl.BlockSpec((1, D_MODEL), const2),
            pl.BlockSpec((1, D_MODEL), const2),
            pl.BlockSpec((D_MODEL, 2 * D_MODEL), const2),
            pl.BlockSpec((A_Q_W, D_MODEL), const2),
            pl.BlockSpec((B_W, D_MODEL), const2),
            pl.BlockSpec((D_MODEL, D_MODEL), const2),
            pl.BlockSpec((D_MODEL, LANES), const2),
            pl.BlockSpec((D_MODEL, LANES), const2),
            pl.BlockSpec((1, LANES), const2),
        ],
        out_specs=(
            pl.BlockSpec((bb, tl, D_MODEL), xmap),
            pl.BlockSpec((m * ROW_TILES, LANES), flatmap),
            pl.BlockSpec((bb, tl, LANES), routemap),
            pl.BlockSpec((bb, tl, LANES), routemap),
            pl.BlockSpec((1, 1, LANES), lambda s: (prev(s) // seg_tiles, 0, 0)),
        ),
        out_shape=(
            jax.ShapeDtypeStruct((bsz, length, D_MODEL), F32),
            jax.ShapeDtypeStruct((n_tok * ROW_TILES, LANES), F32),
            jax.ShapeDtypeStruct((bsz, length, LANES), I32),
            jax.ShapeDtypeStruct((bsz, length, LANES), F32),
            jax.ShapeDtypeStruct((n_seg, 1, LANES), F32),
        ),
        scratch_shapes=[pltpu.VMEM((1, LANES), F32), pltpu.VMEM((m, LANES), F32)],
        compiler_params=_cparams("arbitrary"),
        name="merge",
    )(x, oa, ob, sh_a, sc_a, gt_a, sh_m, sc_m, g_attn.reshape(1, D_MODEL), g_mlp.reshape(1, D_MODEL),
      wg, wa, wb, wo, wr_hi, wr_lo, br)


def _sc_mesh():
    return plsc.VectorSubcoreMesh(core_axis_name="c", subcore_axis_name="s",
                                  num_cores=SC_CORES, num_subcores=SC_SUBCORES)


def _sc_worker():
    return lax.axis_index("c") * SC_SUBCORES + lax.axis_index("s")


def _sc_params():
    return pltpu.CompilerParams(use_tc_tiling_on_sc=True)


def _dispatch(h2_rows, idx_flat, pad_flat, n_rows, tok0, n_tok):
    per_w = n_tok // SC_WORKERS
    n_chunks = per_w // SC_ROWS
    n_idx = n_chunks * TOP_K * SC_ROWS
    n_pad = pad_flat.shape[0] // SC_WORKERS
    pad_chunks = n_pad // SC_ROWS
    assert per_w % (2 * SC_ROWS) == 0 and n_pad % SC_ROWS == 0
    zeros = jnp.zeros((SC_ROWS, ROW_TILES, LANES), F32)

    def body(h2_hbm, idx_hbm, pad_hbm, zeros_hbm, xs_hbm, idx_v, pad_v, buf0, buf1, ls0, ls1, ss0, ss1):
        wid = _sc_worker()
        bufs, lsems, ssems = (buf0, buf1), (ls0, ls1), (ss0, ss1)
        pltpu.sync_copy(idx_hbm.at[pl.ds(pl.multiple_of(wid * n_idx, SC_ROWS), n_idx)], idx_v)
        pltpu.sync_copy(pad_hbm.at[pl.ds(pl.multiple_of(wid * n_pad, SC_ROWS), n_pad)], pad_v)

        pltpu.sync_copy(zeros_hbm, buf0)
        for c in range(pad_chunks):
            pltpu.async_copy(buf0, xs_hbm.at[pad_v.at[pl.ds(c * SC_ROWS, SC_ROWS)]], ss0)
        for c in range(pad_chunks):
            pltpu.make_async_copy(buf0, xs_hbm.at[pad_v.at[pl.ds(c * SC_ROWS, SC_ROWS)]], ss0).wait()

        base = tok0 + wid * per_w

        def load(g, b):
            src = h2_hbm.at[pl.ds(pl.multiple_of(base + g * SC_ROWS, SC_ROWS), SC_ROWS)]
            return pltpu.make_async_copy(src, bufs[b], lsems[b])

        def scatter(g, k, b):
            off = pl.multiple_of((g * TOP_K + k) * SC_ROWS, SC_ROWS)
            return pltpu.make_async_copy(bufs[b], xs_hbm.at[idx_v.at[pl.ds(off, SC_ROWS)]], ssems[b])

        load(0, 0).start()

        @pl.loop(0, n_chunks, step=2)
        def _(g0):
            for b in range(2):
                g = g0 + b
                load(g, b).wait()

                @pl.when(g >= 1)
                def _():
                    for k in range(TOP_K):
                        scatter(g - 1, k, 1 - b).wait()

                @pl.when(g + 1 < n_chunks)
                def _():
                    load(g + 1, 1 - b).start()

                for k in range(TOP_K):
                    scatter(g, k, b).start()

        for k in range(TOP_K):
            scatter(n_chunks - 1, k, (n_chunks - 1) % 2).wait()

    call = pl.kernel(
        body,
        out_type=jax.ShapeDtypeStruct((n_rows + TRASH_ROWS, ROW_TILES, LANES), F32),
        mesh=_sc_mesh(),
        scratch_types=[pltpu.VMEM((n_idx,), I32), pltpu.VMEM((n_pad,), I32),
                       pltpu.VMEM((SC_ROWS, ROW_TILES, LANES), F32), pltpu.VMEM((SC_ROWS, ROW_TILES, LANES), F32),
                       pltpu.SemaphoreType.DMA, pltpu.SemaphoreType.DMA,
                       pltpu.SemaphoreType.DMA, pltpu.SemaphoreType.DMA],
        compiler_params=_sc_params(),
        name="dispatch",
    )
    return call(h2_rows, idx_flat, pad_flat, zeros)


def _gather_back(y_rows, idx_flat, n_tok):
    per_w = n_tok // SC_WORKERS
    n_chunks = per_w // SC_ROWS
    n_items = n_chunks * TOP_K
    n_idx = n_items * SC_ROWS
    assert per_w % SC_ROWS == 0 and n_items % 2 == 0

    def body(y_hbm, idx_hbm, yg_hbm, idx_v, buf0, buf1, gs0, gs1, ws0, ws1):
        wid = _sc_worker()
        bufs, gsems, wsems = (buf0, buf1), (gs0, gs1), (ws0, ws1)
        pltpu.sync_copy(idx_hbm.at[pl.ds(pl.multiple_of(wid * n_idx, SC_ROWS), n_idx)], idx_v)
        base = wid * per_w

        def gather(i, b):
            off = pl.multiple_of(i * SC_ROWS, SC_ROWS)
            return pltpu.make_async_copy(y_hbm.at[idx_v.at[pl.ds(off, SC_ROWS)]], bufs[b], gsems[b])

        def write(i, b):
            g, k = i // TOP_K, i % TOP_K
            dst = yg_hbm.at[k, pl.ds(pl.multiple_of(base + g * SC_ROWS, SC_ROWS), SC_ROWS)]
            return pltpu.make_async_copy(bufs[b], dst, wsems[b])

        gather(0, 0).start()

        @pl.loop(0, n_items, step=2)
        def _(i0):
            for b in range(2):
                i = i0 + b
                gather(i, b).wait()

                @pl.when(i >= 1)
                def _():
                    write(i - 1, 1 - b).wait()

                @pl.when(i + 1 < n_items)
                def _():
                    gather(i + 1, 1 - b).start()

                write(i, b).start()

        write(n_items - 1, (n_items - 1) % 2).wait()

    call = pl.kernel(
        body,
        out_type=jax.ShapeDtypeStruct((TOP_K, n_tok, ROW_TILES, LANES), F32),
        mesh=_sc_mesh(),
        scratch_types=[pltpu.VMEM((n_idx,), I32),
                       pltpu.VMEM((SC_ROWS, ROW_TILES, LANES), F32), pltpu.VMEM((SC_ROWS, ROW_TILES, LANES), F32),
                       pltpu.SemaphoreType.DMA, pltpu.SemaphoreType.DMA,
                       pltpu.SemaphoreType.DMA, pltpu.SemaphoreType.DMA],
        compiler_params=_sc_params(),
        name="gather_back",
    )
    return call(y_rows, idx_flat)


def _expert_kernel(blk_ref, used_ref, xs_ref, wgu_ref, bgu_ref, wd_ref, bd_ref, y_ref, wgu_bf, wd_bf):
    r = pl.program_id(0)
    e = blk_ref[r]
    prev = blk_ref[jnp.maximum(r - 1, 0)]
    active = r < used_ref[0]

    @pl.when(jnp.logical_and(active, jnp.logical_or(r == 0, e != prev)))
    def _():
        wgu_bf[...] = wgu_ref[0].astype(BF16)
        wd_bf[...] = wd_ref[0].astype(BF16)

    @pl.when(active)
    def _():
        x = jnp.concatenate(
            [xs_ref[pl.ds(j, ROUTE_BLOCK, stride=ROW_TILES), :] for j in range(ROW_TILES)], axis=1).astype(BF16)
        acts = []
        for j in range(D_FF // FF_TILE):
            gc = slice(FF_TILE * j, FF_TILE * (j + 1))
            uc = slice(D_FF + FF_TILE * j, D_FF + FF_TILE * (j + 1))
            g = jnp.dot(x, wgu_bf[:, gc], preferred_element_type=F32) + bgu_ref[0, :, gc]
            u = jnp.dot(x, wgu_bf[:, uc], preferred_element_type=F32) + bgu_ref[0, :, uc]
            gate = jnp.minimum(g, SWIGLU_LIMIT)
            up = jnp.clip(u, -SWIGLU_LIMIT, SWIGLU_LIMIT)
            glu = gate * jax.nn.sigmoid(gate * SWIGLU_ALPHA)
            acts.append(((up + 1.0) * glu).astype(BF16))
        act = jnp.concatenate(acts, axis=1)
        y = jnp.dot(act, wd_bf[...], preferred_element_type=F32) + bd_ref[0]
        for j in range(ROW_TILES):
            y_ref[pl.ds(j, ROUTE_BLOCK, stride=ROW_TILES), :] = y[:, LANES * j:LANES * (j + 1)]


def _experts(xs_rows, blk_expert, n_used, w_gate_up, b_gate_up, w_down, b_down):
    n_blocks = blk_expert.shape[0]
    rb = ROUTE_BLOCK * ROW_TILES

    def rowmap(r, blk, used):
        return (jnp.minimum(r, used[0] - 1), 0)

    def wmap(r, blk, used):
        return (blk[jnp.minimum(r, used[0] - 1)], 0, 0)

    return pl.pallas_call(
        _expert_kernel,
        grid_spec=pltpu.PrefetchScalarGridSpec(
            num_scalar_prefetch=2,
            grid=(n_blocks,),
            in_specs=[
                pl.BlockSpec((rb, LANES), rowmap),
                pl.BlockSpec((1, D_MODEL, 2 * D_FF), wmap),
                pl.BlockSpec((1, 1, 2 * D_FF), wmap),
                pl.BlockSpec((1, D_FF, D_MODEL), wmap),
                pl.BlockSpec((1, 1, D_MODEL), wmap),
            ],
            out_specs=pl.BlockSpec((rb, LANES), rowmap),
            scratch_shapes=[pltpu.VMEM((D_MODEL, 2 * D_FF), BF16), pltpu.VMEM((D_FF, D_MODEL), BF16)],
        ),
        out_shape=jax.ShapeDtypeStruct(xs_rows.shape, F32),
        compiler_params=_cparams("arbitrary"),
        name="experts",
    )(blk_expert, n_used, xs_rows, w_gate_up, b_gate_up.reshape(N_EXPERTS, 1, 2 * D_FF),
      w_down, b_down.reshape(N_EXPERTS, 1, D_MODEL))


def _combine_kernel(x1_ref, gt_ref, rw_ref, yg_ref, *rest, bb, tl):
    o_ref = rest[-1]
    m = bb * tl
    rw = rw_ref[...].reshape(m, LANES)
    w = [rw[:, k:k + 1] for k in range(TOP_K)]
    gt = gt_ref[...]
    for j in range(ROW_TILES):
        acc = jnp.zeros((m, LANES), F32)
        for k in range(TOP_K):
            acc = acc + w[k] * yg_ref[k, pl.ds(j, m, stride=ROW_TILES), :]
        cg = slice(LANES * j, LANES * (j + 1))
        o_ref[:, :, cg] = x1_ref[:, :, cg] + gt[:, :, cg] * acc.reshape(bb, tl, LANES)


def _combine(x1, gt_m, rw, yg, y_prev, *, bb, tl, b0, nb):
    bsz, length, _ = x1.shape
    m = bb * tl
    assert bb == 1 or tl == length
    n_l = length // tl
    blk0 = b0 // bb

    def xmap(b, i):
        return (blk0 + b, i, 0)

    in_specs = [
        pl.BlockSpec((bb, tl, D_MODEL), xmap),
        pl.BlockSpec((bb, 1, D_MODEL), lambda b, i: (blk0 + b, 0, 0)),
        pl.BlockSpec((bb, tl, LANES), xmap),
        pl.BlockSpec((TOP_K, m * ROW_TILES, LANES), lambda b, i: (0, b * n_l + i, 0)),
    ]
    args = [x1, gt_m, rw, yg]
    aliases = {}
    if y_prev is not None:
        in_specs.append(pl.BlockSpec(memory_space=pl.ANY))
        args.append(y_prev)
        aliases = {len(args) - 1: 0}
    kern = functools.partial(_combine_kernel, bb=bb, tl=tl)
    return pl.pallas_call(
        kern,
        grid=(nb // bb, n_l),
        in_specs=in_specs,
        out_specs=pl.BlockSpec((bb, tl, D_MODEL), xmap),
        out_shape=jax.ShapeDtypeStruct(x1.shape, F32),
        input_output_aliases=aliases,
        compiler_params=_cparams("arbitrary", "arbitrary"),
        name="combine",
    )(*args)


def _route_plan(ri, counts_f, n_tok):
    idx = ri[..., :TOP_K].reshape(n_tok, TOP_K)
    rank = ri[..., TOP_K:2 * TOP_K].reshape(n_tok, TOP_K)
    counts = counts_f.reshape(-1)[:N_EXPERTS].astype(I32)
    padded = (counts + ROUTE_BLOCK - 1) // ROUTE_BLOCK * ROUTE_BLOCK
    pends = jnp.cumsum(padded)
    pstarts = pends - padded
    expert_ids = jnp.arange(N_EXPERTS, dtype=I32)
    dest = jnp.sum(jnp.where(idx[..., None] == expert_ids, pstarts, 0), axis=-1) + rank
    n_blocks = n_tok * TOP_K // ROUTE_BLOCK + N_EXPERTS
    blk_start = jnp.arange(n_blocks, dtype=I32) * ROUTE_BLOCK
    blk_expert = jnp.minimum(jnp.sum((pends[None, :] <= blk_start[:, None]).astype(I32), axis=1), N_EXPERTS - 1)
    n_used = (pends[-1] // ROUTE_BLOCK).reshape(1)
    n_rows = n_blocks * ROUTE_BLOCK
    per_w = n_tok // SC_WORKERS
    idx_flat = dest.astype(I32).reshape(SC_WORKERS, per_w // SC_ROWS, SC_ROWS, TOP_K)
    idx_flat = idx_flat.transpose(0, 1, 3, 2).reshape(-1)
    r = jnp.arange(ROUTE_BLOCK, dtype=I32)[None, :]
    pad_flat = jnp.where(r < (padded - counts)[:, None], (pstarts + counts)[:, None] + r, n_rows)
    return idx_flat, pad_flat.astype(I32).reshape(-1), blk_expert.astype(I32), n_used.astype(I32), n_rows


def _layer(x, mods, pos, prefix, weights, *, bb, tl, n_c, tl_combine, bb_combine, n_seg):
    (g_attn, g_mlp, w_qkv, gains, sinks, bias, wg, wa, wb, wo, wr, br,
     w_gate_up, b_gate_up, w_down, b_down) = weights
    bsz, length, _ = x.shape
    n_tok = bsz * length
    sh_a, sc_a, gt_a, sh_m, sc_m, gt_m = mods
    pad = 0 if prefix is not None else PAD
    qa, ka, va, qb, kb, vb, ska, sva, skb, svb = _stage1(
        x, sh_a, sc_a, g_attn, w_qkv, gains, pos, bb=bb, tl=tl, pad=pad)
    if prefix is not None:
        ka, va, kb, vb = [jnp.concatenate([p, t], axis=1) for p, t in zip(prefix, (ka, va, kb, vb))]
    oa, ob = _attention(qa, ka, va, qb, kb, vb, sinks, bias, n_c=n_c, mask_prefix=prefix is None)
    x1, h2_rows, ri, rw, counts_f = _merge(x, oa, ob, (sh_a, sc_a, gt_a, sh_m, sc_m), g_attn, g_mlp,
                                           wg, wa, wb, wo, wr, br, bb=bb, tl=tl, n_seg=n_seg)
    h2_rows = h2_rows.reshape(n_tok, ROW_TILES, LANES)
    nb = bsz // n_seg
    seg_tok = nb * length
    y = None
    for s in range(n_seg):
        idx_flat, pad_flat, blk_expert, n_used, n_rows = _route_plan(
            ri[s * nb:(s + 1) * nb], counts_f[s], seg_tok)
        xs = _dispatch(h2_rows, idx_flat, pad_flat, n_rows, s * seg_tok, seg_tok)
        y_rows = _experts(xs.reshape(-1, LANES), blk_expert, n_used, w_gate_up, b_gate_up, w_down, b_down)
        yg = _gather_back(y_rows.reshape(-1, ROW_TILES, LANES), idx_flat, seg_tok)
        y = _combine(x1, gt_m, rw, yg.reshape(TOP_K, seg_tok * ROW_TILES, LANES), y,
                     bb=bb_combine, tl=tl_combine, b0=s * nb, nb=nb)
    return y, (ska, sva, skb, svb)


def _rel_bias_table(rel_table):
    q_pos = jnp.arange(CHUNK)
    k_pos = jnp.arange(BAND_B) - WINDOW_B
    idx = jnp.clip(q_pos[:, None] - k_pos[None, :], -REL_FWD, REL_BACK) + REL_FWD
    return rel_table[:, idx].astype(F32).reshape(B_HEADS // 2, 2 * CHUNK, BAND_B)


def _prep_weights(g_attn, g_mlp, w_in, q_norm_a, k_norm_a, q_norm_b, k_norm_b, sinks_a, rel_table_b,
                  w_branch_a, w_branch_b, w_out, w_router, b_router, w_gate_up, b_gate_up, w_down, b_down):
    order = jnp.asarray(A_HEAD_ORDER)
    w_qa = w_in[:, :A_Q_W].reshape(D_MODEL, A_Q_HEADS, HEAD_DIM)[:, order].reshape(D_MODEL, A_Q_W)
    w_qkv = jnp.concatenate([w_qa, w_in[:, A_Q_W:QKV_W]], axis=1).astype(BF16)
    wg = w_in[:, QKV_W:].astype(BF16)
    wa = w_branch_a.reshape(A_Q_HEADS, HEAD_DIM, D_MODEL)[order].reshape(A_Q_W, D_MODEL).astype(BF16)
    wb = w_branch_b.astype(BF16)
    wo = w_out.astype(BF16)
    wr = jnp.zeros((D_MODEL, LANES), F32).at[:, :N_EXPERTS].set(w_router)
    br = jnp.full((1, LANES), NEG_INF, F32).at[0, :N_EXPERTS].set(b_router)
    bias = _rel_bias_table(rel_table_b)
    gains = (q_norm_a, k_norm_a, q_norm_b, k_norm_b)
    return (g_attn, g_mlp, w_qkv, gains, sinks_a.astype(F32), bias, wg, wa, wb, wo, wr, br,
            w_gate_up, b_gate_up, w_down, b_down)


def _forward(x_prompt, x_sample, c_prompt, c_sample, cache_a_k, cache_a_v, cache_b_k, cache_b_v, params,
             past_len, cfg_prompt, cfg_sample):
    (g_attn, g_mlp, w_ada, b_ada, w_in, q_norm_a, k_norm_a, q_norm_b, k_norm_b, sinks_a, rel_table_b,
     w_branch_a, w_branch_b, w_out, w_router, b_router, w_gate_up, b_gate_up, w_down, b_down) = params
    depth = g_attn.shape[0]
    bp, lp_, _ = x_prompt.shape
    bs, ls, _ = x_sample.shape
    pos_p = jnp.arange(lp_, dtype=I32)
    pos_s = past_len + jnp.arange(ls, dtype=I32)
    yp, ys = x_prompt, x_sample
    st_p, st_s = [], []
    for l in range(depth):
        weights = _prep_weights(g_attn[l], g_mlp[l], w_in[l], q_norm_a[l], k_norm_a[l], q_norm_b[l],
                                k_norm_b[l], sinks_a[l], rel_table_b[l], w_branch_a[l], w_branch_b[l],
                                w_out[l], w_router[l], b_router[l], w_gate_up[l], b_gate_up[l],
                                w_down[l], b_down[l])
        mod = _adaln(jnp.concatenate([c_prompt, c_sample], axis=0), w_ada[l], b_ada[l])
        mod = mod.reshape(bp + bs, 6, 1, D_MODEL)
        mods_p = [mod[:bp, k] for k in range(6)]
        mods_s = [mod[bp:, k] for k in range(6)]
        yp, sp = _layer(yp, mods_p, pos_p, None, weights, **cfg_prompt)
        wa_c = cache_a_k.shape[2]
        wb_c = cache_b_k.shape[2]
        assert wb_c == WINDOW_B and wa_c == WINDOW_A
        zeros_a = jnp.zeros((bs, PAD - wa_c, A_KV_W), BF16)
        prefix = (
            jnp.concatenate([zeros_a, cache_a_k[l].reshape(bs, wa_c, A_KV_W).astype(BF16)], axis=1),
            jnp.concatenate([zeros_a, cache_a_v[l].reshape(bs, wa_c, A_KV_W).astype(BF16)], axis=1),
            cache_b_k[l].reshape(bs, wb_c, B_W).astype(BF16),
            cache_b_v[l].reshape(bs, wb_c, B_W).astype(BF16),
        )
        ys, ss = _layer(ys, mods_s, pos_s, prefix, weights, **cfg_sample)
        st_p.append(sp)
        st_s.append(ss)

    def stack(states, k, heads):
        return jnp.stack([s[k].reshape(s[k].shape[0], s[k].shape[1], heads, HEAD_DIM) for s in states])

    return (yp, ys,
            stack(st_p, 0, A_KV_HEADS), stack(st_p, 1, A_KV_HEADS), stack(st_p, 2, B_HEADS), stack(st_p, 3, B_HEADS),
            stack(st_s, 0, A_KV_HEADS), stack(st_s, 1, A_KV_HEADS), stack(st_s, 2, B_HEADS), stack(st_s, 3, B_HEADS))


PAST_LEN = 2048
CFG_PROMPT = dict(bb=1, tl=512, n_c=4, tl_combine=512, bb_combine=1, n_seg=2)
CFG_SAMPLE = dict(bb=8, tl=64, n_c=1, tl_combine=64, bb_combine=8, n_seg=1)


def kernel(x_prompt, x_sample, c_prompt, c_sample, cache_a_k, cache_a_v, cache_b_k, cache_b_v, g_attn, g_mlp,
           w_ada, b_ada, w_in, q_norm_a, k_norm_a, q_norm_b, k_norm_b, sinks_a, rel_table_b, w_branch_a,
           w_branch_b, w_out, w_router, b_router, w_gate_up, b_gate_up, w_down, b_down):
    params = (g_attn, g_mlp, w_ada, b_ada, w_in, q_norm_a, k_norm_a, q_norm_b, k_norm_b, sinks_a, rel_table_b,
              w_branch_a, w_branch_b, w_out, w_router, b_router, w_gate_up, b_gate_up, w_down, b_down)
    return _forward(x_prompt, x_sample, c_prompt, c_sample, cache_a_k, cache_a_v, cache_b_k, cache_b_v,
                    params, PAST_LEN, CFG_PROMPT, CFG_SAMPLE)
```

```python
import functools

import jax
import jax.numpy as jnp
from jax import lax
from jax.experimental import pallas as pl
from jax.experimental.pallas import tpu as pltpu
from jax.experimental.pallas import tpu_sc as plsc

F32 = jnp.float32
BF16 = jnp.bfloat16
I32 = jnp.int32

D_MODEL = 1024
CHUNK = 64
HEAD_DIM = 64
LANES = 128
SUBLANES = 8
A_Q_HEADS = 8
A_KV_HEADS = 2
A_GROUP = A_Q_HEADS // A_KV_HEADS
B_HEADS = 8
A_Q_W = A_Q_HEADS * HEAD_DIM
A_KV_W = A_KV_HEADS * HEAD_DIM
B_W = B_HEADS * HEAD_DIM
QKV_W = A_Q_W + 2 * A_KV_W + 3 * B_W
WINDOW_A = 128
WINDOW_B = 512
BAND_A = WINDOW_A + CHUNK
BAND_B = WINDOW_B + CHUNK
PAD = WINDOW_B
ROT_DIM = HEAD_DIM // 4
ROPE_THETA = 500000.0
REL_BACK = 4 * CHUNK
REL_FWD = CHUNK - 1
N_EXPERTS = 32
TOP_K = 4
D_FF = D_MODEL
SWIGLU_LIMIT = 7.0
SWIGLU_ALPHA = 1.702
ROUTE_BLOCK = 512
FF_TILE = 256
RMS_EPS = 1e-6
NEG_INF = -1e30
ROW_TILES = D_MODEL // LANES
VMEM_LIMIT = 56 * 1024 * 1024
SC_CORES = 2
SC_SUBCORES = 16
SC_WORKERS = SC_CORES * SC_SUBCORES
SC_ROWS = 32
TRASH_ROWS = 8

A_HEAD_ORDER = (0, 4, 1, 5, 2, 6, 3, 7)


def _cparams(*sem):
    return pltpu.CompilerParams(dimension_semantics=sem, vmem_limit_bytes=VMEM_LIMIT)


def _adaln_kernel(c_ref, w_ref, b_ref, o_ref):
    c = c_ref[...]
    a = c * jax.nn.sigmoid(c)
    o_ref[...] = jnp.dot(a, w_ref[...], preferred_element_type=F32,
                         precision=lax.Precision.HIGHEST) + b_ref[...]


def _adaln(c, w_ada, b_ada):
    n_rows = c.shape[0]
    n_out = w_ada.shape[1]
    tn = 1024
    return pl.pallas_call(
        _adaln_kernel,
        grid=(n_out // tn,),
        in_specs=[pl.BlockSpec((n_rows, D_MODEL), lambda j: (0, 0)),
                  pl.BlockSpec((D_MODEL, tn), lambda j: (0, j)),
                  pl.BlockSpec((1, tn), lambda j: (0, j))],
        out_specs=pl.BlockSpec((n_rows, tn), lambda j: (0, j)),
        out_shape=jax.ShapeDtypeStruct((n_rows, n_out), F32),
        compiler_params=_cparams("arbitrary"),
        name="adaln",
    )(c, w_ada, b_ada.reshape(1, n_out))


def _modulated_norm(x, g, scale, shift):
    ms = jnp.mean(x * x, axis=-1, keepdims=True)
    return (x * lax.rsqrt(ms + RMS_EPS) * g) * (1.0 + scale) + shift


def _head_norm(parts, gain, seg):
    m = parts[0].shape[0]
    sq = jnp.concatenate([p * p for p in parts], axis=0).astype(BF16)
    ms = jnp.dot(sq, seg, preferred_element_type=F32)
    r = lax.rsqrt(ms + RMS_EPS)
    return [parts[j] * r[j * m:(j + 1) * m] * gain for j in range(len(parts))]


def _rope(p, cos, sin_lo, sin_hi):
    return p * cos + pltpu.roll(p, LANES - ROT_DIM // 2, 1) * sin_lo + pltpu.roll(p, ROT_DIM // 2, 1) * sin_hi


def _stage1_kernel(x_ref, sh_ref, sc_ref, g_ref, w_ref, gqa_ref, gka_ref, gqb_ref, gkb_ref, seg_ref,
                   cos_ref, sinlo_ref, sinhi_ref,
                   qa_ref, ka_ref, va_ref, qb_ref, kb_ref, vb_ref, ska_ref, sva_ref, skb_ref, svb_ref,
                   *, bb, tl, pad_tiles, n_tiles, sa_rows, sb_rows):
    i = pl.program_id(1)
    m = bb * tl

    if pad_tiles:
        @pl.when(i < pad_tiles)
        def _():
            ka_ref[...] = jnp.zeros_like(ka_ref)
            va_ref[...] = jnp.zeros_like(va_ref)
            kb_ref[...] = jnp.zeros_like(kb_ref)
            vb_ref[...] = jnp.zeros_like(vb_ref)

    @pl.when(i >= pad_tiles)
    def _():
        h = _modulated_norm(x_ref[...], g_ref[...], sc_ref[...], sh_ref[...])
        hb = h.reshape(m, D_MODEL).astype(BF16)
        qkv = jnp.dot(hb, w_ref[...], preferred_element_type=F32)
        seg = seg_ref[...]

        def cols(lo, width):
            return [qkv[:, lo + LANES * j: lo + LANES * (j + 1)] for j in range(width // LANES)]

        cos = jnp.concatenate([cos_ref[...]] * bb, axis=0)
        sin_lo = jnp.concatenate([sinlo_ref[...]] * bb, axis=0)
        sin_hi = jnp.concatenate([sinhi_ref[...]] * bb, axis=0)
        scale = HEAD_DIM ** -0.5

        qa = [_rope(p, cos, sin_lo, sin_hi) * scale for p in _head_norm(cols(0, A_Q_W), gqa_ref[...], seg)]
        ka = [_rope(p, cos, sin_lo, sin_hi) for p in _head_norm(cols(A_Q_W, A_KV_W), gka_ref[...], seg)]
        va = cols(A_Q_W + A_KV_W, A_KV_W)
        off_b = A_Q_W + 2 * A_KV_W
        qb = [p * scale for p in _head_norm(cols(off_b, B_W), gqb_ref[...], seg)]
        kb = _head_norm(cols(off_b + B_W, B_W), gkb_ref[...], seg)
        vb = cols(off_b + 2 * B_W, B_W)

        def put(ref, parts):
            for j, p in enumerate(parts):
                ref[:, :, LANES * j:LANES * (j + 1)] = p.reshape(bb, tl, LANES).astype(ref.dtype)

        put(qa_ref, qa)
        put(ka_ref, ka)
        put(va_ref, va)
        put(qb_ref, qb)
        put(kb_ref, kb)
        put(vb_ref, vb)

        @pl.when(i == n_tiles - 1)
        def _():
            def put_state(ref, parts, rows):
                for j, p in enumerate(parts):
                    ref[:, :, LANES * j:LANES * (j + 1)] = p.reshape(bb, tl, LANES)[:, tl - rows:, :]

            put_state(ska_ref, ka, sa_rows)
            put_state(sva_ref, va, sa_rows)
            put_state(skb_ref, kb, sb_rows)
            put_state(svb_ref, vb, sb_rows)


def _rope_tables(pos):
    inv_freq = ROPE_THETA ** (-jnp.arange(0, ROT_DIM, 2, dtype=F32) / ROT_DIM)
    ang = pos.astype(F32)[:, None] * inv_freq[None, :]
    half = ROT_DIM // 2
    lane = jnp.arange(LANES) % HEAD_DIM
    ang_l = jnp.tile(ang, (1, LANES // half))
    rot = lane < ROT_DIM
    cos = jnp.where(rot[None, :], jnp.cos(ang_l), 1.0)
    sin_lo = jnp.where((lane < half)[None, :], -jnp.sin(ang_l), 0.0)
    sin_hi = jnp.where(((lane >= half) & rot)[None, :], jnp.sin(ang_l), 0.0)
    return cos, sin_lo, sin_hi


def _stage1(x, shift, scale, g_attn, w_qkv, gains, pos, *, bb, tl, pad):
    bsz, length, _ = x.shape
    n_tiles = length // tl
    pad_tiles = pad // tl
    sa_rows, sb_rows = min(WINDOW_A, length), min(WINDOW_B, length)
    assert bsz % bb == 0 and length % tl == 0 and pad % tl == 0 and sb_rows <= tl
    cos, sin_lo, sin_hi = _rope_tables(pos)
    seg = (jnp.arange(LANES)[:, None] // HEAD_DIM == jnp.arange(LANES)[None, :] // HEAD_DIM)
    seg = (seg.astype(F32) / HEAD_DIM).astype(BF16)
    gq_a, gk_a, gq_b, gk_b = [jnp.tile(g, LANES // HEAD_DIM).reshape(1, LANES) for g in gains]

    def xmap(b, i):
        return (b, jnp.maximum(i - pad_tiles, 0), 0)

    def const2(b, i):
        return (0, 0)

    def tabmap(b, i):
        return (jnp.maximum(i - pad_tiles, 0), 0)

    def modmap(b, i):
        return (b, 0, 0)

    def kvmap(b, i):
        return (b, i, 0)

    lp = length + pad
    kern = functools.partial(_stage1_kernel, bb=bb, tl=tl, pad_tiles=pad_tiles,
                             n_tiles=n_tiles + pad_tiles, sa_rows=sa_rows, sb_rows=sb_rows)
    out_shapes = (
        jax.ShapeDtypeStruct((bsz, length, A_Q_W), BF16),
        jax.ShapeDtypeStruct((bsz, lp, A_KV_W), BF16),
        jax.ShapeDtypeStruct((bsz, lp, A_KV_W), BF16),
        jax.ShapeDtypeStruct((bsz, length, B_W), BF16),
        jax.ShapeDtypeStruct((bsz, lp, B_W), BF16),
        jax.ShapeDtypeStruct((bsz, lp, B_W), BF16),
        jax.ShapeDtypeStruct((bsz, sa_rows, A_KV_W), F32),
        jax.ShapeDtypeStruct((bsz, sa_rows, A_KV_W), F32),
        jax.ShapeDtypeStruct((bsz, sb_rows, B_W), F32),
        jax.ShapeDtypeStruct((bsz, sb_rows, B_W), F32),
    )
    out_specs = (
        pl.BlockSpec((bb, tl, A_Q_W), xmap),
        pl.BlockSpec((bb, tl, A_KV_W), kvmap),
        pl.BlockSpec((bb, tl, A_KV_W), kvmap),
        pl.BlockSpec((bb, tl, B_W), xmap),
        pl.BlockSpec((bb, tl, B_W), kvmap),
        pl.BlockSpec((bb, tl, B_W), kvmap),
        pl.BlockSpec((bb, sa_rows, A_KV_W), modmap),
        pl.BlockSpec((bb, sa_rows, A_KV_W), modmap),
        pl.BlockSpec((bb, sb_rows, B_W), modmap),
        pl.BlockSpec((bb, sb_rows, B_W), modmap),
    )
    in_specs = [
        pl.BlockSpec((bb, tl, D_MODEL), xmap),
        pl.BlockSpec((bb, 1, D_MODEL), modmap),
        pl.BlockSpec((bb, 1, D_MODEL), modmap),
        pl.BlockSpec((1, D_MODEL), const2),
        pl.BlockSpec((D_MODEL, QKV_W), const2),
        pl.BlockSpec((1, LANES), const2),
        pl.BlockSpec((1, LANES), const2),
        pl.BlockSpec((1, LANES), const2),
        pl.BlockSpec((1, LANES), const2),
        pl.BlockSpec((LANES, LANES), const2),
        pl.BlockSpec((tl, LANES), tabmap),
        pl.BlockSpec((tl, LANES), tabmap),
        pl.BlockSpec((tl, LANES), tabmap),
    ]
    return pl.pallas_call(
        kern,
        grid=(bsz // bb, n_tiles + pad_tiles),
        in_specs=in_specs,
        out_specs=out_specs,
        out_shape=out_shapes,
        compiler_params=_cparams("arbitrary", "arbitrary"),
        name="stage1",
    )(x, shift, scale, g_attn.reshape(1, D_MODEL), w_qkv, gq_a, gk_a, gq_b, gk_b, seg, cos, sin_lo, sin_hi)


def _nt_dot(a, b):
    return lax.dot_general(a, b, (((1,), (1,)), ((), ())), preferred_element_type=F32)


def _attn_kernel(sink_ref, qa_ref, ka_ref, va_ref, qb_ref, kb_ref, vb_ref, bias_ref, oa_ref, ob_ref,
                 *, n_c, mask_prefix):
    qi = pl.program_id(1)
    lane = lax.broadcasted_iota(I32, (CHUNK, LANES), 1)
    low = lane < HEAD_DIM
    half_mask = (jnp.where(low, 1.0, 0.0).astype(BF16), jnp.where(low, 0.0, 1.0).astype(BF16))
    n_pairs = B_W // LANES
    rows_b = n_pairs * 2 * CHUNK
    rows_a = A_Q_HEADS * CHUNK
    key_b = lax.broadcasted_iota(I32, (rows_b, BAND_B), 1)
    key_a = lax.broadcasted_iota(I32, (rows_a, BAND_A), 1)
    row_a = lax.broadcasted_iota(I32, (rows_a, 1), 0) // CHUNK

    sink_col = jnp.zeros((rows_a, 1), F32)
    for r in range(A_Q_HEADS):
        sink_col = jnp.where(row_a == r, sink_ref[r], sink_col)

    def chunk(c, masked):
        ci = qi * n_c + c
        row0 = pl.multiple_of(ci * CHUNK, CHUNK)
        rows = slice(c * CHUNK, (c + 1) * CHUNK)
        first_valid = PAD - ci * CHUNK

        s_parts = []
        for j in range(n_pairs):
            cg = slice(LANES * j, LANES * (j + 1))
            q = qb_ref[0, rows, cg]
            qs = jnp.concatenate([q * half_mask[0], q * half_mask[1]], axis=0)
            k = kb_ref[0, pl.ds(row0, BAND_B), cg]
            s_parts.append(_nt_dot(qs, k) + bias_ref[j])
        s = jnp.concatenate(s_parts, axis=0)
        if masked:
            s = jnp.where(key_b >= first_valid, s, NEG_INF)
        mx = jnp.max(s, axis=-1, keepdims=True)
        e = jnp.exp(s - mx)
        inv = 1.0 / jnp.sum(e, axis=-1, keepdims=True)
        p_b = e.astype(BF16)
        for j in range(n_pairs):
            cg = slice(LANES * j, LANES * (j + 1))
            v = vb_ref[0, pl.ds(row0, BAND_B), cg]
            rj = slice(2 * CHUNK * j, 2 * CHUNK * (j + 1))
            o = jnp.dot(p_b[rj], v, preferred_element_type=F32) * inv[rj]
            ob_ref[0, rows, cg] = jnp.where(low, o[:CHUNK], o[CHUNK:]).astype(ob_ref.dtype)

        row_a0 = pl.multiple_of(row0 + (WINDOW_B - WINDOW_A), CHUNK)
        k = ka_ref[0, pl.ds(row_a0, BAND_A), :]
        v = va_ref[0, pl.ds(row_a0, BAND_A), :]
        q = jnp.concatenate(
            [qa_ref[0, rows, LANES * j:LANES * (j + 1)] * half_mask[p]
             for p in range(A_KV_HEADS) for j in range(A_GROUP)], axis=0)
        s = _nt_dot(q, k)
        if masked:
            s = jnp.where(key_a >= first_valid - (WINDOW_B - WINDOW_A), s, NEG_INF)
        mx = jnp.maximum(jnp.max(s, axis=-1, keepdims=True), sink_col)
        e = jnp.exp(s - mx)
        inv = 1.0 / (jnp.sum(e, axis=-1, keepdims=True) + jnp.exp(sink_col - mx))
        o = jnp.dot(e.astype(BF16), v, preferred_element_type=F32) * inv
        for j in range(A_GROUP):
            lo = o[j * CHUNK:(j + 1) * CHUNK]
            hi = o[(A_GROUP + j) * CHUNK:(A_GROUP + j + 1) * CHUNK]
            oa_ref[0, rows, LANES * j:LANES * (j + 1)] = jnp.where(low, lo, hi).astype(oa_ref.dtype)

    steps_with_pad = pl.cdiv(PAD // CHUNK, n_c) if mask_prefix else 0
    if steps_with_pad:
        @pl.when(qi < steps_with_pad)
        def _():
            for c in range(n_c):
                chunk(c, True)

    @pl.when(qi >= steps_with_pad)
    def _():
        for c in range(n_c):
            chunk(c, False)


def _attention(qa, ka, va, qb, kb, vb, sinks, bias, *, n_c, mask_prefix):
    bsz, length, _ = qa.shape
    lp = ka.shape[1]
    qt = n_c * CHUNK
    assert length % qt == 0 and lp == length + PAD

    def qmap(b, i):
        return (b, i, 0)

    def kmap(b, i):
        return (b, 0, 0)

    kern = functools.partial(_attn_kernel, n_c=n_c, mask_prefix=mask_prefix)
    return pl.pallas_call(
        kern,
        grid=(bsz, length // qt),
        in_specs=[
            pl.BlockSpec(memory_space=pltpu.SMEM),
            pl.BlockSpec((1, qt, A_Q_W), qmap),
            pl.BlockSpec((1, lp, A_KV_W), kmap),
            pl.BlockSpec((1, lp, A_KV_W), kmap),
            pl.BlockSpec((1, qt, B_W), qmap),
            pl.BlockSpec((1, lp, B_W), kmap),
            pl.BlockSpec((1, lp, B_W), kmap),
            pl.BlockSpec((B_HEADS // 2, 2 * CHUNK, BAND_B), lambda b, i: (0, 0, 0)),
        ],
        out_specs=(pl.BlockSpec((1, qt, A_Q_W), qmap), pl.BlockSpec((1, qt, B_W), qmap)),
        out_shape=(jax.ShapeDtypeStruct((bsz, length, A_Q_W), BF16),
                   jax.ShapeDtypeStruct((bsz, length, B_W), BF16)),
        compiler_params=_cparams("arbitrary", "arbitrary"),
        name="attention",
    )(sinks, qa, ka, va, qb, kb, vb, bias)


def _merge_kernel(x_ref, oa_ref, ob_ref, sha_ref, sca_ref, gta_ref, shm_ref, scm_ref, ga_ref, gm_ref,
                  wg_ref, wa_ref, wb_ref, wo_ref, wrh_ref, wrl_ref, br_ref,
                  x1_ref, h2_ref, ri_ref, rw_ref, cnt_ref, carry_ref, logit_ref, *, bb, tl, seg_tiles):
    m = bb * tl
    step = pl.program_id(0)

    @pl.when(step == 0)
    def _():
        carry_ref[...] = jnp.zeros_like(carry_ref)
        logit_ref[...] = jnp.zeros_like(logit_ref)

    logits = logit_ref[...]
    lane = lax.broadcasted_iota(I32, (m, LANES), 1)
    lane_f = lane.astype(F32)
    work = logits
    top_v, top_i = [], []
    sel_f = jnp.zeros((m, LANES), F32)
    for _ in range(TOP_K):
        mx = jnp.max(work, axis=-1, keepdims=True)
        idx = jnp.min(jnp.where(work == mx, lane_f, float(LANES)), axis=-1, keepdims=True)
        hit = lane_f == idx
        top_v.append(mx)
        top_i.append(idx)
        sel_f = jnp.where(hit, 1.0, sel_f)
        work = jnp.where(hit, -jnp.inf, work)
    ex = [jnp.exp(v - top_v[0]) for v in top_v]
    inv = 1.0 / (ex[0] + ex[1] + ex[2] + ex[3])

    r_i = lax.broadcasted_iota(I32, (m, m), 0)
    c_i = lax.broadcasted_iota(I32, (m, m), 1)
    before = jnp.where(r_i > c_i, 1.0, 0.0).astype(BF16)
    keep = jnp.where(jnp.maximum(step - 1, 0) % seg_tiles == 0, 0.0, 1.0)
    carry_in = carry_ref[...] * keep
    pos = jnp.dot(before, sel_f.astype(BF16), preferred_element_type=F32) + carry_in
    live = jnp.where(step > 0, 1.0, 0.0)
    carry = carry_in + live * jnp.sum(sel_f, axis=0, keepdims=True)
    carry_ref[...] = carry
    cnt_ref[0] = carry

    ri_f = jnp.zeros((m, LANES), F32)
    rw = jnp.zeros((m, LANES), F32)
    for k in range(TOP_K):
        rank = jnp.sum(jnp.where(lane_f == top_i[k], pos, 0.0), axis=-1, keepdims=True)
        ri_f = jnp.where(lane == k, top_i[k], ri_f)
        ri_f = jnp.where(lane == TOP_K + k, rank, ri_f)
        rw = jnp.where(lane == k, ex[k] * inv, rw)
    ri_ref[...] = ri_f.astype(I32).reshape(bb, tl, LANES)
    rw_ref[...] = rw.reshape(bb, tl, LANES)

    x = x_ref[...]
    h = _modulated_norm(x, ga_ref[...], sca_ref[...], sha_ref[...])
    hb = h.reshape(m, D_MODEL).astype(BF16)
    gates = jnp.dot(hb, wg_ref[...], preferred_element_type=F32)
    ya = jnp.dot(oa_ref[...].reshape(m, A_Q_W), wa_ref[...], preferred_element_type=F32)
    yb = jnp.dot(ob_ref[...].reshape(m, B_W), wb_ref[...], preferred_element_type=F32)
    mix = jax.nn.sigmoid(gates[:, :D_MODEL]) * ya + jax.nn.sigmoid(gates[:, D_MODEL:]) * yb
    z = jnp.dot(mix.astype(BF16), wo_ref[...], preferred_element_type=F32)
    x1 = x + gta_ref[...] * z.reshape(bb, tl, D_MODEL)
    x1_ref[...] = x1

    h2 = _modulated_norm(x1, gm_ref[...], scm_ref[...], shm_ref[...]).reshape(m, D_MODEL)
    for j in range(ROW_TILES):
        h2_ref[pl.ds(j, m, stride=ROW_TILES), :] = h2[:, LANES * j:LANES * (j + 1)]

    h2_hi = h2.astype(BF16)
    h2_lo = (h2 - h2_hi.astype(F32)).astype(BF16)
    wrh = wrh_ref[...]
    logit_ref[...] = (jnp.dot(h2_hi, wrh, preferred_element_type=F32)
                      + jnp.dot(h2_lo, wrh, preferred_element_type=F32)
                      + jnp.dot(h2_hi, wrl_ref[...], preferred_element_type=F32)) + br_ref[...]


def _merge(x, oa, ob, mods, g_attn, g_mlp, wg, wa, wb, wo, wr, br, *, bb, tl, b0, nb):
    _, length, _ = x.shape
    n_tok = nb * length
    m = bb * tl
    assert bb == 1 or tl == length
    sh_a, sc_a, gt_a, sh_m, sc_m = mods
    n_l = length // tl
    n_tiles = (nb // bb) * n_l
    blk0 = b0 // bb
    wr_hi = wr.astype(BF16)
    wr_lo = (wr - wr_hi.astype(F32)).astype(BF16)

    def cur(s):
        return jnp.minimum(s, n_tiles - 1)

    def prev(s):
        return jnp.maximum(s - 1, 0)

    def xmap(s):
        return (blk0 + cur(s) // n_l, cur(s) % n_l, 0)

    def modmap(s):
        return (blk0 + cur(s) // n_l, 0, 0)

    def omap(s):
        return (cur(s) // n_l, cur(s) % n_l, 0)

    def const2(s):
        return (0, 0)

    def flatmap(s):
        return (cur(s), 0)

    def routemap(s):
        return (prev(s) // n_l, prev(s) % n_l, 0)

    modspec = pl.BlockSpec((bb, 1, D_MODEL), modmap)
    kern = functools.partial(_merge_kernel, bb=bb, tl=tl, seg_tiles=n_tiles)
    return pl.pallas_call(
        kern,
        grid=(n_tiles + 1,),
        in_specs=[
            pl.BlockSpec((bb, tl, D_MODEL), xmap),
            pl.BlockSpec((bb, tl, A_Q_W), xmap),
            pl.BlockSpec((bb, tl, B_W), xmap),
            modspec, modspec, modspec, modspec, modspec,
            pl.BlockSpec((1, D_MODEL), const2),
            pl.BlockSpec((1, D_MODEL), const2),
            pl.BlockSpec((D_MODEL, 2 * D_MODEL), const2),
            pl.BlockSpec((A_Q_W, D_MODEL), const2),
            pl.BlockSpec((B_W, D_MODEL), const2),
            pl.BlockSpec((D_MODEL, D_MODEL), const2),
            pl.BlockSpec((D_MODEL, LANES), const2),
            pl.BlockSpec((D_MODEL, LANES), const2),
            pl.BlockSpec((1, LANES), const2),
        ],
        out_specs=(
            pl.BlockSpec((bb, tl, D_MODEL), omap),
            pl.BlockSpec((m * ROW_TILES, LANES), flatmap),
            pl.BlockSpec((bb, tl, LANES), routemap),
            pl.BlockSpec((bb, tl, LANES), routemap),
            pl.BlockSpec((1, 1, LANES), lambda s: (0, 0, 0)),
        ),
        out_shape=(
            jax.ShapeDtypeStruct((nb, length, D_MODEL), F32),
            jax.ShapeDtypeStruct((n_tok * ROW_TILES, LANES), F32),
            jax.ShapeDtypeStruct((nb, length, LANES), I32),
            jax.ShapeDtypeStruct((nb, length, LANES), F32),
            jax.ShapeDtypeStruct((1, 1, LANES), F32),
        ),
        scratch_shapes=[pltpu.VMEM((1, LANES), F32), pltpu.VMEM((m, LANES), F32)],
        compiler_params=_cparams("arbitrary"),
        name="merge",
    )(x, oa, ob, sh_a, sc_a, gt_a, sh_m, sc_m, g_attn.reshape(1, D_MODEL), g_mlp.reshape(1, D_MODEL),
      wg, wa, wb, wo, wr_hi, wr_lo, br)


def _sc_mesh():
    return plsc.VectorSubcoreMesh(core_axis_name="c", subcore_axis_name="s",
                                  num_cores=SC_CORES, num_subcores=SC_SUBCORES)


def _sc_worker():
    return lax.axis_index("c") * SC_SUBCORES + lax.axis_index("s")


def _sc_params():
    return pltpu.CompilerParams(use_tc_tiling_on_sc=True)


def _dispatch(h2_rows, idx_flat, pad_flat, n_rows):
    n_tok = h2_rows.shape[0]
    per_w = n_tok // SC_WORKERS
    n_chunks = per_w // SC_ROWS
    n_idx = n_chunks * TOP_K * SC_ROWS
    n_pad = pad_flat.shape[0] // SC_WORKERS
    pad_chunks = n_pad // SC_ROWS
    assert per_w % (2 * SC_ROWS) == 0 and n_pad % SC_ROWS == 0
    zeros = jnp.zeros((SC_ROWS, ROW_TILES, LANES), F32)

    def body(h2_hbm, idx_hbm, pad_hbm, zeros_hbm, xs_hbm, idx_v, pad_v, buf0, buf1, ls0, ls1, ss0, ss1):
        wid = _sc_worker()
        bufs, lsems, ssems = (buf0, buf1), (ls0, ls1), (ss0, ss1)
        pltpu.sync_copy(idx_hbm.at[pl.ds(pl.multiple_of(wid * n_idx, SC_ROWS), n_idx)], idx_v)
        pltpu.sync_copy(pad_hbm.at[pl.ds(pl.multiple_of(wid * n_pad, SC_ROWS), n_pad)], pad_v)

        pltpu.sync_copy(zeros_hbm, buf0)
        for c in range(pad_chunks):
            pltpu.async_copy(buf0, xs_hbm.at[pad_v.at[pl.ds(c * SC_ROWS, SC_ROWS)]], ss0)
        for c in range(pad_chunks):
            pltpu.make_async_copy(buf0, xs_hbm.at[pad_v.at[pl.ds(c * SC_ROWS, SC_ROWS)]], ss0).wait()

        base = wid * per_w

        def load(g, b):
            src = h2_hbm.at[pl.ds(pl.multiple_of(base + g * SC_ROWS, SC_ROWS), SC_ROWS)]
            return pltpu.make_async_copy(src, bufs[b], lsems[b])

        def scatter(g, k, b):
            off = pl.multiple_of((g * TOP_K + k) * SC_ROWS, SC_ROWS)
            return pltpu.make_async_copy(bufs[b], xs_hbm.at[idx_v.at[pl.ds(off, SC_ROWS)]], ssems[b])

        load(0, 0).start()

        @pl.loop(0, n_chunks, step=2)
        def _(g0):
            for b in range(2):
                g = g0 + b
                load(g, b).wait()

                @pl.when(g >= 1)
                def _():
                    for k in range(TOP_K):
                        scatter(g - 1, k, 1 - b).wait()

                @pl.when(g + 1 < n_chunks)
                def _():
                    load(g + 1, 1 - b).start()

                for k in range(TOP_K):
                    scatter(g, k, b).start()

        for k in range(TOP_K):
            scatter(n_chunks - 1, k, (n_chunks - 1) % 2).wait()

    call = pl.kernel(
        body,
        out_type=jax.ShapeDtypeStruct((n_rows + TRASH_ROWS, ROW_TILES, LANES), F32),
        mesh=_sc_mesh(),
        scratch_types=[pltpu.VMEM((n_idx,), I32), pltpu.VMEM((n_pad,), I32),
                       pltpu.VMEM((SC_ROWS, ROW_TILES, LANES), F32), pltpu.VMEM((SC_ROWS, ROW_TILES, LANES), F32),
                       pltpu.SemaphoreType.DMA, pltpu.SemaphoreType.DMA,
                       pltpu.SemaphoreType.DMA, pltpu.SemaphoreType.DMA],
        compiler_params=_sc_params(),
        name="dispatch",
    )
    return call(h2_rows, idx_flat, pad_flat, zeros)


def _gather_back(y_rows, idx_flat, n_tok):
    per_w = n_tok // SC_WORKERS
    n_chunks = per_w // SC_ROWS
    n_items = n_chunks * TOP_K
    n_idx = n_items * SC_ROWS
    assert per_w % SC_ROWS == 0 and n_items % 2 == 0

    def body(y_hbm, idx_hbm, yg_hbm, idx_v, buf0, buf1, gs0, gs1, ws0, ws1):
        wid = _sc_worker()
        bufs, gsems, wsems = (buf0, buf1), (gs0, gs1), (ws0, ws1)
        pltpu.sync_copy(idx_hbm.at[pl.ds(pl.multiple_of(wid * n_idx, SC_ROWS), n_idx)], idx_v)
        base = wid * per_w

        def gather(i, b):
            off = pl.multiple_of(i * SC_ROWS, SC_ROWS)
            return pltpu.make_async_copy(y_hbm.at[idx_v.at[pl.ds(off, SC_ROWS)]], bufs[b], gsems[b])

        def write(i, b):
            g, k = i // TOP_K, i % TOP_K
            dst = yg_hbm.at[k, pl.ds(pl.multiple_of(base + g * SC_ROWS, SC_ROWS), SC_ROWS)]
            return pltpu.make_async_copy(bufs[b], dst, wsems[b])

        gather(0, 0).start()

        @pl.loop(0, n_items, step=2)
        def _(i0):
            for b in range(2):
                i = i0 + b
                gather(i, b).wait()

                @pl.when(i >= 1)
                def _():
                    write(i - 1, 1 - b).wait()

                @pl.when(i + 1 < n_items)
                def _():
                    gather(i + 1, 1 - b).start()

                write(i, b).start()

        write(n_items - 1, (n_items - 1) % 2).wait()

    call = pl.kernel(
        body,
        out_type=jax.ShapeDtypeStruct((TOP_K, n_tok, ROW_TILES, LANES), F32),
        mesh=_sc_mesh(),
        scratch_types=[pltpu.VMEM((n_idx,), I32),
                       pltpu.VMEM((SC_ROWS, ROW_TILES, LANES), F32), pltpu.VMEM((SC_ROWS, ROW_TILES, LANES), F32),
                       pltpu.SemaphoreType.DMA, pltpu.SemaphoreType.DMA,
                       pltpu.SemaphoreType.DMA, pltpu.SemaphoreType.DMA],
        compiler_params=_sc_params(),
        name="gather_back",
    )
    return call(y_rows, idx_flat)


def _expert_kernel(blk_ref, used_ref, xs_ref, wgu_ref, bgu_ref, wd_ref, bd_ref, y_ref, wgu_bf, wd_bf):
    r = pl.program_id(0)
    e = blk_ref[r]
    prev = blk_ref[jnp.maximum(r - 1, 0)]
    active = r < used_ref[0]

    @pl.when(jnp.logical_and(active, jnp.logical_or(r == 0, e != prev)))
    def _():
        wgu_bf[...] = wgu_ref[0].astype(BF16)
        wd_bf[...] = wd_ref[0].astype(BF16)

    @pl.when(active)
    def _():
        x = jnp.concatenate(
            [xs_ref[pl.ds(j, ROUTE_BLOCK, stride=ROW_TILES), :] for j in range(ROW_TILES)], axis=1).astype(BF16)
        acts = []
        for j in range(D_FF // FF_TILE):
            gc = slice(FF_TILE * j, FF_TILE * (j + 1))
            uc = slice(D_FF + FF_TILE * j, D_FF + FF_TILE * (j + 1))
            g = jnp.dot(x, wgu_bf[:, gc], preferred_element_type=F32) + bgu_ref[0, :, gc]
            u = jnp.dot(x, wgu_bf[:, uc], preferred_element_type=F32) + bgu_ref[0, :, uc]
            gate = jnp.minimum(g, SWIGLU_LIMIT)
            up = jnp.clip(u, -SWIGLU_LIMIT, SWIGLU_LIMIT)
            glu = gate * jax.nn.sigmoid(gate * SWIGLU_ALPHA)
            acts.append(((up + 1.0) * glu).astype(BF16))
        act = jnp.concatenate(acts, axis=1)
        y = jnp.dot(act, wd_bf[...], preferred_element_type=F32) + bd_ref[0]
        for j in range(ROW_TILES):
            y_ref[pl.ds(j, ROUTE_BLOCK, stride=ROW_TILES), :] = y[:, LANES * j:LANES * (j + 1)]


def _experts(xs_rows, blk_expert, n_used, w_gate_up, b_gate_up, w_down, b_down):
    n_blocks = blk_expert.shape[0]
    rb = ROUTE_BLOCK * ROW_TILES

    def rowmap(r, blk, used):
        return (jnp.minimum(r, used[0] - 1), 0)

    def wmap(r, blk, used):
        return (blk[jnp.minimum(r, used[0] - 1)], 0, 0)

    return pl.pallas_call(
        _expert_kernel,
        grid_spec=pltpu.PrefetchScalarGridSpec(
            num_scalar_prefetch=2,
            grid=(n_blocks,),
            in_specs=[
                pl.BlockSpec((rb, LANES), rowmap),
                pl.BlockSpec((1, D_MODEL, 2 * D_FF), wmap),
                pl.BlockSpec((1, 1, 2 * D_FF), wmap),
                pl.BlockSpec((1, D_FF, D_MODEL), wmap),
                pl.BlockSpec((1, 1, D_MODEL), wmap),
            ],
            out_specs=pl.BlockSpec((rb, LANES), rowmap),
            scratch_shapes=[pltpu.VMEM((D_MODEL, 2 * D_FF), BF16), pltpu.VMEM((D_FF, D_MODEL), BF16)],
        ),
        out_shape=jax.ShapeDtypeStruct(xs_rows.shape, F32),
        compiler_params=_cparams("arbitrary"),
        name="experts",
    )(blk_expert, n_used, xs_rows, w_gate_up, b_gate_up.reshape(N_EXPERTS, 1, 2 * D_FF),
      w_down, b_down.reshape(N_EXPERTS, 1, D_MODEL))


def _combine_kernel(x1_ref, gt_ref, rw_ref, yg_ref, *rest, bb, tl):
    o_ref = rest[-1]
    m = bb * tl
    rw = rw_ref[...].reshape(m, LANES)
    w = [rw[:, k:k + 1] for k in range(TOP_K)]
    gt = gt_ref[...]
    for j in range(ROW_TILES):
        acc = jnp.zeros((m, LANES), F32)
        for k in range(TOP_K):
            acc = acc + w[k] * yg_ref[k, pl.ds(j, m, stride=ROW_TILES), :]
        cg = slice(LANES * j, LANES * (j + 1))
        o_ref[:, :, cg] = x1_ref[:, :, cg] + gt[:, :, cg] * acc.reshape(bb, tl, LANES)


def _combine(x1, gt_m, rw, yg, y_prev, *, bb, tl, b0, bsz):
    nb, length, _ = x1.shape
    m = bb * tl
    assert bb == 1 or tl == length
    n_l = length // tl
    blk0 = b0 // bb

    def smap(b, i):
        return (b, i, 0)

    def xmap(b, i):
        return (blk0 + b, i, 0)

    in_specs = [
        pl.BlockSpec((bb, tl, D_MODEL), smap),
        pl.BlockSpec((bb, 1, D_MODEL), lambda b, i: (blk0 + b, 0, 0)),
        pl.BlockSpec((bb, tl, LANES), smap),
        pl.BlockSpec((TOP_K, m * ROW_TILES, LANES), lambda b, i: (0, b * n_l + i, 0)),
    ]
    args = [x1, gt_m, rw, yg]
    aliases = {}
    if y_prev is not None:
        in_specs.append(pl.BlockSpec(memory_space=pl.ANY))
        args.append(y_prev)
        aliases = {len(args) - 1: 0}
    kern = functools.partial(_combine_kernel, bb=bb, tl=tl)
    return pl.pallas_call(
        kern,
        grid=(nb // bb, n_l),
        in_specs=in_specs,
        out_specs=pl.BlockSpec((bb, tl, D_MODEL), xmap),
        out_shape=jax.ShapeDtypeStruct((bsz, length, D_MODEL), F32),
        input_output_aliases=aliases,
        compiler_params=_cparams("arbitrary", "arbitrary"),
        name="combine",
    )(*args)


def _route_plan(ri, counts_f, n_tok):
    idx = ri[..., :TOP_K].reshape(n_tok, TOP_K)
    rank = ri[..., TOP_K:2 * TOP_K].reshape(n_tok, TOP_K)
    counts = counts_f.reshape(-1)[:N_EXPERTS].astype(I32)
    padded = (counts + ROUTE_BLOCK - 1) // ROUTE_BLOCK * ROUTE_BLOCK
    pends = jnp.cumsum(padded)
    pstarts = pends - padded
    expert_ids = jnp.arange(N_EXPERTS, dtype=I32)
    dest = jnp.sum(jnp.where(idx[..., None] == expert_ids, pstarts, 0), axis=-1) + rank
    n_blocks = n_tok * TOP_K // ROUTE_BLOCK + N_EXPERTS
    blk_start = jnp.arange(n_blocks, dtype=I32) * ROUTE_BLOCK
    blk_expert = jnp.minimum(jnp.sum((pends[None, :] <= blk_start[:, None]).astype(I32), axis=1), N_EXPERTS - 1)
    n_used = (pends[-1] // ROUTE_BLOCK).reshape(1)
    n_rows = n_blocks * ROUTE_BLOCK
    per_w = n_tok // SC_WORKERS
    idx_flat = dest.astype(I32).reshape(SC_WORKERS, per_w // SC_ROWS, SC_ROWS, TOP_K)
    idx_flat = idx_flat.transpose(0, 1, 3, 2).reshape(-1)
    r = jnp.arange(ROUTE_BLOCK, dtype=I32)[None, :]
    pad_flat = jnp.where(r < (padded - counts)[:, None], (pstarts + counts)[:, None] + r, n_rows)
    return idx_flat, pad_flat.astype(I32).reshape(-1), blk_expert.astype(I32), n_used.astype(I32), n_rows


def _layer(x, mods, pos, prefix, weights, *, bb, tl, n_c, tl_combine, bb_combine, n_seg):
    (g_attn, g_mlp, w_qkv, gains, sinks, bias, wg, wa, wb, wo, wr, br,
     w_gate_up, b_gate_up, w_down, b_down) = weights
    bsz, length, _ = x.shape
    n_tok = bsz * length
    sh_a, sc_a, gt_a, sh_m, sc_m, gt_m = mods
    pad = 0 if prefix is not None else PAD
    qa, ka, va, qb, kb, vb, ska, sva, skb, svb = _stage1(
        x, sh_a, sc_a, g_attn, w_qkv, gains, pos, bb=bb, tl=tl, pad=pad)
    if prefix is not None:
        ka, va, kb, vb = [jnp.concatenate([p, t], axis=1) for p, t in zip(prefix, (ka, va, kb, vb))]
    oa, ob = _attention(qa, ka, va, qb, kb, vb, sinks, bias, n_c=n_c, mask_prefix=prefix is None)
    nb = bsz // n_seg
    seg_tok = nb * length
    y = None
    for s in range(n_seg):
        x1, h2_rows, ri, rw, counts_f = _merge(x, oa, ob, (sh_a, sc_a, gt_a, sh_m, sc_m), g_attn, g_mlp,
                                               wg, wa, wb, wo, wr, br, bb=bb, tl=tl, b0=s * nb, nb=nb)
        idx_flat, pad_flat, blk_expert, n_used, n_rows = _route_plan(ri, counts_f, seg_tok)
        xs = _dispatch(h2_rows.reshape(seg_tok, ROW_TILES, LANES), idx_flat, pad_flat, n_rows)
        y_rows = _experts(xs.reshape(-1, LANES), blk_expert, n_used, w_gate_up, b_gate_up, w_down, b_down)
        yg = _gather_back(y_rows.reshape(-1, ROW_TILES, LANES), idx_flat, seg_tok)
        y = _combine(x1, gt_m, rw, yg.reshape(TOP_K, seg_tok * ROW_TILES, LANES), y,
                     bb=bb_combine, tl=tl_combine, b0=s * nb, bsz=bsz)
    return y, (ska, sva, skb, svb)


def _rel_bias_table(rel_table):
    q_pos = jnp.arange(CHUNK)
    k_pos = jnp.arange(BAND_B) - WINDOW_B
    idx = jnp.clip(q_pos[:, None] - k_pos[None, :], -REL_FWD, REL_BACK) + REL_FWD
    return rel_table[:, idx].astype(F32).reshape(B_HEADS // 2, 2 * CHUNK, BAND_B)


def _prep_weights(g_attn, g_mlp, w_in, q_norm_a, k_norm_a, q_norm_b, k_norm_b, sinks_a, rel_table_b,
                  w_branch_a, w_branch_b, w_out, w_router, b_router, w_gate_up, b_gate_up, w_down, b_down):
    order = jnp.asarray(A_HEAD_ORDER)
    w_qa = w_in[:, :A_Q_W].reshape(D_MODEL, A_Q_HEADS, HEAD_DIM)[:, order].reshape(D_MODEL, A_Q_W)
    w_qkv = jnp.concatenate([w_qa, w_in[:, A_Q_W:QKV_W]], axis=1).astype(BF16)
    wg = w_in[:, QKV_W:].astype(BF16)
    wa = w_branch_a.reshape(A_Q_HEADS, HEAD_DIM, D_MODEL)[order].reshape(A_Q_W, D_MODEL).astype(BF16)
    wb = w_branch_b.astype(BF16)
    wo = w_out.astype(BF16)
    wr = jnp.zeros((D_MODEL, LANES), F32).at[:, :N_EXPERTS].set(w_router)
    br = jnp.full((1, LANES), NEG_INF, F32).at[0, :N_EXPERTS].set(b_router)
    bias = _rel_bias_table(rel_table_b)
    gains = (q_norm_a, k_norm_a, q_norm_b, k_norm_b)
    return (g_attn, g_mlp, w_qkv, gains, sinks_a.astype(F32), bias, wg, wa, wb, wo, wr, br,
            w_gate_up, b_gate_up, w_down, b_down)


def _forward(x_prompt, x_sample, c_prompt, c_sample, cache_a_k, cache_a_v, cache_b_k, cache_b_v, params,
             past_len, cfg_prompt, cfg_sample):
    (g_attn, g_mlp, w_ada, b_ada, w_in, q_norm_a, k_norm_a, q_norm_b, k_norm_b, sinks_a, rel_table_b,
     w_branch_a, w_branch_b, w_out, w_router, b_router, w_gate_up, b_gate_up, w_down, b_down) = params
    depth = g_attn.shape[0]
    bp, lp_, _ = x_prompt.shape
    bs, ls, _ = x_sample.shape
    pos_p = jnp.arange(lp_, dtype=I32)
    pos_s = past_len + jnp.arange(ls, dtype=I32)
    yp, ys = x_prompt, x_sample
    st_p, st_s = [], []
    for l in range(depth):
        weights = _prep_weights(g_attn[l], g_mlp[l], w_in[l], q_norm_a[l], k_norm_a[l], q_norm_b[l],
                                k_norm_b[l], sinks_a[l], rel_table_b[l], w_branch_a[l], w_branch_b[l],
                                w_out[l], w_router[l], b_router[l], w_gate_up[l], b_gate_up[l],
                                w_down[l], b_down[l])
        mod = _adaln(jnp.concatenate([c_prompt, c_sample], axis=0), w_ada[l], b_ada[l])
        mod = mod.reshape(bp + bs, 6, 1, D_MODEL)
        mods_p = [mod[:bp, k] for k in range(6)]
        mods_s = [mod[bp:, k] for k in range(6)]
        yp, sp = _layer(yp, mods_p, pos_p, None, weights, **cfg_prompt)
        wa_c = cache_a_k.shape[2]
        wb_c = cache_b_k.shape[2]
        assert wb_c == WINDOW_B and wa_c == WINDOW_A
        zeros_a = jnp.zeros((bs, PAD - wa_c, A_KV_W), BF16)
        prefix = (
            jnp.concatenate([zeros_a, cache_a_k[l].reshape(bs, wa_c, A_KV_W).astype(BF16)], axis=1),
            jnp.concatenate([zeros_a, cache_a_v[l].reshape(bs, wa_c, A_KV_W).astype(BF16)], axis=1),
            cache_b_k[l].reshape(bs, wb_c, B_W).astype(BF16),
            cache_b_v[l].reshape(bs, wb_c, B_W).astype(BF16),
        )
        ys, ss = _layer(ys, mods_s, pos_s, prefix, weights, **cfg_sample)
        st_p.append(sp)
        st_s.append(ss)

    def stack(states, k, heads):
        return jnp.stack([s[k].reshape(s[k].shape[0], s[k].shape[1], heads, HEAD_DIM) for s in states])

    return (yp, ys,
            stack(st_p, 0, A_KV_HEADS), stack(st_p, 1, A_KV_HEADS), stack(st_p, 2, B_HEADS), stack(st_p, 3, B_HEADS),
            stack(st_s, 0, A_KV_HEADS), stack(st_s, 1, A_KV_HEADS), stack(st_s, 2, B_HEADS), stack(st_s, 3, B_HEADS))


PAST_LEN = 2048
CFG_PROMPT = dict(bb=1, tl=512, n_c=4, tl_combine=512, bb_combine=1, n_seg=2)
CFG_SAMPLE = dict(bb=8, tl=64, n_c=1, tl_combine=64, bb_combine=8, n_seg=1)


def kernel(x_prompt, x_sample, c_prompt, c_sample, cache_a_k, cache_a_v, cache_b_k, cache_b_v, g_attn, g_mlp,
           w_ada, b_ada, w_in, q_norm_a, k_norm_a, q_norm_b, k_norm_b, sinks_a, rel_table_b, w_branch_a,
           w_branch_b, w_out, w_router, b_router, w_gate_up, b_gate_up, w_down, b_down):
    params = (g_attn, g_mlp, w_ada, b_ada, w_in, q_norm_a, k_norm_a, q_norm_b, k_norm_b, sinks_a, rel_table_b,
              w_branch_a, w_branch_b, w_out, w_router, b_router, w_gate_up, b_gate_up, w_down, b_down)
    return _forward(x_prompt, x_sample, c_prompt, c_sample, cache_a_k, cache_a_v, cache_b_k, cache_b_v,
                    params, PAST_LEN, CFG_PROMPT, CFG_SAMPLE)
```

```python
import functools

import jax
import jax.numpy as jnp
from jax import lax
from jax.experimental import pallas as pl
from jax.experimental.pallas import tpu as pltpu
from jax.experimental.pallas import tpu_sc as plsc

F32 = jnp.float32
BF16 = jnp.bfloat16
I32 = jnp.int32

D_MODEL = 1024
CHUNK = 64
HEAD_DIM = 64
LANES = 128
SUBLANES = 8
A_Q_HEADS = 8
A_KV_HEADS = 2
A_GROUP = A_Q_HEADS // A_KV_HEADS
B_HEADS = 8
A_Q_W = A_Q_HEADS * HEAD_DIM
A_KV_W = A_KV_HEADS * HEAD_DIM
B_W = B_HEADS * HEAD_DIM
QKV_W = A_Q_W + 2 * A_KV_W + 3 * B_W
WINDOW_A = 128
WINDOW_B = 512
BAND_A = WINDOW_A + CHUNK
BAND_B = WINDOW_B + CHUNK
PAD = WINDOW_B
ROT_DIM = HEAD_DIM // 4
ROPE_THETA = 500000.0
REL_BACK = 4 * CHUNK
REL_FWD = CHUNK - 1
N_EXPERTS = 32
TOP_K = 4
D_FF = D_MODEL
SWIGLU_LIMIT = 7.0
SWIGLU_ALPHA = 1.702
ROUTE_BLOCK = 512
FF_TILE = 256
RMS_EPS = 1e-6
NEG_INF = -1e30
ROW_TILES = D_MODEL // LANES
VMEM_LIMIT = 56 * 1024 * 1024
SC_CORES = 2
SC_SUBCORES = 16
SC_WORKERS = SC_CORES * SC_SUBCORES
SC_ROWS = 32
TRASH_ROWS = 8

A_HEAD_ORDER = (0, 4, 1, 5, 2, 6, 3, 7)


def _cparams(*sem):
    return pltpu.CompilerParams(dimension_semantics=sem, vmem_limit_bytes=VMEM_LIMIT)


def _adaln_kernel(c_ref, w_ref, b_ref, o_ref):
    c = c_ref[...]
    a = c * jax.nn.sigmoid(c)
    o_ref[...] = jnp.dot(a, w_ref[...], preferred_element_type=F32,
                         precision=lax.Precision.HIGHEST) + b_ref[...]


def _adaln(c, w_ada, b_ada):
    n_rows = c.shape[0]
    n_out = w_ada.shape[1]
    tn = 1024
    return pl.pallas_call(
        _adaln_kernel,
        grid=(n_out // tn,),
        in_specs=[pl.BlockSpec((n_rows, D_MODEL), lambda j: (0, 0)),
                  pl.BlockSpec((D_MODEL, tn), lambda j: (0, j)),
                  pl.BlockSpec((1, tn), lambda j: (0, j))],
        out_specs=pl.BlockSpec((n_rows, tn), lambda j: (0, j)),
        out_shape=jax.ShapeDtypeStruct((n_rows, n_out), F32),
        compiler_params=_cparams("arbitrary"),
        name="adaln",
    )(c, w_ada, b_ada.reshape(1, n_out))


def _modulated_norm(x, g, scale, shift):
    ms = jnp.mean(x * x, axis=-1, keepdims=True)
    return (x * lax.rsqrt(ms + RMS_EPS) * g) * (1.0 + scale) + shift


def _head_norm(parts, gain, seg):
    m = parts[0].shape[0]
    sq = jnp.concatenate([p * p for p in parts], axis=0).astype(BF16)
    ms = jnp.dot(sq, seg, preferred_element_type=F32)
    r = lax.rsqrt(ms + RMS_EPS)
    return [parts[j] * r[j * m:(j + 1) * m] * gain for j in range(len(parts))]


def _rope(p, cos, sin_lo, sin_hi):
    return p * cos + pltpu.roll(p, LANES - ROT_DIM // 2, 1) * sin_lo + pltpu.roll(p, ROT_DIM // 2, 1) * sin_hi


def _stage1_kernel(x_ref, sh_ref, sc_ref, g_ref, w_ref, gqa_ref, gka_ref, gqb_ref, gkb_ref, seg_ref,
                   cos_ref, sinlo_ref, sinhi_ref,
                   qa_ref, ka_ref, va_ref, qb_ref, kb_ref, vb_ref, ska_ref, sva_ref, skb_ref, svb_ref,
                   *, bb, tl, pad_tiles, n_tiles, sa_rows, sb_rows):
    i = pl.program_id(1)
    m = bb * tl

    if pad_tiles:
        @pl.when(i < pad_tiles)
        def _():
            ka_ref[...] = jnp.zeros_like(ka_ref)
            va_ref[...] = jnp.zeros_like(va_ref)
            kb_ref[...] = jnp.zeros_like(kb_ref)
            vb_ref[...] = jnp.zeros_like(vb_ref)

    @pl.when(i >= pad_tiles)
    def _():
        h = _modulated_norm(x_ref[...], g_ref[...], sc_ref[...], sh_ref[...])
        hb = h.reshape(m, D_MODEL).astype(BF16)
        qkv = jnp.dot(hb, w_ref[...], preferred_element_type=F32)
        seg = seg_ref[...]

        def cols(lo, width):
            return [qkv[:, lo + LANES * j: lo + LANES * (j + 1)] for j in range(width // LANES)]

        cos = jnp.concatenate([cos_ref[...]] * bb, axis=0)
        sin_lo = jnp.concatenate([sinlo_ref[...]] * bb, axis=0)
        sin_hi = jnp.concatenate([sinhi_ref[...]] * bb, axis=0)
        scale = HEAD_DIM ** -0.5

        qa = [_rope(p, cos, sin_lo, sin_hi) * scale for p in _head_norm(cols(0, A_Q_W), gqa_ref[...], seg)]
        ka = [_rope(p, cos, sin_lo, sin_hi) for p in _head_norm(cols(A_Q_W, A_KV_W), gka_ref[...], seg)]
        va = cols(A_Q_W + A_KV_W, A_KV_W)
        off_b = A_Q_W + 2 * A_KV_W
        qb = [p * scale for p in _head_norm(cols(off_b, B_W), gqb_ref[...], seg)]
        kb = _head_norm(cols(off_b + B_W, B_W), gkb_ref[...], seg)
        vb = cols(off_b + 2 * B_W, B_W)

        def put(ref, parts):
            for j, p in enumerate(parts):
                ref[:, :, LANES * j:LANES * (j + 1)] = p.reshape(bb, tl, LANES).astype(ref.dtype)

        put(qa_ref, qa)
        put(ka_ref, ka)
        put(va_ref, va)
        put(qb_ref, qb)
        put(kb_ref, kb)
        put(vb_ref, vb)

        @pl.when(i == n_tiles - 1)
        def _():
            def put_state(ref, parts, rows):
                for j, p in enumerate(parts):
                    ref[:, :, LANES * j:LANES * (j + 1)] = p.reshape(bb, tl, LANES)[:, tl - rows:, :]

            put_state(ska_ref, ka, sa_rows)
            put_state(sva_ref, va, sa_rows)
            put_state(skb_ref, kb, sb_rows)
            put_state(svb_ref, vb, sb_rows)


def _rope_tables(pos):
    inv_freq = ROPE_THETA ** (-jnp.arange(0, ROT_DIM, 2, dtype=F32) / ROT_DIM)
    ang = pos.astype(F32)[:, None] * inv_freq[None, :]
    half = ROT_DIM // 2
    lane = jnp.arange(LANES) % HEAD_DIM
    ang_l = jnp.tile(ang, (1, LANES // half))
    rot = lane < ROT_DIM
    cos = jnp.where(rot[None, :], jnp.cos(ang_l), 1.0)
    sin_lo = jnp.where((lane < half)[None, :], -jnp.sin(ang_l), 0.0)
    sin_hi = jnp.where(((lane >= half) & rot)[None, :], jnp.sin(ang_l), 0.0)
    return cos, sin_lo, sin_hi


def _stage1(x, shift, scale, g_attn, w_qkv, gains, pos, *, bb, tl, pad):
    bsz, length, _ = x.shape
    n_tiles = length // tl
    pad_tiles = pad // tl
    sa_rows, sb_rows = min(WINDOW_A, length), min(WINDOW_B, length)
    assert bsz % bb == 0 and length % tl == 0 and pad % tl == 0 and sb_rows <= tl
    cos, sin_lo, sin_hi = _rope_tables(pos)
    seg = (jnp.arange(LANES)[:, None] // HEAD_DIM == jnp.arange(LANES)[None, :] // HEAD_DIM)
    seg = (seg.astype(F32) / HEAD_DIM).astype(BF16)
    gq_a, gk_a, gq_b, gk_b = [jnp.tile(g, LANES // HEAD_DIM).reshape(1, LANES) for g in gains]

    def xmap(b, i):
        return (b, jnp.maximum(i - pad_tiles, 0), 0)

    def const2(b, i):
        return (0, 0)

    def tabmap(b, i):
        return (jnp.maximum(i - pad_tiles, 0), 0)

    def modmap(b, i):
        return (b, 0, 0)

    def kvmap(b, i):
        return (b, i, 0)

    lp = length + pad
    kern = functools.partial(_stage1_kernel, bb=bb, tl=tl, pad_tiles=pad_tiles,
                             n_tiles=n_tiles + pad_tiles, sa_rows=sa_rows, sb_rows=sb_rows)
    out_shapes = (
        jax.ShapeDtypeStruct((bsz, length, A_Q_W), BF16),
        jax.ShapeDtypeStruct((bsz, lp, A_KV_W), BF16),
        jax.ShapeDtypeStruct((bsz, lp, A_KV_W), BF16),
        jax.ShapeDtypeStruct((bsz, length, B_W), BF16),
        jax.ShapeDtypeStruct((bsz, lp, B_W), BF16),
        jax.ShapeDtypeStruct((bsz, lp, B_W), BF16),
        jax.ShapeDtypeStruct((bsz, sa_rows, A_KV_W), F32),
        jax.ShapeDtypeStruct((bsz, sa_rows, A_KV_W), F32),
        jax.ShapeDtypeStruct((bsz, sb_rows, B_W), F32),
        jax.ShapeDtypeStruct((bsz, sb_rows, B_W), F32),
    )
    out_specs = (
        pl.BlockSpec((bb, tl, A_Q_W), xmap),
        pl.BlockSpec((bb, tl, A_KV_W), kvmap),
        pl.BlockSpec((bb, tl, A_KV_W), kvmap),
        pl.BlockSpec((bb, tl, B_W), xmap),
        pl.BlockSpec((bb, tl, B_W), kvmap),
        pl.BlockSpec((bb, tl, B_W), kvmap),
        pl.BlockSpec((bb, sa_rows, A_KV_W), modmap),
        pl.BlockSpec((bb, sa_rows, A_KV_W), modmap),
        pl.BlockSpec((bb, sb_rows, B_W), modmap),
        pl.BlockSpec((bb, sb_rows, B_W), modmap),
    )
    in_specs = [
        pl.BlockSpec((bb, tl, D_MODEL), xmap),
        pl.BlockSpec((bb, 1, D_MODEL), modmap),
        pl.BlockSpec((bb, 1, D_MODEL), modmap),
        pl.BlockSpec((1, D_MODEL), const2),
        pl.BlockSpec((D_MODEL, QKV_W), const2),
        pl.BlockSpec((1, LANES), const2),
        pl.BlockSpec((1, LANES), const2),
        pl.BlockSpec((1, LANES), const2),
        pl.BlockSpec((1, LANES), const2),
        pl.BlockSpec((LANES, LANES), const2),
        pl.BlockSpec((tl, LANES), tabmap),
        pl.BlockSpec((tl, LANES), tabmap),
        pl.BlockSpec((tl, LANES), tabmap),
    ]
    return pl.pallas_call(
        kern,
        grid=(bsz // bb, n_tiles + pad_tiles),
        in_specs=in_specs,
        out_specs=out_specs,
        out_shape=out_shapes,
        compiler_params=_cparams("arbitrary", "arbitrary"),
        name="stage1",
    )(x, shift, scale, g_attn.reshape(1, D_MODEL), w_qkv, gq_a, gk_a, gq_b, gk_b, seg, cos, sin_lo, sin_hi)


def _nt_dot(a, b):
    return lax.dot_general(a, b, (((1,), (1,)), ((), ())), preferred_element_type=F32)


def _attn_kernel(sink_ref, qa_ref, ka_ref, va_ref, qb_ref, kb_ref, vb_ref, bias_ref, oa_ref, ob_ref,
                 *, n_c, mask_prefix):
    qi = pl.program_id(1)
    lane = lax.broadcasted_iota(I32, (CHUNK, LANES), 1)
    low = lane < HEAD_DIM
    half_mask = (jnp.where(low, 1.0, 0.0).astype(BF16), jnp.where(low, 0.0, 1.0).astype(BF16))
    n_pairs = B_W // LANES
    rows_b = n_pairs * 2 * CHUNK
    rows_a = A_Q_HEADS * CHUNK
    key_b = lax.broadcasted_iota(I32, (rows_b, BAND_B), 1)
    key_a = lax.broadcasted_iota(I32, (rows_a, BAND_A), 1)
    row_a = lax.broadcasted_iota(I32, (rows_a, 1), 0) // CHUNK

    sink_col = jnp.zeros((rows_a, 1), F32)
    for r in range(A_Q_HEADS):
        sink_col = jnp.where(row_a == r, sink_ref[r], sink_col)

    def chunk(c, masked):
        ci = qi * n_c + c
        row0 = pl.multiple_of(ci * CHUNK, CHUNK)
        rows = slice(c * CHUNK, (c + 1) * CHUNK)
        first_valid = PAD - ci * CHUNK

        s_parts = []
        for j in range(n_pairs):
            cg = slice(LANES * j, LANES * (j + 1))
            q = qb_ref[0, rows, cg]
            qs = jnp.concatenate([q * half_mask[0], q * half_mask[1]], axis=0)
            k = kb_ref[0, pl.ds(row0, BAND_B), cg]
            s_parts.append(_nt_dot(qs, k) + bias_ref[j])
        s = jnp.concatenate(s_parts, axis=0)
        if masked:
            s = jnp.where(key_b >= first_valid, s, NEG_INF)
        mx = jnp.max(s, axis=-1, keepdims=True)
        e = jnp.exp(s - mx)
        inv = 1.0 / jnp.sum(e, axis=-1, keepdims=True)
        p_b = e.astype(BF16)
        for j in range(n_pairs):
            cg = slice(LANES * j, LANES * (j + 1))
            v = vb_ref[0, pl.ds(row0, BAND_B), cg]
            rj = slice(2 * CHUNK * j, 2 * CHUNK * (j + 1))
            o = jnp.dot(p_b[rj], v, preferred_element_type=F32) * inv[rj]
            ob_ref[0, rows, cg] = jnp.where(low, o[:CHUNK], o[CHUNK:]).astype(ob_ref.dtype)

        row_a0 = pl.multiple_of(row0 + (WINDOW_B - WINDOW_A), CHUNK)
        k = ka_ref[0, pl.ds(row_a0, BAND_A), :]
        v = va_ref[0, pl.ds(row_a0, BAND_A), :]
        q = jnp.concatenate(
            [qa_ref[0, rows, LANES * j:LANES * (j + 1)] * half_mask[p]
             for p in range(A_KV_HEADS) for j in range(A_GROUP)], axis=0)
        s = _nt_dot(q, k)
        if masked:
            s = jnp.where(key_a >= first_valid - (WINDOW_B - WINDOW_A), s, NEG_INF)
        mx = jnp.maximum(jnp.max(s, axis=-1, keepdims=True), sink_col)
        e = jnp.exp(s - mx)
        inv = 1.0 / (jnp.sum(e, axis=-1, keepdims=True) + jnp.exp(sink_col - mx))
        o = jnp.dot(e.astype(BF16), v, preferred_element_type=F32) * inv
        for j in range(A_GROUP):
            lo = o[j * CHUNK:(j + 1) * CHUNK]
            hi = o[(A_GROUP + j) * CHUNK:(A_GROUP + j + 1) * CHUNK]
            oa_ref[0, rows, LANES * j:LANES * (j + 1)] = jnp.where(low, lo, hi).astype(oa_ref.dtype)

    steps_with_pad = pl.cdiv(PAD // CHUNK, n_c) if mask_prefix else 0
    if steps_with_pad:
        @pl.when(qi < steps_with_pad)
        def _():
            for c in range(n_c):
                chunk(c, True)

    @pl.when(qi >= steps_with_pad)
    def _():
        for c in range(n_c):
            chunk(c, False)


def _attention(qa, ka, va, qb, kb, vb, sinks, bias, *, n_c, mask_prefix):
    bsz, length, _ = qa.shape
    lp = ka.shape[1]
    qt = n_c * CHUNK
    assert length % qt == 0 and lp == length + PAD

    def qmap(b, i):
        return (b, i, 0)

    def kmap(b, i):
        return (b, 0, 0)

    kern = functools.partial(_attn_kernel, n_c=n_c, mask_prefix=mask_prefix)
    return pl.pallas_call(
        kern,
        grid=(bsz, length // qt),
        in_specs=[
            pl.BlockSpec(memory_space=pltpu.SMEM),
            pl.BlockSpec((1, qt, A_Q_W), qmap),
            pl.BlockSpec((1, lp, A_KV_W), kmap),
            pl.BlockSpec((1, lp, A_KV_W), kmap),
            pl.BlockSpec((1, qt, B_W), qmap),
            pl.BlockSpec((1, lp, B_W), kmap),
            pl.BlockSpec((1, lp, B_W), kmap),
            pl.BlockSpec((B_HEADS // 2, 2 * CHUNK, BAND_B), lambda b, i: (0, 0, 0)),
        ],
        out_specs=(pl.BlockSpec((1, qt, A_Q_W), qmap), pl.BlockSpec((1, qt, B_W), qmap)),
        out_shape=(jax.ShapeDtypeStruct((bsz, length, A_Q_W), BF16),
                   jax.ShapeDtypeStruct((bsz, length, B_W), BF16)),
        compiler_params=_cparams("arbitrary", "arbitrary"),
        name="attention",
    )(sinks, qa, ka, va, qb, kb, vb, bias)


def _merge_kernel(x_ref, oa_ref, ob_ref, sha_ref, sca_ref, gta_ref, shm_ref, scm_ref, ga_ref, gm_ref,
                  wg_ref, wa_ref, wb_ref, wo_ref, wrh_ref, wrl_ref, br_ref,
                  x1_ref, h2_ref, ri_ref, rw_ref, cnt_ref, carry_ref, logit_ref, *, bb, tl, seg_tiles):
    m = bb * tl
    step = pl.program_id(0)

    @pl.when(step == 0)
    def _():
        carry_ref[...] = jnp.zeros_like(carry_ref)
        logit_ref[...] = jnp.zeros_like(logit_ref)

    logits = logit_ref[...]
    lane = lax.broadcasted_iota(I32, (m, LANES), 1)
    lane_f = lane.astype(F32)
    work = logits
    top_v, top_i = [], []
    sel_f = jnp.zeros((m, LANES), F32)
    for _ in range(TOP_K):
        mx = jnp.max(work, axis=-1, keepdims=True)
        idx = jnp.min(jnp.where(work == mx, lane_f, float(LANES)), axis=-1, keepdims=True)
        hit = lane_f == idx
        top_v.append(mx)
        top_i.append(idx)
        sel_f = jnp.where(hit, 1.0, sel_f)
        work = jnp.where(hit, -jnp.inf, work)
    ex = [jnp.exp(v - top_v[0]) for v in top_v]
    inv = 1.0 / (ex[0] + ex[1] + ex[2] + ex[3])

    r_i = lax.broadcasted_iota(I32, (m, m), 0)
    c_i = lax.broadcasted_iota(I32, (m, m), 1)
    before = jnp.where(r_i > c_i, 1.0, 0.0).astype(BF16)
    keep = jnp.where(jnp.maximum(step - 1, 0) % seg_tiles == 0, 0.0, 1.0)
    carry_in = carry_ref[...] * keep
    pos = jnp.dot(before, sel_f.astype(BF16), preferred_element_type=F32) + carry_in
    live = jnp.where(step > 0, 1.0, 0.0)
    carry = carry_in + live * jnp.sum(sel_f, axis=0, keepdims=True)
    carry_ref[...] = carry
    cnt_ref[0] = carry

    ri_f = jnp.zeros((m, LANES), F32)
    rw = jnp.zeros((m, LANES), F32)
    for k in range(TOP_K):
        rank = jnp.sum(jnp.where(lane_f == top_i[k], pos, 0.0), axis=-1, keepdims=True)
        ri_f = jnp.where(lane == k, top_i[k], ri_f)
        ri_f = jnp.where(lane == TOP_K + k, rank, ri_f)
        rw = jnp.where(lane == k, ex[k] * inv, rw)
    ri_ref[...] = ri_f.astype(I32).reshape(bb, tl, LANES)
    rw_ref[...] = rw.reshape(bb, tl, LANES)

    x = x_ref[...]
    h = _modulated_norm(x, ga_ref[...], sca_ref[...], sha_ref[...])
    hb = h.reshape(m, D_MODEL).astype(BF16)
    gates = jnp.dot(hb, wg_ref[...], preferred_element_type=F32)
    ya = jnp.dot(oa_ref[...].reshape(m, A_Q_W), wa_ref[...], preferred_element_type=F32)
    yb = jnp.dot(ob_ref[...].reshape(m, B_W), wb_ref[...], preferred_element_type=F32)
    mix = jax.nn.sigmoid(gates[:, :D_MODEL]) * ya + jax.nn.sigmoid(gates[:, D_MODEL:]) * yb
    z = jnp.dot(mix.astype(BF16), wo_ref[...], preferred_element_type=F32)
    x1 = x + gta_ref[...] * z.reshape(bb, tl, D_MODEL)
    x1_ref[...] = x1

    h2 = _modulated_norm(x1, gm_ref[...], scm_ref[...], shm_ref[...]).reshape(m, D_MODEL)
    for j in range(ROW_TILES):
        h2_ref[pl.ds(j, m, stride=ROW_TILES), :] = h2[:, LANES * j:LANES * (j + 1)]

    h2_hi = h2.astype(BF16)
    h2_lo = (h2 - h2_hi.astype(F32)).astype(BF16)
    wrh = wrh_ref[...]
    logit_ref[...] = (jnp.dot(h2_hi, wrh, preferred_element_type=F32)
                      + jnp.dot(h2_lo, wrh, preferred_element_type=F32)
                      + jnp.dot(h2_hi, wrl_ref[...], preferred_element_type=F32)) + br_ref[...]


def _merge(x, oa, ob, mods, g_attn, g_mlp, wg, wa, wb, wo, wr, br, *, bb, tl, b0, nb):
    _, length, _ = x.shape
    n_tok = nb * length
    m = bb * tl
    assert bb == 1 or tl == length
    sh_a, sc_a, gt_a, sh_m, sc_m = mods
    n_l = length // tl
    n_tiles = (nb // bb) * n_l
    blk0 = b0 // bb
    wr_hi = wr.astype(BF16)
    wr_lo = (wr - wr_hi.astype(F32)).astype(BF16)

    def cur(s):
        return jnp.minimum(s, n_tiles - 1)

    def prev(s):
        return jnp.maximum(s - 1, 0)

    def xmap(s):
        return (blk0 + cur(s) // n_l, cur(s) % n_l, 0)

    def modmap(s):
        return (blk0 + cur(s) // n_l, 0, 0)

    def omap(s):
        return (cur(s) // n_l, cur(s) % n_l, 0)

    def const2(s):
        return (0, 0)

    def flatmap(s):
        return (cur(s), 0)

    def routemap(s):
        return (prev(s) // n_l, prev(s) % n_l, 0)

    modspec = pl.BlockSpec((bb, 1, D_MODEL), modmap)
    kern = functools.partial(_merge_kernel, bb=bb, tl=tl, seg_tiles=n_tiles)
    return pl.pallas_call(
        kern,
        grid=(n_tiles + 1,),
        in_specs=[
            pl.BlockSpec((bb, tl, D_MODEL), xmap),
            pl.BlockSpec((bb, tl, A_Q_W), xmap),
            pl.BlockSpec((bb, tl, B_W), xmap),
            modspec, modspec, modspec, modspec, modspec,
            pl.BlockSpec((1, D_MODEL), const2),
            pl.BlockSpec((1, D_MODEL), const2),
            pl.BlockSpec((D_MODEL, 2 * D_MODEL), const2),
            pl.BlockSpec((A_Q_W, D_MODEL), const2),
            pl.BlockSpec((B_W, D_MODEL), const2),
            pl.BlockSpec((D_MODEL, D_MODEL), const2),
            pl.BlockSpec((D_MODEL, LANES), const2),
            pl.BlockSpec((D_MODEL, LANES), const2),
            pl.BlockSpec((1, LANES), const2),
        ],
        out_specs=(
            pl.BlockSpec((bb, tl, D_MODEL), omap),
            pl.BlockSpec((m * ROW_TILES, LANES), flatmap),
            pl.BlockSpec((bb, tl, LANES), routemap),
            pl.BlockSpec((bb, tl, LANES), routemap),
            pl.BlockSpec((1, 1, LANES), lambda s: (0, 0, 0)),
        ),
        out_shape=(
            jax.ShapeDtypeStruct((nb, length, D_MODEL), F32),
            jax.ShapeDtypeStruct((n_tok * ROW_TILES, LANES), F32),
            jax.ShapeDtypeStruct((nb, length, LANES), I32),
            jax.ShapeDtypeStruct((nb, length, LANES), F32),
            jax.ShapeDtypeStruct((1, 1, LANES), F32),
        ),
        scratch_shapes=[pltpu.VMEM((1, LANES), F32), pltpu.VMEM((m, LANES), F32)],
        compiler_params=_cparams("arbitrary"),
        name="merge",
    )(x, oa, ob, sh_a, sc_a, gt_a, sh_m, sc_m, g_attn.reshape(1, D_MODEL), g_mlp.reshape(1, D_MODEL),
      wg, wa, wb, wo, wr_hi, wr_lo, br)


def _sc_mesh():
    return plsc.VectorSubcoreMesh(core_axis_name="c", subcore_axis_name="s",
                                  num_cores=SC_CORES, num_subcores=SC_SUBCORES)


def _sc_worker():
    return lax.axis_index("c") * SC_SUBCORES + lax.axis_index("s")


def _sc_params():
    return pltpu.CompilerParams(use_tc_tiling_on_sc=True)


def _part_chunks(parts):
    assert all(p % (2 * SC_WORKERS * SC_ROWS) == 0 for p in parts)
    chunks = [p // (SC_WORKERS * SC_ROWS) for p in parts]
    starts = [sum(chunks[:i]) for i in range(len(chunks))]
    return chunks, starts


def _dispatch(h2_parts, idx_flat, pad_flat, n_rows):
    n_parts = len(h2_parts)
    chunks, starts = _part_chunks([h.shape[0] for h in h2_parts])
    n_idx = sum(chunks) * TOP_K * SC_ROWS
    n_pad = pad_flat.shape[0] // SC_WORKERS
    pad_chunks = n_pad // SC_ROWS
    assert n_pad % SC_ROWS == 0
    zeros = jnp.zeros((SC_ROWS, ROW_TILES, LANES), F32)

    def body(*refs):
        h2_hbms = refs[:n_parts]
        idx_hbm, pad_hbm, zeros_hbm, xs_hbm, idx_v, pad_v, buf0, buf1, ls0, ls1, ss0, ss1 = refs[n_parts:]
        wid = _sc_worker()
        bufs, lsems, ssems = (buf0, buf1), (ls0, ls1), (ss0, ss1)
        pltpu.sync_copy(idx_hbm.at[pl.ds(pl.multiple_of(wid * n_idx, SC_ROWS), n_idx)], idx_v)
        pltpu.sync_copy(pad_hbm.at[pl.ds(pl.multiple_of(wid * n_pad, SC_ROWS), n_pad)], pad_v)

        pltpu.sync_copy(zeros_hbm, buf0)
        for c in range(pad_chunks):
            pltpu.async_copy(buf0, xs_hbm.at[pad_v.at[pl.ds(c * SC_ROWS, SC_ROWS)]], ss0)
        for c in range(pad_chunks):
            pltpu.make_async_copy(buf0, xs_hbm.at[pad_v.at[pl.ds(c * SC_ROWS, SC_ROWS)]], ss0).wait()

        def run_part(h2_hbm, n_chunks, c0):
            base = wid * (n_chunks * SC_ROWS)

            def load(g, b):
                src = h2_hbm.at[pl.ds(pl.multiple_of(base + g * SC_ROWS, SC_ROWS), SC_ROWS)]
                return pltpu.make_async_copy(src, bufs[b], lsems[b])

            def scatter(g, k, b):
                off = pl.multiple_of(((c0 + g) * TOP_K + k) * SC_ROWS, SC_ROWS)
                return pltpu.make_async_copy(bufs[b], xs_hbm.at[idx_v.at[pl.ds(off, SC_ROWS)]], ssems[b])

            load(0, 0).start()

            @pl.loop(0, n_chunks, step=2)
            def _(g0):
                for b in range(2):
                    g = g0 + b
                    load(g, b).wait()

                    @pl.when(g >= 1)
                    def _():
                        for k in range(TOP_K):
                            scatter(g - 1, k, 1 - b).wait()

                    @pl.when(g + 1 < n_chunks)
                    def _():
                        load(g + 1, 1 - b).start()

                    for k in range(TOP_K):
                        scatter(g, k, b).start()

            for k in range(TOP_K):
                scatter(n_chunks - 1, k, (n_chunks - 1) % 2).wait()

        for h2_hbm, n_chunks, c0 in zip(h2_hbms, chunks, starts):
            run_part(h2_hbm, n_chunks, c0)

    call = pl.kernel(
        body,
        out_type=jax.ShapeDtypeStruct((n_rows + TRASH_ROWS, ROW_TILES, LANES), F32),
        mesh=_sc_mesh(),
        scratch_types=[pltpu.VMEM((n_idx,), I32), pltpu.VMEM((n_pad,), I32),
                       pltpu.VMEM((SC_ROWS, ROW_TILES, LANES), F32), pltpu.VMEM((SC_ROWS, ROW_TILES, LANES), F32),
                       pltpu.SemaphoreType.DMA, pltpu.SemaphoreType.DMA,
                       pltpu.SemaphoreType.DMA, pltpu.SemaphoreType.DMA],
        compiler_params=_sc_params(),
        name="dispatch",
    )
    return call(*h2_parts, idx_flat, pad_flat, zeros)


def _gather_back(y_rows, idx_flat, parts):
    n_parts = len(parts)
    chunks, starts = _part_chunks(parts)
    n_idx = sum(chunks) * TOP_K * SC_ROWS

    def body(*refs):
        y_hbm, idx_hbm = refs[:2]
        yg_hbms = refs[2:2 + n_parts]
        idx_v, buf0, buf1, gs0, gs1, ws0, ws1 = refs[2 + n_parts:]
        wid = _sc_worker()
        bufs, gsems, wsems = (buf0, buf1), (gs0, gs1), (ws0, ws1)
        pltpu.sync_copy(idx_hbm.at[pl.ds(pl.multiple_of(wid * n_idx, SC_ROWS), n_idx)], idx_v)

        def run_part(yg_hbm, n_chunks, c0):
            base = wid * (n_chunks * SC_ROWS)
            n_items = n_chunks * TOP_K

            def gather(i, b):
                off = pl.multiple_of((c0 * TOP_K + i) * SC_ROWS, SC_ROWS)
                return pltpu.make_async_copy(y_hbm.at[idx_v.at[pl.ds(off, SC_ROWS)]], bufs[b], gsems[b])

            def write(i, b):
                g, k = i // TOP_K, i % TOP_K
                dst = yg_hbm.at[k, pl.ds(pl.multiple_of(base + g * SC_ROWS, SC_ROWS), SC_ROWS)]
                return pltpu.make_async_copy(bufs[b], dst, wsems[b])

            gather(0, 0).start()

            @pl.loop(0, n_items, step=2)
            def _(i0):
                for b in range(2):
                    i = i0 + b
                    gather(i, b).wait()

                    @pl.when(i >= 1)
                    def _():
                        write(i - 1, 1 - b).wait()

                    @pl.when(i + 1 < n_items)
                    def _():
                        gather(i + 1, 1 - b).start()

                    write(i, b).start()

            write(n_items - 1, (n_items - 1) % 2).wait()

        for yg_hbm, n_chunks, c0 in zip(yg_hbms, chunks, starts):
            run_part(yg_hbm, n_chunks, c0)

    call = pl.kernel(
        body,
        out_type=tuple(jax.ShapeDtypeStruct((TOP_K, p, ROW_TILES, LANES), F32) for p in parts),
        mesh=_sc_mesh(),
        scratch_types=[pltpu.VMEM((n_idx,), I32),
                       pltpu.VMEM((SC_ROWS, ROW_TILES, LANES), F32), pltpu.VMEM((SC_ROWS, ROW_TILES, LANES), F32),
                       pltpu.SemaphoreType.DMA, pltpu.SemaphoreType.DMA,
                       pltpu.SemaphoreType.DMA, pltpu.SemaphoreType.DMA],
        compiler_params=_sc_params(),
        name="gather_back",
    )
    return call(y_rows, idx_flat)


def _expert_kernel(blk_ref, used_ref, xs_ref, wgu_ref, bgu_ref, wd_ref, bd_ref, y_ref, wgu_bf, wd_bf):
    r = pl.program_id(0)
    e = blk_ref[r]
    prev = blk_ref[jnp.maximum(r - 1, 0)]
    active = r < used_ref[0]

    @pl.when(jnp.logical_and(active, jnp.logical_or(r == 0, e != prev)))
    def _():
        wgu_bf[...] = wgu_ref[0].astype(BF16)
        wd_bf[...] = wd_ref[0].astype(BF16)

    @pl.when(active)
    def _():
        x = jnp.concatenate(
            [xs_ref[pl.ds(j, ROUTE_BLOCK, stride=ROW_TILES), :] for j in range(ROW_TILES)], axis=1).astype(BF16)
        acts = []
        for j in range(D_FF // FF_TILE):
            gc = slice(FF_TILE * j, FF_TILE * (j + 1))
            uc = slice(D_FF + FF_TILE * j, D_FF + FF_TILE * (j + 1))
            g = jnp.dot(x, wgu_bf[:, gc], preferred_element_type=F32) + bgu_ref[0, :, gc]
            u = jnp.dot(x, wgu_bf[:, uc], preferred_element_type=F32) + bgu_ref[0, :, uc]
            gate = jnp.minimum(g, SWIGLU_LIMIT)
            up = jnp.clip(u, -SWIGLU_LIMIT, SWIGLU_LIMIT)
            glu = gate * jax.nn.sigmoid(gate * SWIGLU_ALPHA)
            acts.append(((up + 1.0) * glu).astype(BF16))
        act = jnp.concatenate(acts, axis=1)
        y = jnp.dot(act, wd_bf[...], preferred_element_type=F32) + bd_ref[0]
        for j in range(ROW_TILES):
            y_ref[pl.ds(j, ROUTE_BLOCK, stride=ROW_TILES), :] = y[:, LANES * j:LANES * (j + 1)]


def _experts(xs_rows, blk_expert, n_used, w_gate_up, b_gate_up, w_down, b_down):
    n_blocks = blk_expert.shape[0]
    rb = ROUTE_BLOCK * ROW_TILES

    def rowmap(r, blk, used):
        return (jnp.minimum(r, used[0] - 1), 0)

    def wmap(r, blk, used):
        return (blk[jnp.minimum(r, used[0] - 1)], 0, 0)

    return pl.pallas_call(
        _expert_kernel,
        grid_spec=pltpu.PrefetchScalarGridSpec(
            num_scalar_prefetch=2,
            grid=(n_blocks,),
            in_specs=[
                pl.BlockSpec((rb, LANES), rowmap),
                pl.BlockSpec((1, D_MODEL, 2 * D_FF), wmap),
                pl.BlockSpec((1, 1, 2 * D_FF), wmap),
                pl.BlockSpec((1, D_FF, D_MODEL), wmap),
                pl.BlockSpec((1, 1, D_MODEL), wmap),
            ],
            out_specs=pl.BlockSpec((rb, LANES), rowmap),
            scratch_shapes=[pltpu.VMEM((D_MODEL, 2 * D_FF), BF16), pltpu.VMEM((D_FF, D_MODEL), BF16)],
        ),
        out_shape=jax.ShapeDtypeStruct(xs_rows.shape, F32),
        compiler_params=_cparams("arbitrary"),
        name="experts",
    )(blk_expert, n_used, xs_rows, w_gate_up, b_gate_up.reshape(N_EXPERTS, 1, 2 * D_FF),
      w_down, b_down.reshape(N_EXPERTS, 1, D_MODEL))


def _combine_kernel(x1_ref, gt_ref, rw_ref, yg_ref, *rest, bb, tl):
    o_ref = rest[-1]
    m = bb * tl
    rw = rw_ref[...].reshape(m, LANES)
    w = [rw[:, k:k + 1] for k in range(TOP_K)]
    gt = gt_ref[...]
    for j in range(ROW_TILES):
        acc = jnp.zeros((m, LANES), F32)
        for k in range(TOP_K):
            acc = acc + w[k] * yg_ref[k, pl.ds(j, m, stride=ROW_TILES), :]
        cg = slice(LANES * j, LANES * (j + 1))
        o_ref[:, :, cg] = x1_ref[:, :, cg] + gt[:, :, cg] * acc.reshape(bb, tl, LANES)


def _combine(x1, gt_m, rw, yg, y_prev, *, bb, tl, b0, bsz):
    nb, length, _ = x1.shape
    m = bb * tl
    assert bb == 1 or tl == length
    n_l = length // tl
    blk0 = b0 // bb

    def smap(b, i):
        return (b, i, 0)

    def xmap(b, i):
        return (blk0 + b, i, 0)

    in_specs = [
        pl.BlockSpec((bb, tl, D_MODEL), smap),
        pl.BlockSpec((bb, 1, D_MODEL), lambda b, i: (blk0 + b, 0, 0)),
        pl.BlockSpec((bb, tl, LANES), smap),
        pl.BlockSpec((TOP_K, m * ROW_TILES, LANES), lambda b, i: (0, b * n_l + i, 0)),
    ]
    args = [x1, gt_m, rw, yg]
    aliases = {}
    if y_prev is not None:
        in_specs.append(pl.BlockSpec(memory_space=pl.ANY))
        args.append(y_prev)
        aliases = {len(args) - 1: 0}
    kern = functools.partial(_combine_kernel, bb=bb, tl=tl)
    return pl.pallas_call(
        kern,
        grid=(nb // bb, n_l),
        in_specs=in_specs,
        out_specs=pl.BlockSpec((bb, tl, D_MODEL), xmap),
        out_shape=jax.ShapeDtypeStruct((bsz, length, D_MODEL), F32),
        input_output_aliases=aliases,
        compiler_params=_cparams("arbitrary", "arbitrary"),
        name="combine",
    )(*args)


def _route_plan(routes):
    expert_ids = jnp.arange(N_EXPERTS, dtype=I32)
    part_counts = [c.reshape(-1)[:N_EXPERTS].astype(I32) for _, c in routes]
    counts = sum(part_counts)
    padded = (counts + ROUTE_BLOCK - 1) // ROUTE_BLOCK * ROUTE_BLOCK
    pends = jnp.cumsum(padded)
    pstarts = pends - padded
    n_assign = sum(ri.shape[0] * ri.shape[1] for ri, _ in routes) * TOP_K
    n_blocks = n_assign // ROUTE_BLOCK + N_EXPERTS
    n_rows = n_blocks * ROUTE_BLOCK
    lists = []
    first = jnp.zeros_like(counts)
    for (ri, _), cnt in zip(routes, part_counts):
        n_tok = ri.shape[0] * ri.shape[1]
        idx = ri[..., :TOP_K].reshape(n_tok, TOP_K)
        rank = ri[..., TOP_K:2 * TOP_K].reshape(n_tok, TOP_K)
        dest = jnp.sum(jnp.where(idx[..., None] == expert_ids, pstarts + first, 0), axis=-1) + rank
        per_w = n_tok // SC_WORKERS
        lists.append(dest.astype(I32).reshape(SC_WORKERS, per_w // SC_ROWS, SC_ROWS, TOP_K).transpose(0, 1, 3, 2))
        first = first + cnt
    idx_flat = jnp.concatenate(lists, axis=1).reshape(-1)
    blk_start = jnp.arange(n_blocks, dtype=I32) * ROUTE_BLOCK
    blk_expert = jnp.minimum(jnp.sum((pends[None, :] <= blk_start[:, None]).astype(I32), axis=1), N_EXPERTS - 1)
    n_used = (pends[-1] // ROUTE_BLOCK).reshape(1)
    r = jnp.arange(ROUTE_BLOCK, dtype=I32)[None, :]
    pad_flat = jnp.where(r < (padded - counts)[:, None], (pstarts + counts)[:, None] + r, n_rows)
    return idx_flat, pad_flat.astype(I32).reshape(-1), blk_expert.astype(I32), n_used.astype(I32), n_rows


def _front(x, mods, pos, prefix, weights, *, bb, tl, n_c):
    g_attn, w_qkv, gains, sinks, bias = weights[0], weights[2], weights[3], weights[4], weights[5]
    sh_a, sc_a = mods[0], mods[1]
    pad = 0 if prefix is not None else PAD
    qa, ka, va, qb, kb, vb, ska, sva, skb, svb = _stage1(
        x, sh_a, sc_a, g_attn, w_qkv, gains, pos, bb=bb, tl=tl, pad=pad)
    if prefix is not None:
        ka, va, kb, vb = [jnp.concatenate([p, t], axis=1) for p, t in zip(prefix, (ka, va, kb, vb))]
    oa, ob = _attention(qa, ka, va, qb, kb, vb, sinks, bias, n_c=n_c, mask_prefix=prefix is None)
    return oa, ob, (ska, sva, skb, svb)


def _moe_segment(parts, weights):
    (g_attn, g_mlp, _, _, _, _, wg, wa, wb, wo, wr, br, w_gate_up, b_gate_up, w_down, b_down) = weights
    merged = []
    for p in parts:
        sh_a, sc_a, gt_a, sh_m, sc_m, _ = p["mods"]
        merged.append(_merge(p["x"], p["oa"], p["ob"], (sh_a, sc_a, gt_a, sh_m, sc_m), g_attn, g_mlp,
                             wg, wa, wb, wo, wr, br, bb=p["cfg"]["bb"], tl=p["cfg"]["tl"], b0=p["b0"], nb=p["nb"]))
    toks = [m[0].shape[0] * m[0].shape[1] for m in merged]
    idx_flat, pad_flat, blk_expert, n_used, n_rows = _route_plan([(m[2], m[4]) for m in merged])
    xs = _dispatch([m[1].reshape(t, ROW_TILES, LANES) for m, t in zip(merged, toks)], idx_flat, pad_flat, n_rows)
    y_rows = _experts(xs.reshape(-1, LANES), blk_expert, n_used, w_gate_up, b_gate_up, w_down, b_down)
    ygs = _gather_back(y_rows.reshape(-1, ROW_TILES, LANES), idx_flat, toks)
    outs = []
    for p, m, yg, t in zip(parts, merged, ygs, toks):
        outs.append(_combine(m[0], p["mods"][5], m[3], yg.reshape(TOP_K, t * ROW_TILES, LANES), p["y"],
                             bb=p["cfg"]["bb_combine"], tl=p["cfg"]["tl_combine"], b0=p["b0"],
                             bsz=p["x"].shape[0]))
    return outs


def _rel_bias_table(rel_table):
    n_rel = rel_table.shape[1]
    ext = jnp.concatenate(
        [rel_table, jnp.broadcast_to(rel_table[:, -1:], (B_HEADS, CHUNK + BAND_B - n_rel))], axis=1)
    rows = jnp.stack([ext[:, i:i + BAND_B] for i in range(CHUNK)], axis=1)[..., ::-1]
    return rows.astype(F32).reshape(B_HEADS // 2, 2 * CHUNK, BAND_B)


def _prep_weights(g_attn, g_mlp, w_in, q_norm_a, k_norm_a, q_norm_b, k_norm_b, sinks_a, rel_table_b,
                  w_branch_a, w_branch_b, w_out, w_router, b_router, w_gate_up, b_gate_up, w_down, b_down):
    order = jnp.asarray(A_HEAD_ORDER)
    w_qa = w_in[:, :A_Q_W].reshape(D_MODEL, A_Q_HEADS, HEAD_DIM)[:, order].reshape(D_MODEL, A_Q_W)
    w_qkv = jnp.concatenate([w_qa, w_in[:, A_Q_W:QKV_W]], axis=1).astype(BF16)
    wg = w_in[:, QKV_W:].astype(BF16)
    wa = w_branch_a.reshape(A_Q_HEADS, HEAD_DIM, D_MODEL)[order].reshape(A_Q_W, D_MODEL).astype(BF16)
    wb = w_branch_b.astype(BF16)
    wo = w_out.astype(BF16)
    wr = jnp.zeros((D_MODEL, LANES), F32).at[:, :N_EXPERTS].set(w_router)
    br = jnp.full((1, LANES), NEG_INF, F32).at[0, :N_EXPERTS].set(b_router)
    bias = _rel_bias_table(rel_table_b)
    gains = (q_norm_a, k_norm_a, q_norm_b, k_norm_b)
    return (g_attn, g_mlp, w_qkv, gains, sinks_a.astype(F32), bias, wg, wa, wb, wo, wr, br,
            w_gate_up, b_gate_up, w_down, b_down)


def _forward(x_prompt, x_sample, c_prompt, c_sample, cache_a_k, cache_a_v, cache_b_k, cache_b_v, params,
             past_len, cfg_prompt, cfg_sample):
    (g_attn, g_mlp, w_ada, b_ada, w_in, q_norm_a, k_norm_a, q_norm_b, k_norm_b, sinks_a, rel_table_b,
     w_branch_a, w_branch_b, w_out, w_router, b_router, w_gate_up, b_gate_up, w_down, b_down) = params
    depth = g_attn.shape[0]
    bp, lp_, _ = x_prompt.shape
    bs, ls, _ = x_sample.shape
    pos_p = jnp.arange(lp_, dtype=I32)
    pos_s = past_len + jnp.arange(ls, dtype=I32)
    yp, ys = x_prompt, x_sample
    st_p, st_s = [], []
    for l in range(depth):
        weights = _prep_weights(g_attn[l], g_mlp[l], w_in[l], q_norm_a[l], k_norm_a[l], q_norm_b[l],
                                k_norm_b[l], sinks_a[l], rel_table_b[l], w_branch_a[l], w_branch_b[l],
                                w_out[l], w_router[l], b_router[l], w_gate_up[l], b_gate_up[l],
                                w_down[l], b_down[l])
        mod = _adaln(jnp.concatenate([c_prompt, c_sample], axis=0), w_ada[l], b_ada[l])
        mod = mod.reshape(bp + bs, 6, 1, D_MODEL)
        mods_p = [mod[:bp, k] for k in range(6)]
        mods_s = [mod[bp:, k] for k in range(6)]
        wa_c = cache_a_k.shape[2]
        wb_c = cache_b_k.shape[2]
        assert wb_c == WINDOW_B and wa_c == WINDOW_A
        zeros_a = jnp.zeros((bs, PAD - wa_c, A_KV_W), BF16)
        prefix = (
            jnp.concatenate([zeros_a, cache_a_k[l].reshape(bs, wa_c, A_KV_W).astype(BF16)], axis=1),
            jnp.concatenate([zeros_a, cache_a_v[l].reshape(bs, wa_c, A_KV_W).astype(BF16)], axis=1),
            cache_b_k[l].reshape(bs, wb_c, B_W).astype(BF16),
            cache_b_v[l].reshape(bs, wb_c, B_W).astype(BF16),
        )
        front = {k: cfg_prompt[k] for k in ("bb", "tl", "n_c")}
        oa_p, ob_p, sp = _front(yp, mods_p, pos_p, None, weights, **front)
        front = {k: cfg_sample[k] for k in ("bb", "tl", "n_c")}
        oa_s, ob_s, ss = _front(ys, mods_s, pos_s, prefix, weights, **front)
        n_seg = cfg_prompt["n_seg"]
        nb = bp // n_seg
        xp, xs_in = yp, ys
        yp = None
        for s in range(n_seg):
            parts = [dict(x=xp, oa=oa_p, ob=ob_p, mods=mods_p, b0=s * nb, nb=nb, cfg=cfg_prompt, y=yp)]
            if s == n_seg - 1:
                parts.append(dict(x=xs_in, oa=oa_s, ob=ob_s, mods=mods_s, b0=0, nb=bs, cfg=cfg_sample, y=None))
            outs = _moe_segment(parts, weights)
            yp = outs[0]
            if s == n_seg - 1:
                ys = outs[1]
        st_p.append(sp)
        st_s.append(ss)

    def stack(states, k, heads):
        return jnp.stack([s[k].reshape(s[k].shape[0], s[k].shape[1], heads, HEAD_DIM) for s in states])

    return (yp, ys,
            stack(st_p, 0, A_KV_HEADS), stack(st_p, 1, A_KV_HEADS), stack(st_p, 2, B_HEADS), stack(st_p, 3, B_HEADS),
            stack(st_s, 0, A_KV_HEADS), stack(st_s, 1, A_KV_HEADS), stack(st_s, 2, B_HEADS), stack(st_s, 3, B_HEADS))


PAST_LEN = 2048
CFG_PROMPT = dict(bb=1, tl=512, n_c=4, tl_combine=512, bb_combine=1, n_seg=2)
CFG_SAMPLE = dict(bb=8, tl=64, n_c=1, tl_combine=64, bb_combine=8)


def kernel(x_prompt, x_sample, c_prompt, c_sample, cache_a_k, cache_a_v, cache_b_k, cache_b_v, g_attn, g_mlp,
           w_ada, b_ada, w_in, q_norm_a, k_norm_a, q_norm_b, k_norm_b, sinks_a, rel_table_b, w_branch_a,
           w_branch_b, w_out, w_router, b_router, w_gate_up, b_gate_up, w_down, b_down):
    params = (g_attn, g_mlp, w_ada, b_ada, w_in, q_norm_a, k_norm_a, q_norm_b, k_norm_b, sinks_a, rel_table_b,
              w_branch_a, w_branch_b, w_out, w_router, b_router, w_gate_up, b_gate_up, w_down, b_down)
    return _forward(x_prompt, x_sample, c_prompt, c_sample, cache_a_k, cache_a_v, cache_b_k, cache_b_v,
                    params, PAST_LEN, CFG_PROMPT, CFG_SAMPLE)
```

```python
import functools

import jax
import jax.numpy as jnp
from jax import lax
from jax.experimental import pallas as pl
from jax.experimental.pallas import tpu as pltpu
from jax.experimental.pallas import tpu_sc as plsc

F32 = jnp.float32
BF16 = jnp.bfloat16
I32 = jnp.int32

D_MODEL = 1024
CHUNK = 64
HEAD_DIM = 64
LANES = 128
SUBLANES = 8
A_Q_HEADS = 8
A_KV_HEADS = 2
A_GROUP = A_Q_HEADS // A_KV_HEADS
B_HEADS = 8
A_Q_W = A_Q_HEADS * HEAD_DIM
A_KV_W = A_KV_HEADS * HEAD_DIM
B_W = B_HEADS * HEAD_DIM
QKV_W = A_Q_W + 2 * A_KV_W + 3 * B_W
WINDOW_A = 128
WINDOW_B = 512
BAND_A = WINDOW_A + CHUNK
BAND_B = WINDOW_B + CHUNK
PAD = WINDOW_B
ROT_DIM = HEAD_DIM // 4
ROPE_THETA = 500000.0
REL_BACK = 4 * CHUNK
REL_FWD = CHUNK - 1
N_EXPERTS = 32
TOP_K = 4
D_FF = D_MODEL
SWIGLU_LIMIT = 7.0
SWIGLU_ALPHA = 1.702
ROUTE_BLOCK = 512
FF_TILE = 256
RMS_EPS = 1e-6
NEG_INF = -1e30
HIGH_HALF = -65536
HALF_W = D_MODEL // 2
ROW_TILES = HALF_W // LANES
VMEM_LIMIT = 56 * 1024 * 1024
SC_CORES = 2
SC_SUBCORES = 16
SC_WORKERS = SC_CORES * SC_SUBCORES
SC_ROWS = 32
TRASH_ROWS = 8

A_HEAD_ORDER = (0, 4, 1, 5, 2, 6, 3, 7)


def _cparams(*sem):
    return pltpu.CompilerParams(dimension_semantics=sem, vmem_limit_bytes=VMEM_LIMIT)


def _adaln_kernel(c_ref, w_ref, b_ref, o_ref):
    c = c_ref[...]
    a = c * jax.nn.sigmoid(c)
    o_ref[...] = jnp.dot(a, w_ref[...], preferred_element_type=F32,
                         precision=lax.Precision.HIGHEST) + b_ref[...]


def _adaln(c, w_ada, b_ada):
    n_rows = c.shape[0]
    n_out = w_ada.shape[1]
    tn = 1024
    return pl.pallas_call(
        _adaln_kernel,
        grid=(n_out // tn,),
        in_specs=[pl.BlockSpec((n_rows, D_MODEL), lambda j: (0, 0)),
                  pl.BlockSpec((D_MODEL, tn), lambda j: (0, j)),
                  pl.BlockSpec((1, tn), lambda j: (0, j))],
        out_specs=pl.BlockSpec((n_rows, tn), lambda j: (0, j)),
        out_shape=jax.ShapeDtypeStruct((n_rows, n_out), F32),
        compiler_params=_cparams("arbitrary"),
        name="adaln",
    )(c, w_ada, b_ada.reshape(1, n_out))


def _pack_rows(ref, x):
    m = x.shape[0]
    for j in range(ROW_TILES):
        lo = x[:, LANES * j:LANES * (j + 1)].astype(BF16).astype(F32)
        hi = x[:, HALF_W + LANES * j:HALF_W + LANES * (j + 1)].astype(BF16).astype(F32)
        word = (lax.bitcast_convert_type(hi, I32) & HIGH_HALF) | lax.shift_right_logical(
            lax.bitcast_convert_type(lo, I32), 16)
        ref[pl.ds(j, m, stride=ROW_TILES), :] = word


def _unpack_rows(words):
    lo = [lax.bitcast_convert_type(lax.shift_left(w, 16), F32) for w in words]
    hi = [lax.bitcast_convert_type(w & HIGH_HALF, F32) for w in words]
    return lo, hi


def _modulated_norm(x, g, scale, shift):
    ms = jnp.mean(x * x, axis=-1, keepdims=True)
    return (x * lax.rsqrt(ms + RMS_EPS) * g) * (1.0 + scale) + shift


def _head_norm(parts, gain, seg):
    m = parts[0].shape[0]
    sq = jnp.concatenate([p * p for p in parts], axis=0).astype(BF16)
    ms = jnp.dot(sq, seg, preferred_element_type=F32)
    r = lax.rsqrt(ms + RMS_EPS)
    return [parts[j] * r[j * m:(j + 1) * m] * gain for j in range(len(parts))]


def _rope(p, cos, sin_lo, sin_hi):
    return p * cos + pltpu.roll(p, LANES - ROT_DIM // 2, 1) * sin_lo + pltpu.roll(p, ROT_DIM // 2, 1) * sin_hi


def _stage1_kernel(x_ref, sh_ref, sc_ref, g_ref, w_ref, gqa_ref, gka_ref, gqb_ref, gkb_ref, seg_ref,
                   cos_ref, sinlo_ref, sinhi_ref,
                   qa_ref, ka_ref, va_ref, qb_ref, kb_ref, vb_ref, ska_ref, sva_ref, skb_ref, svb_ref,
                   *, bb, tl, pad_tiles, n_tiles, sa_rows, sb_rows):
    i = pl.program_id(1)
    m = bb * tl

    if pad_tiles:
        @pl.when(i < pad_tiles)
        def _():
            ka_ref[...] = jnp.zeros_like(ka_ref)
            va_ref[...] = jnp.zeros_like(va_ref)
            kb_ref[...] = jnp.zeros_like(kb_ref)
            vb_ref[...] = jnp.zeros_like(vb_ref)

    @pl.when(i >= pad_tiles)
    def _():
        h = _modulated_norm(x_ref[...], g_ref[...], sc_ref[...], sh_ref[...])
        hb = h.reshape(m, D_MODEL).astype(BF16)
        qkv = jnp.dot(hb, w_ref[...], preferred_element_type=F32)
        seg = seg_ref[...]

        def cols(lo, width):
            return [qkv[:, lo + LANES * j: lo + LANES * (j + 1)] for j in range(width // LANES)]

        cos = jnp.concatenate([cos_ref[...]] * bb, axis=0)
        sin_lo = jnp.concatenate([sinlo_ref[...]] * bb, axis=0)
        sin_hi = jnp.concatenate([sinhi_ref[...]] * bb, axis=0)
        scale = HEAD_DIM ** -0.5

        qa = [_rope(p, cos, sin_lo, sin_hi) * scale for p in _head_norm(cols(0, A_Q_W), gqa_ref[...], seg)]
        ka = [_rope(p, cos, sin_lo, sin_hi) for p in _head_norm(cols(A_Q_W, A_KV_W), gka_ref[...], seg)]
        va = cols(A_Q_W + A_KV_W, A_KV_W)
        off_b = A_Q_W + 2 * A_KV_W
        qb = [p * scale for p in _head_norm(cols(off_b, B_W), gqb_ref[...], seg)]
        kb = _head_norm(cols(off_b + B_W, B_W), gkb_ref[...], seg)
        vb = cols(off_b + 2 * B_W, B_W)

        def put(ref, parts):
            for j, p in enumerate(parts):
                ref[:, :, LANES * j:LANES * (j + 1)] = p.reshape(bb, tl, LANES).astype(ref.dtype)

        put(qa_ref, qa)
        put(ka_ref, ka)
        put(va_ref, va)
        put(qb_ref, qb)
        put(kb_ref, kb)
        put(vb_ref, vb)

        @pl.when(i == n_tiles - 1)
        def _():
            def put_state(ref, parts, rows):
                for j, p in enumerate(parts):
                    ref[:, :, LANES * j:LANES * (j + 1)] = p.reshape(bb, tl, LANES)[:, tl - rows:, :]

            put_state(ska_ref, ka, sa_rows)
            put_state(sva_ref, va, sa_rows)
            put_state(skb_ref, kb, sb_rows)
            put_state(svb_ref, vb, sb_rows)


def _rope_tables(pos):
    inv_freq = ROPE_THETA ** (-jnp.arange(0, ROT_DIM, 2, dtype=F32) / ROT_DIM)
    ang = pos.astype(F32)[:, None] * inv_freq[None, :]
    half = ROT_DIM // 2
    lane = jnp.arange(LANES) % HEAD_DIM
    ang_l = jnp.tile(ang, (1, LANES // half))
    rot = lane < ROT_DIM
    cos = jnp.where(rot[None, :], jnp.cos(ang_l), 1.0)
    sin_lo = jnp.where((lane < half)[None, :], -jnp.sin(ang_l), 0.0)
    sin_hi = jnp.where(((lane >= half) & rot)[None, :], jnp.sin(ang_l), 0.0)
    return cos, sin_lo, sin_hi


def _stage1(x, shift, scale, g_attn, w_qkv, gains, pos, *, bb, tl, pad):
    bsz, length, _ = x.shape
    n_tiles = length // tl
    pad_tiles = pad // tl
    sa_rows, sb_rows = min(WINDOW_A, length), min(WINDOW_B, length)
    assert bsz % bb == 0 and length % tl == 0 and pad % tl == 0 and sb_rows <= tl
    cos, sin_lo, sin_hi = _rope_tables(pos)
    seg = (jnp.arange(LANES)[:, None] // HEAD_DIM == jnp.arange(LANES)[None, :] // HEAD_DIM)
    seg = (seg.astype(F32) / HEAD_DIM).astype(BF16)
    gq_a, gk_a, gq_b, gk_b = [jnp.tile(g, LANES // HEAD_DIM).reshape(1, LANES) for g in gains]

    def xmap(b, i):
        return (b, jnp.maximum(i - pad_tiles, 0), 0)

    def const2(b, i):
        return (0, 0)

    def tabmap(b, i):
        return (jnp.maximum(i - pad_tiles, 0), 0)

    def modmap(b, i):
        return (b, 0, 0)

    def kvmap(b, i):
        return (b, i, 0)

    lp = length + pad
    kern = functools.partial(_stage1_kernel, bb=bb, tl=tl, pad_tiles=pad_tiles,
                             n_tiles=n_tiles + pad_tiles, sa_rows=sa_rows, sb_rows=sb_rows)
    out_shapes = (
        jax.ShapeDtypeStruct((bsz, length, A_Q_W), BF16),
        jax.ShapeDtypeStruct((bsz, lp, A_KV_W), BF16),
        jax.ShapeDtypeStruct((bsz, lp, A_KV_W), BF16),
        jax.ShapeDtypeStruct((bsz, length, B_W), BF16),
        jax.ShapeDtypeStruct((bsz, lp, B_W), BF16),
        jax.ShapeDtypeStruct((bsz, lp, B_W), BF16),
        jax.ShapeDtypeStruct((bsz, sa_rows, A_KV_W), F32),
        jax.ShapeDtypeStruct((bsz, sa_rows, A_KV_W), F32),
        jax.ShapeDtypeStruct((bsz, sb_rows, B_W), F32),
        jax.ShapeDtypeStruct((bsz, sb_rows, B_W), F32),
    )
    out_specs = (
        pl.BlockSpec((bb, tl, A_Q_W), xmap),
        pl.BlockSpec((bb, tl, A_KV_W), kvmap),
        pl.BlockSpec((bb, tl, A_KV_W), kvmap),
        pl.BlockSpec((bb, tl, B_W), xmap),
        pl.BlockSpec((bb, tl, B_W), kvmap),
        pl.BlockSpec((bb, tl, B_W), kvmap),
        pl.BlockSpec((bb, sa_rows, A_KV_W), modmap),
        pl.BlockSpec((bb, sa_rows, A_KV_W), modmap),
        pl.BlockSpec((bb, sb_rows, B_W), modmap),
        pl.BlockSpec((bb, sb_rows, B_W), modmap),
    )
    in_specs = [
        pl.BlockSpec((bb, tl, D_MODEL), xmap),
        pl.BlockSpec((bb, 1, D_MODEL), modmap),
        pl.BlockSpec((bb, 1, D_MODEL), modmap),
        pl.BlockSpec((1, D_MODEL), const2),
        pl.BlockSpec((D_MODEL, QKV_W), const2),
        pl.BlockSpec((1, LANES), const2),
        pl.BlockSpec((1, LANES), const2),
        pl.BlockSpec((1, LANES), const2),
        pl.BlockSpec((1, LANES), const2),
        pl.BlockSpec((LANES, LANES), const2),
        pl.BlockSpec((tl, LANES), tabmap),
        pl.BlockSpec((tl, LANES), tabmap),
        pl.BlockSpec((tl, LANES), tabmap),
    ]
    return pl.pallas_call(
        kern,
        grid=(bsz // bb, n_tiles + pad_tiles),
        in_specs=in_specs,
        out_specs=out_specs,
        out_shape=out_shapes,
        compiler_params=_cparams("arbitrary", "arbitrary"),
        name="stage1",
    )(x, shift, scale, g_attn.reshape(1, D_MODEL), w_qkv, gq_a, gk_a, gq_b, gk_b, seg, cos, sin_lo, sin_hi)


def _nt_dot(a, b):
    return lax.dot_general(a, b, (((1,), (1,)), ((), ())), preferred_element_type=F32)


def _attn_kernel(sink_ref, qa_ref, ka_ref, va_ref, qb_ref, kb_ref, vb_ref, bias_ref, oa_ref, ob_ref,
                 *, n_c, mask_prefix):
    qi = pl.program_id(1)
    lane = lax.broadcasted_iota(I32, (CHUNK, LANES), 1)
    low = lane < HEAD_DIM
    half_mask = (jnp.where(low, 1.0, 0.0).astype(BF16), jnp.where(low, 0.0, 1.0).astype(BF16))
    n_pairs = B_W // LANES
    rows_b = n_pairs * 2 * CHUNK
    rows_a = A_Q_HEADS * CHUNK
    key_b = lax.broadcasted_iota(I32, (rows_b, BAND_B), 1)
    key_a = lax.broadcasted_iota(I32, (rows_a, BAND_A), 1)
    row_a = lax.broadcasted_iota(I32, (rows_a, 1), 0) // CHUNK

    sink_col = jnp.zeros((rows_a, 1), F32)
    for r in range(A_Q_HEADS):
        sink_col = jnp.where(row_a == r, sink_ref[r], sink_col)

    def chunk(c, masked):
        ci = qi * n_c + c
        row0 = pl.multiple_of(ci * CHUNK, CHUNK)
        rows = slice(c * CHUNK, (c + 1) * CHUNK)
        first_valid = PAD - ci * CHUNK

        s_parts = []
        for j in range(n_pairs):
            cg = slice(LANES * j, LANES * (j + 1))
            q = qb_ref[0, rows, cg]
            qs = jnp.concatenate([q * half_mask[0], q * half_mask[1]], axis=0)
            k = kb_ref[0, pl.ds(row0, BAND_B), cg]
            s_parts.append(_nt_dot(qs, k) + bias_ref[j])
        s = jnp.concatenate(s_parts, axis=0)
        if masked:
            s = jnp.where(key_b >= first_valid, s, NEG_INF)
        mx = jnp.max(s, axis=-1, keepdims=True)
        e = jnp.exp(s - mx)
        inv = 1.0 / jnp.sum(e, axis=-1, keepdims=True)
        p_b = e.astype(BF16)
        for j in range(n_pairs):
            cg = slice(LANES * j, LANES * (j + 1))
            v = vb_ref[0, pl.ds(row0, BAND_B), cg]
            rj = slice(2 * CHUNK * j, 2 * CHUNK * (j + 1))
            o = jnp.dot(p_b[rj], v, preferred_element_type=F32) * inv[rj]
            ob_ref[0, rows, cg] = jnp.where(low, o[:CHUNK], o[CHUNK:]).astype(ob_ref.dtype)

        row_a0 = pl.multiple_of(row0 + (WINDOW_B - WINDOW_A), CHUNK)
        k = ka_ref[0, pl.ds(row_a0, BAND_A), :]
        v = va_ref[0, pl.ds(row_a0, BAND_A), :]
        q = jnp.concatenate(
            [qa_ref[0, rows, LANES * j:LANES * (j + 1)] * half_mask[p]
             for p in range(A_KV_HEADS) for j in range(A_GROUP)], axis=0)
        s = _nt_dot(q, k)
        if masked:
            s = jnp.where(key_a >= first_valid - (WINDOW_B - WINDOW_A), s, NEG_INF)
        mx = jnp.maximum(jnp.max(s, axis=-1, keepdims=True), sink_col)
        e = jnp.exp(s - mx)
        inv = 1.0 / (jnp.sum(e, axis=-1, keepdims=True) + jnp.exp(sink_col - mx))
        o = jnp.dot(e.astype(BF16), v, preferred_element_type=F32) * inv
        for j in range(A_GROUP):
            lo = o[j * CHUNK:(j + 1) * CHUNK]
            hi = o[(A_GROUP + j) * CHUNK:(A_GROUP + j + 1) * CHUNK]
            oa_ref[0, rows, LANES * j:LANES * (j + 1)] = jnp.where(low, lo, hi).astype(oa_ref.dtype)

    steps_with_pad = pl.cdiv(PAD // CHUNK, n_c) if mask_prefix else 0
    if steps_with_pad:
        @pl.when(qi < steps_with_pad)
        def _():
            for c in range(n_c):
                chunk(c, True)

    @pl.when(qi >= steps_with_pad)
    def _():
        for c in range(n_c):
            chunk(c, False)


def _attention(qa, ka, va, qb, kb, vb, sinks, bias, *, n_c, mask_prefix):
    bsz, length, _ = qa.shape
    lp = ka.shape[1]
    qt = n_c * CHUNK
    assert length % qt == 0 and lp == length + PAD

    def qmap(b, i):
        return (b, i, 0)

    def kmap(b, i):
        return (b, 0, 0)

    kern = functools.partial(_attn_kernel, n_c=n_c, mask_prefix=mask_prefix)
    return pl.pallas_call(
        kern,
        grid=(bsz, length // qt),
        in_specs=[
            pl.BlockSpec(memory_space=pltpu.SMEM),
            pl.BlockSpec((1, qt, A_Q_W), qmap),
            pl.BlockSpec((1, lp, A_KV_W), kmap),
            pl.BlockSpec((1, lp, A_KV_W), kmap),
            pl.BlockSpec((1, qt, B_W), qmap),
            pl.BlockSpec((1, lp, B_W), kmap),
            pl.BlockSpec((1, lp, B_W), kmap),
            pl.BlockSpec((B_HEADS // 2, 2 * CHUNK, BAND_B), lambda b, i: (0, 0, 0)),
        ],
        out_specs=(pl.BlockSpec((1, qt, A_Q_W), qmap), pl.BlockSpec((1, qt, B_W), qmap)),
        out_shape=(jax.ShapeDtypeStruct((bsz, length, A_Q_W), BF16),
                   jax.ShapeDtypeStruct((bsz, length, B_W), BF16)),
        compiler_params=_cparams("arbitrary", "arbitrary"),
        name="attention",
    )(sinks, qa, ka, va, qb, kb, vb, bias)


def _merge_kernel(x_ref, oa_ref, ob_ref, sha_ref, sca_ref, gta_ref, shm_ref, scm_ref, ga_ref, gm_ref,
                  wg_ref, wa_ref, wb_ref, wo_ref, wrh_ref, wrl_ref, br_ref,
                  x1_ref, h2_ref, ri_ref, rw_ref, cnt_ref, carry_ref, logit_ref, *, bb, tl, seg_tiles):
    m = bb * tl
    step = pl.program_id(0)

    @pl.when(step == 0)
    def _():
        carry_ref[...] = jnp.zeros_like(carry_ref)
        logit_ref[...] = jnp.zeros_like(logit_ref)

    logits = logit_ref[...]
    lane = lax.broadcasted_iota(I32, (m, LANES), 1)
    lane_f = lane.astype(F32)
    work = logits
    top_v, top_i = [], []
    sel_f = jnp.zeros((m, LANES), F32)
    for _ in range(TOP_K):
        mx = jnp.max(work, axis=-1, keepdims=True)
        idx = jnp.min(jnp.where(work == mx, lane_f, float(LANES)), axis=-1, keepdims=True)
        hit = lane_f == idx
        top_v.append(mx)
        top_i.append(idx)
        sel_f = jnp.where(hit, 1.0, sel_f)
        work = jnp.where(hit, -jnp.inf, work)
    ex = [jnp.exp(v - top_v[0]) for v in top_v]
    inv = 1.0 / (ex[0] + ex[1] + ex[2] + ex[3])

    r_i = lax.broadcasted_iota(I32, (m, m), 0)
    c_i = lax.broadcasted_iota(I32, (m, m), 1)
    before = jnp.where(r_i > c_i, 1.0, 0.0).astype(BF16)
    keep = jnp.where(jnp.maximum(step - 1, 0) % seg_tiles == 0, 0.0, 1.0)
    carry_in = carry_ref[...] * keep
    pos = jnp.dot(before, sel_f.astype(BF16), preferred_element_type=F32) + carry_in
    live = jnp.where(step > 0, 1.0, 0.0)
    carry = carry_in + live * jnp.sum(sel_f, axis=0, keepdims=True)
    carry_ref[...] = carry
    cnt_ref[0] = carry

    ri_f = jnp.zeros((m, LANES), F32)
    rw = jnp.zeros((m, LANES), F32)
    for k in range(TOP_K):
        rank = jnp.sum(jnp.where(lane_f == top_i[k], pos, 0.0), axis=-1, keepdims=True)
        ri_f = jnp.where(lane == k, top_i[k], ri_f)
        ri_f = jnp.where(lane == TOP_K + k, rank, ri_f)
        rw = jnp.where(lane == k, ex[k] * inv, rw)
    ri_ref[...] = ri_f.astype(I32).reshape(bb, tl, LANES)
    rw_ref[...] = rw.reshape(bb, tl, LANES)

    x = x_ref[...]
    h = _modulated_norm(x, ga_ref[...], sca_ref[...], sha_ref[...])
    hb = h.reshape(m, D_MODEL).astype(BF16)
    gates = jnp.dot(hb, wg_ref[...], preferred_element_type=F32)
    ya = jnp.dot(oa_ref[...].reshape(m, A_Q_W), wa_ref[...], preferred_element_type=F32)
    yb = jnp.dot(ob_ref[...].reshape(m, B_W), wb_ref[...], preferred_element_type=F32)
    mix = jax.nn.sigmoid(gates[:, :D_MODEL]) * ya + jax.nn.sigmoid(gates[:, D_MODEL:]) * yb
    z = jnp.dot(mix.astype(BF16), wo_ref[...], preferred_element_type=F32)
    x1 = x + gta_ref[...] * z.reshape(bb, tl, D_MODEL)
    x1_ref[...] = x1

    h2 = _modulated_norm(x1, gm_ref[...], scm_ref[...], shm_ref[...]).reshape(m, D_MODEL)
    _pack_rows(h2_ref, h2)

    h2_hi = h2.astype(BF16)
    h2_lo = (h2 - h2_hi.astype(F32)).astype(BF16)
    wrh = wrh_ref[...]
    logit_ref[...] = (jnp.dot(h2_hi, wrh, preferred_element_type=F32)
                      + jnp.dot(h2_lo, wrh, preferred_element_type=F32)
                      + jnp.dot(h2_hi, wrl_ref[...], preferred_element_type=F32)) + br_ref[...]


def _merge(x, oa, ob, mods, g_attn, g_mlp, wg, wa, wb, wo, wr, br, *, bb, tl, b0, nb):
    _, length, _ = x.shape
    n_tok = nb * length
    m = bb * tl
    assert bb == 1 or tl == length
    sh_a, sc_a, gt_a, sh_m, sc_m = mods
    n_l = length // tl
    n_tiles = (nb // bb) * n_l
    blk0 = b0 // bb
    wr_hi = wr.astype(BF16)
    wr_lo = (wr - wr_hi.astype(F32)).astype(BF16)

    def cur(s):
        return jnp.minimum(s, n_tiles - 1)

    def prev(s):
        return jnp.maximum(s - 1, 0)

    def xmap(s):
        return (blk0 + cur(s) // n_l, cur(s) % n_l, 0)

    def modmap(s):
        return (blk0 + cur(s) // n_l, 0, 0)

    def omap(s):
        return (cur(s) // n_l, cur(s) % n_l, 0)

    def const2(s):
        return (0, 0)

    def flatmap(s):
        return (cur(s), 0)

    def routemap(s):
        return (prev(s) // n_l, prev(s) % n_l, 0)

    modspec = pl.BlockSpec((bb, 1, D_MODEL), modmap)
    kern = functools.partial(_merge_kernel, bb=bb, tl=tl, seg_tiles=n_tiles)
    return pl.pallas_call(
        kern,
        grid=(n_tiles + 1,),
        in_specs=[
            pl.BlockSpec((bb, tl, D_MODEL), xmap),
            pl.BlockSpec((bb, tl, A_Q_W), xmap),
            pl.BlockSpec((bb, tl, B_W), xmap),
            modspec, modspec, modspec, modspec, modspec,
            pl.BlockSpec((1, D_MODEL), const2),
            pl.BlockSpec((1, D_MODEL), const2),
            pl.BlockSpec((D_MODEL, 2 * D_MODEL), const2),
            pl.BlockSpec((A_Q_W, D_MODEL), const2),
            pl.BlockSpec((B_W, D_MODEL), const2),
            pl.BlockSpec((D_MODEL, D_MODEL), const2),
            pl.BlockSpec((D_MODEL, LANES), const2),
            pl.BlockSpec((D_MODEL, LANES), const2),
            pl.BlockSpec((1, LANES), const2),
        ],
        out_specs=(
            pl.BlockSpec((bb, tl, D_MODEL), omap),
            pl.BlockSpec((m * ROW_TILES, LANES), flatmap),
            pl.BlockSpec((bb, tl, LANES), routemap),
            pl.BlockSpec((bb, tl, LANES), routemap),
            pl.BlockSpec((1, 1, LANES), lambda s: (0, 0, 0)),
        ),
        out_shape=(
            jax.ShapeDtypeStruct((nb, length, D_MODEL), F32),
            jax.ShapeDtypeStruct((n_tok * ROW_TILES, LANES), I32),
            jax.ShapeDtypeStruct((nb, length, LANES), I32),
            jax.ShapeDtypeStruct((nb, length, LANES), F32),
            jax.ShapeDtypeStruct((1, 1, LANES), F32),
        ),
        scratch_shapes=[pltpu.VMEM((1, LANES), F32), pltpu.VMEM((m, LANES), F32)],
        compiler_params=_cparams("arbitrary"),
        name="merge",
    )(x, oa, ob, sh_a, sc_a, gt_a, sh_m, sc_m, g_attn.reshape(1, D_MODEL), g_mlp.reshape(1, D_MODEL),
      wg, wa, wb, wo, wr_hi, wr_lo, br)


def _sc_mesh():
    return plsc.VectorSubcoreMesh(core_axis_name="c", subcore_axis_name="s",
                                  num_cores=SC_CORES, num_subcores=SC_SUBCORES)


def _sc_worker():
    return lax.axis_index("c") * SC_SUBCORES + lax.axis_index("s")


def _sc_params():
    return pltpu.CompilerParams(use_tc_tiling_on_sc=True)


def _part_chunks(parts):
    assert all(p % (2 * SC_WORKERS * SC_ROWS) == 0 for p in parts)
    chunks = [p // (SC_WORKERS * SC_ROWS) for p in parts]
    starts = [sum(chunks[:i]) for i in range(len(chunks))]
    return chunks, starts


def _dispatch(h2_parts, idx_flat, pad_flat, n_rows):
    n_parts = len(h2_parts)
    chunks, starts = _part_chunks([h.shape[0] for h in h2_parts])
    n_idx = sum(chunks) * TOP_K * SC_ROWS
    n_pad = pad_flat.shape[0] // SC_WORKERS
    pad_chunks = n_pad // SC_ROWS
    assert n_pad % SC_ROWS == 0
    zeros = jnp.zeros((SC_ROWS, ROW_TILES, LANES), I32)

    def body(*refs):
        h2_hbms = refs[:n_parts]
        idx_hbm, pad_hbm, zeros_hbm, xs_hbm, idx_v, pad_v, buf0, buf1, ls0, ls1, ss0, ss1 = refs[n_parts:]
        wid = _sc_worker()
        bufs, lsems, ssems = (buf0, buf1), (ls0, ls1), (ss0, ss1)
        pltpu.sync_copy(idx_hbm.at[pl.ds(pl.multiple_of(wid * n_idx, SC_ROWS), n_idx)], idx_v)
        pltpu.sync_copy(pad_hbm.at[pl.ds(pl.multiple_of(wid * n_pad, SC_ROWS), n_pad)], pad_v)

        pltpu.sync_copy(zeros_hbm, buf0)
        for c in range(pad_chunks):
            pltpu.async_copy(buf0, xs_hbm.at[pad_v.at[pl.ds(c * SC_ROWS, SC_ROWS)]], ss0)
        for c in range(pad_chunks):
            pltpu.make_async_copy(buf0, xs_hbm.at[pad_v.at[pl.ds(c * SC_ROWS, SC_ROWS)]], ss0).wait()

        def run_part(h2_hbm, n_chunks, c0):
            base = wid * (n_chunks * SC_ROWS)

            def load(g, b):
                src = h2_hbm.at[pl.ds(pl.multiple_of(base + g * SC_ROWS, SC_ROWS), SC_ROWS)]
                return pltpu.make_async_copy(src, bufs[b], lsems[b])

            def scatter(g, k, b):
                off = pl.multiple_of(((c0 + g) * TOP_K + k) * SC_ROWS, SC_ROWS)
                return pltpu.make_async_copy(bufs[b], xs_hbm.at[idx_v.at[pl.ds(off, SC_ROWS)]], ssems[b])

            load(0, 0).start()

            @pl.loop(0, n_chunks, step=2)
            def _(g0):
                for b in range(2):
                    g = g0 + b
                    load(g, b).wait()

                    @pl.when(g >= 1)
                    def _():
                        for k in range(TOP_K):
                            scatter(g - 1, k, 1 - b).wait()

                    @pl.when(g + 1 < n_chunks)
                    def _():
                        load(g + 1, 1 - b).start()

                    for k in range(TOP_K):
                        scatter(g, k, b).start()

            for k in range(TOP_K):
                scatter(n_chunks - 1, k, (n_chunks - 1) % 2).wait()

        for h2_hbm, n_chunks, c0 in zip(h2_hbms, chunks, starts):
            run_part(h2_hbm, n_chunks, c0)

    call = pl.kernel(
        body,
        out_type=jax.ShapeDtypeStruct((n_rows + TRASH_ROWS, ROW_TILES, LANES), I32),
        mesh=_sc_mesh(),
        scratch_types=[pltpu.VMEM((n_idx,), I32), pltpu.VMEM((n_pad,), I32),
                       pltpu.VMEM((SC_ROWS, ROW_TILES, LANES), I32), pltpu.VMEM((SC_ROWS, ROW_TILES, LANES), I32),
                       pltpu.SemaphoreType.DMA, pltpu.SemaphoreType.DMA,
                       pltpu.SemaphoreType.DMA, pltpu.SemaphoreType.DMA],
        compiler_params=_sc_params(),
        name="dispatch",
    )
    return call(*h2_parts, idx_flat, pad_flat, zeros)


def _gather_back(y_rows, idx_flat, parts):
    n_parts = len(parts)
    chunks, starts = _part_chunks(parts)
    n_idx = sum(chunks) * TOP_K * SC_ROWS

    def body(*refs):
        y_hbm, idx_hbm = refs[:2]
        yg_hbms = refs[2:2 + n_parts]
        idx_v, buf0, buf1, gs0, gs1, ws0, ws1 = refs[2 + n_parts:]
        wid = _sc_worker()
        bufs, gsems, wsems = (buf0, buf1), (gs0, gs1), (ws0, ws1)
        pltpu.sync_copy(idx_hbm.at[pl.ds(pl.multiple_of(wid * n_idx, SC_ROWS), n_idx)], idx_v)

        def run_part(yg_hbm, n_chunks, c0):
            base = wid * (n_chunks * SC_ROWS)
            n_items = n_chunks * TOP_K

            def gather(i, b):
                off = pl.multiple_of((c0 * TOP_K + i) * SC_ROWS, SC_ROWS)
                return pltpu.make_async_copy(y_hbm.at[idx_v.at[pl.ds(off, SC_ROWS)]], bufs[b], gsems[b])

            def write(i, b):
                g, k = i // TOP_K, i % TOP_K
                dst = yg_hbm.at[k, pl.ds(pl.multiple_of(base + g * SC_ROWS, SC_ROWS), SC_ROWS)]
                return pltpu.make_async_copy(bufs[b], dst, wsems[b])

            gather(0, 0).start()

            @pl.loop(0, n_items, step=2)
            def _(i0):
                for b in range(2):
                    i = i0 + b
                    gather(i, b).wait()

                    @pl.when(i >= 1)
                    def _():
                        write(i - 1, 1 - b).wait()

                    @pl.when(i + 1 < n_items)
                    def _():
                        gather(i + 1, 1 - b).start()

                    write(i, b).start()

            write(n_items - 1, (n_items - 1) % 2).wait()

        for yg_hbm, n_chunks, c0 in zip(yg_hbms, chunks, starts):
            run_part(yg_hbm, n_chunks, c0)

    call = pl.kernel(
        body,
        out_type=tuple(jax.ShapeDtypeStruct((TOP_K, p, ROW_TILES, LANES), I32) for p in parts),
        mesh=_sc_mesh(),
        scratch_types=[pltpu.VMEM((n_idx,), I32),
                       pltpu.VMEM((SC_ROWS, ROW_TILES, LANES), I32), pltpu.VMEM((SC_ROWS, ROW_TILES, LANES), I32),
                       pltpu.SemaphoreType.DMA, pltpu.SemaphoreType.DMA,
                       pltpu.SemaphoreType.DMA, pltpu.SemaphoreType.DMA],
        compiler_params=_sc_params(),
        name="gather_back",
    )
    return call(y_rows, idx_flat)


def _expert_kernel(blk_ref, used_ref, xs_ref, wgu_ref, bgu_ref, wd_ref, bd_ref, y_ref, wgu_bf, wd_bf):
    r = pl.program_id(0)
    e = blk_ref[r]
    prev = blk_ref[jnp.maximum(r - 1, 0)]
    active = r < used_ref[0]

    @pl.when(jnp.logical_and(active, jnp.logical_or(r == 0, e != prev)))
    def _():
        wgu_bf[...] = wgu_ref[0].astype(BF16)
        wd_bf[...] = wd_ref[0].astype(BF16)

    @pl.when(active)
    def _():
        lo, hi = _unpack_rows([xs_ref[pl.ds(j, ROUTE_BLOCK, stride=ROW_TILES), :] for j in range(ROW_TILES)])
        x = jnp.concatenate(lo + hi, axis=1).astype(BF16)
        acts = []
        for j in range(D_FF // FF_TILE):
            gc = slice(FF_TILE * j, FF_TILE * (j + 1))
            uc = slice(D_FF + FF_TILE * j, D_FF + FF_TILE * (j + 1))
            g = jnp.dot(x, wgu_bf[:, gc], preferred_element_type=F32) + bgu_ref[0, :, gc]
            u = jnp.dot(x, wgu_bf[:, uc], preferred_element_type=F32) + bgu_ref[0, :, uc]
            gate = jnp.minimum(g, SWIGLU_LIMIT)
            up = jnp.clip(u, -SWIGLU_LIMIT, SWIGLU_LIMIT)
            glu = gate * jax.nn.sigmoid(gate * SWIGLU_ALPHA)
            acts.append(((up + 1.0) * glu).astype(BF16))
        act = jnp.concatenate(acts, axis=1)
        y = jnp.dot(act, wd_bf[...], preferred_element_type=F32) + bd_ref[0]
        _pack_rows(y_ref, y)


def _experts(xs_rows, blk_expert, n_used, w_gate_up, b_gate_up, w_down, b_down):
    n_blocks = blk_expert.shape[0]
    rb = ROUTE_BLOCK * ROW_TILES

    def rowmap(r, blk, used):
        return (jnp.minimum(r, used[0] - 1), 0)

    def wmap(r, blk, used):
        return (blk[jnp.minimum(r, used[0] - 1)], 0, 0)

    return pl.pallas_call(
        _expert_kernel,
        grid_spec=pltpu.PrefetchScalarGridSpec(
            num_scalar_prefetch=2,
            grid=(n_blocks,),
            in_specs=[
                pl.BlockSpec((rb, LANES), rowmap),
                pl.BlockSpec((1, D_MODEL, 2 * D_FF), wmap),
                pl.BlockSpec((1, 1, 2 * D_FF), wmap),
                pl.BlockSpec((1, D_FF, D_MODEL), wmap),
                pl.BlockSpec((1, 1, D_MODEL), wmap),
            ],
            out_specs=pl.BlockSpec((rb, LANES), rowmap),
            scratch_shapes=[pltpu.VMEM((D_MODEL, 2 * D_FF), BF16), pltpu.VMEM((D_FF, D_MODEL), BF16)],
        ),
        out_shape=jax.ShapeDtypeStruct(xs_rows.shape, I32),
        compiler_params=_cparams("arbitrary"),
        name="experts",
    )(blk_expert, n_used, xs_rows, w_gate_up, b_gate_up.reshape(N_EXPERTS, 1, 2 * D_FF),
      w_down, b_down.reshape(N_EXPERTS, 1, D_MODEL))


def _combine_kernel(x1_ref, gt_ref, rw_ref, yg_ref, *rest, bb, tl):
    o_ref = rest[-1]
    m = bb * tl
    rw = rw_ref[...].reshape(m, LANES)
    w = [rw[:, k:k + 1] for k in range(TOP_K)]
    gt = gt_ref[...]
    for j in range(ROW_TILES):
        acc_lo = jnp.zeros((m, LANES), F32)
        acc_hi = jnp.zeros((m, LANES), F32)
        for k in range(TOP_K):
            lo, hi = _unpack_rows([yg_ref[k, pl.ds(j, m, stride=ROW_TILES), :]])
            acc_lo = acc_lo + w[k] * lo[0]
            acc_hi = acc_hi + w[k] * hi[0]
        for acc, c0 in ((acc_lo, LANES * j), (acc_hi, HALF_W + LANES * j)):
            cg = slice(c0, c0 + LANES)
            o_ref[:, :, cg] = x1_ref[:, :, cg] + gt[:, :, cg] * acc.reshape(bb, tl, LANES)


def _combine(x1, gt_m, rw, yg, y_prev, *, bb, tl, b0, bsz):
    nb, length, _ = x1.shape
    m = bb * tl
    assert bb == 1 or tl == length
    n_l = length // tl
    blk0 = b0 // bb

    def smap(b, i):
        return (b, i, 0)

    def xmap(b, i):
        return (blk0 + b, i, 0)

    in_specs = [
        pl.BlockSpec((bb, tl, D_MODEL), smap),
        pl.BlockSpec((bb, 1, D_MODEL), lambda b, i: (blk0 + b, 0, 0)),
        pl.BlockSpec((bb, tl, LANES), smap),
        pl.BlockSpec((TOP_K, m * ROW_TILES, LANES), lambda b, i: (0, b * n_l + i, 0)),
    ]
    args = [x1, gt_m, rw, yg]
    aliases = {}
    if y_prev is not None:
        in_specs.append(pl.BlockSpec(memory_space=pl.ANY))
        args.append(y_prev)
        aliases = {len(args) - 1: 0}
    kern = functools.partial(_combine_kernel, bb=bb, tl=tl)
    return pl.pallas_call(
        kern,
        grid=(nb // bb, n_l),
        in_specs=in_specs,
        out_specs=pl.BlockSpec((bb, tl, D_MODEL), xmap),
        out_shape=jax.ShapeDtypeStruct((bsz, length, D_MODEL), F32),
        input_output_aliases=aliases,
        compiler_params=_cparams("arbitrary", "arbitrary"),
        name="combine",
    )(*args)


def _route_plan(routes):
    expert_ids = jnp.arange(N_EXPERTS, dtype=I32)
    part_counts = [c.reshape(-1)[:N_EXPERTS].astype(I32) for _, c in routes]
    counts = sum(part_counts)
    padded = (counts + ROUTE_BLOCK - 1) // ROUTE_BLOCK * ROUTE_BLOCK
    pends = jnp.cumsum(padded)
    pstarts = pends - padded
    n_assign = sum(ri.shape[0] * ri.shape[1] for ri, _ in routes) * TOP_K
    n_blocks = n_assign // ROUTE_BLOCK + N_EXPERTS
    n_rows = n_blocks * ROUTE_BLOCK
    lists = []
    first = jnp.zeros_like(counts)
    for (ri, _), cnt in zip(routes, part_counts):
        n_tok = ri.shape[0] * ri.shape[1]
        idx = ri[..., :TOP_K].reshape(n_tok, TOP_K)
        rank = ri[..., TOP_K:2 * TOP_K].reshape(n_tok, TOP_K)
        dest = jnp.sum(jnp.where(idx[..., None] == expert_ids, pstarts + first, 0), axis=-1) + rank
        per_w = n_tok // SC_WORKERS
        lists.append(dest.astype(I32).reshape(SC_WORKERS, per_w // SC_ROWS, SC_ROWS, TOP_K).transpose(0, 1, 3, 2))
        first = first + cnt
    idx_flat = jnp.concatenate(lists, axis=1).reshape(-1)
    blk_start = jnp.arange(n_blocks, dtype=I32) * ROUTE_BLOCK
    blk_expert = jnp.minimum(jnp.sum((pends[None, :] <= blk_start[:, None]).astype(I32), axis=1), N_EXPERTS - 1)
    n_used = (pends[-1] // ROUTE_BLOCK).reshape(1)
    r = jnp.arange(ROUTE_BLOCK, dtype=I32)[None, :]
    pad_flat = jnp.where(r < (padded - counts)[:, None], (pstarts + counts)[:, None] + r, n_rows)
    return idx_flat, pad_flat.astype(I32).reshape(-1), blk_expert.astype(I32), n_used.astype(I32), n_rows


def _front(x, mods, pos, prefix, weights, *, bb, tl, n_c):
    g_attn, w_qkv, gains, sinks, bias = weights[0], weights[2], weights[3], weights[4], weights[5]
    sh_a, sc_a = mods[0], mods[1]
    pad = 0 if prefix is not None else PAD
    qa, ka, va, qb, kb, vb, ska, sva, skb, svb = _stage1(
        x, sh_a, sc_a, g_attn, w_qkv, gains, pos, bb=bb, tl=tl, pad=pad)
    if prefix is not None:
        ka, va, kb, vb = [jnp.concatenate([p, t], axis=1) for p, t in zip(prefix, (ka, va, kb, vb))]
    oa, ob = _attention(qa, ka, va, qb, kb, vb, sinks, bias, n_c=n_c, mask_prefix=prefix is None)
    return oa, ob, (ska, sva, skb, svb)


def _moe_segment(parts, weights):
    (g_attn, g_mlp, _, _, _, _, wg, wa, wb, wo, wr, br, w_gate_up, b_gate_up, w_down, b_down) = weights
    merged = []
    for p in parts:
        sh_a, sc_a, gt_a, sh_m, sc_m, _ = p["mods"]
        merged.append(_merge(p["x"], p["oa"], p["ob"], (sh_a, sc_a, gt_a, sh_m, sc_m), g_attn, g_mlp,
                             wg, wa, wb, wo, wr, br, bb=p["cfg"]["bb"], tl=p["cfg"]["tl"], b0=p["b0"], nb=p["nb"]))
    toks = [m[0].shape[0] * m[0].shape[1] for m in merged]
    idx_flat, pad_flat, blk_expert, n_used, n_rows = _route_plan([(m[2], m[4]) for m in merged])
    xs = _dispatch([m[1].reshape(t, ROW_TILES, LANES) for m, t in zip(merged, toks)], idx_flat, pad_flat, n_rows)
    y_rows = _experts(xs.reshape(-1, LANES), blk_expert, n_used, w_gate_up, b_gate_up, w_down, b_down)
    ygs = _gather_back(y_rows.reshape(-1, ROW_TILES, LANES), idx_flat, toks)
    outs = []
    for p, m, yg, t in zip(parts, merged, ygs, toks):
        outs.append(_combine(m[0], p["mods"][5], m[3], yg.reshape(TOP_K, t * ROW_TILES, LANES), p["y"],
                             bb=p["cfg"]["bb_combine"], tl=p["cfg"]["tl_combine"], b0=p["b0"],
                             bsz=p["x"].shape[0]))
    return outs


def _rel_bias_table(rel_table):
    n_rel = rel_table.shape[1]
    ext = jnp.concatenate(
        [rel_table, jnp.broadcast_to(rel_table[:, -1:], (B_HEADS, CHUNK + BAND_B - n_rel))], axis=1)
    rows = jnp.stack([ext[:, i:i + BAND_B] for i in range(CHUNK)], axis=1)[..., ::-1]
    return rows.astype(F32).reshape(B_HEADS // 2, 2 * CHUNK, BAND_B)


def _prep_weights(g_attn, g_mlp, w_in, q_norm_a, k_norm_a, q_norm_b, k_norm_b, sinks_a, rel_table_b,
                  w_branch_a, w_branch_b, w_out, w_router, b_router, w_gate_up, b_gate_up, w_down, b_down):
    order = jnp.asarray(A_HEAD_ORDER)
    w_qa = w_in[:, :A_Q_W].reshape(D_MODEL, A_Q_HEADS, HEAD_DIM)[:, order].reshape(D_MODEL, A_Q_W)
    w_qkv = jnp.concatenate([w_qa, w_in[:, A_Q_W:QKV_W]], axis=1).astype(BF16)
    wg = w_in[:, QKV_W:].astype(BF16)
    wa = w_branch_a.reshape(A_Q_HEADS, HEAD_DIM, D_MODEL)[order].reshape(A_Q_W, D_MODEL).astype(BF16)
    wb = w_branch_b.astype(BF16)
    wo = w_out.astype(BF16)
    wr = jnp.zeros((D_MODEL, LANES), F32).at[:, :N_EXPERTS].set(w_router)
    br = jnp.full((1, LANES), NEG_INF, F32).at[0, :N_EXPERTS].set(b_router)
    bias = _rel_bias_table(rel_table_b)
    gains = (q_norm_a, k_norm_a, q_norm_b, k_norm_b)
    return (g_attn, g_mlp, w_qkv, gains, sinks_a.astype(F32), bias, wg, wa, wb, wo, wr, br,
            w_gate_up, b_gate_up, w_down, b_down)


def _forward(x_prompt, x_sample, c_prompt, c_sample, cache_a_k, cache_a_v, cache_b_k, cache_b_v, params,
             past_len, cfg_prompt, cfg_sample):
    (g_attn, g_mlp, w_ada, b_ada, w_in, q_norm_a, k_norm_a, q_norm_b, k_norm_b, sinks_a, rel_table_b,
     w_branch_a, w_branch_b, w_out, w_router, b_router, w_gate_up, b_gate_up, w_down, b_down) = params
    depth = g_attn.shape[0]
    bp, lp_, _ = x_prompt.shape
    bs, ls, _ = x_sample.shape
    pos_p = jnp.arange(lp_, dtype=I32)
    pos_s = past_len + jnp.arange(ls, dtype=I32)
    yp, ys = x_prompt, x_sample
    st_p, st_s = [], []
    for l in range(depth):
        weights = _prep_weights(g_attn[l], g_mlp[l], w_in[l], q_norm_a[l], k_norm_a[l], q_norm_b[l],
                                k_norm_b[l], sinks_a[l], rel_table_b[l], w_branch_a[l], w_branch_b[l],
                                w_out[l], w_router[l], b_router[l], w_gate_up[l], b_gate_up[l],
                                w_down[l], b_down[l])
        mod = _adaln(jnp.concatenate([c_prompt, c_sample], axis=0), w_ada[l], b_ada[l])
        mod = mod.reshape(bp + bs, 6, 1, D_MODEL)
        mods_p = [mod[:bp, k] for k in range(6)]
        mods_s = [mod[bp:, k] for k in range(6)]
        wa_c = cache_a_k.shape[2]
        wb_c = cache_b_k.shape[2]
        assert wb_c == WINDOW_B and wa_c == WINDOW_A
        zeros_a = jnp.zeros((bs, PAD - wa_c, A_KV_W), BF16)
        prefix = (
            jnp.concatenate([zeros_a, cache_a_k[l].reshape(bs, wa_c, A_KV_W).astype(BF16)], axis=1),
            jnp.concatenate([zeros_a, cache_a_v[l].reshape(bs, wa_c, A_KV_W).astype(BF16)], axis=1),
            cache_b_k[l].reshape(bs, wb_c, B_W).astype(BF16),
            cache_b_v[l].reshape(bs, wb_c, B_W).astype(BF16),
        )
        front = {k: cfg_prompt[k] for k in ("bb", "tl", "n_c")}
        oa_p, ob_p, sp = _front(yp, mods_p, pos_p, None, weights, **front)
        front = {k: cfg_sample[k] for k in ("bb", "tl", "n_c")}
        oa_s, ob_s, ss = _front(ys, mods_s, pos_s, prefix, weights, **front)
        n_seg = cfg_prompt["n_seg"]
        nb = bp // n_seg
        xp, xs_in = yp, ys
        yp = None
        for s in range(n_seg):
            parts = [dict(x=xp, oa=oa_p, ob=ob_p, mods=mods_p, b0=s * nb, nb=nb, cfg=cfg_prompt, y=yp)]
            if s == n_seg - 1:
                parts.append(dict(x=xs_in, oa=oa_s, ob=ob_s, mods=mods_s, b0=0, nb=bs, cfg=cfg_sample, y=None))
            outs = _moe_segment(parts, weights)
            yp = outs[0]
            if s == n_seg - 1:
                ys = outs[1]
        st_p.append(sp)
        st_s.append(ss)

    def stack(states, k, heads):
        return jnp.stack([s[k].reshape(s[k].shape[0], s[k].shape[1], heads, HEAD_DIM) for s in states])

    return (yp, ys,
            stack(st_p, 0, A_KV_HEADS), stack(st_p, 1, A_KV_HEADS), stack(st_p, 2, B_HEADS), stack(st_p, 3, B_HEADS),
            stack(st_s, 0, A_KV_HEADS), stack(st_s, 1, A_KV_HEADS), stack(st_s, 2, B_HEADS), stack(st_s, 3, B_HEADS))


PAST_LEN = 2048
CFG_PROMPT = dict(bb=1, tl=512, n_c=4, tl_combine=512, bb_combine=1, n_seg=2)
CFG_SAMPLE = dict(bb=8, tl=64, n_c=1, tl_combine=64, bb_combine=8)


def kernel(x_prompt, x_sample, c_prompt, c_sample, cache_a_k, cache_a_v, cache_b_k, cache_b_v, g_attn, g_mlp,
           w_ada, b_ada, w_in, q_norm_a, k_norm_a, q_norm_b, k_norm_b, sinks_a, rel_table_b, w_branch_a,
           w_branch_b, w_out, w_router, b_router, w_gate_up, b_gate_up, w_down, b_down):
    params = (g_attn, g_mlp, w_ada, b_ada, w_in, q_norm_a, k_norm_a, q_norm_b, k_norm_b, sinks_a, rel_table_b,
              w_branch_a, w_branch_b, w_out, w_router, b_router, w_gate_up, b_gate_up, w_down, b_down)
    return _forward(x_prompt, x_sample, c_prompt, c_sample, cache_a_k, cache_a_v, cache_b_k, cache_b_v,
                    params, PAST_LEN, CFG_PROMPT, CFG_SAMPLE)
```

```python
import functools

import jax
import jax.numpy as jnp
from jax import lax
from jax.experimental import pallas as pl
from jax.experimental.pallas import tpu as pltpu
from jax.experimental.pallas import tpu_sc as plsc

F32 = jnp.float32
BF16 = jnp.bfloat16
I32 = jnp.int32

D_MODEL = 1024
CHUNK = 64
HEAD_DIM = 64
LANES = 128
SUBLANES = 8
A_Q_HEADS = 8
A_KV_HEADS = 2
A_GROUP = A_Q_HEADS // A_KV_HEADS
B_HEADS = 8
A_Q_W = A_Q_HEADS * HEAD_DIM
A_KV_W = A_KV_HEADS * HEAD_DIM
B_W = B_HEADS * HEAD_DIM
QKV_W = A_Q_W + 2 * A_KV_W + 3 * B_W
WINDOW_A = 128
WINDOW_B = 512
BAND_A = WINDOW_A + CHUNK
BAND_B = WINDOW_B + CHUNK
PAD = WINDOW_B
ROT_DIM = HEAD_DIM // 4
ROPE_THETA = 500000.0
REL_BACK = 4 * CHUNK
REL_FWD = CHUNK - 1
N_EXPERTS = 32
TOP_K = 4
D_FF = D_MODEL
SWIGLU_LIMIT = 7.0
SWIGLU_ALPHA = 1.702
ROUTE_BLOCK = 512
FF_TILE = 256
RMS_EPS = 1e-6
NEG_INF = -1e30
HIGH_HALF = -65536
LOG2_E = 1.4426950408889634
HALF_W = D_MODEL // 2
ROW_TILES = HALF_W // LANES
VMEM_LIMIT = 56 * 1024 * 1024
SC_CORES = 2
SC_SUBCORES = 16
SC_WORKERS = SC_CORES * SC_SUBCORES
SC_ROWS = 32
TRASH_ROWS = 8

A_HEAD_ORDER = (0, 4, 1, 5, 2, 6, 3, 7)


def _cparams(*sem):
    return pltpu.CompilerParams(dimension_semantics=sem, vmem_limit_bytes=VMEM_LIMIT)


def _adaln_kernel(c_ref, w_ref, b_ref, o_ref):
    c = c_ref[...]
    a = c * jax.nn.sigmoid(c)
    o_ref[...] = jnp.dot(a, w_ref[...], preferred_element_type=F32,
                         precision=lax.Precision.HIGHEST) + b_ref[...]


def _adaln(c, w_ada, b_ada):
    n_rows = c.shape[0]
    n_out = w_ada.shape[1]
    tn = 1024
    return pl.pallas_call(
        _adaln_kernel,
        grid=(n_out // tn,),
        in_specs=[pl.BlockSpec((n_rows, D_MODEL), lambda j: (0, 0)),
                  pl.BlockSpec((D_MODEL, tn), lambda j: (0, j)),
                  pl.BlockSpec((1, tn), lambda j: (0, j))],
        out_specs=pl.BlockSpec((n_rows, tn), lambda j: (0, j)),
        out_shape=jax.ShapeDtypeStruct((n_rows, n_out), F32),
        compiler_params=_cparams("arbitrary"),
        name="adaln",
    )(c, w_ada, b_ada.reshape(1, n_out))


def _pack_rows(ref, x):
    m = x.shape[0]
    for j in range(ROW_TILES):
        lo = x[:, LANES * j:LANES * (j + 1)].astype(BF16).astype(F32)
        hi = x[:, HALF_W + LANES * j:HALF_W + LANES * (j + 1)].astype(BF16).astype(F32)
        word = (lax.bitcast_convert_type(hi, I32) & HIGH_HALF) | lax.shift_right_logical(
            lax.bitcast_convert_type(lo, I32), 16)
        ref[pl.ds(j, m, stride=ROW_TILES), :] = word


def _unpack_rows(words):
    lo = [lax.bitcast_convert_type(lax.shift_left(w, 16), F32) for w in words]
    hi = [lax.bitcast_convert_type(w & HIGH_HALF, F32) for w in words]
    return lo, hi


def _modulated_norm(x, g, scale, shift):
    ms = jnp.mean(x * x, axis=-1, keepdims=True)
    return (x * lax.rsqrt(ms + RMS_EPS) * g) * (1.0 + scale) + shift


def _head_norm(parts, gain, seg):
    m = parts[0].shape[0]
    sq = jnp.concatenate([p * p for p in parts], axis=0).astype(BF16)
    ms = jnp.dot(sq, seg, preferred_element_type=F32)
    r = lax.rsqrt(ms + RMS_EPS)
    return [parts[j] * r[j * m:(j + 1) * m] * gain for j in range(len(parts))]


def _rope(p, cos, sin_lo, sin_hi):
    return p * cos + pltpu.roll(p, LANES - ROT_DIM // 2, 1) * sin_lo + pltpu.roll(p, ROT_DIM // 2, 1) * sin_hi


def _stage1_kernel(x_ref, sh_ref, sc_ref, g_ref, w_ref, gqa_ref, gka_ref, gqb_ref, gkb_ref, seg_ref,
                   cos_ref, sinlo_ref, sinhi_ref,
                   qa_ref, ka_ref, va_ref, qb_ref, kb_ref, vb_ref, ska_ref, sva_ref, skb_ref, svb_ref,
                   *, bb, tl, pad_tiles, n_tiles, sa_rows, sb_rows):
    i = pl.program_id(1)
    m = bb * tl

    if pad_tiles:
        @pl.when(i < pad_tiles)
        def _():
            ka_ref[...] = jnp.zeros_like(ka_ref)
            va_ref[...] = jnp.zeros_like(va_ref)
            kb_ref[...] = jnp.zeros_like(kb_ref)
            vb_ref[...] = jnp.zeros_like(vb_ref)

    @pl.when(i >= pad_tiles)
    def _():
        h = _modulated_norm(x_ref[...], g_ref[...], sc_ref[...], sh_ref[...])
        hb = h.reshape(m, D_MODEL).astype(BF16)
        qkv = jnp.dot(hb, w_ref[...], preferred_element_type=F32)
        seg = seg_ref[...]

        def cols(lo, width):
            return [qkv[:, lo + LANES * j: lo + LANES * (j + 1)] for j in range(width // LANES)]

        cos = jnp.concatenate([cos_ref[...]] * bb, axis=0)
        sin_lo = jnp.concatenate([sinlo_ref[...]] * bb, axis=0)
        sin_hi = jnp.concatenate([sinhi_ref[...]] * bb, axis=0)
        scale = HEAD_DIM ** -0.5 * LOG2_E

        qa = [_rope(p, cos, sin_lo, sin_hi) * scale for p in _head_norm(cols(0, A_Q_W), gqa_ref[...], seg)]
        ka = [_rope(p, cos, sin_lo, sin_hi) for p in _head_norm(cols(A_Q_W, A_KV_W), gka_ref[...], seg)]
        va = cols(A_Q_W + A_KV_W, A_KV_W)
        off_b = A_Q_W + 2 * A_KV_W
        qb = [p * scale for p in _head_norm(cols(off_b, B_W), gqb_ref[...], seg)]
        kb = _head_norm(cols(off_b + B_W, B_W), gkb_ref[...], seg)
        vb = cols(off_b + 2 * B_W, B_W)

        def put(ref, parts):
            for j, p in enumerate(parts):
                ref[:, :, LANES * j:LANES * (j + 1)] = p.reshape(bb, tl, LANES).astype(ref.dtype)

        put(qa_ref, qa)
        put(ka_ref, ka)
        put(va_ref, va)
        put(qb_ref, qb)
        put(kb_ref, kb)
        put(vb_ref, vb)

        @pl.when(i == n_tiles - 1)
        def _():
            def put_state(ref, parts, rows):
                for j, p in enumerate(parts):
                    ref[:, :, LANES * j:LANES * (j + 1)] = p.reshape(bb, tl, LANES)[:, tl - rows:, :]

            put_state(ska_ref, ka, sa_rows)
            put_state(sva_ref, va, sa_rows)
            put_state(skb_ref, kb, sb_rows)
            put_state(svb_ref, vb, sb_rows)


def _rope_tables(pos):
    inv_freq = ROPE_THETA ** (-jnp.arange(0, ROT_DIM, 2, dtype=F32) / ROT_DIM)
    ang = pos.astype(F32)[:, None] * inv_freq[None, :]
    half = ROT_DIM // 2
    lane = jnp.arange(LANES) % HEAD_DIM
    ang_l = jnp.tile(ang, (1, LANES // half))
    rot = lane < ROT_DIM
    cos = jnp.where(rot[None, :], jnp.cos(ang_l), 1.0)
    sin_lo = jnp.where((lane < half)[None, :], -jnp.sin(ang_l), 0.0)
    sin_hi = jnp.where(((lane >= half) & rot)[None, :], jnp.sin(ang_l), 0.0)
    return cos, sin_lo, sin_hi


def _stage1(x, shift, scale, g_attn, w_qkv, gains, pos, *, bb, tl, pad):
    bsz, length, _ = x.shape
    n_tiles = length // tl
    pad_tiles = pad // tl
    sa_rows, sb_rows = min(WINDOW_A, length), min(WINDOW_B, length)
    assert bsz % bb == 0 and length % tl == 0 and pad % tl == 0 and sb_rows <= tl
    cos, sin_lo, sin_hi = _rope_tables(pos)
    seg = (jnp.arange(LANES)[:, None] // HEAD_DIM == jnp.arange(LANES)[None, :] // HEAD_DIM)
    seg = (seg.astype(F32) / HEAD_DIM).astype(BF16)
    gq_a, gk_a, gq_b, gk_b = [jnp.tile(g, LANES // HEAD_DIM).reshape(1, LANES) for g in gains]

    def xmap(b, i):
        return (b, jnp.maximum(i - pad_tiles, 0), 0)

    def const2(b, i):
        return (0, 0)

    def tabmap(b, i):
        return (jnp.maximum(i - pad_tiles, 0), 0)

    def modmap(b, i):
        return (b, 0, 0)

    def kvmap(b, i):
        return (b, i, 0)

    lp = length + pad
    kern = functools.partial(_stage1_kernel, bb=bb, tl=tl, pad_tiles=pad_tiles,
                             n_tiles=n_tiles + pad_tiles, sa_rows=sa_rows, sb_rows=sb_rows)
    out_shapes = (
        jax.ShapeDtypeStruct((bsz, length, A_Q_W), BF16),
        jax.ShapeDtypeStruct((bsz, lp, A_KV_W), BF16),
        jax.ShapeDtypeStruct((bsz, lp, A_KV_W), BF16),
        jax.ShapeDtypeStruct((bsz, length, B_W), BF16),
        jax.ShapeDtypeStruct((bsz, lp, B_W), BF16),
        jax.ShapeDtypeStruct((bsz, lp, B_W), BF16),
        jax.ShapeDtypeStruct((bsz, sa_rows, A_KV_W), F32),
        jax.ShapeDtypeStruct((bsz, sa_rows, A_KV_W), F32),
        jax.ShapeDtypeStruct((bsz, sb_rows, B_W), F32),
        jax.ShapeDtypeStruct((bsz, sb_rows, B_W), F32),
    )
    out_specs = (
        pl.BlockSpec((bb, tl, A_Q_W), xmap),
        pl.BlockSpec((bb, tl, A_KV_W), kvmap),
        pl.BlockSpec((bb, tl, A_KV_W), kvmap),
        pl.BlockSpec((bb, tl, B_W), xmap),
        pl.BlockSpec((bb, tl, B_W), kvmap),
        pl.BlockSpec((bb, tl, B_W), kvmap),
        pl.BlockSpec((bb, sa_rows, A_KV_W), modmap),
        pl.BlockSpec((bb, sa_rows, A_KV_W), modmap),
        pl.BlockSpec((bb, sb_rows, B_W), modmap),
        pl.BlockSpec((bb, sb_rows, B_W), modmap),
    )
    in_specs = [
        pl.BlockSpec((bb, tl, D_MODEL), xmap),
        pl.BlockSpec((bb, 1, D_MODEL), modmap),
        pl.BlockSpec((bb, 1, D_MODEL), modmap),
        pl.BlockSpec((1, D_MODEL), const2),
        pl.BlockSpec((D_MODEL, QKV_W), const2),
        pl.BlockSpec((1, LANES), const2),
        pl.BlockSpec((1, LANES), const2),
        pl.BlockSpec((1, LANES), const2),
        pl.BlockSpec((1, LANES), const2),
        pl.BlockSpec((LANES, LANES), const2),
        pl.BlockSpec((tl, LANES), tabmap),
        pl.BlockSpec((tl, LANES), tabmap),
        pl.BlockSpec((tl, LANES), tabmap),
    ]
    return pl.pallas_call(
        kern,
        grid=(bsz // bb, n_tiles + pad_tiles),
        in_specs=in_specs,
        out_specs=out_specs,
        out_shape=out_shapes,
        compiler_params=_cparams("arbitrary", "arbitrary"),
        name="stage1",
    )(x, shift, scale, g_attn.reshape(1, D_MODEL), w_qkv, gq_a, gk_a, gq_b, gk_b, seg, cos, sin_lo, sin_hi)


def _nt_dot(a, b):
    return lax.dot_general(a, b, (((1,), (1,)), ((), ())), preferred_element_type=F32)


def _attn_kernel(sink_ref, qa_ref, ka_ref, va_ref, qb_ref, kb_ref, vb_ref, bias_ref, oa_ref, ob_ref,
                 *, n_c, mask_prefix):
    qi = pl.program_id(1)
    lane = lax.broadcasted_iota(I32, (CHUNK, LANES), 1)
    low = lane < HEAD_DIM
    half_mask = (jnp.where(low, 1.0, 0.0).astype(BF16), jnp.where(low, 0.0, 1.0).astype(BF16))
    n_pairs = B_W // LANES
    rows_b = n_pairs * 2 * CHUNK
    rows_a = A_Q_HEADS * CHUNK
    key_b = lax.broadcasted_iota(I32, (rows_b, BAND_B), 1)
    key_a = lax.broadcasted_iota(I32, (rows_a, BAND_A), 1)
    row_a = lax.broadcasted_iota(I32, (rows_a, 1), 0) // CHUNK

    sink_col = jnp.zeros((rows_a, 1), F32)
    for r in range(A_Q_HEADS):
        sink_col = jnp.where(row_a == r, sink_ref[r], sink_col)

    def chunk(c, masked):
        ci = qi * n_c + c
        row0 = pl.multiple_of(ci * CHUNK, CHUNK)
        rows = slice(c * CHUNK, (c + 1) * CHUNK)
        first_valid = PAD - ci * CHUNK

        s_parts = []
        for j in range(n_pairs):
            cg = slice(LANES * j, LANES * (j + 1))
            q = qb_ref[0, rows, cg]
            qs = jnp.concatenate([q * half_mask[0], q * half_mask[1]], axis=0)
            k = kb_ref[0, pl.ds(row0, BAND_B), cg]
            s_parts.append(_nt_dot(qs, k) + bias_ref[j])
        s = jnp.concatenate(s_parts, axis=0)
        if masked:
            s = jnp.where(key_b >= first_valid, s, NEG_INF)
        mx = jnp.max(s, axis=-1, keepdims=True)
        e = jnp.exp2(s - mx)
        inv = 1.0 / jnp.sum(e, axis=-1, keepdims=True)
        p_b = e.astype(BF16)
        for j in range(n_pairs):
            cg = slice(LANES * j, LANES * (j + 1))
            v = vb_ref[0, pl.ds(row0, BAND_B), cg]
            rj = slice(2 * CHUNK * j, 2 * CHUNK * (j + 1))
            o = jnp.dot(p_b[rj], v, preferred_element_type=F32) * inv[rj]
            ob_ref[0, rows, cg] = jnp.where(low, o[:CHUNK], o[CHUNK:]).astype(ob_ref.dtype)

        row_a0 = pl.multiple_of(row0 + (WINDOW_B - WINDOW_A), CHUNK)
        k = ka_ref[0, pl.ds(row_a0, BAND_A), :]
        v = va_ref[0, pl.ds(row_a0, BAND_A), :]
        q = jnp.concatenate(
            [qa_ref[0, rows, LANES * j:LANES * (j + 1)] * half_mask[p]
             for p in range(A_KV_HEADS) for j in range(A_GROUP)], axis=0)
        s = _nt_dot(q, k)
        if masked:
            s = jnp.where(key_a >= first_valid - (WINDOW_B - WINDOW_A), s, NEG_INF)
        mx = jnp.maximum(jnp.max(s, axis=-1, keepdims=True), sink_col)
        e = jnp.exp2(s - mx)
        inv = 1.0 / (jnp.sum(e, axis=-1, keepdims=True) + jnp.exp2(sink_col - mx))
        o = jnp.dot(e.astype(BF16), v, preferred_element_type=F32) * inv
        for j in range(A_GROUP):
            lo = o[j * CHUNK:(j + 1) * CHUNK]
            hi = o[(A_GROUP + j) * CHUNK:(A_GROUP + j + 1) * CHUNK]
            oa_ref[0, rows, LANES * j:LANES * (j + 1)] = jnp.where(low, lo, hi).astype(oa_ref.dtype)

    steps_with_pad = pl.cdiv(PAD // CHUNK, n_c) if mask_prefix else 0
    if steps_with_pad:
        @pl.when(qi < steps_with_pad)
        def _():
            for c in range(n_c):
                chunk(c, True)

    @pl.when(qi >= steps_with_pad)
    def _():
        for c in range(n_c):
            chunk(c, False)


def _attention(qa, ka, va, qb, kb, vb, sinks, bias, *, n_c, mask_prefix):
    bsz, length, _ = qa.shape
    lp = ka.shape[1]
    qt = n_c * CHUNK
    assert length % qt == 0 and lp == length + PAD

    def qmap(b, i):
        return (b, i, 0)

    def kmap(b, i):
        return (b, 0, 0)

    kern = functools.partial(_attn_kernel, n_c=n_c, mask_prefix=mask_prefix)
    return pl.pallas_call(
        kern,
        grid=(bsz, length // qt),
        in_specs=[
            pl.BlockSpec(memory_space=pltpu.SMEM),
            pl.BlockSpec((1, qt, A_Q_W), qmap),
            pl.BlockSpec((1, lp, A_KV_W), kmap),
            pl.BlockSpec((1, lp, A_KV_W), kmap),
            pl.BlockSpec((1, qt, B_W), qmap),
            pl.BlockSpec((1, lp, B_W), kmap),
            pl.BlockSpec((1, lp, B_W), kmap),
            pl.BlockSpec((B_HEADS // 2, 2 * CHUNK, BAND_B), lambda b, i: (0, 0, 0)),
        ],
        out_specs=(pl.BlockSpec((1, qt, A_Q_W), qmap), pl.BlockSpec((1, qt, B_W), qmap)),
        out_shape=(jax.ShapeDtypeStruct((bsz, length, A_Q_W), BF16),
                   jax.ShapeDtypeStruct((bsz, length, B_W), BF16)),
        compiler_params=_cparams("arbitrary", "arbitrary"),
        name="attention",
    )(sinks, qa, ka, va, qb, kb, vb, bias)


def _merge_kernel(x_ref, oa_ref, ob_ref, sha_ref, sca_ref, gta_ref, shm_ref, scm_ref, ga_ref, gm_ref,
                  wg_ref, wa_ref, wb_ref, wo_ref, wrh_ref, wrl_ref, br_ref,
                  x1_ref, h2_ref, ri_ref, rw_ref, cnt_ref, carry_ref, logit_ref, *, bb, tl, seg_tiles):
    m = bb * tl
    step = pl.program_id(0)

    @pl.when(step == 0)
    def _():
        carry_ref[...] = jnp.zeros_like(carry_ref)
        logit_ref[...] = jnp.zeros_like(logit_ref)

    logits = logit_ref[...]
    lane = lax.broadcasted_iota(I32, (m, LANES), 1)
    lane_f = lane.astype(F32)
    work = logits
    top_v, top_i = [], []
    sel_f = jnp.zeros((m, LANES), F32)
    for _ in range(TOP_K):
        mx = jnp.max(work, axis=-1, keepdims=True)
        idx = jnp.min(jnp.where(work == mx, lane_f, float(LANES)), axis=-1, keepdims=True)
        hit = lane_f == idx
        top_v.append(mx)
        top_i.append(idx)
        sel_f = jnp.where(hit, 1.0, sel_f)
        work = jnp.where(hit, -jnp.inf, work)
    ex = [jnp.exp(v - top_v[0]) for v in top_v]
    inv = 1.0 / (ex[0] + ex[1] + ex[2] + ex[3])

    r_i = lax.broadcasted_iota(I32, (m, m), 0)
    c_i = lax.broadcasted_iota(I32, (m, m), 1)
    before = jnp.where(r_i > c_i, 1.0, 0.0).astype(BF16)
    keep = jnp.where(jnp.maximum(step - 1, 0) % seg_tiles == 0, 0.0, 1.0)
    carry_in = carry_ref[...] * keep
    pos = jnp.dot(before, sel_f.astype(BF16), preferred_element_type=F32) + carry_in
    live = jnp.where(step > 0, 1.0, 0.0)
    carry = carry_in + live * jnp.sum(sel_f, axis=0, keepdims=True)
    carry_ref[...] = carry
    cnt_ref[0] = carry

    ri_f = jnp.zeros((m, LANES), F32)
    rw = jnp.zeros((m, LANES), F32)
    for k in range(TOP_K):
        rank = jnp.sum(jnp.where(lane_f == top_i[k], pos, 0.0), axis=-1, keepdims=True)
        ri_f = jnp.where(lane == k, top_i[k], ri_f)
        ri_f = jnp.where(lane == TOP_K + k, rank, ri_f)
        rw = jnp.where(lane == k, ex[k] * inv, rw)
    ri_ref[...] = ri_f.astype(I32).reshape(bb, tl, LANES)
    rw_ref[...] = rw.reshape(bb, tl, LANES)

    x = x_ref[...]
    h = _modulated_norm(x, ga_ref[...], sca_ref[...], sha_ref[...])
    hb = h.reshape(m, D_MODEL).astype(BF16)
    gates = jnp.dot(hb, wg_ref[...], preferred_element_type=F32)
    ya = jnp.dot(oa_ref[...].reshape(m, A_Q_W), wa_ref[...], preferred_element_type=F32)
    yb = jnp.dot(ob_ref[...].reshape(m, B_W), wb_ref[...], preferred_element_type=F32)
    mix = jax.nn.sigmoid(gates[:, :D_MODEL]) * ya + jax.nn.sigmoid(gates[:, D_MODEL:]) * yb
    z = jnp.dot(mix.astype(BF16), wo_ref[...], preferred_element_type=F32)
    x1 = x + gta_ref[...] * z.reshape(bb, tl, D_MODEL)
    x1_ref[...] = x1

    h2 = _modulated_norm(x1, gm_ref[...], scm_ref[...], shm_ref[...]).reshape(m, D_MODEL)
    _pack_rows(h2_ref, h2)

    h2_hi = h2.astype(BF16)
    h2_lo = (h2 - h2_hi.astype(F32)).astype(BF16)
    wrh = wrh_ref[...]
    logit_ref[...] = (jnp.dot(h2_hi, wrh, preferred_element_type=F32)
                      + jnp.dot(h2_lo, wrh, preferred_element_type=F32)
                      + jnp.dot(h2_hi, wrl_ref[...], preferred_element_type=F32)) + br_ref[...]


def _merge(x, oa, ob, mods, g_attn, g_mlp, wg, wa, wb, wo, wr, br, *, bb, tl, b0, nb):
    _, length, _ = x.shape
    n_tok = nb * length
    m = bb * tl
    assert bb == 1 or tl == length
    sh_a, sc_a, gt_a, sh_m, sc_m = mods
    n_l = length // tl
    n_tiles = (nb // bb) * n_l
    blk0 = b0 // bb
    wr_hi = wr.astype(BF16)
    wr_lo = (wr - wr_hi.astype(F32)).astype(BF16)

    def cur(s):
        return jnp.minimum(s, n_tiles - 1)

    def prev(s):
        return jnp.maximum(s - 1, 0)

    def xmap(s):
        return (blk0 + cur(s) // n_l, cur(s) % n_l, 0)

    def modmap(s):
        return (blk0 + cur(s) // n_l, 0, 0)

    def omap(s):
        return (cur(s) // n_l, cur(s) % n_l, 0)

    def const2(s):
        return (0, 0)

    def flatmap(s):
        return (cur(s), 0)

    def routemap(s):
        return (prev(s) // n_l, prev(s) % n_l, 0)

    modspec = pl.BlockSpec((bb, 1, D_MODEL), modmap)
    kern = functools.partial(_merge_kernel, bb=bb, tl=tl, seg_tiles=n_tiles)
    return pl.pallas_call(
        kern,
        grid=(n_tiles + 1,),
        in_specs=[
            pl.BlockSpec((bb, tl, D_MODEL), xmap),
            pl.BlockSpec((bb, tl, A_Q_W), xmap),
            pl.BlockSpec((bb, tl, B_W), xmap),
            modspec, modspec, modspec, modspec, modspec,
            pl.BlockSpec((1, D_MODEL), const2),
            pl.BlockSpec((1, D_MODEL), const2),
            pl.BlockSpec((D_MODEL, 2 * D_MODEL), const2),
            pl.BlockSpec((A_Q_W, D_MODEL), const2),
            pl.BlockSpec((B_W, D_MODEL), const2),
            pl.BlockSpec((D_MODEL, D_MODEL), const2),
            pl.BlockSpec((D_MODEL, LANES), const2),
            pl.BlockSpec((D_MODEL, LANES), const2),
            pl.BlockSpec((1, LANES), const2),
        ],
        out_specs=(
            pl.BlockSpec((bb, tl, D_MODEL), omap),
            pl.BlockSpec((m * ROW_TILES, LANES), flatmap),
            pl.BlockSpec((bb, tl, LANES), routemap),
            pl.BlockSpec((bb, tl, LANES), routemap),
            pl.BlockSpec((1, 1, LANES), lambda s: (0, 0, 0)),
        ),
        out_shape=(
            jax.ShapeDtypeStruct((nb, length, D_MODEL), F32),
            jax.ShapeDtypeStruct((n_tok * ROW_TILES, LANES), I32),
            jax.ShapeDtypeStruct((nb, length, LANES), I32),
            jax.ShapeDtypeStruct((nb, length, LANES), F32),
            jax.ShapeDtypeStruct((1, 1, LANES), F32),
        ),
        scratch_shapes=[pltpu.VMEM((1, LANES), F32), pltpu.VMEM((m, LANES), F32)],
        compiler_params=_cparams("arbitrary"),
        name="merge",
    )(x, oa, ob, sh_a, sc_a, gt_a, sh_m, sc_m, g_attn.reshape(1, D_MODEL), g_mlp.reshape(1, D_MODEL),
      wg, wa, wb, wo, wr_hi, wr_lo, br)


def _sc_mesh():
    return plsc.VectorSubcoreMesh(core_axis_name="c", subcore_axis_name="s",
                                  num_cores=SC_CORES, num_subcores=SC_SUBCORES)


def _sc_worker():
    return lax.axis_index("c") * SC_SUBCORES + lax.axis_index("s")


def _sc_params():
    return pltpu.CompilerParams(use_tc_tiling_on_sc=True)


def _part_chunks(parts):
    assert all(p % (2 * SC_WORKERS * SC_ROWS) == 0 for p in parts)
    chunks = [p // (SC_WORKERS * SC_ROWS) for p in parts]
    starts = [sum(chunks[:i]) for i in range(len(chunks))]
    return chunks, starts


def _dispatch(h2_parts, idx_flat, pad_flat, n_rows):
    n_parts = len(h2_parts)
    chunks, starts = _part_chunks([h.shape[0] for h in h2_parts])
    n_idx = sum(chunks) * TOP_K * SC_ROWS
    n_pad = pad_flat.shape[0] // SC_WORKERS
    pad_chunks = n_pad // SC_ROWS
    assert n_pad % SC_ROWS == 0
    zeros = jnp.zeros((SC_ROWS, ROW_TILES, LANES), I32)

    def body(*refs):
        h2_hbms = refs[:n_parts]
        idx_hbm, pad_hbm, zeros_hbm, xs_hbm, idx_v, pad_v, buf0, buf1, ls0, ls1, ss0, ss1 = refs[n_parts:]
        wid = _sc_worker()
        bufs, lsems, ssems = (buf0, buf1), (ls0, ls1), (ss0, ss1)
        pltpu.sync_copy(idx_hbm.at[pl.ds(pl.multiple_of(wid * n_idx, SC_ROWS), n_idx)], idx_v)
        pltpu.sync_copy(pad_hbm.at[pl.ds(pl.multiple_of(wid * n_pad, SC_ROWS), n_pad)], pad_v)

        pltpu.sync_copy(zeros_hbm, buf0)
        for c in range(pad_chunks):
            pltpu.async_copy(buf0, xs_hbm.at[pad_v.at[pl.ds(c * SC_ROWS, SC_ROWS)]], ss0)
        for c in range(pad_chunks):
            pltpu.make_async_copy(buf0, xs_hbm.at[pad_v.at[pl.ds(c * SC_ROWS, SC_ROWS)]], ss0).wait()

        def run_part(h2_hbm, n_chunks, c0):
            base = wid * (n_chunks * SC_ROWS)

            def load(g, b):
                src = h2_hbm.at[pl.ds(pl.multiple_of(base + g * SC_ROWS, SC_ROWS), SC_ROWS)]
                return pltpu.make_async_copy(src, bufs[b], lsems[b])

            def scatter(g, k, b):
                off = pl.multiple_of(((c0 + g) * TOP_K + k) * SC_ROWS, SC_ROWS)
                return pltpu.make_async_copy(bufs[b], xs_hbm.at[idx_v.at[pl.ds(off, SC_ROWS)]], ssems[b])

            load(0, 0).start()

            @pl.loop(0, n_chunks, step=2)
            def _(g0):
                for b in range(2):
                    g = g0 + b
                    load(g, b).wait()

                    @pl.when(g >= 1)
                    def _():
                        for k in range(TOP_K):
                            scatter(g - 1, k, 1 - b).wait()

                    @pl.when(g + 1 < n_chunks)
                    def _():
                        load(g + 1, 1 - b).start()

                    for k in range(TOP_K):
                        scatter(g, k, b).start()

            for k in range(TOP_K):
                scatter(n_chunks - 1, k, (n_chunks - 1) % 2).wait()

        for h2_hbm, n_chunks, c0 in zip(h2_hbms, chunks, starts):
            run_part(h2_hbm, n_chunks, c0)

    call = pl.kernel(
        body,
        out_type=jax.ShapeDtypeStruct((n_rows + TRASH_ROWS, ROW_TILES, LANES), I32),
        mesh=_sc_mesh(),
        scratch_types=[pltpu.VMEM((n_idx,), I32), pltpu.VMEM((n_pad,), I32),
                       pltpu.VMEM((SC_ROWS, ROW_TILES, LANES), I32), pltpu.VMEM((SC_ROWS, ROW_TILES, LANES), I32),
                       pltpu.SemaphoreType.DMA, pltpu.SemaphoreType.DMA,
                       pltpu.SemaphoreType.DMA, pltpu.SemaphoreType.DMA],
        compiler_params=_sc_params(),
        name="dispatch",
    )
    return call(*h2_parts, idx_flat, pad_flat, zeros)


def _gather_back(y_rows, idx_flat, parts):
    n_parts = len(parts)
    chunks, starts = _part_chunks(parts)
    n_idx = sum(chunks) * TOP_K * SC_ROWS

    def body(*refs):
        y_hbm, idx_hbm = refs[:2]
        yg_hbms = refs[2:2 + n_parts]
        idx_v, buf0, buf1, gs0, gs1, ws0, ws1 = refs[2 + n_parts:]
        wid = _sc_worker()
        bufs, gsems, wsems = (buf0, buf1), (gs0, gs1), (ws0, ws1)
        pltpu.sync_copy(idx_hbm.at[pl.ds(pl.multiple_of(wid * n_idx, SC_ROWS), n_idx)], idx_v)

        def run_part(yg_hbm, n_chunks, c0):
            base = wid * (n_chunks * SC_ROWS)
            n_items = n_chunks * TOP_K

            def gather(i, b):
                off = pl.multiple_of((c0 * TOP_K + i) * SC_ROWS, SC_ROWS)
                return pltpu.make_async_copy(y_hbm.at[idx_v.at[pl.ds(off, SC_ROWS)]], bufs[b], gsems[b])

            def write(i, b):
                g, k = i // TOP_K, i % TOP_K
                dst = yg_hbm.at[k, pl.ds(pl.multiple_of(base + g * SC_ROWS, SC_ROWS), SC_ROWS)]
                return pltpu.make_async_copy(bufs[b], dst, wsems[b])

            gather(0, 0).start()

            @pl.loop(0, n_items, step=2)
            def _(i0):
                for b in range(2):
                    i = i0 + b
                    gather(i, b).wait()

                    @pl.when(i >= 1)
                    def _():
                        write(i - 1, 1 - b).wait()

                    @pl.when(i + 1 < n_items)
                    def _():
                        gather(i + 1, 1 - b).start()

                    write(i, b).start()

            write(n_items - 1, (n_items - 1) % 2).wait()

        for yg_hbm, n_chunks, c0 in zip(yg_hbms, chunks, starts):
            run_part(yg_hbm, n_chunks, c0)

    call = pl.kernel(
        body,
        out_type=tuple(jax.ShapeDtypeStruct((TOP_K, p, ROW_TILES, LANES), I32) for p in parts),
        mesh=_sc_mesh(),
        scratch_types=[pltpu.VMEM((n_idx,), I32),
                       pltpu.VMEM((SC_ROWS, ROW_TILES, LANES), I32), pltpu.VMEM((SC_ROWS, ROW_TILES, LANES), I32),
                       pltpu.SemaphoreType.DMA, pltpu.SemaphoreType.DMA,
                       pltpu.SemaphoreType.DMA, pltpu.SemaphoreType.DMA],
        compiler_params=_sc_params(),
        name="gather_back",
    )
    return call(y_rows, idx_flat)


def _expert_kernel(blk_ref, used_ref, next_ref, slot_ref, xs_ref, wgu_hbm, bgu_ref, wd_hbm, bd_ref, y_ref,
                   wgu_buf, wd_buf, wgu_bf, wd_bf, sem_gu, sem_d):
    r = pl.program_id(0)
    e = blk_ref[r]
    prev = blk_ref[jnp.maximum(r - 1, 0)]
    active = r < used_ref[0]
    slot = slot_ref[r]

    def fetch(expert, s):
        return (pltpu.make_async_copy(wgu_hbm.at[expert], wgu_buf.at[s], sem_gu.at[s]),
                pltpu.make_async_copy(wd_hbm.at[expert], wd_buf.at[s], sem_d.at[s]))

    @pl.when(jnp.logical_and(active, r == 0))
    def _():
        for copy in fetch(e, slot):
            copy.start()

    @pl.when(jnp.logical_and(active, jnp.logical_or(r == 0, e != prev)))
    def _():
        for copy in fetch(e, slot):
            copy.wait()
        wgu_bf[...] = wgu_buf[slot].astype(BF16)
        wd_bf[...] = wd_buf[slot].astype(BF16)
        nxt = next_ref[r]

        @pl.when(nxt >= 0)
        def _():
            for copy in fetch(nxt, 1 - slot):
                copy.start()

    @pl.when(active)
    def _():
        lo, hi = _unpack_rows([xs_ref[pl.ds(j, ROUTE_BLOCK, stride=ROW_TILES), :] for j in range(ROW_TILES)])
        x = jnp.concatenate(lo + hi, axis=1).astype(BF16)
        acts = []
        for j in range(D_FF // FF_TILE):
            gc = slice(FF_TILE * j, FF_TILE * (j + 1))
            uc = slice(D_FF + FF_TILE * j, D_FF + FF_TILE * (j + 1))
            g = jnp.dot(x, wgu_bf[:, gc], preferred_element_type=F32) + bgu_ref[0, :, gc]
            u = jnp.dot(x, wgu_bf[:, uc], preferred_element_type=F32) + bgu_ref[0, :, uc]
            gate = jnp.minimum(g, SWIGLU_LIMIT)
            up = jnp.clip(u, -SWIGLU_LIMIT, SWIGLU_LIMIT)
            glu = gate * jax.nn.sigmoid(gate * SWIGLU_ALPHA)
            acts.append(((up + 1.0) * glu).astype(BF16))
        act = jnp.concatenate(acts, axis=1)
        y = jnp.dot(act, wd_bf[...], preferred_element_type=F32) + bd_ref[0]
        _pack_rows(y_ref, y)


def _experts(xs_rows, blk_expert, n_used, blk_next, blk_slot, w_gate_up, b_gate_up, w_down, b_down):
    n_blocks = blk_expert.shape[0]
    rb = ROUTE_BLOCK * ROW_TILES

    def rowmap(r, blk, used, nxt, slot):
        return (jnp.minimum(r, used[0] - 1), 0)

    def bmap(r, blk, used, nxt, slot):
        return (blk[jnp.minimum(r, used[0] - 1)], 0, 0)

    return pl.pallas_call(
        _expert_kernel,
        grid_spec=pltpu.PrefetchScalarGridSpec(
            num_scalar_prefetch=4,
            grid=(n_blocks,),
            in_specs=[
                pl.BlockSpec((rb, LANES), rowmap),
                pl.BlockSpec(memory_space=pl.ANY),
                pl.BlockSpec((1, 1, 2 * D_FF), bmap),
                pl.BlockSpec(memory_space=pl.ANY),
                pl.BlockSpec((1, 1, D_MODEL), bmap),
            ],
            out_specs=pl.BlockSpec((rb, LANES), rowmap),
            scratch_shapes=[pltpu.VMEM((2, D_MODEL, 2 * D_FF), F32), pltpu.VMEM((2, D_FF, D_MODEL), F32),
                            pltpu.VMEM((D_MODEL, 2 * D_FF), BF16), pltpu.VMEM((D_FF, D_MODEL), BF16),
                            pltpu.SemaphoreType.DMA((2,)), pltpu.SemaphoreType.DMA((2,))],
        ),
        out_shape=jax.ShapeDtypeStruct(xs_rows.shape, I32),
        compiler_params=_cparams("arbitrary"),
        name="experts",
    )(blk_expert, n_used, blk_next, blk_slot, xs_rows, w_gate_up, b_gate_up.reshape(N_EXPERTS, 1, 2 * D_FF),
      w_down, b_down.reshape(N_EXPERTS, 1, D_MODEL))


def _combine_kernel(x1_ref, gt_ref, rw_ref, yg_ref, *rest, bb, tl):
    o_ref = rest[-1]
    m = bb * tl
    rw = rw_ref[...].reshape(m, LANES)
    w = [rw[:, k:k + 1] for k in range(TOP_K)]
    gt = gt_ref[...]
    for j in range(ROW_TILES):
        acc_lo = jnp.zeros((m, LANES), F32)
        acc_hi = jnp.zeros((m, LANES), F32)
        for k in range(TOP_K):
            lo, hi = _unpack_rows([yg_ref[k, pl.ds(j, m, stride=ROW_TILES), :]])
            acc_lo = acc_lo + w[k] * lo[0]
            acc_hi = acc_hi + w[k] * hi[0]
        for acc, c0 in ((acc_lo, LANES * j), (acc_hi, HALF_W + LANES * j)):
            cg = slice(c0, c0 + LANES)
            o_ref[:, :, cg] = x1_ref[:, :, cg] + gt[:, :, cg] * acc.reshape(bb, tl, LANES)


def _combine(x1, gt_m, rw, yg, y_prev, *, bb, tl, b0, bsz):
    nb, length, _ = x1.shape
    m = bb * tl
    assert bb == 1 or tl == length
    n_l = length // tl
    blk0 = b0 // bb

    def smap(b, i):
        return (b, i, 0)

    def xmap(b, i):
        return (blk0 + b, i, 0)

    in_specs = [
        pl.BlockSpec((bb, tl, D_MODEL), smap),
        pl.BlockSpec((bb, 1, D_MODEL), lambda b, i: (blk0 + b, 0, 0)),
        pl.BlockSpec((bb, tl, LANES), smap),
        pl.BlockSpec((TOP_K, m * ROW_TILES, LANES), lambda b, i: (0, b * n_l + i, 0)),
    ]
    args = [x1, gt_m, rw, yg]
    aliases = {}
    if y_prev is not None:
        in_specs.append(pl.BlockSpec(memory_space=pl.ANY))
        args.append(y_prev)
        aliases = {len(args) - 1: 0}
    kern = functools.partial(_combine_kernel, bb=bb, tl=tl)
    return pl.pallas_call(
        kern,
        grid=(nb // bb, n_l),
        in_specs=in_specs,
        out_specs=pl.BlockSpec((bb, tl, D_MODEL), xmap),
        out_shape=jax.ShapeDtypeStruct((bsz, length, D_MODEL), F32),
        input_output_aliases=aliases,
        compiler_params=_cparams("arbitrary", "arbitrary"),
        name="combine",
    )(*args)


def _route_plan(routes):
    expert_ids = jnp.arange(N_EXPERTS, dtype=I32)
    part_counts = [c.reshape(-1)[:N_EXPERTS].astype(I32) for _, c in routes]
    counts = sum(part_counts)
    padded = (counts + ROUTE_BLOCK - 1) // ROUTE_BLOCK * ROUTE_BLOCK
    pends = jnp.cumsum(padded)
    pstarts = pends - padded
    n_assign = sum(ri.shape[0] * ri.shape[1] for ri, _ in routes) * TOP_K
    n_blocks = n_assign // ROUTE_BLOCK + N_EXPERTS
    n_rows = n_blocks * ROUTE_BLOCK
    lists = []
    first = jnp.zeros_like(counts)
    for (ri, _), cnt in zip(routes, part_counts):
        n_tok = ri.shape[0] * ri.shape[1]
        idx = ri[..., :TOP_K].reshape(n_tok, TOP_K)
        rank = ri[..., TOP_K:2 * TOP_K].reshape(n_tok, TOP_K)
        dest = jnp.sum(jnp.where(idx[..., None] == expert_ids, pstarts + first, 0), axis=-1) + rank
        per_w = n_tok // SC_WORKERS
        lists.append(dest.astype(I32).reshape(SC_WORKERS, per_w // SC_ROWS, SC_ROWS, TOP_K).transpose(0, 1, 3, 2))
        first = first + cnt
    idx_flat = jnp.concatenate(lists, axis=1).reshape(-1)
    blk_start = jnp.arange(n_blocks, dtype=I32) * ROUTE_BLOCK
    blk_expert = jnp.minimum(jnp.sum((pends[None, :] <= blk_start[:, None]).astype(I32), axis=1), N_EXPERTS - 1)
    n_used = (pends[-1] // ROUTE_BLOCK).reshape(1)
    nonempty = padded > 0
    later = nonempty[None, :] & (expert_ids[None, :] > expert_ids[:, None])
    next_e = jnp.min(jnp.where(later, expert_ids[None, :], N_EXPERTS), axis=1)
    next_e = jnp.where(next_e == N_EXPERTS, -1, next_e)
    slot_e = (jnp.cumsum(nonempty.astype(I32)) - 1) % 2
    own = blk_expert[:, None] == expert_ids[None, :]
    blk_next = jnp.sum(jnp.where(own, next_e[None, :], 0), axis=1)
    blk_slot = jnp.sum(jnp.where(own, slot_e[None, :], 0), axis=1)
    r = jnp.arange(ROUTE_BLOCK, dtype=I32)[None, :]
    pad_flat = jnp.where(r < (padded - counts)[:, None], (pstarts + counts)[:, None] + r, n_rows)
    blocks = (blk_expert.astype(I32), n_used.astype(I32), blk_next.astype(I32), blk_slot.astype(I32))
    return idx_flat, pad_flat.astype(I32).reshape(-1), blocks, n_rows


def _front(x, mods, pos, prefix, weights, *, bb, tl, n_c):
    g_attn, w_qkv, gains, sinks, bias = weights[0], weights[2], weights[3], weights[4], weights[5]
    sh_a, sc_a = mods[0], mods[1]
    pad = 0 if prefix is not None else PAD
    qa, ka, va, qb, kb, vb, ska, sva, skb, svb = _stage1(
        x, sh_a, sc_a, g_attn, w_qkv, gains, pos, bb=bb, tl=tl, pad=pad)
    if prefix is not None:
        ka, va, kb, vb = [jnp.concatenate([p, t], axis=1) for p, t in zip(prefix, (ka, va, kb, vb))]
    oa, ob = _attention(qa, ka, va, qb, kb, vb, sinks, bias, n_c=n_c, mask_prefix=prefix is None)
    return oa, ob, (ska, sva, skb, svb)


def _moe_segment(parts, weights):
    (g_attn, g_mlp, _, _, _, _, wg, wa, wb, wo, wr, br, w_gate_up, b_gate_up, w_down, b_down) = weights
    merged = []
    for p in parts:
        sh_a, sc_a, gt_a, sh_m, sc_m, _ = p["mods"]
        merged.append(_merge(p["x"], p["oa"], p["ob"], (sh_a, sc_a, gt_a, sh_m, sc_m), g_attn, g_mlp,
                             wg, wa, wb, wo, wr, br, bb=p["cfg"]["bb"], tl=p["cfg"]["tl"], b0=p["b0"], nb=p["nb"]))
    toks = [m[0].shape[0] * m[0].shape[1] for m in merged]
    idx_flat, pad_flat, blocks, n_rows = _route_plan([(m[2], m[4]) for m in merged])
    xs = _dispatch([m[1].reshape(t, ROW_TILES, LANES) for m, t in zip(merged, toks)], idx_flat, pad_flat, n_rows)
    y_rows = _experts(xs.reshape(-1, LANES), *blocks, w_gate_up, b_gate_up, w_down, b_down)
    ygs = _gather_back(y_rows.reshape(-1, ROW_TILES, LANES), idx_flat, toks)
    outs = []
    for p, m, yg, t in zip(parts, merged, ygs, toks):
        outs.append(_combine(m[0], p["mods"][5], m[3], yg.reshape(TOP_K, t * ROW_TILES, LANES), p["y"],
                             bb=p["cfg"]["bb_combine"], tl=p["cfg"]["tl_combine"], b0=p["b0"],
                             bsz=p["x"].shape[0]))
    return outs


def _rel_bias_table(rel_table):
    n_rel = rel_table.shape[1]
    ext = jnp.concatenate(
        [rel_table, jnp.broadcast_to(rel_table[:, -1:], (B_HEADS, CHUNK + BAND_B - n_rel))], axis=1)
    rows = jnp.stack([ext[:, i:i + BAND_B] for i in range(CHUNK)], axis=1)[..., ::-1]
    return rows.astype(F32).reshape(B_HEADS // 2, 2 * CHUNK, BAND_B)


def _prep_weights(g_attn, g_mlp, w_in, q_norm_a, k_norm_a, q_norm_b, k_norm_b, sinks_a, rel_table_b,
                  w_branch_a, w_branch_b, w_out, w_router, b_router, w_gate_up, b_gate_up, w_down, b_down):
    order = jnp.asarray(A_HEAD_ORDER)
    w_qa = w_in[:, :A_Q_W].reshape(D_MODEL, A_Q_HEADS, HEAD_DIM)[:, order].reshape(D_MODEL, A_Q_W)
    w_qkv = jnp.concatenate([w_qa, w_in[:, A_Q_W:QKV_W]], axis=1).astype(BF16)
    wg = w_in[:, QKV_W:].astype(BF16)
    wa = w_branch_a.reshape(A_Q_HEADS, HEAD_DIM, D_MODEL)[order].reshape(A_Q_W, D_MODEL).astype(BF16)
    wb = w_branch_b.astype(BF16)
    wo = w_out.astype(BF16)
    wr = jnp.zeros((D_MODEL, LANES), F32).at[:, :N_EXPERTS].set(w_router)
    br = jnp.full((1, LANES), NEG_INF, F32).at[0, :N_EXPERTS].set(b_router)
    bias = _rel_bias_table(rel_table_b) * LOG2_E
    gains = (q_norm_a, k_norm_a, q_norm_b, k_norm_b)
    return (g_attn, g_mlp, w_qkv, gains, sinks_a.astype(F32) * LOG2_E, bias, wg, wa, wb, wo, wr, br,
            w_gate_up, b_gate_up, w_down, b_down)


def _forward(x_prompt, x_sample, c_prompt, c_sample, cache_a_k, cache_a_v, cache_b_k, cache_b_v, params,
             past_len, cfg_prompt, cfg_sample):
    (g_attn, g_mlp, w_ada, b_ada, w_in, q_norm_a, k_norm_a, q_norm_b, k_norm_b, sinks_a, rel_table_b,
     w_branch_a, w_branch_b, w_out, w_router, b_router, w_gate_up, b_gate_up, w_down, b_down) = params
    depth = g_attn.shape[0]
    bp, lp_, _ = x_prompt.shape
    bs, ls, _ = x_sample.shape
    pos_p = jnp.arange(lp_, dtype=I32)
    pos_s = past_len + jnp.arange(ls, dtype=I32)
    yp, ys = x_prompt, x_sample
    st_p, st_s = [], []
    for l in range(depth):
        weights = _prep_weights(g_attn[l], g_mlp[l], w_in[l], q_norm_a[l], k_norm_a[l], q_norm_b[l],
                                k_norm_b[l], sinks_a[l], rel_table_b[l], w_branch_a[l], w_branch_b[l],
                                w_out[l], w_router[l], b_router[l], w_gate_up[l], b_gate_up[l],
                                w_down[l], b_down[l])
        mod = _adaln(jnp.concatenate([c_prompt, c_sample], axis=0), w_ada[l], b_ada[l])
        mod = mod.reshape(bp + bs, 6, 1, D_MODEL)
        mods_p = [mod[:bp, k] for k in range(6)]
        mods_s = [mod[bp:, k] for k in range(6)]
        wa_c = cache_a_k.shape[2]
        wb_c = cache_b_k.shape[2]
        assert wb_c == WINDOW_B and wa_c == WINDOW_A
        zeros_a = jnp.zeros((bs, PAD - wa_c, A_KV_W), BF16)
        prefix = (
            jnp.concatenate([zeros_a, cache_a_k[l].reshape(bs, wa_c, A_KV_W).astype(BF16)], axis=1),
            jnp.concatenate([zeros_a, cache_a_v[l].reshape(bs, wa_c, A_KV_W).astype(BF16)], axis=1),
            cache_b_k[l].reshape(bs, wb_c, B_W).astype(BF16),
            cache_b_v[l].reshape(bs, wb_c, B_W).astype(BF16),
        )
        front = {k: cfg_prompt[k] for k in ("bb", "tl", "n_c")}
        oa_p, ob_p, sp = _front(yp, mods_p, pos_p, None, weights, **front)
        front = {k: cfg_sample[k] for k in ("bb", "tl", "n_c")}
        oa_s, ob_s, ss = _front(ys, mods_s, pos_s, prefix, weights, **front)
        n_seg = cfg_prompt["n_seg"]
        nb = bp // n_seg
        xp, xs_in = yp, ys
        yp = None
        for s in range(n_seg):
            parts = [dict(x=xp, oa=oa_p, ob=ob_p, mods=mods_p, b0=s * nb, nb=nb, cfg=cfg_prompt, y=yp)]
            if s == n_seg - 1:
                parts.append(dict(x=xs_in, oa=oa_s, ob=ob_s, mods=mods_s, b0=0, nb=bs, cfg=cfg_sample, y=None))
            outs = _moe_segment(parts, weights)
            yp = outs[0]
            if s == n_seg - 1:
                ys = outs[1]
        st_p.append(sp)
        st_s.append(ss)

    def stack(states, k, heads):
        return jnp.stack([s[k].reshape(s[k].shape[0], s[k].shape[1], heads, HEAD_DIM) for s in states])

    return (yp, ys,
            stack(st_p, 0, A_KV_HEADS), stack(st_p, 1, A_KV_HEADS), stack(st_p, 2, B_HEADS), stack(st_p, 3, B_HEADS),
            stack(st_s, 0, A_KV_HEADS), stack(st_s, 1, A_KV_HEADS), stack(st_s, 2, B_HEADS), stack(st_s, 3, B_HEADS))


PAST_LEN = 2048
CFG_PROMPT = dict(bb=1, tl=512, n_c=4, tl_combine=512, bb_combine=1, n_seg=2)
CFG_SAMPLE = dict(bb=8, tl=64, n_c=1, tl_combine=64, bb_combine=8)


def kernel(x_prompt, x_sample, c_prompt, c_sample, cache_a_k, cache_a_v, cache_b_k, cache_b_v, g_attn, g_mlp,
           w_ada, b_ada, w_in, q_norm_a, k_norm_a, q_norm_b, k_norm_b, sinks_a, rel_table_b, w_branch_a,
           w_branch_b, w_out, w_router, b_router, w_gate_up, b_gate_up, w_down, b_down):
    params = (g_attn, g_mlp, w_ada, b_ada, w_in, q_norm_a, k_norm_a, q_norm_b, k_norm_b, sinks_a, rel_table_b,
              w_branch_a, w_branch_b, w_out, w_router, b_router, w_gate_up, b_gate_up, w_down, b_down)
    return _forward(x_prompt, x_sample, c_prompt, c_sample, cache_a_k, cache_a_v, cache_b_k, cache_b_v,
                    params, PAST_LEN, CFG_PROMPT, CFG_SAMPLE)
```

```python
import functools

import jax
import jax.numpy as jnp
from jax import lax
from jax.experimental import pallas as pl
from jax.experimental.pallas import tpu as pltpu
from jax.experimental.pallas import tpu_sc as plsc

F32 = jnp.float32
BF16 = jnp.bfloat16
I32 = jnp.int32

D_MODEL = 1024
CHUNK = 64
HEAD_DIM = 64
LANES = 128
SUBLANES = 8
A_Q_HEADS = 8
A_KV_HEADS = 2
A_GROUP = A_Q_HEADS // A_KV_HEADS
B_HEADS = 8
A_Q_W = A_Q_HEADS * HEAD_DIM
A_KV_W = A_KV_HEADS * HEAD_DIM
B_W = B_HEADS * HEAD_DIM
QKV_W = A_Q_W + 2 * A_KV_W + 3 * B_W
WINDOW_A = 128
WINDOW_B = 512
BAND_A = WINDOW_A + CHUNK
BAND_B = WINDOW_B + CHUNK
PAD = WINDOW_B
ROT_DIM = HEAD_DIM // 4
ROPE_THETA = 500000.0
REL_BACK = 4 * CHUNK
REL_FWD = CHUNK - 1
N_EXPERTS = 32
TOP_K = 4
D_FF = D_MODEL
SWIGLU_LIMIT = 7.0
SWIGLU_ALPHA = 1.702
ROUTE_BLOCK = 512
FF_TILE = 256
RMS_EPS = 1e-6
NEG_INF = -1e30
HIGH_HALF = -65536
LOG2_E = 1.4426950408889634
HALF_W = D_MODEL // 2
ROW_TILES = HALF_W // LANES
VMEM_LIMIT = 56 * 1024 * 1024
SC_CORES = 2
SC_SUBCORES = 16
SC_WORKERS = SC_CORES * SC_SUBCORES
SC_ROWS = 32
TRASH_ROWS = 8

A_HEAD_ORDER = (0, 4, 1, 5, 2, 6, 3, 7)


def _cparams(*sem):
    return pltpu.CompilerParams(dimension_semantics=sem, vmem_limit_bytes=VMEM_LIMIT)


def _adaln_kernel(c_ref, w_ref, b_ref, o_ref):
    c = c_ref[...]
    a = c * jax.nn.sigmoid(c)
    o_ref[...] = jnp.dot(a, w_ref[...], preferred_element_type=F32,
                         precision=lax.Precision.HIGHEST) + b_ref[...]


def _adaln(c, w_ada, b_ada):
    n_rows = c.shape[0]
    n_out = w_ada.shape[1]
    tn = 1024
    return pl.pallas_call(
        _adaln_kernel,
        grid=(n_out // tn,),
        in_specs=[pl.BlockSpec((n_rows, D_MODEL), lambda j: (0, 0)),
                  pl.BlockSpec((D_MODEL, tn), lambda j: (0, j)),
                  pl.BlockSpec((1, tn), lambda j: (0, j))],
        out_specs=pl.BlockSpec((n_rows, tn), lambda j: (0, j)),
        out_shape=jax.ShapeDtypeStruct((n_rows, n_out), F32),
        compiler_params=_cparams("arbitrary"),
        name="adaln",
    )(c, w_ada, b_ada.reshape(1, n_out))


def _pack_rows(ref, x):
    m = x.shape[0]
    for j in range(ROW_TILES):
        lo = x[:, LANES * j:LANES * (j + 1)].astype(BF16).astype(F32)
        hi = x[:, HALF_W + LANES * j:HALF_W + LANES * (j + 1)].astype(BF16).astype(F32)
        word = (lax.bitcast_convert_type(hi, I32) & HIGH_HALF) | lax.shift_right_logical(
            lax.bitcast_convert_type(lo, I32), 16)
        ref[pl.ds(j, m, stride=ROW_TILES), :] = word


def _unpack_rows(words):
    lo = [lax.bitcast_convert_type(lax.shift_left(w, 16), F32) for w in words]
    hi = [lax.bitcast_convert_type(w & HIGH_HALF, F32) for w in words]
    return lo, hi


def _modulated_norm(x, g, scale, shift):
    ms = jnp.mean(x * x, axis=-1, keepdims=True)
    return (x * lax.rsqrt(ms + RMS_EPS) * g) * (1.0 + scale) + shift


def _head_norms(groups, seg):
    flat = [p for parts, _ in groups for p in parts]
    m = flat[0].shape[0]
    sq = [p * p for p in flat]
    if len(sq) % 2:
        sq.append(sq[-1])
    pairs = [jnp.concatenate(sq[i:i + 2], axis=1) for i in range(0, len(sq), 2)]
    ms = jnp.dot(jnp.concatenate(pairs, axis=0).astype(BF16), seg, preferred_element_type=F32)
    r = lax.rsqrt(ms + RMS_EPS)
    out, n = [], 0
    for parts, gain in groups:
        normed = []
        for p in parts:
            rn = r[(n // 2) * m:(n // 2 + 1) * m, (n % 2) * LANES:(n % 2 + 1) * LANES]
            normed.append(p * rn * gain)
            n += 1
        out.append(normed)
    return out


def _rope(p, cos, sin_lo, sin_hi):
    return p * cos + pltpu.roll(p, LANES - ROT_DIM // 2, 1) * sin_lo + pltpu.roll(p, ROT_DIM // 2, 1) * sin_hi


def _stage1_kernel(x_ref, sh_ref, sc_ref, g_ref, w_ref, gqa_ref, gka_ref, gqb_ref, gkb_ref, seg_ref,
                   cos_ref, sinlo_ref, sinhi_ref,
                   qa_ref, ka_ref, va_ref, qb_ref, kb_ref, vb_ref, ska_ref, sva_ref, skb_ref, svb_ref,
                   *, bb, tl, pad_tiles, n_tiles, sa_rows, sb_rows):
    i = pl.program_id(1)
    m = bb * tl

    if pad_tiles:
        @pl.when(i < pad_tiles)
        def _():
            ka_ref[...] = jnp.zeros_like(ka_ref)
            va_ref[...] = jnp.zeros_like(va_ref)
            kb_ref[...] = jnp.zeros_like(kb_ref)
            vb_ref[...] = jnp.zeros_like(vb_ref)

    @pl.when(i >= pad_tiles)
    def _():
        h = _modulated_norm(x_ref[...], g_ref[...], sc_ref[...], sh_ref[...])
        hb = h.reshape(m, D_MODEL).astype(BF16)
        qkv = jnp.dot(hb, w_ref[...], preferred_element_type=F32)
        seg = seg_ref[...]

        def cols(lo, width):
            return [qkv[:, lo + LANES * j: lo + LANES * (j + 1)] for j in range(width // LANES)]

        cos = jnp.concatenate([cos_ref[...]] * bb, axis=0)
        sin_lo = jnp.concatenate([sinlo_ref[...]] * bb, axis=0)
        sin_hi = jnp.concatenate([sinhi_ref[...]] * bb, axis=0)
        scale = HEAD_DIM ** -0.5 * LOG2_E

        off_b = A_Q_W + 2 * A_KV_W
        qa, ka, qb, kb = _head_norms(
            [(cols(0, A_Q_W), gqa_ref[...]), (cols(A_Q_W, A_KV_W), gka_ref[...]),
             (cols(off_b, B_W), gqb_ref[...]), (cols(off_b + B_W, B_W), gkb_ref[...])], seg)
        qa = [_rope(p, cos, sin_lo, sin_hi) * scale for p in qa]
        ka = [_rope(p, cos, sin_lo, sin_hi) for p in ka]
        va = cols(A_Q_W + A_KV_W, A_KV_W)
        qb = [p * scale for p in qb]
        vb = cols(off_b + 2 * B_W, B_W)

        def put(ref, parts):
            for j, p in enumerate(parts):
                ref[:, :, LANES * j:LANES * (j + 1)] = p.reshape(bb, tl, LANES).astype(ref.dtype)

        put(qa_ref, qa)
        put(ka_ref, ka)
        put(va_ref, va)
        put(qb_ref, qb)
        put(kb_ref, kb)
        put(vb_ref, vb)

        @pl.when(i == n_tiles - 1)
        def _():
            def put_state(ref, parts, rows):
                for j, p in enumerate(parts):
                    ref[:, :, LANES * j:LANES * (j + 1)] = p.reshape(bb, tl, LANES)[:, tl - rows:, :]

            put_state(ska_ref, ka, sa_rows)
            put_state(sva_ref, va, sa_rows)
            put_state(skb_ref, kb, sb_rows)
            put_state(svb_ref, vb, sb_rows)


def _rope_tables(pos):
    inv_freq = ROPE_THETA ** (-jnp.arange(0, ROT_DIM, 2, dtype=F32) / ROT_DIM)
    ang = pos.astype(F32)[:, None] * inv_freq[None, :]
    half = ROT_DIM // 2
    lane = jnp.arange(LANES) % HEAD_DIM
    ang_l = jnp.tile(ang, (1, LANES // half))
    rot = lane < ROT_DIM
    cos = jnp.where(rot[None, :], jnp.cos(ang_l), 1.0)
    sin_lo = jnp.where((lane < half)[None, :], -jnp.sin(ang_l), 0.0)
    sin_hi = jnp.where(((lane >= half) & rot)[None, :], jnp.sin(ang_l), 0.0)
    return cos, sin_lo, sin_hi


def _stage1(x, shift, scale, g_attn, w_qkv, gains, pos, *, bb, tl, pad):
    bsz, length, _ = x.shape
    n_tiles = length // tl
    pad_tiles = pad // tl
    sa_rows, sb_rows = min(WINDOW_A, length), min(WINDOW_B, length)
    assert bsz % bb == 0 and length % tl == 0 and pad % tl == 0 and sb_rows <= tl
    cos, sin_lo, sin_hi = _rope_tables(pos)
    seg_ids = jnp.arange(2 * LANES) // HEAD_DIM
    seg = ((seg_ids[:, None] == seg_ids[None, :]).astype(F32) / HEAD_DIM).astype(BF16)
    gq_a, gk_a, gq_b, gk_b = [jnp.tile(g, LANES // HEAD_DIM).reshape(1, LANES) for g in gains]

    def xmap(b, i):
        return (b, jnp.maximum(i - pad_tiles, 0), 0)

    def const2(b, i):
        return (0, 0)

    def tabmap(b, i):
        return (jnp.maximum(i - pad_tiles, 0), 0)

    def modmap(b, i):
        return (b, 0, 0)

    def kvmap(b, i):
        return (b, i, 0)

    lp = length + pad
    kern = functools.partial(_stage1_kernel, bb=bb, tl=tl, pad_tiles=pad_tiles,
                             n_tiles=n_tiles + pad_tiles, sa_rows=sa_rows, sb_rows=sb_rows)
    out_shapes = (
        jax.ShapeDtypeStruct((bsz, length, A_Q_W), BF16),
        jax.ShapeDtypeStruct((bsz, lp, A_KV_W), BF16),
        jax.ShapeDtypeStruct((bsz, lp, A_KV_W), BF16),
        jax.ShapeDtypeStruct((bsz, length, B_W), BF16),
        jax.ShapeDtypeStruct((bsz, lp, B_W), BF16),
        jax.ShapeDtypeStruct((bsz, lp, B_W), BF16),
        jax.ShapeDtypeStruct((bsz, sa_rows, A_KV_W), F32),
        jax.ShapeDtypeStruct((bsz, sa_rows, A_KV_W), F32),
        jax.ShapeDtypeStruct((bsz, sb_rows, B_W), F32),
        jax.ShapeDtypeStruct((bsz, sb_rows, B_W), F32),
    )
    out_specs = (
        pl.BlockSpec((bb, tl, A_Q_W), xmap),
        pl.BlockSpec((bb, tl, A_KV_W), kvmap),
        pl.BlockSpec((bb, tl, A_KV_W), kvmap),
        pl.BlockSpec((bb, tl, B_W), xmap),
        pl.BlockSpec((bb, tl, B_W), kvmap),
        pl.BlockSpec((bb, tl, B_W), kvmap),
        pl.BlockSpec((bb, sa_rows, A_KV_W), modmap),
        pl.BlockSpec((bb, sa_rows, A_KV_W), modmap),
        pl.BlockSpec((bb, sb_rows, B_W), modmap),
        pl.BlockSpec((bb, sb_rows, B_W), modmap),
    )
    in_specs = [
        pl.BlockSpec((bb, tl, D_MODEL), xmap),
        pl.BlockSpec((bb, 1, D_MODEL), modmap),
        pl.BlockSpec((bb, 1, D_MODEL), modmap),
        pl.BlockSpec((1, D_MODEL), const2),
        pl.BlockSpec((D_MODEL, QKV_W), const2),
        pl.BlockSpec((1, LANES), const2),
        pl.BlockSpec((1, LANES), const2),
        pl.BlockSpec((1, LANES), const2),
        pl.BlockSpec((1, LANES), const2),
        pl.BlockSpec((2 * LANES, 2 * LANES), const2),
        pl.BlockSpec((tl, LANES), tabmap),
        pl.BlockSpec((tl, LANES), tabmap),
        pl.BlockSpec((tl, LANES), tabmap),
    ]
    return pl.pallas_call(
        kern,
        grid=(bsz // bb, n_tiles + pad_tiles),
        in_specs=in_specs,
        out_specs=out_specs,
        out_shape=out_shapes,
        compiler_params=_cparams("arbitrary", "arbitrary"),
        name="stage1",
    )(x, shift, scale, g_attn.reshape(1, D_MODEL), w_qkv, gq_a, gk_a, gq_b, gk_b, seg, cos, sin_lo, sin_hi)


def _nt_dot(a, b):
    return lax.dot_general(a, b, (((1,), (1,)), ((), ())), preferred_element_type=F32)


def _attn_kernel(sink_ref, qa_ref, ka_ref, va_ref, qb_ref, kb_ref, vb_ref, bias_ref, oa_ref, ob_ref,
                 *, n_c, mask_prefix):
    qi = pl.program_id(1)
    lane = lax.broadcasted_iota(I32, (CHUNK, LANES), 1)
    low = lane < HEAD_DIM
    half_mask = (jnp.where(low, 1.0, 0.0).astype(BF16), jnp.where(low, 0.0, 1.0).astype(BF16))
    n_pairs = B_W // LANES
    rows_b = n_pairs * 2 * CHUNK
    rows_a = A_Q_HEADS * CHUNK
    key_b = lax.broadcasted_iota(I32, (rows_b, BAND_B), 1)
    key_a = lax.broadcasted_iota(I32, (rows_a, BAND_A), 1)
    row_a = lax.broadcasted_iota(I32, (rows_a, 1), 0) // CHUNK

    sink_col = jnp.zeros((rows_a, 1), F32)
    for r in range(A_Q_HEADS):
        sink_col = jnp.where(row_a == r, sink_ref[r], sink_col)

    def chunk(c, masked):
        ci = qi * n_c + c
        row0 = pl.multiple_of(ci * CHUNK, CHUNK)
        rows = slice(c * CHUNK, (c + 1) * CHUNK)
        first_valid = PAD - ci * CHUNK

        s_parts = []
        for j in range(n_pairs):
            cg = slice(LANES * j, LANES * (j + 1))
            q = qb_ref[0, rows, cg]
            qs = jnp.concatenate([q * half_mask[0], q * half_mask[1]], axis=0)
            k = kb_ref[0, pl.ds(row0, BAND_B), cg]
            s_parts.append(_nt_dot(qs, k) + bias_ref[j])
        s = jnp.concatenate(s_parts, axis=0)
        if masked:
            s = jnp.where(key_b >= first_valid, s, NEG_INF)
        mx = jnp.max(s, axis=-1, keepdims=True)
        e = jnp.exp2(s - mx)
        inv = 1.0 / jnp.sum(e, axis=-1, keepdims=True)
        p_b = e.astype(BF16)
        for j in range(n_pairs):
            cg = slice(LANES * j, LANES * (j + 1))
            v = vb_ref[0, pl.ds(row0, BAND_B), cg]
            rj = slice(2 * CHUNK * j, 2 * CHUNK * (j + 1))
            o = jnp.dot(p_b[rj], v, preferred_element_type=F32) * inv[rj]
            ob_ref[0, rows, cg] = jnp.where(low, o[:CHUNK], o[CHUNK:]).astype(ob_ref.dtype)

        row_a0 = pl.multiple_of(row0 + (WINDOW_B - WINDOW_A), CHUNK)
        k = ka_ref[0, pl.ds(row_a0, BAND_A), :]
        v = va_ref[0, pl.ds(row_a0, BAND_A), :]
        q = jnp.concatenate(
            [qa_ref[0, rows, LANES * j:LANES * (j + 1)] * half_mask[p]
             for p in range(A_KV_HEADS) for j in range(A_GROUP)], axis=0)
        s = _nt_dot(q, k)
        if masked:
            s = jnp.where(key_a >= first_valid - (WINDOW_B - WINDOW_A), s, NEG_INF)
        mx = jnp.maximum(jnp.max(s, axis=-1, keepdims=True), sink_col)
        e = jnp.exp2(s - mx)
        inv = 1.0 / (jnp.sum(e, axis=-1, keepdims=True) + jnp.exp2(sink_col - mx))
        o = jnp.dot(e.astype(BF16), v, preferred_element_type=F32) * inv
        for j in range(A_GROUP):
            lo = o[j * CHUNK:(j + 1) * CHUNK]
            hi = o[(A_GROUP + j) * CHUNK:(A_GROUP + j + 1) * CHUNK]
            oa_ref[0, rows, LANES * j:LANES * (j + 1)] = jnp.where(low, lo, hi).astype(oa_ref.dtype)

    steps_with_pad = pl.cdiv(PAD // CHUNK, n_c) if mask_prefix else 0
    if steps_with_pad:
        @pl.when(qi < steps_with_pad)
        def _():
            for c in range(n_c):
                chunk(c, True)

    @pl.when(qi >= steps_with_pad)
    def _():
        for c in range(n_c):
            chunk(c, False)


def _attention(qa, ka, va, qb, kb, vb, sinks, bias, *, n_c, mask_prefix):
    bsz, length, _ = qa.shape
    lp = ka.shape[1]
    qt = n_c * CHUNK
    assert length % qt == 0 and lp == length + PAD

    def qmap(b, i):
        return (b, i, 0)

    def kmap(b, i):
        return (b, 0, 0)

    kern = functools.partial(_attn_kernel, n_c=n_c, mask_prefix=mask_prefix)
    return pl.pallas_call(
        kern,
        grid=(bsz, length // qt),
        in_specs=[
            pl.BlockSpec(memory_space=pltpu.SMEM),
            pl.BlockSpec((1, qt, A_Q_W), qmap),
            pl.BlockSpec((1, lp, A_KV_W), kmap),
            pl.BlockSpec((1, lp, A_KV_W), kmap),
            pl.BlockSpec((1, qt, B_W), qmap),
            pl.BlockSpec((1, lp, B_W), kmap),
            pl.BlockSpec((1, lp, B_W), kmap),
            pl.BlockSpec((B_HEADS // 2, 2 * CHUNK, BAND_B), lambda b, i: (0, 0, 0)),
        ],
        out_specs=(pl.BlockSpec((1, qt, A_Q_W), qmap), pl.BlockSpec((1, qt, B_W), qmap)),
        out_shape=(jax.ShapeDtypeStruct((bsz, length, A_Q_W), BF16),
                   jax.ShapeDtypeStruct((bsz, length, B_W), BF16)),
        compiler_params=_cparams("arbitrary", "arbitrary"),
        name="attention",
    )(sinks, qa, ka, va, qb, kb, vb, bias)


def _merge_kernel(x_ref, oa_ref, ob_ref, sha_ref, sca_ref, gta_ref, shm_ref, scm_ref, ga_ref, gm_ref,
                  wg_ref, wa_ref, wb_ref, wo_ref, wrh_ref, wrl_ref, br_ref,
                  x1_ref, h2_ref, ri_ref, rw_ref, cnt_ref, carry_ref, logit_ref, *, bb, tl, seg_tiles):
    m = bb * tl
    step = pl.program_id(0)

    @pl.when(step == 0)
    def _():
        carry_ref[...] = jnp.zeros_like(carry_ref)
        logit_ref[...] = jnp.zeros_like(logit_ref)

    logits = logit_ref[...]
    lane = lax.broadcasted_iota(I32, (m, LANES), 1)
    lane_f = lane.astype(F32)
    work = logits
    top_v, top_i = [], []
    sel_f = jnp.zeros((m, LANES), F32)
    for _ in range(TOP_K):
        mx = jnp.max(work, axis=-1, keepdims=True)
        idx = jnp.min(jnp.where(work == mx, lane_f, float(LANES)), axis=-1, keepdims=True)
        hit = lane_f == idx
        top_v.append(mx)
        top_i.append(idx)
        sel_f = jnp.where(hit, 1.0, sel_f)
        work = jnp.where(hit, -jnp.inf, work)
    ex = [jnp.exp(v - top_v[0]) for v in top_v]
    inv = 1.0 / (ex[0] + ex[1] + ex[2] + ex[3])

    r_i = lax.broadcasted_iota(I32, (m, m), 0)
    c_i = lax.broadcasted_iota(I32, (m, m), 1)
    before = jnp.where(r_i > c_i, 1.0, 0.0).astype(BF16)
    keep = jnp.where(jnp.maximum(step - 1, 0) % seg_tiles == 0, 0.0, 1.0)
    carry_in = carry_ref[...] * keep
    pos = jnp.dot(before, sel_f.astype(BF16), preferred_element_type=F32) + carry_in
    live = jnp.where(step > 0, 1.0, 0.0)
    carry = carry_in + live * jnp.sum(sel_f, axis=0, keepdims=True)
    carry_ref[...] = carry
    cnt_ref[0] = carry

    ri_f = jnp.zeros((m, LANES), F32)
    rw = jnp.zeros((m, LANES), F32)
    for k in range(TOP_K):
        rank = jnp.sum(jnp.where(lane_f == top_i[k], pos, 0.0), axis=-1, keepdims=True)
        ri_f = jnp.where(lane == k, top_i[k], ri_f)
        ri_f = jnp.where(lane == TOP_K + k, rank, ri_f)
        rw = jnp.where(lane == k, ex[k] * inv, rw)
    ri_ref[...] = ri_f.astype(I32).reshape(bb, tl, LANES)
    rw_ref[...] = rw.reshape(bb, tl, LANES)

    x = x_ref[...]
    h = _modulated_norm(x, ga_ref[...], sca_ref[...], sha_ref[...])
    hb = h.reshape(m, D_MODEL).astype(BF16)
    gates = jnp.dot(hb, wg_ref[...], preferred_element_type=F32)
    ya = jnp.dot(oa_ref[...].reshape(m, A_Q_W), wa_ref[...], preferred_element_type=F32)
    yb = jnp.dot(ob_ref[...].reshape(m, B_W), wb_ref[...], preferred_element_type=F32)
    mix = jax.nn.sigmoid(gates[:, :D_MODEL]) * ya + jax.nn.sigmoid(gates[:, D_MODEL:]) * yb
    z = jnp.dot(mix.astype(BF16), wo_ref[...], preferred_element_type=F32)
    x1 = x + gta_ref[...] * z.reshape(bb, tl, D_MODEL)
    x1_ref[...] = x1

    h2 = _modulated_norm(x1, gm_ref[...], scm_ref[...], shm_ref[...]).reshape(m, D_MODEL)
    _pack_rows(h2_ref, h2)

    h2_hi = h2.astype(BF16)
    h2_lo = (h2 - h2_hi.astype(F32)).astype(BF16)
    wrh = wrh_ref[...]
    logit_ref[...] = (jnp.dot(h2_hi, wrh, preferred_element_type=F32)
                      + jnp.dot(h2_lo, wrh, preferred_element_type=F32)
                      + jnp.dot(h2_hi, wrl_ref[...], preferred_element_type=F32)) + br_ref[...]


def _merge(x, oa, ob, mods, g_attn, g_mlp, wg, wa, wb, wo, wr, br, *, bb, tl, b0, nb):
    _, length, _ = x.shape
    n_tok = nb * length
    m = bb * tl
    assert bb == 1 or tl == length
    sh_a, sc_a, gt_a, sh_m, sc_m = mods
    n_l = length // tl
    n_tiles = (nb // bb) * n_l
    blk0 = b0 // bb
    wr_hi = wr.astype(BF16)
    wr_lo = (wr - wr_hi.astype(F32)).astype(BF16)

    def cur(s):
        return jnp.minimum(s, n_tiles - 1)

    def prev(s):
        return jnp.maximum(s - 1, 0)

    def xmap(s):
        return (blk0 + cur(s) // n_l, cur(s) % n_l, 0)

    def modmap(s):
        return (blk0 + cur(s) // n_l, 0, 0)

    def omap(s):
        return (cur(s) // n_l, cur(s) % n_l, 0)

    def const2(s):
        return (0, 0)

    def flatmap(s):
        return (cur(s), 0)

    def routemap(s):
        return (prev(s) // n_l, prev(s) % n_l, 0)

    modspec = pl.BlockSpec((bb, 1, D_MODEL), modmap)
    kern = functools.partial(_merge_kernel, bb=bb, tl=tl, seg_tiles=n_tiles)
    return pl.pallas_call(
        kern,
        grid=(n_tiles + 1,),
        in_specs=[
            pl.BlockSpec((bb, tl, D_MODEL), xmap),
            pl.BlockSpec((bb, tl, A_Q_W), xmap),
            pl.BlockSpec((bb, tl, B_W), xmap),
            modspec, modspec, modspec, modspec, modspec,
            pl.BlockSpec((1, D_MODEL), const2),
            pl.BlockSpec((1, D_MODEL), const2),
            pl.BlockSpec((D_MODEL, 2 * D_MODEL), const2),
            pl.BlockSpec((A_Q_W, D_MODEL), const2),
            pl.BlockSpec((B_W, D_MODEL), const2),
            pl.BlockSpec((D_MODEL, D_MODEL), const2),
            pl.BlockSpec((D_MODEL, LANES), const2),
            pl.BlockSpec((D_MODEL, LANES), const2),
            pl.BlockSpec((1, LANES), const2),
        ],
        out_specs=(
            pl.BlockSpec((bb, tl, D_MODEL), omap),
            pl.BlockSpec((m * ROW_TILES, LANES), flatmap),
            pl.BlockSpec((bb, tl, LANES), routemap),
            pl.BlockSpec((bb, tl, LANES), routemap),
            pl.BlockSpec((1, 1, LANES), lambda s: (0, 0, 0)),
        ),
        out_shape=(
            jax.ShapeDtypeStruct((nb, length, D_MODEL), F32),
            jax.ShapeDtypeStruct((n_tok * ROW_TILES, LANES), I32),
            jax.ShapeDtypeStruct((nb, length, LANES), I32),
            jax.ShapeDtypeStruct((nb, length, LANES), F32),
            jax.ShapeDtypeStruct((1, 1, LANES), F32),
        ),
        scratch_shapes=[pltpu.VMEM((1, LANES), F32), pltpu.VMEM((m, LANES), F32)],
        compiler_params=_cparams("arbitrary"),
        name="merge",
    )(x, oa, ob, sh_a, sc_a, gt_a, sh_m, sc_m, g_attn.reshape(1, D_MODEL), g_mlp.reshape(1, D_MODEL),
      wg, wa, wb, wo, wr_hi, wr_lo, br)


def _sc_mesh():
    return plsc.VectorSubcoreMesh(core_axis_name="c", subcore_axis_name="s",
                                  num_cores=SC_CORES, num_subcores=SC_SUBCORES)


def _sc_worker():
    return lax.axis_index("c") * SC_SUBCORES + lax.axis_index("s")


def _sc_params():
    return pltpu.CompilerParams(use_tc_tiling_on_sc=True)


def _part_chunks(parts):
    assert all(p % (2 * SC_WORKERS * SC_ROWS) == 0 for p in parts)
    chunks = [p // (SC_WORKERS * SC_ROWS) for p in parts]
    starts = [sum(chunks[:i]) for i in range(len(chunks))]
    return chunks, starts


def _dispatch(h2_parts, idx_flat, pad_flat, n_rows):
    n_parts = len(h2_parts)
    chunks, starts = _part_chunks([h.shape[0] for h in h2_parts])
    n_idx = sum(chunks) * TOP_K * SC_ROWS
    n_pad = pad_flat.shape[0] // SC_WORKERS
    pad_chunks = n_pad // SC_ROWS
    assert n_pad % SC_ROWS == 0
    zeros = jnp.zeros((SC_ROWS, ROW_TILES, LANES), I32)

    def body(*refs):
        h2_hbms = refs[:n_parts]
        idx_hbm, pad_hbm, zeros_hbm, xs_hbm, idx_v, pad_v, buf0, buf1, ls0, ls1, ss0, ss1 = refs[n_parts:]
        wid = _sc_worker()
        bufs, lsems, ssems = (buf0, buf1), (ls0, ls1), (ss0, ss1)
        pltpu.sync_copy(idx_hbm.at[pl.ds(pl.multiple_of(wid * n_idx, SC_ROWS), n_idx)], idx_v)
        pltpu.sync_copy(pad_hbm.at[pl.ds(pl.multiple_of(wid * n_pad, SC_ROWS), n_pad)], pad_v)

        pltpu.sync_copy(zeros_hbm, buf0)
        for c in range(pad_chunks):
            pltpu.async_copy(buf0, xs_hbm.at[pad_v.at[pl.ds(c * SC_ROWS, SC_ROWS)]], ss0)
        for c in range(pad_chunks):
            pltpu.make_async_copy(buf0, xs_hbm.at[pad_v.at[pl.ds(c * SC_ROWS, SC_ROWS)]], ss0).wait()

        def run_part(h2_hbm, n_chunks, c0):
            base = wid * (n_chunks * SC_ROWS)

            def load(g, b):
                src = h2_hbm.at[pl.ds(pl.multiple_of(base + g * SC_ROWS, SC_ROWS), SC_ROWS)]
                return pltpu.make_async_copy(src, bufs[b], lsems[b])

            def scatter(g, k, b):
                off = pl.multiple_of(((c0 + g) * TOP_K + k) * SC_ROWS, SC_ROWS)
                return pltpu.make_async_copy(bufs[b], xs_hbm.at[idx_v.at[pl.ds(off, SC_ROWS)]], ssems[b])

            load(0, 0).start()

            @pl.loop(0, n_chunks, step=2)
            def _(g0):
                for b in range(2):
                    g = g0 + b
                    load(g, b).wait()

                    @pl.when(g >= 1)
                    def _():
                        for k in range(TOP_K):
                            scatter(g - 1, k, 1 - b).wait()

                    @pl.when(g + 1 < n_chunks)
                    def _():
                        load(g + 1, 1 - b).start()

                    for k in range(TOP_K):
                        scatter(g, k, b).start()

            for k in range(TOP_K):
                scatter(n_chunks - 1, k, (n_chunks - 1) % 2).wait()

        for h2_hbm, n_chunks, c0 in zip(h2_hbms, chunks, starts):
            run_part(h2_hbm, n_chunks, c0)

    call = pl.kernel(
        body,
        out_type=jax.ShapeDtypeStruct((n_rows + TRASH_ROWS, ROW_TILES, LANES), I32),
        mesh=_sc_mesh(),
        scratch_types=[pltpu.VMEM((n_idx,), I32), pltpu.VMEM((n_pad,), I32),
                       pltpu.VMEM((SC_ROWS, ROW_TILES, LANES), I32), pltpu.VMEM((SC_ROWS, ROW_TILES, LANES), I32),
                       pltpu.SemaphoreType.DMA, pltpu.SemaphoreType.DMA,
                       pltpu.SemaphoreType.DMA, pltpu.SemaphoreType.DMA],
        compiler_params=_sc_params(),
        name="dispatch",
    )
    return call(*h2_parts, idx_flat, pad_flat, zeros)


def _gather_back(y_rows, idx_flat, parts):
    n_parts = len(parts)
    chunks, starts = _part_chunks(parts)
    n_idx = sum(chunks) * TOP_K * SC_ROWS

    def body(*refs):
        y_hbm, idx_hbm = refs[:2]
        yg_hbms = refs[2:2 + n_parts]
        idx_v, buf0, buf1, gs0, gs1, ws0, ws1 = refs[2 + n_parts:]
        wid = _sc_worker()
        bufs, gsems, wsems = (buf0, buf1), (gs0, gs1), (ws0, ws1)
        pltpu.sync_copy(idx_hbm.at[pl.ds(pl.multiple_of(wid * n_idx, SC_ROWS), n_idx)], idx_v)

        def run_part(yg_hbm, n_chunks, c0):
            base = wid * (n_chunks * SC_ROWS)
            n_items = n_chunks * TOP_K

            def gather(i, b):
                off = pl.multiple_of((c0 * TOP_K + i) * SC_ROWS, SC_ROWS)
                return pltpu.make_async_copy(y_hbm.at[idx_v.at[pl.ds(off, SC_ROWS)]], bufs[b], gsems[b])

            def write(i, b):
                g, k = i // TOP_K, i % TOP_K
                dst = yg_hbm.at[k, pl.ds(pl.multiple_of(base + g * SC_ROWS, SC_ROWS), SC_ROWS)]
                return pltpu.make_async_copy(bufs[b], dst, wsems[b])

            gather(0, 0).start()

            @pl.loop(0, n_items, step=2)
            def _(i0):
                for b in range(2):
                    i = i0 + b
                    gather(i, b).wait()

                    @pl.when(i >= 1)
                    def _():
                        write(i - 1, 1 - b).wait()

                    @pl.when(i + 1 < n_items)
                    def _():
                        gather(i + 1, 1 - b).start()

                    write(i, b).start()

            write(n_items - 1, (n_items - 1) % 2).wait()

        for yg_hbm, n_chunks, c0 in zip(yg_hbms, chunks, starts):
            run_part(yg_hbm, n_chunks, c0)

    call = pl.kernel(
        body,
        out_type=tuple(jax.ShapeDtypeStruct((TOP_K, p, ROW_TILES, LANES), I32) for p in parts),
        mesh=_sc_mesh(),
        scratch_types=[pltpu.VMEM((n_idx,), I32),
                       pltpu.VMEM((SC_ROWS, ROW_TILES, LANES), I32), pltpu.VMEM((SC_ROWS, ROW_TILES, LANES), I32),
                       pltpu.SemaphoreType.DMA, pltpu.SemaphoreType.DMA,
                       pltpu.SemaphoreType.DMA, pltpu.SemaphoreType.DMA],
        compiler_params=_sc_params(),
        name="gather_back",
    )
    return call(y_rows, idx_flat)


def _expert_kernel(start_ref, count_ref, xs_hbm, wgu_ref, bgu_ref, wd_ref, bd_ref, y_hbm,
                   xbuf, ybuf, wgu_bf, wd_bf, sem_in, sem_out):
    e = pl.program_id(0)
    first = start_ref[e]
    n = count_ref[e]
    rb = ROUTE_BLOCK * ROW_TILES

    def rows(blk):
        return pl.ds(pl.multiple_of((first + blk) * rb, rb), rb)

    def copy_in(blk, slot):
        return pltpu.make_async_copy(xs_hbm.at[rows(blk), :], xbuf.at[slot], sem_in.at[slot])

    def copy_out(blk, slot):
        return pltpu.make_async_copy(ybuf.at[slot], y_hbm.at[rows(blk), :], sem_out.at[slot])

    def block(i, carry):
        slot = i % 2
        copy_in(i, slot).wait()

        @pl.when(i + 1 < n)
        def _():
            copy_in(i + 1, 1 - slot).start()

        @pl.when(i >= 2)
        def _():
            copy_out(i - 2, slot).wait()

        xs_ref = xbuf.at[slot]
        lo, hi = _unpack_rows([xs_ref[pl.ds(j, ROUTE_BLOCK, stride=ROW_TILES), :] for j in range(ROW_TILES)])
        x = jnp.concatenate(lo + hi, axis=1).astype(BF16)
        acts = []
        for j in range(D_FF // FF_TILE):
            gc = slice(FF_TILE * j, FF_TILE * (j + 1))
            uc = slice(D_FF + FF_TILE * j, D_FF + FF_TILE * (j + 1))
            g = jnp.dot(x, wgu_bf[:, gc], preferred_element_type=F32) + bgu_ref[0, :, gc]
            u = jnp.dot(x, wgu_bf[:, uc], preferred_element_type=F32) + bgu_ref[0, :, uc]
            gate = jnp.minimum(g, SWIGLU_LIMIT)
            up = jnp.clip(u, -SWIGLU_LIMIT, SWIGLU_LIMIT)
            glu = gate * jax.nn.sigmoid(gate * SWIGLU_ALPHA)
            acts.append(((up + 1.0) * glu).astype(BF16))
        act = jnp.concatenate(acts, axis=1)
        y = jnp.dot(act, wd_bf[...], preferred_element_type=F32) + bd_ref[0]
        _pack_rows(ybuf.at[slot], y)
        copy_out(i, slot).start()
        return carry

    @pl.when(n > 0)
    def _():
        copy_in(0, 0).start()
        wgu_bf[...] = wgu_ref[0].astype(BF16)
        wd_bf[...] = wd_ref[0].astype(BF16)
        lax.fori_loop(0, n, block, 0)

        @pl.when(n >= 2)
        def _():
            copy_out(n - 2, n % 2).wait()
        copy_out(n - 1, (n - 1) % 2).wait()


def _experts(xs_rows, blk_first, blk_count, w_gate_up, b_gate_up, w_down, b_down):
    def wmap(e, first, count):
        return (e, 0, 0)

    return pl.pallas_call(
        _expert_kernel,
        grid_spec=pltpu.PrefetchScalarGridSpec(
            num_scalar_prefetch=2,
            grid=(N_EXPERTS,),
            in_specs=[
                pl.BlockSpec(memory_space=pl.ANY),
                pl.BlockSpec((1, D_MODEL, 2 * D_FF), wmap),
                pl.BlockSpec((1, 1, 2 * D_FF), wmap),
                pl.BlockSpec((1, D_FF, D_MODEL), wmap),
                pl.BlockSpec((1, 1, D_MODEL), wmap),
            ],
            out_specs=pl.BlockSpec(memory_space=pl.ANY),
            scratch_shapes=[pltpu.VMEM((2, ROUTE_BLOCK * ROW_TILES, LANES), I32),
                            pltpu.VMEM((2, ROUTE_BLOCK * ROW_TILES, LANES), I32),
                            pltpu.VMEM((D_MODEL, 2 * D_FF), BF16), pltpu.VMEM((D_FF, D_MODEL), BF16),
                            pltpu.SemaphoreType.DMA((2,)), pltpu.SemaphoreType.DMA((2,))],
        ),
        out_shape=jax.ShapeDtypeStruct(xs_rows.shape, I32),
        compiler_params=_cparams("arbitrary"),
        name="experts",
    )(blk_first, blk_count, xs_rows, w_gate_up, b_gate_up.reshape(N_EXPERTS, 1, 2 * D_FF),
      w_down, b_down.reshape(N_EXPERTS, 1, D_MODEL))


def _combine_kernel(x1_ref, gt_ref, rw_ref, yg_ref, *rest, bb, tl):
    o_ref = rest[-1]
    m = bb * tl
    rw = rw_ref[...].reshape(m, LANES)
    w = [rw[:, k:k + 1] for k in range(TOP_K)]
    gt = gt_ref[...]
    for j in range(ROW_TILES):
        acc_lo = jnp.zeros((m, LANES), F32)
        acc_hi = jnp.zeros((m, LANES), F32)
        for k in range(TOP_K):
            lo, hi = _unpack_rows([yg_ref[k, pl.ds(j, m, stride=ROW_TILES), :]])
            acc_lo = acc_lo + w[k] * lo[0]
            acc_hi = acc_hi + w[k] * hi[0]
        for acc, c0 in ((acc_lo, LANES * j), (acc_hi, HALF_W + LANES * j)):
            cg = slice(c0, c0 + LANES)
            o_ref[:, :, cg] = x1_ref[:, :, cg] + gt[:, :, cg] * acc.reshape(bb, tl, LANES)


def _combine(x1, gt_m, rw, yg, y_prev, *, bb, tl, b0, bsz):
    nb, length, _ = x1.shape
    m = bb * tl
    assert bb == 1 or tl == length
    n_l = length // tl
    blk0 = b0 // bb

    def smap(b, i):
        return (b, i, 0)

    def xmap(b, i):
        return (blk0 + b, i, 0)

    in_specs = [
        pl.BlockSpec((bb, tl, D_MODEL), smap),
        pl.BlockSpec((bb, 1, D_MODEL), lambda b, i: (blk0 + b, 0, 0)),
        pl.BlockSpec((bb, tl, LANES), smap),
        pl.BlockSpec((TOP_K, m * ROW_TILES, LANES), lambda b, i: (0, b * n_l + i, 0)),
    ]
    args = [x1, gt_m, rw, yg]
    aliases = {}
    if y_prev is not None:
        in_specs.append(pl.BlockSpec(memory_space=pl.ANY))
        args.append(y_prev)
        aliases = {len(args) - 1: 0}
    kern = functools.partial(_combine_kernel, bb=bb, tl=tl)
    return pl.pallas_call(
        kern,
        grid=(nb // bb, n_l),
        in_specs=in_specs,
        out_specs=pl.BlockSpec((bb, tl, D_MODEL), xmap),
        out_shape=jax.ShapeDtypeStruct((bsz, length, D_MODEL), F32),
        input_output_aliases=aliases,
        compiler_params=_cparams("arbitrary", "arbitrary"),
        name="combine",
    )(*args)


def _route_plan(routes):
    expert_ids = jnp.arange(N_EXPERTS, dtype=I32)
    part_counts = [c.reshape(-1)[:N_EXPERTS].astype(I32) for _, c in routes]
    counts = sum(part_counts)
    padded = (counts + ROUTE_BLOCK - 1) // ROUTE_BLOCK * ROUTE_BLOCK
    pends = jnp.cumsum(padded)
    pstarts = pends - padded
    n_assign = sum(ri.shape[0] * ri.shape[1] for ri, _ in routes) * TOP_K
    n_blocks = n_assign // ROUTE_BLOCK + N_EXPERTS
    n_rows = n_blocks * ROUTE_BLOCK
    lists = []
    first = jnp.zeros_like(counts)
    for (ri, _), cnt in zip(routes, part_counts):
        n_tok = ri.shape[0] * ri.shape[1]
        idx = ri[..., :TOP_K].reshape(n_tok, TOP_K)
        rank = ri[..., TOP_K:2 * TOP_K].reshape(n_tok, TOP_K)
        dest = jnp.sum(jnp.where(idx[..., None] == expert_ids, pstarts + first, 0), axis=-1) + rank
        per_w = n_tok // SC_WORKERS
        lists.append(dest.astype(I32).reshape(SC_WORKERS, per_w // SC_ROWS, SC_ROWS, TOP_K).transpose(0, 1, 3, 2))
        first = first + cnt
    idx_flat = jnp.concatenate(lists, axis=1).reshape(-1)
    r = jnp.arange(ROUTE_BLOCK, dtype=I32)[None, :]
    pad_flat = jnp.where(r < (padded - counts)[:, None], (pstarts + counts)[:, None] + r, n_rows)
    blocks = ((pstarts // ROUTE_BLOCK).astype(I32), (padded // ROUTE_BLOCK).astype(I32))
    return idx_flat, pad_flat.astype(I32).reshape(-1), blocks, n_rows


def _front(x, mods, pos, prefix, weights, *, bb, tl, n_c):
    g_attn, w_qkv, gains, sinks, bias = weights[0], weights[2], weights[3], weights[4], weights[5]
    sh_a, sc_a = mods[0], mods[1]
    pad = 0 if prefix is not None else PAD
    qa, ka, va, qb, kb, vb, ska, sva, skb, svb = _stage1(
        x, sh_a, sc_a, g_attn, w_qkv, gains, pos, bb=bb, tl=tl, pad=pad)
    if prefix is not None:
        ka, va, kb, vb = [jnp.concatenate([p, t], axis=1) for p, t in zip(prefix, (ka, va, kb, vb))]
    oa, ob = _attention(qa, ka, va, qb, kb, vb, sinks, bias, n_c=n_c, mask_prefix=prefix is None)
    return oa, ob, (ska, sva, skb, svb)


def _moe_segment(parts, weights):
    (g_attn, g_mlp, _, _, _, _, wg, wa, wb, wo, wr, br, w_gate_up, b_gate_up, w_down, b_down) = weights
    merged = []
    for p in parts:
        sh_a, sc_a, gt_a, sh_m, sc_m, _ = p["mods"]
        merged.append(_merge(p["x"], p["oa"], p["ob"], (sh_a, sc_a, gt_a, sh_m, sc_m), g_attn, g_mlp,
                             wg, wa, wb, wo, wr, br, bb=p["cfg"]["bb"], tl=p["cfg"]["tl"], b0=p["b0"], nb=p["nb"]))
    toks = [m[0].shape[0] * m[0].shape[1] for m in merged]
    idx_flat, pad_flat, blocks, n_rows = _route_plan([(m[2], m[4]) for m in merged])
    xs = _dispatch([m[1].reshape(t, ROW_TILES, LANES) for m, t in zip(merged, toks)], idx_flat, pad_flat, n_rows)
    y_rows = _experts(xs.reshape(-1, LANES), *blocks, w_gate_up, b_gate_up, w_down, b_down)
    ygs = _gather_back(y_rows.reshape(-1, ROW_TILES, LANES), idx_flat, toks)
    outs = []
    for p, m, yg, t in zip(parts, merged, ygs, toks):
        outs.append(_combine(m[0], p["mods"][5], m[3], yg.reshape(TOP_K, t * ROW_TILES, LANES), p["y"],
                             bb=p["cfg"]["bb_combine"], tl=p["cfg"]["tl_combine"], b0=p["b0"],
                             bsz=p["x"].shape[0]))
    return outs


def _rel_bias_table(rel_table):
    n_rel = rel_table.shape[1]
    ext = jnp.concatenate(
        [rel_table, jnp.broadcast_to(rel_table[:, -1:], (B_HEADS, CHUNK + BAND_B - n_rel))], axis=1)
    rows = jnp.stack([ext[:, i:i + BAND_B] for i in range(CHUNK)], axis=1)[..., ::-1]
    return rows.astype(F32).reshape(B_HEADS // 2, 2 * CHUNK, BAND_B)


def _prep_weights(g_attn, g_mlp, w_in, q_norm_a, k_norm_a, q_norm_b, k_norm_b, sinks_a, rel_table_b,
                  w_branch_a, w_branch_b, w_out, w_router, b_router, w_gate_up, b_gate_up, w_down, b_down):
    order = jnp.asarray(A_HEAD_ORDER)
    w_qa = w_in[:, :A_Q_W].reshape(D_MODEL, A_Q_HEADS, HEAD_DIM)[:, order].reshape(D_MODEL, A_Q_W)
    w_qkv = jnp.concatenate([w_qa, w_in[:, A_Q_W:QKV_W]], axis=1).astype(BF16)
    wg = w_in[:, QKV_W:].astype(BF16)
    wa = w_branch_a.reshape(A_Q_HEADS, HEAD_DIM, D_MODEL)[order].reshape(A_Q_W, D_MODEL).astype(BF16)
    wb = w_branch_b.astype(BF16)
    wo = w_out.astype(BF16)
    wr = jnp.zeros((D_MODEL, LANES), F32).at[:, :N_EXPERTS].set(w_router)
    br = jnp.full((1, LANES), NEG_INF, F32).at[0, :N_EXPERTS].set(b_router)
    bias = _rel_bias_table(rel_table_b) * LOG2_E
    gains = (q_norm_a, k_norm_a, q_norm_b, k_norm_b)
    return (g_attn, g_mlp, w_qkv, gains, sinks_a.astype(F32) * LOG2_E, bias, wg, wa, wb, wo, wr, br,
            w_gate_up, b_gate_up, w_down, b_down)


def _forward(x_prompt, x_sample, c_prompt, c_sample, cache_a_k, cache_a_v, cache_b_k, cache_b_v, params,
             past_len, cfg_prompt, cfg_sample):
    (g_attn, g_mlp, w_ada, b_ada, w_in, q_norm_a, k_norm_a, q_norm_b, k_norm_b, sinks_a, rel_table_b,
     w_branch_a, w_branch_b, w_out, w_router, b_router, w_gate_up, b_gate_up, w_down, b_down) = params
    depth = g_attn.shape[0]
    bp, lp_, _ = x_prompt.shape
    bs, ls, _ = x_sample.shape
    pos_p = jnp.arange(lp_, dtype=I32)
    pos_s = past_len + jnp.arange(ls, dtype=I32)
    yp, ys = x_prompt, x_sample
    st_p, st_s = [], []
    for l in range(depth):
        weights = _prep_weights(g_attn[l], g_mlp[l], w_in[l], q_norm_a[l], k_norm_a[l], q_norm_b[l],
                                k_norm_b[l], sinks_a[l], rel_table_b[l], w_branch_a[l], w_branch_b[l],
                                w_out[l], w_router[l], b_router[l], w_gate_up[l], b_gate_up[l],
                                w_down[l], b_down[l])
        mod = _adaln(jnp.concatenate([c_prompt, c_sample], axis=0), w_ada[l], b_ada[l])
        mod = mod.reshape(bp + bs, 6, 1, D_MODEL)
        mods_p = [mod[:bp, k] for k in range(6)]
        mods_s = [mod[bp:, k] for k in range(6)]
        wa_c = cache_a_k.shape[2]
        wb_c = cache_b_k.shape[2]
        assert wb_c == WINDOW_B and wa_c == WINDOW_A
        zeros_a = jnp.zeros((bs, PAD - wa_c, A_KV_W), BF16)
        prefix = (
            jnp.concatenate([zeros_a, cache_a_k[l].reshape(bs, wa_c, A_KV_W).astype(BF16)], axis=1),
            jnp.concatenate([zeros_a, cache_a_v[l].reshape(bs, wa_c, A_KV_W).astype(BF16)], axis=1),
            cache_b_k[l].reshape(bs, wb_c, B_W).astype(BF16),
            cache_b_v[l].reshape(bs, wb_c, B_W).astype(BF16),
        )
        front = {k: cfg_prompt[k] for k in ("bb", "tl", "n_c")}
        oa_p, ob_p, sp = _front(yp, mods_p, pos_p, None, weights, **front)
        front = {k: cfg_sample[k] for k in ("bb", "tl", "n_c")}
        oa_s, ob_s, ss = _front(ys, mods_s, pos_s, prefix, weights, **front)
        n_seg = cfg_prompt["n_seg"]
        nb = bp // n_seg
        xp, xs_in = yp, ys
        yp = None
        for s in range(n_seg):
            parts = [dict(x=xp, oa=oa_p, ob=ob_p, mods=mods_p, b0=s * nb, nb=nb, cfg=cfg_prompt, y=yp)]
            if s == n_seg - 1:
                parts.append(dict(x=xs_in, oa=oa_s, ob=ob_s, mods=mods_s, b0=0, nb=bs, cfg=cfg_sample, y=None))
            outs = _moe_segment(parts, weights)
            yp = outs[0]
            if s == n_seg - 1:
                ys = outs[1]
        st_p.append(sp)
        st_s.append(ss)

    def stack(states, k, heads):
        return jnp.stack([s[k].reshape(s[k].shape[0], s[k].shape[1], heads, HEAD_DIM) for s in states])

    return (yp, ys,
            stack(st_p, 0, A_KV_HEADS), stack(st_p, 1, A_KV_HEADS), stack(st_p, 2, B_HEADS), stack(st_p, 3, B_HEADS),
            stack(st_s, 0, A_KV_HEADS), stack(st_s, 1, A_KV_HEADS), stack(st_s, 2, B_HEADS), stack(st_s, 3, B_HEADS))


PAST_LEN = 2048
CFG_PROMPT = dict(bb=1, tl=512, n_c=4, tl_combine=512, bb_combine=1, n_seg=2)
CFG_SAMPLE = dict(bb=8, tl=64, n_c=1, tl_combine=64, bb_combine=8)


def kernel(x_prompt, x_sample, c_prompt, c_sample, cache_a_k, cache_a_v, cache_b_k, cache_b_v, g_attn, g_mlp,
           w_ada, b_ada, w_in, q_norm_a, k_norm_a, q_norm_b, k_norm_b, sinks_a, rel_table_b, w_branch_a,
           w_branch_b, w_out, w_router, b_router, w_gate_up, b_gate_up, w_down, b_down):
    params = (g_attn, g_mlp, w_ada, b_ada, w_in, q_norm_a, k_norm_a, q_norm_b, k_norm_b, sinks_a, rel_table_b,
              w_branch_a, w_branch_b, w_out, w_router, b_router, w_gate_up, b_gate_up, w_down, b_down)
    return _forward(x_prompt, x_sample, c_prompt, c_sample, cache_a_k, cache_a_v, cache_b_k, cache_b_v,
                    params, PAST_LEN, CFG_PROMPT, CFG_SAMPLE)
```

```python
import functools

import jax
import jax.numpy as jnp
from jax import lax
from jax.experimental import pallas as pl
from jax.experimental.pallas import tpu as pltpu
from jax.experimental.pallas import tpu_sc as plsc

F32 = jnp.float32
BF16 = jnp.bfloat16
I32 = jnp.int32

D_MODEL = 1024
CHUNK = 64
HEAD_DIM = 64
LANES = 128
SUBLANES = 8
A_Q_HEADS = 8
A_KV_HEADS = 2
A_GROUP = A_Q_HEADS // A_KV_HEADS
B_HEADS = 8
A_Q_W = A_Q_HEADS * HEAD_DIM
A_KV_W = A_KV_HEADS * HEAD_DIM
B_W = B_HEADS * HEAD_DIM
QKV_W = A_Q_W + 2 * A_KV_W + 3 * B_W
WINDOW_A = 128
WINDOW_B = 512
BAND_A = WINDOW_A + CHUNK
BAND_B = WINDOW_B + CHUNK
PAD = WINDOW_B
ROT_DIM = HEAD_DIM // 4
ROPE_THETA = 500000.0
REL_BACK = 4 * CHUNK
REL_FWD = CHUNK - 1
N_EXPERTS = 32
TOP_K = 4
D_FF = D_MODEL
SWIGLU_LIMIT = 7.0
SWIGLU_ALPHA = 1.702
ROUTE_BLOCK = 512
FF_TILE = 256
RMS_EPS = 1e-6
NEG_INF = -1e30
HIGH_HALF = -65536
LOG2_E = 1.4426950408889634
HALF_W = D_MODEL // 2
ROW_TILES = HALF_W // LANES
VMEM_LIMIT = 56 * 1024 * 1024
SC_CORES = 2
SC_SUBCORES = 16
SC_WORKERS = SC_CORES * SC_SUBCORES
SC_ROWS = 32
TRASH_ROWS = 8

A_HEAD_ORDER = (0, 4, 1, 5, 2, 6, 3, 7)


def _cparams(*sem):
    return pltpu.CompilerParams(dimension_semantics=sem, vmem_limit_bytes=VMEM_LIMIT)


def _adaln_kernel(c_ref, w_ref, b_ref, o_ref):
    c = c_ref[...]
    a = c * jax.nn.sigmoid(c)
    o_ref[...] = jnp.dot(a, w_ref[...], preferred_element_type=F32,
                         precision=lax.Precision.HIGHEST) + b_ref[...]


def _adaln(c, w_ada, b_ada):
    n_rows = c.shape[0]
    n_out = w_ada.shape[1]
    tn = 1024
    return pl.pallas_call(
        _adaln_kernel,
        grid=(n_out // tn,),
        in_specs=[pl.BlockSpec((n_rows, D_MODEL), lambda j: (0, 0)),
                  pl.BlockSpec((D_MODEL, tn), lambda j: (0, j)),
                  pl.BlockSpec((1, tn), lambda j: (0, j))],
        out_specs=pl.BlockSpec((n_rows, tn), lambda j: (0, j)),
        out_shape=jax.ShapeDtypeStruct((n_rows, n_out), F32),
        compiler_params=_cparams("arbitrary"),
        name="adaln",
    )(c, w_ada, b_ada.reshape(1, n_out))


def _pack_rows(ref, x):
    m = x.shape[0]
    for j in range(ROW_TILES):
        lo = x[:, LANES * j:LANES * (j + 1)].astype(BF16).astype(F32)
        hi = x[:, HALF_W + LANES * j:HALF_W + LANES * (j + 1)].astype(BF16).astype(F32)
        word = (lax.bitcast_convert_type(hi, I32) & HIGH_HALF) | lax.shift_right_logical(
            lax.bitcast_convert_type(lo, I32), 16)
        ref[pl.ds(j, m, stride=ROW_TILES), :] = word


def _unpack_rows(words):
    lo = [lax.bitcast_convert_type(lax.shift_left(w, 16), F32) for w in words]
    hi = [lax.bitcast_convert_type(w & HIGH_HALF, F32) for w in words]
    return lo, hi


def _modulated_norm(x, g, scale, shift):
    ms = jnp.mean(x * x, axis=-1, keepdims=True)
    return (x * lax.rsqrt(ms + RMS_EPS) * g) * (1.0 + scale) + shift


def _head_norms(groups, seg):
    flat = [p for parts, _ in groups for p in parts]
    m = flat[0].shape[0]
    sq = [p * p for p in flat]
    if len(sq) % 2:
        sq.append(sq[-1])
    pairs = [jnp.concatenate(sq[i:i + 2], axis=1) for i in range(0, len(sq), 2)]
    ms = jnp.dot(jnp.concatenate(pairs, axis=0).astype(BF16), seg, preferred_element_type=F32)
    r = lax.rsqrt(ms + RMS_EPS)
    out, n = [], 0
    for parts, gain in groups:
        normed = []
        for p in parts:
            rn = r[(n // 2) * m:(n // 2 + 1) * m, (n % 2) * LANES:(n % 2 + 1) * LANES]
            normed.append(p * rn * gain)
            n += 1
        out.append(normed)
    return out


def _rope(p, cos, sin_lo, sin_hi):
    return p * cos + pltpu.roll(p, LANES - ROT_DIM // 2, 1) * sin_lo + pltpu.roll(p, ROT_DIM // 2, 1) * sin_hi


def _stage1_kernel(x_ref, sh_ref, sc_ref, g_ref, w_ref, gqa_ref, gka_ref, gqb_ref, gkb_ref, seg_ref,
                   cos_ref, sinlo_ref, sinhi_ref,
                   qa_ref, ka_ref, va_ref, qb_ref, kb_ref, vb_ref, ska_ref, sva_ref, skb_ref, svb_ref,
                   *, bb, tl, pad_tiles, n_tiles, sa_rows, sb_rows):
    i = pl.program_id(1)
    m = bb * tl

    if pad_tiles:
        @pl.when(i < pad_tiles)
        def _():
            ka_ref[...] = jnp.zeros_like(ka_ref)
            va_ref[...] = jnp.zeros_like(va_ref)
            kb_ref[...] = jnp.zeros_like(kb_ref)
            vb_ref[...] = jnp.zeros_like(vb_ref)

    @pl.when(i >= pad_tiles)
    def _():
        h = _modulated_norm(x_ref[...], g_ref[...], sc_ref[...], sh_ref[...])
        hb = h.reshape(m, D_MODEL).astype(BF16)
        qkv = jnp.dot(hb, w_ref[...], preferred_element_type=F32)
        seg = seg_ref[...]

        def cols(lo, width):
            return [qkv[:, lo + LANES * j: lo + LANES * (j + 1)] for j in range(width // LANES)]

        cos = jnp.concatenate([cos_ref[...]] * bb, axis=0)
        sin_lo = jnp.concatenate([sinlo_ref[...]] * bb, axis=0)
        sin_hi = jnp.concatenate([sinhi_ref[...]] * bb, axis=0)
        scale = HEAD_DIM ** -0.5 * LOG2_E

        off_b = A_Q_W + 2 * A_KV_W
        qa, ka, qb, kb = _head_norms(
            [(cols(0, A_Q_W), gqa_ref[...]), (cols(A_Q_W, A_KV_W), gka_ref[...]),
             (cols(off_b, B_W), gqb_ref[...]), (cols(off_b + B_W, B_W), gkb_ref[...])], seg)
        qa = [_rope(p, cos, sin_lo, sin_hi) * scale for p in qa]
        ka = [_rope(p, cos, sin_lo, sin_hi) for p in ka]
        va = cols(A_Q_W + A_KV_W, A_KV_W)
        qb = [p * scale for p in qb]
        vb = cols(off_b + 2 * B_W, B_W)

        def put(ref, parts):
            for j, p in enumerate(parts):
                ref[:, :, LANES * j:LANES * (j + 1)] = p.reshape(bb, tl, LANES).astype(ref.dtype)

        put(qa_ref, qa)
        put(ka_ref, ka)
        put(va_ref, va)
        put(qb_ref, qb)
        put(kb_ref, kb)
        put(vb_ref, vb)

        @pl.when(i == n_tiles - 1)
        def _():
            def put_state(ref, parts, rows):
                for j, p in enumerate(parts):
                    ref[:, :, LANES * j:LANES * (j + 1)] = p.reshape(bb, tl, LANES)[:, tl - rows:, :]

            put_state(ska_ref, ka, sa_rows)
            put_state(sva_ref, va, sa_rows)
            put_state(skb_ref, kb, sb_rows)
            put_state(svb_ref, vb, sb_rows)


def _rope_tables(pos):
    inv_freq = ROPE_THETA ** (-jnp.arange(0, ROT_DIM, 2, dtype=F32) / ROT_DIM)
    ang = pos.astype(F32)[:, None] * inv_freq[None, :]
    half = ROT_DIM // 2
    lane = jnp.arange(LANES) % HEAD_DIM
    ang_l = jnp.tile(ang, (1, LANES // half))
    rot = lane < ROT_DIM
    cos = jnp.where(rot[None, :], jnp.cos(ang_l), 1.0)
    sin_lo = jnp.where((lane < half)[None, :], -jnp.sin(ang_l), 0.0)
    sin_hi = jnp.where(((lane >= half) & rot)[None, :], jnp.sin(ang_l), 0.0)
    return cos, sin_lo, sin_hi


def _stage1(x, shift, scale, g_attn, w_qkv, gains, pos, *, bb, tl, pad):
    bsz, length, _ = x.shape
    n_tiles = length // tl
    pad_tiles = pad // tl
    sa_rows, sb_rows = min(WINDOW_A, length), min(WINDOW_B, length)
    assert bsz % bb == 0 and length % tl == 0 and pad % tl == 0 and sb_rows <= tl
    cos, sin_lo, sin_hi = _rope_tables(pos)
    seg_ids = jnp.arange(2 * LANES) // HEAD_DIM
    seg = ((seg_ids[:, None] == seg_ids[None, :]).astype(F32) / HEAD_DIM).astype(BF16)
    gq_a, gk_a, gq_b, gk_b = [jnp.tile(g, LANES // HEAD_DIM).reshape(1, LANES) for g in gains]

    def xmap(b, i):
        return (b, jnp.maximum(i - pad_tiles, 0), 0)

    def const2(b, i):
        return (0, 0)

    def tabmap(b, i):
        return (jnp.maximum(i - pad_tiles, 0), 0)

    def modmap(b, i):
        return (b, 0, 0)

    def kvmap(b, i):
        return (b, i, 0)

    lp = length + pad
    kern = functools.partial(_stage1_kernel, bb=bb, tl=tl, pad_tiles=pad_tiles,
                             n_tiles=n_tiles + pad_tiles, sa_rows=sa_rows, sb_rows=sb_rows)
    out_shapes = (
        jax.ShapeDtypeStruct((bsz, length, A_Q_W), BF16),
        jax.ShapeDtypeStruct((bsz, lp, A_KV_W), BF16),
        jax.ShapeDtypeStruct((bsz, lp, A_KV_W), BF16),
        jax.ShapeDtypeStruct((bsz, length, B_W), BF16),
        jax.ShapeDtypeStruct((bsz, lp, B_W), BF16),
        jax.ShapeDtypeStruct((bsz, lp, B_W), BF16),
        jax.ShapeDtypeStruct((bsz, sa_rows, A_KV_W), F32),
        jax.ShapeDtypeStruct((bsz, sa_rows, A_KV_W), F32),
        jax.ShapeDtypeStruct((bsz, sb_rows, B_W), F32),
        jax.ShapeDtypeStruct((bsz, sb_rows, B_W), F32),
    )
    out_specs = (
        pl.BlockSpec((bb, tl, A_Q_W), xmap),
        pl.BlockSpec((bb, tl, A_KV_W), kvmap),
        pl.BlockSpec((bb, tl, A_KV_W), kvmap),
        pl.BlockSpec((bb, tl, B_W), xmap),
        pl.BlockSpec((bb, tl, B_W), kvmap),
        pl.BlockSpec((bb, tl, B_W), kvmap),
        pl.BlockSpec((bb, sa_rows, A_KV_W), modmap),
        pl.BlockSpec((bb, sa_rows, A_KV_W), modmap),
        pl.BlockSpec((bb, sb_rows, B_W), modmap),
        pl.BlockSpec((bb, sb_rows, B_W), modmap),
    )
    in_specs = [
        pl.BlockSpec((bb, tl, D_MODEL), xmap),
        pl.BlockSpec((bb, 1, D_MODEL), modmap),
        pl.BlockSpec((bb, 1, D_MODEL), modmap),
        pl.BlockSpec((1, D_MODEL), const2),
        pl.BlockSpec((D_MODEL, QKV_W), const2),
        pl.BlockSpec((1, LANES), const2),
        pl.BlockSpec((1, LANES), const2),
        pl.BlockSpec((1, LANES), const2),
        pl.BlockSpec((1, LANES), const2),
        pl.BlockSpec((2 * LANES, 2 * LANES), const2),
        pl.BlockSpec((tl, LANES), tabmap),
        pl.BlockSpec((tl, LANES), tabmap),
        pl.BlockSpec((tl, LANES), tabmap),
    ]
    return pl.pallas_call(
        kern,
        grid=(bsz // bb, n_tiles + pad_tiles),
        in_specs=in_specs,
        out_specs=out_specs,
        out_shape=out_shapes,
        compiler_params=_cparams("arbitrary", "arbitrary"),
        name="stage1",
    )(x, shift, scale, g_attn.reshape(1, D_MODEL), w_qkv, gq_a, gk_a, gq_b, gk_b, seg, cos, sin_lo, sin_hi)


def _nt_dot(a, b):
    return lax.dot_general(a, b, (((1,), (1,)), ((), ())), preferred_element_type=F32)


def _attn_kernel(sink_ref, qa_ref, ka_ref, va_ref, qb_ref, kb_ref, vb_ref, bias_ref, oa_ref, ob_ref,
                 *, n_c, mask_prefix):
    qi = pl.program_id(1)
    lane = lax.broadcasted_iota(I32, (CHUNK, LANES), 1)
    low = lane < HEAD_DIM
    half_mask = (jnp.where(low, 1.0, 0.0).astype(BF16), jnp.where(low, 0.0, 1.0).astype(BF16))
    n_pairs = B_W // LANES
    rows_b = n_pairs * 2 * CHUNK
    rows_a = A_Q_HEADS * CHUNK
    key_b = lax.broadcasted_iota(I32, (rows_b, BAND_B), 1)
    key_a = lax.broadcasted_iota(I32, (rows_a, BAND_A), 1)
    row_a = lax.broadcasted_iota(I32, (rows_a, 1), 0) // CHUNK

    sink_col = jnp.zeros((rows_a, 1), F32)
    for r in range(A_Q_HEADS):
        sink_col = jnp.where(row_a == r, sink_ref[r], sink_col)

    def chunk(c, masked):
        ci = qi * n_c + c
        row0 = pl.multiple_of(ci * CHUNK, CHUNK)
        rows = slice(c * CHUNK, (c + 1) * CHUNK)
        first_valid = PAD - ci * CHUNK

        s_parts = []
        for j in range(n_pairs):
            cg = slice(LANES * j, LANES * (j + 1))
            q = qb_ref[0, rows, cg]
            qs = jnp.concatenate([q * half_mask[0], q * half_mask[1]], axis=0)
            k = kb_ref[0, pl.ds(row0, BAND_B), cg]
            s_parts.append(_nt_dot(qs, k) + bias_ref[j])
        s = jnp.concatenate(s_parts, axis=0)
        if masked:
            s = jnp.where(key_b >= first_valid, s, NEG_INF)
        mx = jnp.max(s, axis=-1, keepdims=True)
        e = jnp.exp2(s - mx)
        inv = 1.0 / jnp.sum(e, axis=-1, keepdims=True)
        p_b = e.astype(BF16)
        for j in range(n_pairs):
            cg = slice(LANES * j, LANES * (j + 1))
            v = vb_ref[0, pl.ds(row0, BAND_B), cg]
            rj = slice(2 * CHUNK * j, 2 * CHUNK * (j + 1))
            o = jnp.dot(p_b[rj], v, preferred_element_type=F32) * inv[rj]
            ob_ref[0, rows, cg] = jnp.where(low, o[:CHUNK], o[CHUNK:]).astype(ob_ref.dtype)

        row_a0 = pl.multiple_of(row0 + (WINDOW_B - WINDOW_A), CHUNK)
        k = ka_ref[0, pl.ds(row_a0, BAND_A), :]
        v = va_ref[0, pl.ds(row_a0, BAND_A), :]
        q = jnp.concatenate(
            [qa_ref[0, rows, LANES * j:LANES * (j + 1)] * half_mask[p]
             for p in range(A_KV_HEADS) for j in range(A_GROUP)], axis=0)
        s = _nt_dot(q, k)
        if masked:
            s = jnp.where(key_a >= first_valid - (WINDOW_B - WINDOW_A), s, NEG_INF)
        mx = jnp.maximum(jnp.max(s, axis=-1, keepdims=True), sink_col)
        e = jnp.exp2(s - mx)
        inv = 1.0 / (jnp.sum(e, axis=-1, keepdims=True) + jnp.exp2(sink_col - mx))
        o = jnp.dot(e.astype(BF16), v, preferred_element_type=F32) * inv
        for j in range(A_GROUP):
            lo = o[j * CHUNK:(j + 1) * CHUNK]
            hi = o[(A_GROUP + j) * CHUNK:(A_GROUP + j + 1) * CHUNK]
            oa_ref[0, rows, LANES * j:LANES * (j + 1)] = jnp.where(low, lo, hi).astype(oa_ref.dtype)

    steps_with_pad = pl.cdiv(PAD // CHUNK, n_c) if mask_prefix else 0
    if steps_with_pad:
        @pl.when(qi < steps_with_pad)
        def _():
            for c in range(n_c):
                chunk(c, True)

    @pl.when(qi >= steps_with_pad)
    def _():
        for c in range(n_c):
            chunk(c, False)


def _attention(qa, ka, va, qb, kb, vb, sinks, bias, *, n_c, mask_prefix):
    bsz, length, _ = qa.shape
    lp = ka.shape[1]
    qt = n_c * CHUNK
    assert length % qt == 0 and lp == length + PAD

    def qmap(b, i):
        return (b, i, 0)

    def kmap(b, i):
        return (b, 0, 0)

    kern = functools.partial(_attn_kernel, n_c=n_c, mask_prefix=mask_prefix)
    return pl.pallas_call(
        kern,
        grid=(bsz, length // qt),
        in_specs=[
            pl.BlockSpec(memory_space=pltpu.SMEM),
            pl.BlockSpec((1, qt, A_Q_W), qmap),
            pl.BlockSpec((1, lp, A_KV_W), kmap),
            pl.BlockSpec((1, lp, A_KV_W), kmap),
            pl.BlockSpec((1, qt, B_W), qmap),
            pl.BlockSpec((1, lp, B_W), kmap),
            pl.BlockSpec((1, lp, B_W), kmap),
            pl.BlockSpec((B_HEADS // 2, 2 * CHUNK, BAND_B), lambda b, i: (0, 0, 0)),
        ],
        out_specs=(pl.BlockSpec((1, qt, A_Q_W), qmap), pl.BlockSpec((1, qt, B_W), qmap)),
        out_shape=(jax.ShapeDtypeStruct((bsz, length, A_Q_W), BF16),
                   jax.ShapeDtypeStruct((bsz, length, B_W), BF16)),
        compiler_params=_cparams("arbitrary", "arbitrary"),
        name="attention",
    )(sinks, qa, ka, va, qb, kb, vb, bias)


def _merge_kernel(x_ref, oa_ref, ob_ref, sha_ref, sca_ref, gta_ref, shm_ref, scm_ref, ga_ref, gm_ref,
                  wg_ref, wa_ref, wb_ref, wo_ref, wrh_ref, wrl_ref, br_ref,
                  x1_ref, h2_ref, ri_ref, rw_ref, cnt_ref, carry_ref, logit_ref, *, bb, tl, seg_tiles):
    m = bb * tl
    step = pl.program_id(0)

    @pl.when(step == 0)
    def _():
        carry_ref[...] = jnp.zeros_like(carry_ref)
        logit_ref[...] = jnp.zeros_like(logit_ref)

    logits = logit_ref[...]
    lane = lax.broadcasted_iota(I32, (m, LANES), 1)
    lane_f = lane.astype(F32)
    work = logits
    top_v, top_i = [], []
    sel_f = jnp.zeros((m, LANES), F32)
    for _ in range(TOP_K):
        mx = jnp.max(work, axis=-1, keepdims=True)
        idx = jnp.min(jnp.where(work == mx, lane_f, float(LANES)), axis=-1, keepdims=True)
        hit = lane_f == idx
        top_v.append(mx)
        top_i.append(idx)
        sel_f = jnp.where(hit, 1.0, sel_f)
        work = jnp.where(hit, -jnp.inf, work)
    ex = [jnp.exp(v - top_v[0]) for v in top_v]
    inv = 1.0 / (ex[0] + ex[1] + ex[2] + ex[3])

    r_i = lax.broadcasted_iota(I32, (m, m), 0)
    c_i = lax.broadcasted_iota(I32, (m, m), 1)
    before = jnp.where(r_i > c_i, 1.0, 0.0).astype(BF16)
    keep = jnp.where(jnp.maximum(step - 1, 0) % seg_tiles == 0, 0.0, 1.0)
    carry_in = carry_ref[...] * keep
    pos = jnp.dot(before, sel_f.astype(BF16), preferred_element_type=F32) + carry_in
    live = jnp.where(step > 0, 1.0, 0.0)
    carry = carry_in + live * jnp.sum(sel_f, axis=0, keepdims=True)
    carry_ref[...] = carry
    cnt_ref[0] = carry

    ri_f = jnp.zeros((m, LANES), F32)
    rw = jnp.zeros((m, LANES), F32)
    for k in range(TOP_K):
        rank = jnp.sum(jnp.where(lane_f == top_i[k], pos, 0.0), axis=-1, keepdims=True)
        ri_f = jnp.where(lane == k, top_i[k], ri_f)
        ri_f = jnp.where(lane == TOP_K + k, rank, ri_f)
        rw = jnp.where(lane == k, ex[k] * inv, rw)
    ri_ref[...] = ri_f.astype(I32).reshape(bb, tl, LANES)
    rw_ref[...] = rw.reshape(bb, tl, LANES)

    x = x_ref[...]
    h = _modulated_norm(x, ga_ref[...], sca_ref[...], sha_ref[...])
    hb = h.reshape(m, D_MODEL).astype(BF16)
    gates = jnp.dot(hb, wg_ref[...], preferred_element_type=F32)
    ya = jnp.dot(oa_ref[...].reshape(m, A_Q_W), wa_ref[...], preferred_element_type=F32)
    yb = jnp.dot(ob_ref[...].reshape(m, B_W), wb_ref[...], preferred_element_type=F32)
    mix = jax.nn.sigmoid(gates[:, :D_MODEL]) * ya + jax.nn.sigmoid(gates[:, D_MODEL:]) * yb
    z = jnp.dot(mix.astype(BF16), wo_ref[...], preferred_element_type=F32)
    x1 = x + gta_ref[...] * z.reshape(bb, tl, D_MODEL)
    x1_ref[...] = x1

    h2 = _modulated_norm(x1, gm_ref[...], scm_ref[...], shm_ref[...]).reshape(m, D_MODEL)
    _pack_rows(h2_ref, h2)

    h2_hi = h2.astype(BF16)
    h2_lo = (h2 - h2_hi.astype(F32)).astype(BF16)
    wrh = wrh_ref[...]
    logit_ref[...] = (jnp.dot(h2_hi, wrh, preferred_element_type=F32)
                      + jnp.dot(h2_lo, wrh, preferred_element_type=F32)
                      + jnp.dot(h2_hi, wrl_ref[...], preferred_element_type=F32)) + br_ref[...]


def _merge(x, oa, ob, mods, g_attn, g_mlp, wg, wa, wb, wo, wr, br, *, bb, tl, b0, nb):
    _, length, _ = x.shape
    n_tok = nb * length
    m = bb * tl
    assert bb == 1 or tl == length
    sh_a, sc_a, gt_a, sh_m, sc_m = mods
    n_l = length // tl
    n_tiles = (nb // bb) * n_l
    blk0 = b0 // bb
    wr_hi = wr.astype(BF16)
    wr_lo = (wr - wr_hi.astype(F32)).astype(BF16)

    def cur(s):
        return jnp.minimum(s, n_tiles - 1)

    def prev(s):
        return jnp.maximum(s - 1, 0)

    def xmap(s):
        return (blk0 + cur(s) // n_l, cur(s) % n_l, 0)

    def modmap(s):
        return (blk0 + cur(s) // n_l, 0, 0)

    def omap(s):
        return (cur(s) // n_l, cur(s) % n_l, 0)

    def const2(s):
        return (0, 0)

    def flatmap(s):
        return (cur(s), 0)

    def routemap(s):
        return (prev(s) // n_l, prev(s) % n_l, 0)

    modspec = pl.BlockSpec((bb, 1, D_MODEL), modmap)
    kern = functools.partial(_merge_kernel, bb=bb, tl=tl, seg_tiles=n_tiles)
    return pl.pallas_call(
        kern,
        grid=(n_tiles + 1,),
        in_specs=[
            pl.BlockSpec((bb, tl, D_MODEL), xmap),
            pl.BlockSpec((bb, tl, A_Q_W), xmap),
            pl.BlockSpec((bb, tl, B_W), xmap),
            modspec, modspec, modspec, modspec, modspec,
            pl.BlockSpec((1, D_MODEL), const2),
            pl.BlockSpec((1, D_MODEL), const2),
            pl.BlockSpec((D_MODEL, 2 * D_MODEL), const2),
            pl.BlockSpec((A_Q_W, D_MODEL), const2),
            pl.BlockSpec((B_W, D_MODEL), const2),
            pl.BlockSpec((D_MODEL, D_MODEL), const2),
            pl.BlockSpec((D_MODEL, LANES), const2),
            pl.BlockSpec((D_MODEL, LANES), const2),
            pl.BlockSpec((1, LANES), const2),
        ],
        out_specs=(
            pl.BlockSpec((bb, tl, D_MODEL), omap),
            pl.BlockSpec((m * ROW_TILES, LANES), flatmap),
            pl.BlockSpec((bb, tl, LANES), routemap),
            pl.BlockSpec((bb, tl, LANES), routemap),
            pl.BlockSpec((1, 1, LANES), lambda s: (0, 0, 0)),
        ),
        out_shape=(
            jax.ShapeDtypeStruct((nb, length, D_MODEL), F32),
            jax.ShapeDtypeStruct((n_tok * ROW_TILES, LANES), I32),
            jax.ShapeDtypeStruct((nb, length, LANES), I32),
            jax.ShapeDtypeStruct((nb, length, LANES), F32),
            jax.ShapeDtypeStruct((1, 1, LANES), F32),
        ),
        scratch_shapes=[pltpu.VMEM((1, LANES), F32), pltpu.VMEM((m, LANES), F32)],
        compiler_params=_cparams("arbitrary"),
        name="merge",
    )(x, oa, ob, sh_a, sc_a, gt_a, sh_m, sc_m, g_attn.reshape(1, D_MODEL), g_mlp.reshape(1, D_MODEL),
      wg, wa, wb, wo, wr_hi, wr_lo, br)


def _sc_mesh():
    return plsc.VectorSubcoreMesh(core_axis_name="c", subcore_axis_name="s",
                                  num_cores=SC_CORES, num_subcores=SC_SUBCORES)


def _sc_worker():
    return lax.axis_index("c") * SC_SUBCORES + lax.axis_index("s")


def _sc_params():
    return pltpu.CompilerParams(use_tc_tiling_on_sc=True)


def _part_chunks(parts):
    assert all(p % (2 * SC_WORKERS * SC_ROWS) == 0 for p in parts)
    chunks = [p // (SC_WORKERS * SC_ROWS) for p in parts]
    starts = [sum(chunks[:i]) for i in range(len(chunks))]
    return chunks, starts


def _dispatch(h2_parts, idx_flat, pad_flat, n_rows):
    n_parts = len(h2_parts)
    chunks, starts = _part_chunks([h.shape[0] for h in h2_parts])
    n_idx = sum(chunks) * TOP_K * SC_ROWS
    n_pad = pad_flat.shape[0] // SC_WORKERS
    pad_chunks = n_pad // SC_ROWS
    assert n_pad % SC_ROWS == 0
    zeros = jnp.zeros((SC_ROWS, ROW_TILES, LANES), I32)

    def body(*refs):
        h2_hbms = refs[:n_parts]
        idx_hbm, pad_hbm, zeros_hbm, xs_hbm, idx_v, pad_v, buf0, buf1, ls0, ls1, ss0, ss1 = refs[n_parts:]
        wid = _sc_worker()
        bufs, lsems, ssems = (buf0, buf1), (ls0, ls1), (ss0, ss1)
        pltpu.sync_copy(idx_hbm.at[pl.ds(pl.multiple_of(wid * n_idx, SC_ROWS), n_idx)], idx_v)
        pltpu.sync_copy(pad_hbm.at[pl.ds(pl.multiple_of(wid * n_pad, SC_ROWS), n_pad)], pad_v)

        pltpu.sync_copy(zeros_hbm, buf0)
        for c in range(pad_chunks):
            pltpu.async_copy(buf0, xs_hbm.at[pad_v.at[pl.ds(c * SC_ROWS, SC_ROWS)]], ss0)
        for c in range(pad_chunks):
            pltpu.make_async_copy(buf0, xs_hbm.at[pad_v.at[pl.ds(c * SC_ROWS, SC_ROWS)]], ss0).wait()

        def run_part(h2_hbm, n_chunks, c0):
            base = wid * (n_chunks * SC_ROWS)

            def load(g, b):
                src = h2_hbm.at[pl.ds(pl.multiple_of(base + g * SC_ROWS, SC_ROWS), SC_ROWS)]
                return pltpu.make_async_copy(src, bufs[b], lsems[b])

            def scatter(g, k, b):
                off = pl.multiple_of(((c0 + g) * TOP_K + k) * SC_ROWS, SC_ROWS)
                return pltpu.make_async_copy(bufs[b], xs_hbm.at[idx_v.at[pl.ds(off, SC_ROWS)]], ssems[b])

            load(0, 0).start()

            @pl.loop(0, n_chunks, step=2)
            def _(g0):
                for b in range(2):
                    g = g0 + b
                    load(g, b).wait()

                    @pl.when(g >= 1)
                    def _():
                        for k in range(TOP_K):
                            scatter(g - 1, k, 1 - b).wait()

                    @pl.when(g + 1 < n_chunks)
                    def _():
                        load(g + 1, 1 - b).start()

                    for k in range(TOP_K):
                        scatter(g, k, b).start()

            for k in range(TOP_K):
                scatter(n_chunks - 1, k, (n_chunks - 1) % 2).wait()

        for h2_hbm, n_chunks, c0 in zip(h2_hbms, chunks, starts):
            run_part(h2_hbm, n_chunks, c0)

    call = pl.kernel(
        body,
        out_type=jax.ShapeDtypeStruct((n_rows + TRASH_ROWS, ROW_TILES, LANES), I32),
        mesh=_sc_mesh(),
        scratch_types=[pltpu.VMEM((n_idx,), I32), pltpu.VMEM((n_pad,), I32),
                       pltpu.VMEM((SC_ROWS, ROW_TILES, LANES), I32), pltpu.VMEM((SC_ROWS, ROW_TILES, LANES), I32),
                       pltpu.SemaphoreType.DMA, pltpu.SemaphoreType.DMA,
                       pltpu.SemaphoreType.DMA, pltpu.SemaphoreType.DMA],
        compiler_params=_sc_params(),
        name="dispatch",
    )
    return call(*h2_parts, idx_flat, pad_flat, zeros)


def _gather_back(y_rows, idx_flat, parts):
    n_parts = len(parts)
    chunks, starts = _part_chunks(parts)
    n_idx = sum(chunks) * TOP_K * SC_ROWS

    def body(*refs):
        y_hbm, idx_hbm = refs[:2]
        yg_hbms = refs[2:2 + n_parts]
        idx_v, buf0, buf1, gs0, gs1, ws0, ws1 = refs[2 + n_parts:]
        wid = _sc_worker()
        bufs, gsems, wsems = (buf0, buf1), (gs0, gs1), (ws0, ws1)
        pltpu.sync_copy(idx_hbm.at[pl.ds(pl.multiple_of(wid * n_idx, SC_ROWS), n_idx)], idx_v)

        def run_part(yg_hbm, n_chunks, c0):
            base = wid * (n_chunks * SC_ROWS)
            n_items = n_chunks * TOP_K

            def gather(i, b):
                off = pl.multiple_of((c0 * TOP_K + i) * SC_ROWS, SC_ROWS)
                return pltpu.make_async_copy(y_hbm.at[idx_v.at[pl.ds(off, SC_ROWS)]], bufs[b], gsems[b])

            def write(i, b):
                g, k = i // TOP_K, i % TOP_K
                dst = yg_hbm.at[k, pl.ds(pl.multiple_of(base + g * SC_ROWS, SC_ROWS), SC_ROWS)]
                return pltpu.make_async_copy(bufs[b], dst, wsems[b])

            gather(0, 0).start()

            @pl.loop(0, n_items, step=2)
            def _(i0):
                for b in range(2):
                    i = i0 + b
                    gather(i, b).wait()

                    @pl.when(i >= 1)
                    def _():
                        write(i - 1, 1 - b).wait()

                    @pl.when(i + 1 < n_items)
                    def _():
                        gather(i + 1, 1 - b).start()

                    write(i, b).start()

            write(n_items - 1, (n_items - 1) % 2).wait()

        for yg_hbm, n_chunks, c0 in zip(yg_hbms, chunks, starts):
            run_part(yg_hbm, n_chunks, c0)

    call = pl.kernel(
        body,
        out_type=tuple(jax.ShapeDtypeStruct((TOP_K, p, ROW_TILES, LANES), I32) for p in parts),
        mesh=_sc_mesh(),
        scratch_types=[pltpu.VMEM((n_idx,), I32),
                       pltpu.VMEM((SC_ROWS, ROW_TILES, LANES), I32), pltpu.VMEM((SC_ROWS, ROW_TILES, LANES), I32),
                       pltpu.SemaphoreType.DMA, pltpu.SemaphoreType.DMA,
                       pltpu.SemaphoreType.DMA, pltpu.SemaphoreType.DMA],
        compiler_params=_sc_params(),
        name="gather_back",
    )
    return call(y_rows, idx_flat)


def _expert_kernel(blk_ref, used_ref, next_ref, slot_ref, xs_ref, wgu_hbm, bgu_ref, wd_hbm, bd_ref, y_ref,
                   wgu_buf, wd_buf, wgu_bf, wd_bf, sem_gu, sem_d):
    r = pl.program_id(0)
    e = blk_ref[r]
    prev = blk_ref[jnp.maximum(r - 1, 0)]
    active = r < used_ref[0]
    slot = slot_ref[r]

    def fetch(expert, s):
        return (pltpu.make_async_copy(wgu_hbm.at[expert], wgu_buf.at[s], sem_gu.at[s]),
                pltpu.make_async_copy(wd_hbm.at[expert], wd_buf.at[s], sem_d.at[s]))

    @pl.when(jnp.logical_and(active, r == 0))
    def _():
        for copy in fetch(e, slot):
            copy.start()

    @pl.when(jnp.logical_and(active, jnp.logical_or(r == 0, e != prev)))
    def _():
        for copy in fetch(e, slot):
            copy.wait()
        wgu_bf[...] = wgu_buf[slot].astype(BF16)
        wd_bf[...] = wd_buf[slot].astype(BF16)
        nxt = next_ref[r]

        @pl.when(nxt >= 0)
        def _():
            for copy in fetch(nxt, 1 - slot):
                copy.start()

    @pl.when(active)
    def _():
        lo, hi = _unpack_rows([xs_ref[pl.ds(j, ROUTE_BLOCK, stride=ROW_TILES), :] for j in range(ROW_TILES)])
        x = jnp.concatenate(lo + hi, axis=1).astype(BF16)
        acts = []
        for j in range(D_FF // FF_TILE):
            gc = slice(FF_TILE * j, FF_TILE * (j + 1))
            uc = slice(D_FF + FF_TILE * j, D_FF + FF_TILE * (j + 1))
            g = jnp.dot(x, wgu_bf[:, gc], preferred_element_type=F32) + bgu_ref[0, :, gc]
            u = jnp.dot(x, wgu_bf[:, uc], preferred_element_type=F32) + bgu_ref[0, :, uc]
            gate = jnp.minimum(g, SWIGLU_LIMIT)
            up = jnp.clip(u, -SWIGLU_LIMIT, SWIGLU_LIMIT)
            glu = gate * jax.nn.sigmoid(gate * SWIGLU_ALPHA)
            acts.append(((up + 1.0) * glu).astype(BF16))
        act = jnp.concatenate(acts, axis=1)
        y = jnp.dot(act, wd_bf[...], preferred_element_type=F32) + bd_ref[0]
        _pack_rows(y_ref, y)


def _experts(xs_rows, blk_expert, n_used, blk_next, blk_slot, w_gate_up, b_gate_up, w_down, b_down):
    n_blocks = blk_expert.shape[0]
    rb = ROUTE_BLOCK * ROW_TILES

    def rowmap(r, blk, used, nxt, slot):
        return (jnp.minimum(r, used[0] - 1), 0)

    def bmap(r, blk, used, nxt, slot):
        return (blk[jnp.minimum(r, used[0] - 1)], 0, 0)

    return pl.pallas_call(
        _expert_kernel,
        grid_spec=pltpu.PrefetchScalarGridSpec(
            num_scalar_prefetch=4,
            grid=(n_blocks,),
            in_specs=[
                pl.BlockSpec((rb, LANES), rowmap),
                pl.BlockSpec(memory_space=pl.ANY),
                pl.BlockSpec((1, 1, 2 * D_FF), bmap),
                pl.BlockSpec(memory_space=pl.ANY),
                pl.BlockSpec((1, 1, D_MODEL), bmap),
            ],
            out_specs=pl.BlockSpec((rb, LANES), rowmap),
            scratch_shapes=[pltpu.VMEM((2, D_MODEL, 2 * D_FF), F32), pltpu.VMEM((2, D_FF, D_MODEL), F32),
                            pltpu.VMEM((D_MODEL, 2 * D_FF), BF16), pltpu.VMEM((D_FF, D_MODEL), BF16),
                            pltpu.SemaphoreType.DMA((2,)), pltpu.SemaphoreType.DMA((2,))],
        ),
        out_shape=jax.ShapeDtypeStruct(xs_rows.shape, I32),
        compiler_params=_cparams("arbitrary"),
        name="experts",
    )(blk_expert, n_used, blk_next, blk_slot, xs_rows, w_gate_up, b_gate_up.reshape(N_EXPERTS, 1, 2 * D_FF),
      w_down, b_down.reshape(N_EXPERTS, 1, D_MODEL))


def _combine_kernel(x1_ref, gt_ref, rw_ref, yg_ref, *rest, bb, tl):
    o_ref = rest[-1]
    m = bb * tl
    rw = rw_ref[...].reshape(m, LANES)
    w = [rw[:, k:k + 1] for k in range(TOP_K)]
    gt = gt_ref[...]
    for j in range(ROW_TILES):
        acc_lo = jnp.zeros((m, LANES), F32)
        acc_hi = jnp.zeros((m, LANES), F32)
        for k in range(TOP_K):
            lo, hi = _unpack_rows([yg_ref[k, pl.ds(j, m, stride=ROW_TILES), :]])
            acc_lo = acc_lo + w[k] * lo[0]
            acc_hi = acc_hi + w[k] * hi[0]
        for acc, c0 in ((acc_lo, LANES * j), (acc_hi, HALF_W + LANES * j)):
            cg = slice(c0, c0 + LANES)
            o_ref[:, :, cg] = x1_ref[:, :, cg] + gt[:, :, cg] * acc.reshape(bb, tl, LANES)


def _combine(x1, gt_m, rw, yg, y_prev, *, bb, tl, b0, bsz):
    nb, length, _ = x1.shape
    m = bb * tl
    assert bb == 1 or tl == length
    n_l = length // tl
    blk0 = b0 // bb

    def smap(b, i):
        return (b, i, 0)

    def xmap(b, i):
        return (blk0 + b, i, 0)

    in_specs = [
        pl.BlockSpec((bb, tl, D_MODEL), smap),
        pl.BlockSpec((bb, 1, D_MODEL), lambda b, i: (blk0 + b, 0, 0)),
        pl.BlockSpec((bb, tl, LANES), smap),
        pl.BlockSpec((TOP_K, m * ROW_TILES, LANES), lambda b, i: (0, b * n_l + i, 0)),
    ]
    args = [x1, gt_m, rw, yg]
    aliases = {}
    if y_prev is not None:
        in_specs.append(pl.BlockSpec(memory_space=pl.ANY))
        args.append(y_prev)
        aliases = {len(args) - 1: 0}
    kern = functools.partial(_combine_kernel, bb=bb, tl=tl)
    return pl.pallas_call(
        kern,
        grid=(nb // bb, n_l),
        in_specs=in_specs,
        out_specs=pl.BlockSpec((bb, tl, D_MODEL), xmap),
        out_shape=jax.ShapeDtypeStruct((bsz, length, D_MODEL), F32),
        input_output_aliases=aliases,
        compiler_params=_cparams("arbitrary", "arbitrary"),
        name="combine",
    )(*args)


def _route_plan(routes):
    expert_ids = jnp.arange(N_EXPERTS, dtype=I32)
    part_counts = [c.reshape(-1)[:N_EXPERTS].astype(I32) for _, c in routes]
    counts = sum(part_counts)
    padded = (counts + ROUTE_BLOCK - 1) // ROUTE_BLOCK * ROUTE_BLOCK
    pends = jnp.cumsum(padded)
    pstarts = pends - padded
    n_assign = sum(ri.shape[0] * ri.shape[1] for ri, _ in routes) * TOP_K
    n_blocks = n_assign // ROUTE_BLOCK + N_EXPERTS
    n_rows = n_blocks * ROUTE_BLOCK
    lists = []
    first = jnp.zeros_like(counts)
    for (ri, _), cnt in zip(routes, part_counts):
        n_tok = ri.shape[0] * ri.shape[1]
        idx = ri[..., :TOP_K].reshape(n_tok, TOP_K)
        rank = ri[..., TOP_K:2 * TOP_K].reshape(n_tok, TOP_K)
        dest = jnp.sum(jnp.where(idx[..., None] == expert_ids, pstarts + first, 0), axis=-1) + rank
        per_w = n_tok // SC_WORKERS
        lists.append(dest.astype(I32).reshape(SC_WORKERS, per_w // SC_ROWS, SC_ROWS, TOP_K).transpose(0, 1, 3, 2))
        first = first + cnt
    idx_flat = jnp.concatenate(lists, axis=1).reshape(-1)
    blk_start = jnp.arange(n_blocks, dtype=I32) * ROUTE_BLOCK
    blk_expert = jnp.minimum(jnp.sum((pends[None, :] <= blk_start[:, None]).astype(I32), axis=1), N_EXPERTS - 1)
    n_used = (pends[-1] // ROUTE_BLOCK).reshape(1)
    nonempty = padded > 0
    later = nonempty[None, :] & (expert_ids[None, :] > expert_ids[:, None])
    next_e = jnp.min(jnp.where(later, expert_ids[None, :], N_EXPERTS), axis=1)
    next_e = jnp.where(next_e == N_EXPERTS, -1, next_e)
    slot_e = (jnp.cumsum(nonempty.astype(I32)) - 1) % 2
    own = blk_expert[:, None] == expert_ids[None, :]
    blk_next = jnp.sum(jnp.where(own, next_e[None, :], 0), axis=1)
    blk_slot = jnp.sum(jnp.where(own, slot_e[None, :], 0), axis=1)
    r = jnp.arange(ROUTE_BLOCK, dtype=I32)[None, :]
    pad_flat = jnp.where(r < (padded - counts)[:, None], (pstarts + counts)[:, None] + r, n_rows)
    blocks = (blk_expert.astype(I32), n_used.astype(I32), blk_next.astype(I32), blk_slot.astype(I32))
    return idx_flat, pad_flat.astype(I32).reshape(-1), blocks, n_rows


def _front(x, mods, pos, prefix, weights, *, bb, tl, n_c):
    g_attn, w_qkv, gains, sinks, bias = weights[0], weights[2], weights[3], weights[4], weights[5]
    sh_a, sc_a = mods[0], mods[1]
    pad = 0 if prefix is not None else PAD
    qa, ka, va, qb, kb, vb, ska, sva, skb, svb = _stage1(
        x, sh_a, sc_a, g_attn, w_qkv, gains, pos, bb=bb, tl=tl, pad=pad)
    if prefix is not None:
        ka, va, kb, vb = [jnp.concatenate([p, t], axis=1) for p, t in zip(prefix, (ka, va, kb, vb))]
    oa, ob = _attention(qa, ka, va, qb, kb, vb, sinks, bias, n_c=n_c, mask_prefix=prefix is None)
    return oa, ob, (ska, sva, skb, svb)


def _moe_segment(parts, weights):
    (g_attn, g_mlp, _, _, _, _, wg, wa, wb, wo, wr, br, w_gate_up, b_gate_up, w_down, b_down) = weights
    merged = []
    for p in parts:
        sh_a, sc_a, gt_a, sh_m, sc_m, _ = p["mods"]
        merged.append(_merge(p["x"], p["oa"], p["ob"], (sh_a, sc_a, gt_a, sh_m, sc_m), g_attn, g_mlp,
                             wg, wa, wb, wo, wr, br, bb=p["cfg"]["bb"], tl=p["cfg"]["tl"], b0=p["b0"], nb=p["nb"]))
    toks = [m[0].shape[0] * m[0].shape[1] for m in merged]
    idx_flat, pad_flat, blocks, n_rows = _route_plan([(m[2], m[4]) for m in merged])
    xs = _dispatch([m[1].reshape(t, ROW_TILES, LANES) for m, t in zip(merged, toks)], idx_flat, pad_flat, n_rows)
    y_rows = _experts(xs.reshape(-1, LANES), *blocks, w_gate_up, b_gate_up, w_down, b_down)
    ygs = _gather_back(y_rows.reshape(-1, ROW_TILES, LANES), idx_flat, toks)
    outs = []
    for p, m, yg, t in zip(parts, merged, ygs, toks):
        outs.append(_combine(m[0], p["mods"][5], m[3], yg.reshape(TOP_K, t * ROW_TILES, LANES), p["y"],
                             bb=p["cfg"]["bb_combine"], tl=p["cfg"]["tl_combine"], b0=p["b0"],
                             bsz=p["x"].shape[0]))
    return outs


def _rel_bias_table(rel_table):
    n_rel = rel_table.shape[1]
    ext = jnp.concatenate(
        [rel_table, jnp.broadcast_to(rel_table[:, -1:], (B_HEADS, CHUNK + BAND_B - n_rel))], axis=1)
    rows = jnp.stack([ext[:, i:i + BAND_B] for i in range(CHUNK)], axis=1)[..., ::-1]
    return rows.astype(F32).reshape(B_HEADS // 2, 2 * CHUNK, BAND_B)


def _prep_weights(g_attn, g_mlp, w_in, q_norm_a, k_norm_a, q_norm_b, k_norm_b, sinks_a, rel_table_b,
                  w_branch_a, w_branch_b, w_out, w_router, b_router, w_gate_up, b_gate_up, w_down, b_down):
    order = jnp.asarray(A_HEAD_ORDER)
    w_qa = w_in[:, :A_Q_W].reshape(D_MODEL, A_Q_HEADS, HEAD_DIM)[:, order].reshape(D_MODEL, A_Q_W)
    w_qkv = jnp.concatenate([w_qa, w_in[:, A_Q_W:QKV_W]], axis=1).astype(BF16)
    wg = w_in[:, QKV_W:].astype(BF16)
    wa = w_branch_a.reshape(A_Q_HEADS, HEAD_DIM, D_MODEL)[order].reshape(A_Q_W, D_MODEL).astype(BF16)
    wb = w_branch_b.astype(BF16)
    wo = w_out.astype(BF16)
    wr = jnp.zeros((D_MODEL, LANES), F32).at[:, :N_EXPERTS].set(w_router)
    br = jnp.full((1, LANES), NEG_INF, F32).at[0, :N_EXPERTS].set(b_router)
    bias = _rel_bias_table(rel_table_b) * LOG2_E
    gains = (q_norm_a, k_norm_a, q_norm_b, k_norm_b)
    return (g_attn, g_mlp, w_qkv, gains, sinks_a.astype(F32) * LOG2_E, bias, wg, wa, wb, wo, wr, br,
            w_gate_up, b_gate_up, w_down, b_down)


def _forward(x_prompt, x_sample, c_prompt, c_sample, cache_a_k, cache_a_v, cache_b_k, cache_b_v, params,
             past_len, cfg_prompt, cfg_sample):
    (g_attn, g_mlp, w_ada, b_ada, w_in, q_norm_a, k_norm_a, q_norm_b, k_norm_b, sinks_a, rel_table_b,
     w_branch_a, w_branch_b, w_out, w_router, b_router, w_gate_up, b_gate_up, w_down, b_down) = params
    depth = g_attn.shape[0]
    bp, lp_, _ = x_prompt.shape
    bs, ls, _ = x_sample.shape
    pos_p = jnp.arange(lp_, dtype=I32)
    pos_s = past_len + jnp.arange(ls, dtype=I32)
    yp, ys = x_prompt, x_sample
    st_p, st_s = [], []
    for l in range(depth):
        weights = _prep_weights(g_attn[l], g_mlp[l], w_in[l], q_norm_a[l], k_norm_a[l], q_norm_b[l],
                                k_norm_b[l], sinks_a[l], rel_table_b[l], w_branch_a[l], w_branch_b[l],
                                w_out[l], w_router[l], b_router[l], w_gate_up[l], b_gate_up[l],
                                w_down[l], b_down[l])
        mod = _adaln(jnp.concatenate([c_prompt, c_sample], axis=0), w_ada[l], b_ada[l])
        mod = mod.reshape(bp + bs, 6, 1, D_MODEL)
        mods_p = [mod[:bp, k] for k in range(6)]
        mods_s = [mod[bp:, k] for k in range(6)]
        wa_c = cache_a_k.shape[2]
        wb_c = cache_b_k.shape[2]
        assert wb_c == WINDOW_B and wa_c == WINDOW_A
        zeros_a = jnp.zeros((bs, PAD - wa_c, A_KV_W), BF16)
        prefix = (
            jnp.concatenate([zeros_a, cache_a_k[l].reshape(bs, wa_c, A_KV_W).astype(BF16)], axis=1),
            jnp.concatenate([zeros_a, cache_a_v[l].reshape(bs, wa_c, A_KV_W).astype(BF16)], axis=1),
            cache_b_k[l].reshape(bs, wb_c, B_W).astype(BF16),
            cache_b_v[l].reshape(bs, wb_c, B_W).astype(BF16),
        )
        front = {k: cfg_prompt[k] for k in ("bb", "tl", "n_c")}
        oa_p, ob_p, sp = _front(yp, mods_p, pos_p, None, weights, **front)
        front = {k: cfg_sample[k] for k in ("bb", "tl", "n_c")}
        oa_s, ob_s, ss = _front(ys, mods_s, pos_s, prefix, weights, **front)
        n_seg = cfg_prompt["n_seg"]
        nb = bp // n_seg
        xp, xs_in = yp, ys
        yp = None
        for s in range(n_seg):
            parts = [dict(x=xp, oa=oa_p, ob=ob_p, mods=mods_p, b0=s * nb, nb=nb, cfg=cfg_prompt, y=yp)]
            if s == n_seg - 1:
                parts.append(dict(x=xs_in, oa=oa_s, ob=ob_s, mods=mods_s, b0=0, nb=bs, cfg=cfg_sample, y=None))
            outs = _moe_segment(parts, weights)
            yp = outs[0]
            if s == n_seg - 1:
                ys = outs[1]
        st_p.append(sp)
        st_s.append(ss)

    def stack(states, k, heads):
        return jnp.stack([s[k].reshape(s[k].shape[0], s[k].shape[1], heads, HEAD_DIM) for s in states])

    return (yp, ys,
            stack(st_p, 0, A_KV_HEADS), stack(st_p, 1, A_KV_HEADS), stack(st_p, 2, B_HEADS), stack(st_p, 3, B_HEADS),
            stack(st_s, 0, A_KV_HEADS), stack(st_s, 1, A_KV_HEADS), stack(st_s, 2, B_HEADS), stack(st_s, 3, B_HEADS))


PAST_LEN = 2048
CFG_PROMPT = dict(bb=1, tl=512, n_c=8, tl_combine=512, bb_combine=1, n_seg=2)
CFG_SAMPLE = dict(bb=8, tl=64, n_c=1, tl_combine=64, bb_combine=8)


def kernel(x_prompt, x_sample, c_prompt, c_sample, cache_a_k, cache_a_v, cache_b_k, cache_b_v, g_attn, g_mlp,
           w_ada, b_ada, w_in, q_norm_a, k_norm_a, q_norm_b, k_norm_b, sinks_a, rel_table_b, w_branch_a,
           w_branch_b, w_out, w_router, b_router, w_gate_up, b_gate_up, w_down, b_down):
    params = (g_attn, g_mlp, w_ada, b_ada, w_in, q_norm_a, k_norm_a, q_norm_b, k_norm_b, sinks_a, rel_table_b,
              w_branch_a, w_branch_b, w_out, w_router, b_router, w_gate_up, b_gate_up, w_down, b_down)
    return _forward(x_prompt, x_sample, c_prompt, c_sample, cache_a_k, cache_a_v, cache_b_k, cache_b_v,
                    params, PAST_LEN, CFG_PROMPT, CFG_SAMPLE)
```

```python
import functools

import jax
import jax.numpy as jnp
from jax import lax
from jax.experimental import pallas as pl
from jax.experimental.pallas import tpu as pltpu
from jax.experimental.pallas import tpu_sc as plsc

F32 = jnp.float32
BF16 = jnp.bfloat16
I32 = jnp.int32

D_MODEL = 1024
CHUNK = 64
HEAD_DIM = 64
LANES = 128
SUBLANES = 8
A_Q_HEADS = 8
A_KV_HEADS = 2
A_GROUP = A_Q_HEADS // A_KV_HEADS
B_HEADS = 8
A_Q_W = A_Q_HEADS * HEAD_DIM
A_KV_W = A_KV_HEADS * HEAD_DIM
B_W = B_HEADS * HEAD_DIM
QKV_W = A_Q_W + 2 * A_KV_W + 3 * B_W
WINDOW_A = 128
WINDOW_B = 512
BAND_A = WINDOW_A + CHUNK
BAND_B = WINDOW_B + CHUNK
PAD = WINDOW_B
ROT_DIM = HEAD_DIM // 4
ROPE_THETA = 500000.0
REL_BACK = 4 * CHUNK
REL_FWD = CHUNK - 1
N_EXPERTS = 32
TOP_K = 4
D_FF = D_MODEL
SWIGLU_LIMIT = 7.0
SWIGLU_ALPHA = 1.702
ROUTE_BLOCK = 512
FF_TILE = 256
RMS_EPS = 1e-6
NEG_INF = -1e30
HIGH_HALF = -65536
LOG2_E = 1.4426950408889634
HALF_W = D_MODEL // 2
ROW_TILES = HALF_W // LANES
VMEM_LIMIT = 56 * 1024 * 1024
SC_CORES = 2
SC_SUBCORES = 16
SC_WORKERS = SC_CORES * SC_SUBCORES
SC_ROWS = 32
TRASH_ROWS = 8

A_HEAD_ORDER = (0, 4, 1, 5, 2, 6, 3, 7)


def _cparams(*sem):
    return pltpu.CompilerParams(dimension_semantics=sem, vmem_limit_bytes=VMEM_LIMIT)


def _adaln_kernel(c_ref, w_ref, b_ref, o_ref):
    c = c_ref[...]
    a = c * jax.nn.sigmoid(c)
    o_ref[...] = jnp.dot(a, w_ref[...], preferred_element_type=F32,
                         precision=lax.Precision.HIGHEST) + b_ref[...]


def _adaln(c, w_ada, b_ada):
    n_rows = c.shape[0]
    n_out = w_ada.shape[1]
    tn = 1024
    return pl.pallas_call(
        _adaln_kernel,
        grid=(n_out // tn,),
        in_specs=[pl.BlockSpec((n_rows, D_MODEL), lambda j: (0, 0)),
                  pl.BlockSpec((D_MODEL, tn), lambda j: (0, j)),
                  pl.BlockSpec((1, tn), lambda j: (0, j))],
        out_specs=pl.BlockSpec((n_rows, tn), lambda j: (0, j)),
        out_shape=jax.ShapeDtypeStruct((n_rows, n_out), F32),
        compiler_params=_cparams("arbitrary"),
        name="adaln",
    )(c, w_ada, b_ada.reshape(1, n_out))


def _pack_rows(ref, x):
    m = x.shape[0]
    for j in range(ROW_TILES):
        lo = x[:, LANES * j:LANES * (j + 1)].astype(BF16).astype(F32)
        hi = x[:, HALF_W + LANES * j:HALF_W + LANES * (j + 1)].astype(BF16).astype(F32)
        word = (lax.bitcast_convert_type(hi, I32) & HIGH_HALF) | lax.shift_right_logical(
            lax.bitcast_convert_type(lo, I32), 16)
        ref[pl.ds(j, m, stride=ROW_TILES), :] = word


def _unpack_rows(words):
    lo = [lax.bitcast_convert_type(lax.shift_left(w, 16), F32) for w in words]
    hi = [lax.bitcast_convert_type(w & HIGH_HALF, F32) for w in words]
    return lo, hi


def _modulated_norm(x, g, scale, shift):
    ms = jnp.mean(x * x, axis=-1, keepdims=True)
    return (x * lax.rsqrt(ms + RMS_EPS) * g) * (1.0 + scale) + shift


def _head_norms(groups, seg):
    flat = [p for parts, _ in groups for p in parts]
    m = flat[0].shape[0]
    sq = [p * p for p in flat]
    if len(sq) % 2:
        sq.append(sq[-1])
    pairs = [jnp.concatenate(sq[i:i + 2], axis=1) for i in range(0, len(sq), 2)]
    ms = jnp.dot(jnp.concatenate(pairs, axis=0).astype(BF16), seg, preferred_element_type=F32)
    r = lax.rsqrt(ms + RMS_EPS)
    out, n = [], 0
    for parts, gain in groups:
        normed = []
        for p in parts:
            rn = r[(n // 2) * m:(n // 2 + 1) * m, (n % 2) * LANES:(n % 2 + 1) * LANES]
            normed.append(p * rn * gain)
            n += 1
        out.append(normed)
    return out


def _rope(p, cos, sin_lo, sin_hi):
    return p * cos + pltpu.roll(p, LANES - ROT_DIM // 2, 1) * sin_lo + pltpu.roll(p, ROT_DIM // 2, 1) * sin_hi


def _stage1_kernel(x_ref, sh_ref, sc_ref, g_ref, w_ref, gqa_ref, gka_ref, gqb_ref, gkb_ref, seg_ref,
                   cos_ref, sinlo_ref, sinhi_ref,
                   qa_ref, ka_ref, va_ref, qb_ref, kb_ref, vb_ref, ska_ref, sva_ref, skb_ref, svb_ref,
                   *, bb, tl, pad_tiles, n_tiles, sa_rows, sb_rows):
    i = pl.program_id(1)
    m = bb * tl

    if pad_tiles:
        @pl.when(i < pad_tiles)
        def _():
            ka_ref[...] = jnp.zeros_like(ka_ref)
            va_ref[...] = jnp.zeros_like(va_ref)
            kb_ref[...] = jnp.zeros_like(kb_ref)
            vb_ref[...] = jnp.zeros_like(vb_ref)

    @pl.when(i >= pad_tiles)
    def _():
        h = _modulated_norm(x_ref[...], g_ref[...], sc_ref[...], sh_ref[...])
        hb = h.reshape(m, D_MODEL).astype(BF16)
        qkv = jnp.dot(hb, w_ref[...], preferred_element_type=F32)
        seg = seg_ref[...]

        def cols(lo, width):
            return [qkv[:, lo + LANES * j: lo + LANES * (j + 1)] for j in range(width // LANES)]

        cos = jnp.concatenate([cos_ref[...]] * bb, axis=0)
        sin_lo = jnp.concatenate([sinlo_ref[...]] * bb, axis=0)
        sin_hi = jnp.concatenate([sinhi_ref[...]] * bb, axis=0)
        scale = HEAD_DIM ** -0.5 * LOG2_E

        off_b = A_Q_W + 2 * A_KV_W
        qa, ka, qb, kb = _head_norms(
            [(cols(0, A_Q_W), gqa_ref[...]), (cols(A_Q_W, A_KV_W), gka_ref[...]),
             (cols(off_b, B_W), gqb_ref[...]), (cols(off_b + B_W, B_W), gkb_ref[...])], seg)
        qa = [_rope(p, cos, sin_lo, sin_hi) * scale for p in qa]
        ka = [_rope(p, cos, sin_lo, sin_hi) for p in ka]
        va = cols(A_Q_W + A_KV_W, A_KV_W)
        qb = [p * scale for p in qb]
        vb = cols(off_b + 2 * B_W, B_W)

        def put(ref, parts):
            for j, p in enumerate(parts):
                ref[:, :, LANES * j:LANES * (j + 1)] = p.reshape(bb, tl, LANES).astype(ref.dtype)

        put(qa_ref, qa)
        put(ka_ref, ka)
        put(va_ref, va)
        put(qb_ref, qb)
        put(kb_ref, kb)
        put(vb_ref, vb)

        @pl.when(i == n_tiles - 1)
        def _():
            def put_state(ref, parts, rows):
                for j, p in enumerate(parts):
                    ref[:, :, LANES * j:LANES * (j + 1)] = p.reshape(bb, tl, LANES)[:, tl - rows:, :]

            put_state(ska_ref, ka, sa_rows)
            put_state(sva_ref, va, sa_rows)
            put_state(skb_ref, kb, sb_rows)
            put_state(svb_ref, vb, sb_rows)


def _rope_tables(pos):
    inv_freq = ROPE_THETA ** (-jnp.arange(0, ROT_DIM, 2, dtype=F32) / ROT_DIM)
    ang = pos.astype(F32)[:, None] * inv_freq[None, :]
    half = ROT_DIM // 2
    lane = jnp.arange(LANES) % HEAD_DIM
    ang_l = jnp.tile(ang, (1, LANES // half))
    rot = lane < ROT_DIM
    cos = jnp.where(rot[None, :], jnp.cos(ang_l), 1.0)
    sin_lo = jnp.where((lane < half)[None, :], -jnp.sin(ang_l), 0.0)
    sin_hi = jnp.where(((lane >= half) & rot)[None, :], jnp.sin(ang_l), 0.0)
    return cos, sin_lo, sin_hi


def _stage1(x, shift, scale, g_attn, w_qkv, gains, pos, *, bb, tl, pad):
    bsz, length, _ = x.shape
    n_tiles = length // tl
    pad_tiles = pad // tl
    sa_rows, sb_rows = min(WINDOW_A, length), min(WINDOW_B, length)
    assert bsz % bb == 0 and length % tl == 0 and pad % tl == 0 and sb_rows <= tl
    cos, sin_lo, sin_hi = _rope_tables(pos)
    seg_ids = jnp.arange(2 * LANES) // HEAD_DIM
    seg = ((seg_ids[:, None] == seg_ids[None, :]).astype(F32) / HEAD_DIM).astype(BF16)
    gq_a, gk_a, gq_b, gk_b = [jnp.tile(g, LANES // HEAD_DIM).reshape(1, LANES) for g in gains]

    def xmap(b, i):
        return (b, jnp.maximum(i - pad_tiles, 0), 0)

    def const2(b, i):
        return (0, 0)

    def tabmap(b, i):
        return (jnp.maximum(i - pad_tiles, 0), 0)

    def modmap(b, i):
        return (b, 0, 0)

    def kvmap(b, i):
        return (b, i, 0)

    lp = length + pad
    kern = functools.partial(_stage1_kernel, bb=bb, tl=tl, pad_tiles=pad_tiles,
                             n_tiles=n_tiles + pad_tiles, sa_rows=sa_rows, sb_rows=sb_rows)
    out_shapes = (
        jax.ShapeDtypeStruct((bsz, length, A_Q_W), BF16),
        jax.ShapeDtypeStruct((bsz, lp, A_KV_W), BF16),
        jax.ShapeDtypeStruct((bsz, lp, A_KV_W), BF16),
        jax.ShapeDtypeStruct((bsz, length, B_W), BF16),
        jax.ShapeDtypeStruct((bsz, lp, B_W), BF16),
        jax.ShapeDtypeStruct((bsz, lp, B_W), BF16),
        jax.ShapeDtypeStruct((bsz, sa_rows, A_KV_W), F32),
        jax.ShapeDtypeStruct((bsz, sa_rows, A_KV_W), F32),
        jax.ShapeDtypeStruct((bsz, sb_rows, B_W), F32),
        jax.ShapeDtypeStruct((bsz, sb_rows, B_W), F32),
    )
    out_specs = (
        pl.BlockSpec((bb, tl, A_Q_W), xmap),
        pl.BlockSpec((bb, tl, A_KV_W), kvmap),
        pl.BlockSpec((bb, tl, A_KV_W), kvmap),
        pl.BlockSpec((bb, tl, B_W), xmap),
        pl.BlockSpec((bb, tl, B_W), kvmap),
        pl.BlockSpec((bb, tl, B_W), kvmap),
        pl.BlockSpec((bb, sa_rows, A_KV_W), modmap),
        pl.BlockSpec((bb, sa_rows, A_KV_W), modmap),
        pl.BlockSpec((bb, sb_rows, B_W), modmap),
        pl.BlockSpec((bb, sb_rows, B_W), modmap),
    )
    in_specs = [
        pl.BlockSpec((bb, tl, D_MODEL), xmap),
        pl.BlockSpec((bb, 1, D_MODEL), modmap),
        pl.BlockSpec((bb, 1, D_MODEL), modmap),
        pl.BlockSpec((1, D_MODEL), const2),
        pl.BlockSpec((D_MODEL, QKV_W), const2),
        pl.BlockSpec((1, LANES), const2),
        pl.BlockSpec((1, LANES), const2),
        pl.BlockSpec((1, LANES), const2),
        pl.BlockSpec((1, LANES), const2),
        pl.BlockSpec((2 * LANES, 2 * LANES), const2),
        pl.BlockSpec((tl, LANES), tabmap),
        pl.BlockSpec((tl, LANES), tabmap),
        pl.BlockSpec((tl, LANES), tabmap),
    ]
    return pl.pallas_call(
        kern,
        grid=(bsz // bb, n_tiles + pad_tiles),
        in_specs=in_specs,
        out_specs=out_specs,
        out_shape=out_shapes,
        compiler_params=_cparams("arbitrary", "arbitrary"),
        name="stage1",
    )(x, shift, scale, g_attn.reshape(1, D_MODEL), w_qkv, gq_a, gk_a, gq_b, gk_b, seg, cos, sin_lo, sin_hi)


def _nt_dot(a, b):
    return lax.dot_general(a, b, (((1,), (1,)), ((), ())), preferred_element_type=F32)


def _attn_kernel(sink_ref, qa_ref, ka_ref, va_ref, qb_ref, kb_ref, vb_ref, bias_ref, oa_ref, ob_ref,
                 *, n_c, mask_prefix):
    qi = pl.program_id(1)
    lane = lax.broadcasted_iota(I32, (CHUNK, LANES), 1)
    low = lane < HEAD_DIM
    half_mask = (jnp.where(low, 1.0, 0.0).astype(BF16), jnp.where(low, 0.0, 1.0).astype(BF16))
    n_pairs = B_W // LANES
    rows_b = n_pairs * 2 * CHUNK
    rows_a = A_Q_HEADS * CHUNK
    key_b = lax.broadcasted_iota(I32, (rows_b, BAND_B), 1)
    key_a = lax.broadcasted_iota(I32, (rows_a, BAND_A), 1)
    row_a = lax.broadcasted_iota(I32, (rows_a, 1), 0) // CHUNK

    sink_col = jnp.zeros((rows_a, 1), F32)
    for r in range(A_Q_HEADS):
        sink_col = jnp.where(row_a == r, sink_ref[r], sink_col)

    def chunk(c, masked):
        ci = qi * n_c + c
        row0 = pl.multiple_of(ci * CHUNK, CHUNK)
        rows = slice(c * CHUNK, (c + 1) * CHUNK)
        first_valid = PAD - ci * CHUNK

        s_parts = []
        for j in range(n_pairs):
            cg = slice(LANES * j, LANES * (j + 1))
            q = qb_ref[0, rows, cg]
            qs = jnp.concatenate([q * half_mask[0], q * half_mask[1]], axis=0)
            k = kb_ref[0, pl.ds(row0, BAND_B), cg]
            s_parts.append(_nt_dot(qs, k) + bias_ref[j])
        s = jnp.concatenate(s_parts, axis=0)
        if masked:
            s = jnp.where(key_b >= first_valid, s, NEG_INF)
        mx = jnp.max(s, axis=-1, keepdims=True)
        e = jnp.exp2(s - mx)
        inv = 1.0 / jnp.sum(e, axis=-1, keepdims=True)
        p_b = e.astype(BF16)
        for j in range(n_pairs):
            cg = slice(LANES * j, LANES * (j + 1))
            v = vb_ref[0, pl.ds(row0, BAND_B), cg]
            rj = slice(2 * CHUNK * j, 2 * CHUNK * (j + 1))
            o = jnp.dot(p_b[rj], v, preferred_element_type=F32) * inv[rj]
            ob_ref[0, rows, cg] = jnp.where(low, o[:CHUNK], o[CHUNK:]).astype(ob_ref.dtype)

        row_a0 = pl.multiple_of(row0 + (WINDOW_B - WINDOW_A), CHUNK)
        k = ka_ref[0, pl.ds(row_a0, BAND_A), :]
        v = va_ref[0, pl.ds(row_a0, BAND_A), :]
        q = jnp.concatenate(
            [qa_ref[0, rows, LANES * j:LANES * (j + 1)] * half_mask[p]
             for p in range(A_KV_HEADS) for j in range(A_GROUP)], axis=0)
        s = _nt_dot(q, k)
        if masked:
            s = jnp.where(key_a >= first_valid - (WINDOW_B - WINDOW_A), s, NEG_INF)
        mx = jnp.maximum(jnp.max(s, axis=-1, keepdims=True), sink_col)
        e = jnp.exp2(s - mx)
        inv = 1.0 / (jnp.sum(e, axis=-1, keepdims=True) + jnp.exp2(sink_col - mx))
        o = jnp.dot(e.astype(BF16), v, preferred_element_type=F32) * inv
        for j in range(A_GROUP):
            lo = o[j * CHUNK:(j + 1) * CHUNK]
            hi = o[(A_GROUP + j) * CHUNK:(A_GROUP + j + 1) * CHUNK]
            oa_ref[0, rows, LANES * j:LANES * (j + 1)] = jnp.where(low, lo, hi).astype(oa_ref.dtype)

    steps_with_pad = pl.cdiv(PAD // CHUNK, n_c) if mask_prefix else 0
    if steps_with_pad:
        @pl.when(qi < steps_with_pad)
        def _():
            for c in range(n_c):
                chunk(c, True)

    @pl.when(qi >= steps_with_pad)
    def _():
        for c in range(n_c):
            chunk(c, False)


def _attention(qa, ka, va, qb, kb, vb, sinks, bias, *, n_c, mask_prefix):
    bsz, length, _ = qa.shape
    lp = ka.shape[1]
    qt = n_c * CHUNK
    assert length % qt == 0 and lp == length + PAD

    def qmap(b, i):
        return (b, i, 0)

    def kmap(b, i):
        return (b, 0, 0)

    kern = functools.partial(_attn_kernel, n_c=n_c, mask_prefix=mask_prefix)
    return pl.pallas_call(
        kern,
        grid=(bsz, length // qt),
        in_specs=[
            pl.BlockSpec(memory_space=pltpu.SMEM),
            pl.BlockSpec((1, qt, A_Q_W), qmap),
            pl.BlockSpec((1, lp, A_KV_W), kmap),
            pl.BlockSpec((1, lp, A_KV_W), kmap),
            pl.BlockSpec((1, qt, B_W), qmap),
            pl.BlockSpec((1, lp, B_W), kmap),
            pl.BlockSpec((1, lp, B_W), kmap),
            pl.BlockSpec((B_HEADS // 2, 2 * CHUNK, BAND_B), lambda b, i: (0, 0, 0)),
        ],
        out_specs=(pl.BlockSpec((1, qt, A_Q_W), qmap), pl.BlockSpec((1, qt, B_W), qmap)),
        out_shape=(jax.ShapeDtypeStruct((bsz, length, A_Q_W), BF16),
                   jax.ShapeDtypeStruct((bsz, length, B_W), BF16)),
        compiler_params=_cparams("arbitrary", "arbitrary"),
        name="attention",
    )(sinks, qa, ka, va, qb, kb, vb, bias)


def _merge_kernel(x_ref, oa_ref, ob_ref, sha_ref, sca_ref, gta_ref, shm_ref, scm_ref, ga_ref, gm_ref,
                  wg_ref, wa_ref, wb_ref, wo_ref, wrh_ref, wrl_ref, br_ref,
                  x1_ref, h2_ref, ri_ref, rw_ref, cnt_ref, carry_ref, logit_ref, *, bb, tl, seg_tiles):
    m = bb * tl
    step = pl.program_id(0)

    @pl.when(step == 0)
    def _():
        carry_ref[...] = jnp.zeros_like(carry_ref)
        logit_ref[...] = jnp.zeros_like(logit_ref)

    logits = logit_ref[...]
    lane = lax.broadcasted_iota(I32, (m, LANES), 1)
    lane_f = lane.astype(F32)
    work = logits
    top_v, top_i = [], []
    sel_f = jnp.zeros((m, LANES), F32)
    for _ in range(TOP_K):
        mx = jnp.max(work, axis=-1, keepdims=True)
        idx = jnp.min(jnp.where(work == mx, lane_f, float(LANES)), axis=-1, keepdims=True)
        hit = lane_f == idx
        top_v.append(mx)
        top_i.append(idx)
        sel_f = jnp.where(hit, 1.0, sel_f)
        work = jnp.where(hit, -jnp.inf, work)
    ex = [jnp.exp(v - top_v[0]) for v in top_v]
    inv = 1.0 / (ex[0] + ex[1] + ex[2] + ex[3])

    r_i = lax.broadcasted_iota(I32, (m, m), 0)
    c_i = lax.broadcasted_iota(I32, (m, m), 1)
    before = jnp.where(r_i > c_i, 1.0, 0.0).astype(BF16)
    keep = jnp.where(jnp.maximum(step - 1, 0) % seg_tiles == 0, 0.0, 1.0)
    carry_in = carry_ref[...] * keep
    pos = jnp.dot(before, sel_f.astype(BF16), preferred_element_type=F32) + carry_in
    live = jnp.where(step > 0, 1.0, 0.0)
    carry = carry_in + live * jnp.sum(sel_f, axis=0, keepdims=True)
    carry_ref[...] = carry
    cnt_ref[0] = carry

    ri_f = jnp.zeros((m, LANES), F32)
    rw = jnp.zeros((m, LANES), F32)
    for k in range(TOP_K):
        rank = jnp.sum(jnp.where(lane_f == top_i[k], pos, 0.0), axis=-1, keepdims=True)
        ri_f = jnp.where(lane == k, top_i[k], ri_f)
        ri_f = jnp.where(lane == TOP_K + k, rank, ri_f)
        rw = jnp.where(lane == k, ex[k] * inv, rw)
    ri_ref[0] = jnp.transpose(ri_f)[:2 * TOP_K].astype(I32)
    rw_ref[...] = rw.reshape(bb, tl, LANES)

    x = x_ref[...]
    h = _modulated_norm(x, ga_ref[...], sca_ref[...], sha_ref[...])
    hb = h.reshape(m, D_MODEL).astype(BF16)
    gates = jnp.dot(hb, wg_ref[...], preferred_element_type=F32)
    ya = jnp.dot(oa_ref[...].reshape(m, A_Q_W), wa_ref[...], preferred_element_type=F32)
    yb = jnp.dot(ob_ref[...].reshape(m, B_W), wb_ref[...], preferred_element_type=F32)
    mix = jax.nn.sigmoid(gates[:, :D_MODEL]) * ya + jax.nn.sigmoid(gates[:, D_MODEL:]) * yb
    z = jnp.dot(mix.astype(BF16), wo_ref[...], preferred_element_type=F32)
    x1 = x + gta_ref[...] * z.reshape(bb, tl, D_MODEL)
    x1_ref[...] = x1

    h2 = _modulated_norm(x1, gm_ref[...], scm_ref[...], shm_ref[...]).reshape(m, D_MODEL)
    _pack_rows(h2_ref, h2)

    h2_hi = h2.astype(BF16)
    h2_lo = (h2 - h2_hi.astype(F32)).astype(BF16)
    wrh = wrh_ref[...]
    logit_ref[...] = (jnp.dot(h2_hi, wrh, preferred_element_type=F32)
                      + jnp.dot(h2_lo, wrh, preferred_element_type=F32)
                      + jnp.dot(h2_hi, wrl_ref[...], preferred_element_type=F32)) + br_ref[...]


def _merge(x, oa, ob, mods, g_attn, g_mlp, wg, wa, wb, wo, wr, br, *, bb, tl, b0, nb):
    _, length, _ = x.shape
    n_tok = nb * length
    m = bb * tl
    assert bb == 1 or tl == length
    sh_a, sc_a, gt_a, sh_m, sc_m = mods
    n_l = length // tl
    n_tiles = (nb // bb) * n_l
    blk0 = b0 // bb
    wr_hi = wr.astype(BF16)
    wr_lo = (wr - wr_hi.astype(F32)).astype(BF16)

    def cur(s):
        return jnp.minimum(s, n_tiles - 1)

    def prev(s):
        return jnp.maximum(s - 1, 0)

    def xmap(s):
        return (blk0 + cur(s) // n_l, cur(s) % n_l, 0)

    def modmap(s):
        return (blk0 + cur(s) // n_l, 0, 0)

    def omap(s):
        return (cur(s) // n_l, cur(s) % n_l, 0)

    def const2(s):
        return (0, 0)

    def flatmap(s):
        return (cur(s), 0)

    def routemap(s):
        return (prev(s) // n_l, prev(s) % n_l, 0)

    modspec = pl.BlockSpec((bb, 1, D_MODEL), modmap)
    kern = functools.partial(_merge_kernel, bb=bb, tl=tl, seg_tiles=n_tiles)
    return pl.pallas_call(
        kern,
        grid=(n_tiles + 1,),
        in_specs=[
            pl.BlockSpec((bb, tl, D_MODEL), xmap),
            pl.BlockSpec((bb, tl, A_Q_W), xmap),
            pl.BlockSpec((bb, tl, B_W), xmap),
            modspec, modspec, modspec, modspec, modspec,
            pl.BlockSpec((1, D_MODEL), const2),
            pl.BlockSpec((1, D_MODEL), const2),
            pl.BlockSpec((D_MODEL, 2 * D_MODEL), const2),
            pl.BlockSpec((A_Q_W, D_MODEL), const2),
            pl.BlockSpec((B_W, D_MODEL), const2),
            pl.BlockSpec((D_MODEL, D_MODEL), const2),
            pl.BlockSpec((D_MODEL, LANES), const2),
            pl.BlockSpec((D_MODEL, LANES), const2),
            pl.BlockSpec((1, LANES), const2),
        ],
        out_specs=(
            pl.BlockSpec((bb, tl, D_MODEL), omap),
            pl.BlockSpec((m * ROW_TILES, LANES), flatmap),
            pl.BlockSpec((1, 2 * TOP_K, m), lambda s: (prev(s), 0, 0)),
            pl.BlockSpec((bb, tl, LANES), routemap),
            pl.BlockSpec((1, 1, LANES), lambda s: (0, 0, 0)),
        ),
        out_shape=(
            jax.ShapeDtypeStruct((nb, length, D_MODEL), F32),
            jax.ShapeDtypeStruct((n_tok * ROW_TILES, LANES), I32),
            jax.ShapeDtypeStruct((n_tiles, 2 * TOP_K, m), I32),
            jax.ShapeDtypeStruct((nb, length, LANES), F32),
            jax.ShapeDtypeStruct((1, 1, LANES), F32),
        ),
        scratch_shapes=[pltpu.VMEM((1, LANES), F32), pltpu.VMEM((m, LANES), F32)],
        compiler_params=_cparams("arbitrary"),
        name="merge",
    )(x, oa, ob, sh_a, sc_a, gt_a, sh_m, sc_m, g_attn.reshape(1, D_MODEL), g_mlp.reshape(1, D_MODEL),
      wg, wa, wb, wo, wr_hi, wr_lo, br)


def _sc_mesh():
    return plsc.VectorSubcoreMesh(core_axis_name="c", subcore_axis_name="s",
                                  num_cores=SC_CORES, num_subcores=SC_SUBCORES)


def _sc_worker():
    return lax.axis_index("c") * SC_SUBCORES + lax.axis_index("s")


def _sc_params():
    return pltpu.CompilerParams(use_tc_tiling_on_sc=True)


def _part_chunks(parts):
    assert all(p % (2 * SC_WORKERS * SC_ROWS) == 0 for p in parts)
    chunks = [p // (SC_WORKERS * SC_ROWS) for p in parts]
    starts = [sum(chunks[:i]) for i in range(len(chunks))]
    return chunks, starts


def _dispatch(h2_parts, idx_flat, pad_flat, n_rows):
    n_parts = len(h2_parts)
    chunks, starts = _part_chunks([h.shape[0] for h in h2_parts])
    n_idx = sum(chunks) * TOP_K * SC_ROWS
    n_pad = pad_flat.shape[0] // SC_WORKERS
    pad_chunks = n_pad // SC_ROWS
    assert n_pad % SC_ROWS == 0
    zeros = jnp.zeros((SC_ROWS, ROW_TILES, LANES), I32)

    def body(*refs):
        h2_hbms = refs[:n_parts]
        idx_hbm, pad_hbm, zeros_hbm, xs_hbm, idx_v, pad_v, buf0, buf1, ls0, ls1, ss0, ss1 = refs[n_parts:]
        wid = _sc_worker()
        bufs, lsems, ssems = (buf0, buf1), (ls0, ls1), (ss0, ss1)
        pltpu.sync_copy(idx_hbm.at[pl.ds(pl.multiple_of(wid * n_idx, SC_ROWS), n_idx)], idx_v)
        pltpu.sync_copy(pad_hbm.at[pl.ds(pl.multiple_of(wid * n_pad, SC_ROWS), n_pad)], pad_v)

        pltpu.sync_copy(zeros_hbm, buf0)
        for c in range(pad_chunks):
            pltpu.async_copy(buf0, xs_hbm.at[pad_v.at[pl.ds(c * SC_ROWS, SC_ROWS)]], ss0)
        for c in range(pad_chunks):
            pltpu.make_async_copy(buf0, xs_hbm.at[pad_v.at[pl.ds(c * SC_ROWS, SC_ROWS)]], ss0).wait()

        def run_part(h2_hbm, n_chunks, c0):
            base = wid * (n_chunks * SC_ROWS)

            def load(g, b):
                src = h2_hbm.at[pl.ds(pl.multiple_of(base + g * SC_ROWS, SC_ROWS), SC_ROWS)]
                return pltpu.make_async_copy(src, bufs[b], lsems[b])

            def scatter(g, k, b):
                off = pl.multiple_of(((c0 + g) * TOP_K + k) * SC_ROWS, SC_ROWS)
                return pltpu.make_async_copy(bufs[b], xs_hbm.at[idx_v.at[pl.ds(off, SC_ROWS)]], ssems[b])

            load(0, 0).start()

            @pl.loop(0, n_chunks, step=2)
            def _(g0):
                for b in range(2):
                    g = g0 + b
                    load(g, b).wait()

                    @pl.when(g >= 1)
                    def _():
                        for k in range(TOP_K):
                            scatter(g - 1, k, 1 - b).wait()

                    @pl.when(g + 1 < n_chunks)
                    def _():
                        load(g + 1, 1 - b).start()

                    for k in range(TOP_K):
                        scatter(g, k, b).start()

            for k in range(TOP_K):
                scatter(n_chunks - 1, k, (n_chunks - 1) % 2).wait()

        for h2_hbm, n_chunks, c0 in zip(h2_hbms, chunks, starts):
            run_part(h2_hbm, n_chunks, c0)

    call = pl.kernel(
        body,
        out_type=jax.ShapeDtypeStruct((n_rows + TRASH_ROWS, ROW_TILES, LANES), I32),
        mesh=_sc_mesh(),
        scratch_types=[pltpu.VMEM((n_idx,), I32), pltpu.VMEM((n_pad,), I32),
                       pltpu.VMEM((SC_ROWS, ROW_TILES, LANES), I32), pltpu.VMEM((SC_ROWS, ROW_TILES, LANES), I32),
                       pltpu.SemaphoreType.DMA, pltpu.SemaphoreType.DMA,
                       pltpu.SemaphoreType.DMA, pltpu.SemaphoreType.DMA],
        compiler_params=_sc_params(),
        name="dispatch",
    )
    return call(*h2_parts, idx_flat, pad_flat, zeros)


def _gather_back(y_rows, idx_flat, parts):
    n_parts = len(parts)
    chunks, starts = _part_chunks(parts)
    n_idx = sum(chunks) * TOP_K * SC_ROWS

    def body(*refs):
        y_hbm, idx_hbm = refs[:2]
        yg_hbms = refs[2:2 + n_parts]
        idx_v, buf0, buf1, gs0, gs1, ws0, ws1 = refs[2 + n_parts:]
        wid = _sc_worker()
        bufs, gsems, wsems = (buf0, buf1), (gs0, gs1), (ws0, ws1)
        pltpu.sync_copy(idx_hbm.at[pl.ds(pl.multiple_of(wid * n_idx, SC_ROWS), n_idx)], idx_v)

        def run_part(yg_hbm, n_chunks, c0):
            base = wid * (n_chunks * SC_ROWS)
            n_items = n_chunks * TOP_K

            def gather(i, b):
                off = pl.multiple_of((c0 * TOP_K + i) * SC_ROWS, SC_ROWS)
                return pltpu.make_async_copy(y_hbm.at[idx_v.at[pl.ds(off, SC_ROWS)]], bufs[b], gsems[b])

            def write(i, b):
                g, k = i // TOP_K, i % TOP_K
                dst = yg_hbm.at[k, pl.ds(pl.multiple_of(base + g * SC_ROWS, SC_ROWS), SC_ROWS)]
                return pltpu.make_async_copy(bufs[b], dst, wsems[b])

            gather(0, 0).start()

            @pl.loop(0, n_items, step=2)
            def _(i0):
                for b in range(2):
                    i = i0 + b
                    gather(i, b).wait()

                    @pl.when(i >= 1)
                    def _():
                        write(i - 1, 1 - b).wait()

                    @pl.when(i + 1 < n_items)
                    def _():
                        gather(i + 1, 1 - b).start()

                    write(i, b).start()

            write(n_items - 1, (n_items - 1) % 2).wait()

        for yg_hbm, n_chunks, c0 in zip(yg_hbms, chunks, starts):
            run_part(yg_hbm, n_chunks, c0)

    call = pl.kernel(
        body,
        out_type=tuple(jax.ShapeDtypeStruct((TOP_K, p, ROW_TILES, LANES), I32) for p in parts),
        mesh=_sc_mesh(),
        scratch_types=[pltpu.VMEM((n_idx,), I32),
                       pltpu.VMEM((SC_ROWS, ROW_TILES, LANES), I32), pltpu.VMEM((SC_ROWS, ROW_TILES, LANES), I32),
                       pltpu.SemaphoreType.DMA, pltpu.SemaphoreType.DMA,
                       pltpu.SemaphoreType.DMA, pltpu.SemaphoreType.DMA],
        compiler_params=_sc_params(),
        name="gather_back",
    )
    return call(y_rows, idx_flat)


def _expert_kernel(blk_ref, used_ref, next_ref, slot_ref, xs_ref, wgu_hbm, bgu_ref, wd_hbm, bd_ref, y_ref,
                   wgu_buf, wd_buf, wgu_bf, wd_bf, sem_gu, sem_d):
    r = pl.program_id(0)
    e = blk_ref[r]
    prev = blk_ref[jnp.maximum(r - 1, 0)]
    active = r < used_ref[0]
    slot = slot_ref[r]

    def fetch(expert, s):
        return (pltpu.make_async_copy(wgu_hbm.at[expert], wgu_buf.at[s], sem_gu.at[s]),
                pltpu.make_async_copy(wd_hbm.at[expert], wd_buf.at[s], sem_d.at[s]))

    @pl.when(jnp.logical_and(active, r == 0))
    def _():
        for copy in fetch(e, slot):
            copy.start()

    @pl.when(jnp.logical_and(active, jnp.logical_or(r == 0, e != prev)))
    def _():
        for copy in fetch(e, slot):
            copy.wait()
        wgu_bf[...] = wgu_buf[slot].astype(BF16)
        wd_bf[...] = wd_buf[slot].astype(BF16)
        nxt = next_ref[r]

        @pl.when(nxt >= 0)
        def _():
            for copy in fetch(nxt, 1 - slot):
                copy.start()

    @pl.when(active)
    def _():
        lo, hi = _unpack_rows([xs_ref[pl.ds(j, ROUTE_BLOCK, stride=ROW_TILES), :] for j in range(ROW_TILES)])
        x = jnp.concatenate(lo + hi, axis=1).astype(BF16)
        acts = []
        for j in range(D_FF // FF_TILE):
            gc = slice(FF_TILE * j, FF_TILE * (j + 1))
            uc = slice(D_FF + FF_TILE * j, D_FF + FF_TILE * (j + 1))
            g = jnp.dot(x, wgu_bf[:, gc], preferred_element_type=F32) + bgu_ref[0, :, gc]
            u = jnp.dot(x, wgu_bf[:, uc], preferred_element_type=F32) + bgu_ref[0, :, uc]
            gate = jnp.minimum(g, SWIGLU_LIMIT)
            up = jnp.clip(u, -SWIGLU_LIMIT, SWIGLU_LIMIT)
            glu = gate * jax.nn.sigmoid(gate * SWIGLU_ALPHA)
            acts.append(((up + 1.0) * glu).astype(BF16))
        act = jnp.concatenate(acts, axis=1)
        y = jnp.dot(act, wd_bf[...], preferred_element_type=F32) + bd_ref[0]
        _pack_rows(y_ref, y)


def _experts(xs_rows, blk_expert, n_used, blk_next, blk_slot, w_gate_up, b_gate_up, w_down, b_down):
    n_blocks = blk_expert.shape[0]
    rb = ROUTE_BLOCK * ROW_TILES

    def rowmap(r, blk, used, nxt, slot):
        return (jnp.minimum(r, used[0] - 1), 0)

    def bmap(r, blk, used, nxt, slot):
        return (blk[jnp.minimum(r, used[0] - 1)], 0, 0)

    return pl.pallas_call(
        _expert_kernel,
        grid_spec=pltpu.PrefetchScalarGridSpec(
            num_scalar_prefetch=4,
            grid=(n_blocks,),
            in_specs=[
                pl.BlockSpec((rb, LANES), rowmap),
                pl.BlockSpec(memory_space=pl.ANY),
                pl.BlockSpec((1, 1, 2 * D_FF), bmap),
                pl.BlockSpec(memory_space=pl.ANY),
                pl.BlockSpec((1, 1, D_MODEL), bmap),
            ],
            out_specs=pl.BlockSpec((rb, LANES), rowmap),
            scratch_shapes=[pltpu.VMEM((2, D_MODEL, 2 * D_FF), F32), pltpu.VMEM((2, D_FF, D_MODEL), F32),
                            pltpu.VMEM((D_MODEL, 2 * D_FF), BF16), pltpu.VMEM((D_FF, D_MODEL), BF16),
                            pltpu.SemaphoreType.DMA((2,)), pltpu.SemaphoreType.DMA((2,))],
        ),
        out_shape=jax.ShapeDtypeStruct(xs_rows.shape, I32),
        compiler_params=_cparams("arbitrary"),
        name="experts",
    )(blk_expert, n_used, blk_next, blk_slot, xs_rows, w_gate_up, b_gate_up.reshape(N_EXPERTS, 1, 2 * D_FF),
      w_down, b_down.reshape(N_EXPERTS, 1, D_MODEL))


def _combine_kernel(x1_ref, gt_ref, rw_ref, yg_ref, *rest, bb, tl):
    o_ref = rest[-1]
    m = bb * tl
    rw = rw_ref[...].reshape(m, LANES)
    w = [rw[:, k:k + 1] for k in range(TOP_K)]
    gt = gt_ref[...]
    for j in range(ROW_TILES):
        acc_lo = jnp.zeros((m, LANES), F32)
        acc_hi = jnp.zeros((m, LANES), F32)
        for k in range(TOP_K):
            lo, hi = _unpack_rows([yg_ref[k, pl.ds(j, m, stride=ROW_TILES), :]])
            acc_lo = acc_lo + w[k] * lo[0]
            acc_hi = acc_hi + w[k] * hi[0]
        for acc, c0 in ((acc_lo, LANES * j), (acc_hi, HALF_W + LANES * j)):
            cg = slice(c0, c0 + LANES)
            o_ref[:, :, cg] = x1_ref[:, :, cg] + gt[:, :, cg] * acc.reshape(bb, tl, LANES)


def _combine(x1, gt_m, rw, yg, y_prev, *, bb, tl, b0, bsz):
    nb, length, _ = x1.shape
    m = bb * tl
    assert bb == 1 or tl == length
    n_l = length // tl
    blk0 = b0 // bb

    def smap(b, i):
        return (b, i, 0)

    def xmap(b, i):
        return (blk0 + b, i, 0)

    in_specs = [
        pl.BlockSpec((bb, tl, D_MODEL), smap),
        pl.BlockSpec((bb, 1, D_MODEL), lambda b, i: (blk0 + b, 0, 0)),
        pl.BlockSpec((bb, tl, LANES), smap),
        pl.BlockSpec((TOP_K, m * ROW_TILES, LANES), lambda b, i: (0, b * n_l + i, 0)),
    ]
    args = [x1, gt_m, rw, yg]
    aliases = {}
    if y_prev is not None:
        in_specs.append(pl.BlockSpec(memory_space=pl.ANY))
        args.append(y_prev)
        aliases = {len(args) - 1: 0}
    kern = functools.partial(_combine_kernel, bb=bb, tl=tl)
    return pl.pallas_call(
        kern,
        grid=(nb // bb, n_l),
        in_specs=in_specs,
        out_specs=pl.BlockSpec((bb, tl, D_MODEL), xmap),
        out_shape=jax.ShapeDtypeStruct((bsz, length, D_MODEL), F32),
        input_output_aliases=aliases,
        compiler_params=_cparams("arbitrary", "arbitrary"),
        name="combine",
    )(*args)


def _route_plan(routes):
    expert_ids = jnp.arange(N_EXPERTS, dtype=I32)
    part_counts = [c.reshape(-1)[:N_EXPERTS].astype(I32) for _, c in routes]
    counts = sum(part_counts)
    padded = (counts + ROUTE_BLOCK - 1) // ROUTE_BLOCK * ROUTE_BLOCK
    pends = jnp.cumsum(padded)
    pstarts = pends - padded
    n_assign = sum(ri.shape[0] * ri.shape[2] for ri, _ in routes) * TOP_K
    n_blocks = n_assign // ROUTE_BLOCK + N_EXPERTS
    n_rows = n_blocks * ROUTE_BLOCK
    lists = []
    first = jnp.zeros_like(counts)
    for (ri, _), cnt in zip(routes, part_counts):
        n_tiles, _, m = ri.shape
        idx, rank = ri[:, :TOP_K], ri[:, TOP_K:]
        dest = jnp.sum(jnp.where(idx[..., None] == expert_ids, pstarts + first, 0), axis=-1) + rank
        dest = dest.astype(I32).reshape(n_tiles, TOP_K, m // SC_ROWS, SC_ROWS).transpose(0, 2, 1, 3)
        lists.append(dest.reshape(SC_WORKERS, n_tiles * m // (SC_WORKERS * SC_ROWS), TOP_K, SC_ROWS))
        first = first + cnt
    idx_flat = jnp.concatenate(lists, axis=1).reshape(-1)
    blk_start = jnp.arange(n_blocks, dtype=I32) * ROUTE_BLOCK
    blk_expert = jnp.minimum(jnp.sum((pends[None, :] <= blk_start[:, None]).astype(I32), axis=1), N_EXPERTS - 1)
    n_used = (pends[-1] // ROUTE_BLOCK).reshape(1)
    nonempty = padded > 0
    later = nonempty[None, :] & (expert_ids[None, :] > expert_ids[:, None])
    next_e = jnp.min(jnp.where(later, expert_ids[None, :], N_EXPERTS), axis=1)
    next_e = jnp.where(next_e == N_EXPERTS, -1, next_e)
    slot_e = (jnp.cumsum(nonempty.astype(I32)) - 1) % 2
    own = blk_expert[:, None] == expert_ids[None, :]
    blk_next = jnp.sum(jnp.where(own, next_e[None, :], 0), axis=1)
    blk_slot = jnp.sum(jnp.where(own, slot_e[None, :], 0), axis=1)
    r = jnp.arange(ROUTE_BLOCK, dtype=I32)[None, :]
    pad_flat = jnp.where(r < (padded - counts)[:, None], (pstarts + counts)[:, None] + r, n_rows)
    blocks = (blk_expert.astype(I32), n_used.astype(I32), blk_next.astype(I32), blk_slot.astype(I32))
    return idx_flat, pad_flat.astype(I32).reshape(-1), blocks, n_rows


def _front(x, mods, pos, prefix, weights, *, bb, tl, n_c):
    g_attn, w_qkv, gains, sinks, bias = weights[0], weights[2], weights[3], weights[4], weights[5]
    sh_a, sc_a = mods[0], mods[1]
    pad = 0 if prefix is not None else PAD
    qa, ka, va, qb, kb, vb, ska, sva, skb, svb = _stage1(
        x, sh_a, sc_a, g_attn, w_qkv, gains, pos, bb=bb, tl=tl, pad=pad)
    if prefix is not None:
        ka, va, kb, vb = [jnp.concatenate([p, t], axis=1) for p, t in zip(prefix, (ka, va, kb, vb))]
    oa, ob = _attention(qa, ka, va, qb, kb, vb, sinks, bias, n_c=n_c, mask_prefix=prefix is None)
    return oa, ob, (ska, sva, skb, svb)


def _moe_segment(parts, weights):
    (g_attn, g_mlp, _, _, _, _, wg, wa, wb, wo, wr, br, w_gate_up, b_gate_up, w_down, b_down) = weights
    merged = []
    for p in parts:
        sh_a, sc_a, gt_a, sh_m, sc_m, _ = p["mods"]
        merged.append(_merge(p["x"], p["oa"], p["ob"], (sh_a, sc_a, gt_a, sh_m, sc_m), g_attn, g_mlp,
                             wg, wa, wb, wo, wr, br, bb=p["cfg"]["bb"], tl=p["cfg"]["tl"], b0=p["b0"], nb=p["nb"]))
    toks = [m[0].shape[0] * m[0].shape[1] for m in merged]
    idx_flat, pad_flat, blocks, n_rows = _route_plan([(m[2], m[4]) for m in merged])
    xs = _dispatch([m[1].reshape(t, ROW_TILES, LANES) for m, t in zip(merged, toks)], idx_flat, pad_flat, n_rows)
    y_rows = _experts(xs.reshape(-1, LANES), *blocks, w_gate_up, b_gate_up, w_down, b_down)
    ygs = _gather_back(y_rows.reshape(-1, ROW_TILES, LANES), idx_flat, toks)
    outs = []
    for p, m, yg, t in zip(parts, merged, ygs, toks):
        outs.append(_combine(m[0], p["mods"][5], m[3], yg.reshape(TOP_K, t * ROW_TILES, LANES), p["y"],
                             bb=p["cfg"]["bb_combine"], tl=p["cfg"]["tl_combine"], b0=p["b0"],
                             bsz=p["x"].shape[0]))
    return outs


def _rel_bias_table(rel_table):
    n_rel = rel_table.shape[1]
    ext = jnp.concatenate(
        [rel_table, jnp.broadcast_to(rel_table[:, -1:], (B_HEADS, CHUNK + BAND_B - n_rel))], axis=1)
    rows = jnp.stack([ext[:, i:i + BAND_B] for i in range(CHUNK)], axis=1)[..., ::-1]
    return rows.astype(F32).reshape(B_HEADS // 2, 2 * CHUNK, BAND_B)


def _prep_weights(g_attn, g_mlp, w_in, q_norm_a, k_norm_a, q_norm_b, k_norm_b, sinks_a, rel_table_b,
                  w_branch_a, w_branch_b, w_out, w_router, b_router, w_gate_up, b_gate_up, w_down, b_down):
    order = jnp.asarray(A_HEAD_ORDER)
    w_qa = w_in[:, :A_Q_W].reshape(D_MODEL, A_Q_HEADS, HEAD_DIM)[:, order].reshape(D_MODEL, A_Q_W)
    w_qkv = jnp.concatenate([w_qa, w_in[:, A_Q_W:QKV_W]], axis=1).astype(BF16)
    wg = w_in[:, QKV_W:].astype(BF16)
    wa = w_branch_a.reshape(A_Q_HEADS, HEAD_DIM, D_MODEL)[order].reshape(A_Q_W, D_MODEL).astype(BF16)
    wb = w_branch_b.astype(BF16)
    wo = w_out.astype(BF16)
    wr = jnp.zeros((D_MODEL, LANES), F32).at[:, :N_EXPERTS].set(w_router)
    br = jnp.full((1, LANES), NEG_INF, F32).at[0, :N_EXPERTS].set(b_router)
    bias = _rel_bias_table(rel_table_b) * LOG2_E
    gains = (q_norm_a, k_norm_a, q_norm_b, k_norm_b)
    return (g_attn, g_mlp, w_qkv, gains, sinks_a.astype(F32) * LOG2_E, bias, wg, wa, wb, wo, wr, br,
            w_gate_up, b_gate_up, w_down, b_down)


def _forward(x_prompt, x_sample, c_prompt, c_sample, cache_a_k, cache_a_v, cache_b_k, cache_b_v, params,
             past_len, cfg_prompt, cfg_sample):
    (g_attn, g_mlp, w_ada, b_ada, w_in, q_norm_a, k_norm_a, q_norm_b, k_norm_b, sinks_a, rel_table_b,
     w_branch_a, w_branch_b, w_out, w_router, b_router, w_gate_up, b_gate_up, w_down, b_down) = params
    depth = g_attn.shape[0]
    bp, lp_, _ = x_prompt.shape
    bs, ls, _ = x_sample.shape
    pos_p = jnp.arange(lp_, dtype=I32)
    pos_s = past_len + jnp.arange(ls, dtype=I32)
    yp, ys = x_prompt, x_sample
    st_p, st_s = [], []
    for l in range(depth):
        weights = _prep_weights(g_attn[l], g_mlp[l], w_in[l], q_norm_a[l], k_norm_a[l], q_norm_b[l],
                                k_norm_b[l], sinks_a[l], rel_table_b[l], w_branch_a[l], w_branch_b[l],
                                w_out[l], w_router[l], b_router[l], w_gate_up[l], b_gate_up[l],
                                w_down[l], b_down[l])
        mod = _adaln(jnp.concatenate([c_prompt, c_sample], axis=0), w_ada[l], b_ada[l])
        mod = mod.reshape(bp + bs, 6, 1, D_MODEL)
        mods_p = [mod[:bp, k] for k in range(6)]
        mods_s = [mod[bp:, k] for k in range(6)]
        wa_c = cache_a_k.shape[2]
        wb_c = cache_b_k.shape[2]
        assert wb_c == WINDOW_B and wa_c == WINDOW_A
        zeros_a = jnp.zeros((bs, PAD - wa_c, A_KV_W), BF16)
        prefix = (
            jnp.concatenate([zeros_a, cache_a_k[l].reshape(bs, wa_c, A_KV_W).astype(BF16)], axis=1),
            jnp.concatenate([zeros_a, cache_a_v[l].reshape(bs, wa_c, A_KV_W).astype(BF16)], axis=1),
            cache_b_k[l].reshape(bs, wb_c, B_W).astype(BF16),
            cache_b_v[l].reshape(bs, wb_c, B_W).astype(BF16),
        )
        front = {k: cfg_prompt[k] for k in ("bb", "tl", "n_c")}
        oa_p, ob_p, sp = _front(yp, mods_p, pos_p, None, weights, **front)
        front = {k: cfg_sample[k] for k in ("bb", "tl", "n_c")}
        oa_s, ob_s, ss = _front(ys, mods_s, pos_s, prefix, weights, **front)
        n_seg = cfg_prompt["n_seg"]
        nb = bp // n_seg
        xp, xs_in = yp, ys
        yp = None
        for s in range(n_seg):
            parts = [dict(x=xp, oa=oa_p, ob=ob_p, mods=mods_p, b0=s * nb, nb=nb, cfg=cfg_prompt, y=yp)]
            if s == n_seg - 1:
                parts.append(dict(x=xs_in, oa=oa_s, ob=ob_s, mods=mods_s, b0=0, nb=bs, cfg=cfg_sample, y=None))
            outs = _moe_segment(parts, weights)
            yp = outs[0]
            if s == n_seg - 1:
                ys = outs[1]
        st_p.append(sp)
        st_s.append(ss)

    def stack(states, k, heads):
        return jnp.stack([s[k].reshape(s[k].shape[0], s[k].shape[1], heads, HEAD_DIM) for s in states])

    return (yp, ys,
            stack(st_p, 0, A_KV_HEADS), stack(st_p, 1, A_KV_HEADS), stack(st_p, 2, B_HEADS), stack(st_p, 3, B_HEADS),
            stack(st_s, 0, A_KV_HEADS), stack(st_s, 1, A_KV_HEADS), stack(st_s, 2, B_HEADS), stack(st_s, 3, B_HEADS))


PAST_LEN = 2048
CFG_PROMPT = dict(bb=1, tl=512, n_c=8, tl_combine=512, bb_combine=1, n_seg=2)
CFG_SAMPLE = dict(bb=8, tl=64, n_c=1, tl_combine=64, bb_combine=8)


def kernel(x_prompt, x_sample, c_prompt, c_sample, cache_a_k, cache_a_v, cache_b_k, cache_b_v, g_attn, g_mlp,
           w_ada, b_ada, w_in, q_norm_a, k_norm_a, q_norm_b, k_norm_b, sinks_a, rel_table_b, w_branch_a,
           w_branch_b, w_out, w_router, b_router, w_gate_up, b_gate_up, w_down, b_down):
    params = (g_attn, g_mlp, w_ada, b_ada, w_in, q_norm_a, k_norm_a, q_norm_b, k_norm_b, sinks_a, rel_table_b,
              w_branch_a, w_branch_b, w_out, w_router, b_router, w_gate_up, b_gate_up, w_down, b_down)
    return _forward(x_prompt, x_sample, c_prompt, c_sample, cache_a_k, cache_a_v, cache_b_k, cache_b_v,
                    params, PAST_LEN, CFG_PROMPT, CFG_SAMPLE)
```

```python
import functools

import jax
import jax.numpy as jnp
from jax import lax
from jax.experimental import pallas as pl
from jax.experimental.pallas import tpu as pltpu
from jax.experimental.pallas import tpu_sc as plsc

F32 = jnp.float32
BF16 = jnp.bfloat16
I32 = jnp.int32

D_MODEL = 1024
CHUNK = 64
HEAD_DIM = 64
LANES = 128
SUBLANES = 8
A_Q_HEADS = 8
A_KV_HEADS = 2
A_GROUP = A_Q_HEADS // A_KV_HEADS
B_HEADS = 8
A_Q_W = A_Q_HEADS * HEAD_DIM
A_KV_W = A_KV_HEADS * HEAD_DIM
B_W = B_HEADS * HEAD_DIM
QKV_W = A_Q_W + 2 * A_KV_W + 3 * B_W
WINDOW_A = 128
WINDOW_B = 512
BAND_A = WINDOW_A + CHUNK
BAND_B = WINDOW_B + CHUNK
PAD = WINDOW_B
ROT_DIM = HEAD_DIM // 4
ROPE_THETA = 500000.0
REL_BACK = 4 * CHUNK
REL_FWD = CHUNK - 1
N_EXPERTS = 32
TOP_K = 4
D_FF = D_MODEL
SWIGLU_LIMIT = 7.0
SWIGLU_ALPHA = 1.702
ROUTE_BLOCK = 512
FF_TILE = 256
RMS_EPS = 1e-6
NEG_INF = -1e30
HIGH_HALF = -65536
LOG2_E = 1.4426950408889634
HALF_W = D_MODEL // 2
ROW_TILES = HALF_W // LANES
VMEM_LIMIT = 56 * 1024 * 1024
SC_CORES = 2
SC_SUBCORES = 16
SC_WORKERS = SC_CORES * SC_SUBCORES
SC_ROWS = 32
TRASH_ROWS = 8

A_HEAD_ORDER = (0, 4, 1, 5, 2, 6, 3, 7)


def _cparams(*sem):
    return pltpu.CompilerParams(dimension_semantics=sem, vmem_limit_bytes=VMEM_LIMIT)


def _adaln_kernel(c_ref, w_ref, b_ref, o_ref):
    c = c_ref[...]
    a = c * jax.nn.sigmoid(c)
    o_ref[...] = jnp.dot(a, w_ref[...], preferred_element_type=F32,
                         precision=lax.Precision.HIGHEST) + b_ref[...]


def _adaln(c, w_ada, b_ada):
    n_rows = c.shape[0]
    n_out = w_ada.shape[1]
    tn = 1024
    return pl.pallas_call(
        _adaln_kernel,
        grid=(n_out // tn,),
        in_specs=[pl.BlockSpec((n_rows, D_MODEL), lambda j: (0, 0)),
                  pl.BlockSpec((D_MODEL, tn), lambda j: (0, j)),
                  pl.BlockSpec((1, tn), lambda j: (0, j))],
        out_specs=pl.BlockSpec((n_rows, tn), lambda j: (0, j)),
        out_shape=jax.ShapeDtypeStruct((n_rows, n_out), F32),
        compiler_params=_cparams("arbitrary"),
        name="adaln",
    )(c, w_ada, b_ada.reshape(1, n_out))


def _pack_rows(ref, x):
    m = x.shape[0]
    for j in range(ROW_TILES):
        lo = x[:, LANES * j:LANES * (j + 1)].astype(BF16).astype(F32)
        hi = x[:, HALF_W + LANES * j:HALF_W + LANES * (j + 1)].astype(BF16).astype(F32)
        word = (lax.bitcast_convert_type(hi, I32) & HIGH_HALF) | lax.shift_right_logical(
            lax.bitcast_convert_type(lo, I32), 16)
        ref[pl.ds(j, m, stride=ROW_TILES), :] = word


def _unpack_rows(words):
    lo = [lax.bitcast_convert_type(lax.shift_left(w, 16), F32) for w in words]
    hi = [lax.bitcast_convert_type(w & HIGH_HALF, F32) for w in words]
    return lo, hi


def _modulated_norm(x, g, scale, shift):
    ms = jnp.mean(x * x, axis=-1, keepdims=True)
    return (x * lax.rsqrt(ms + RMS_EPS) * g) * (1.0 + scale) + shift


def _head_norms(groups, seg):
    flat = [p for parts, _ in groups for p in parts]
    m = flat[0].shape[0]
    sq = [p * p for p in flat]
    if len(sq) % 2:
        sq.append(sq[-1])
    pairs = [jnp.concatenate(sq[i:i + 2], axis=1) for i in range(0, len(sq), 2)]
    ms = jnp.dot(jnp.concatenate(pairs, axis=0).astype(BF16), seg, preferred_element_type=F32)
    r = lax.rsqrt(ms + RMS_EPS)
    out, n = [], 0
    for parts, gain in groups:
        normed = []
        for p in parts:
            rn = r[(n // 2) * m:(n // 2 + 1) * m, (n % 2) * LANES:(n % 2 + 1) * LANES]
            normed.append(p * rn * gain)
            n += 1
        out.append(normed)
    return out


def _rope(p, cos, sin_lo, sin_hi):
    return p * cos + pltpu.roll(p, LANES - ROT_DIM // 2, 1) * sin_lo + pltpu.roll(p, ROT_DIM // 2, 1) * sin_hi


def _stage1_kernel(x_ref, sh_ref, sc_ref, g_ref, w_ref, gqa_ref, gka_ref, gqb_ref, gkb_ref, seg_ref,
                   cos_ref, sinlo_ref, sinhi_ref,
                   qa_ref, ka_ref, va_ref, qb_ref, kb_ref, vb_ref, ska_ref, sva_ref, skb_ref, svb_ref,
                   *, bb, tl, pad_tiles, n_tiles, sa_rows, sb_rows):
    i = pl.program_id(1)
    m = bb * tl

    if pad_tiles:
        @pl.when(i < pad_tiles)
        def _():
            ka_ref[...] = jnp.zeros_like(ka_ref)
            va_ref[...] = jnp.zeros_like(va_ref)
            kb_ref[...] = jnp.zeros_like(kb_ref)
            vb_ref[...] = jnp.zeros_like(vb_ref)

    @pl.when(i >= pad_tiles)
    def _():
        h = _modulated_norm(x_ref[...], g_ref[...], sc_ref[...], sh_ref[...])
        hb = h.reshape(m, D_MODEL).astype(BF16)
        qkv = jnp.dot(hb, w_ref[...], preferred_element_type=F32)
        seg = seg_ref[...]

        def cols(lo, width):
            return [qkv[:, lo + LANES * j: lo + LANES * (j + 1)] for j in range(width // LANES)]

        cos = jnp.concatenate([cos_ref[...]] * bb, axis=0)
        sin_lo = jnp.concatenate([sinlo_ref[...]] * bb, axis=0)
        sin_hi = jnp.concatenate([sinhi_ref[...]] * bb, axis=0)
        scale = HEAD_DIM ** -0.5 * LOG2_E

        off_b = A_Q_W + 2 * A_KV_W
        qa, ka, qb, kb = _head_norms(
            [(cols(0, A_Q_W), gqa_ref[...]), (cols(A_Q_W, A_KV_W), gka_ref[...]),
             (cols(off_b, B_W), gqb_ref[...]), (cols(off_b + B_W, B_W), gkb_ref[...])], seg)
        qa = [_rope(p, cos, sin_lo, sin_hi) * scale for p in qa]
        ka = [_rope(p, cos, sin_lo, sin_hi) for p in ka]
        va = cols(A_Q_W + A_KV_W, A_KV_W)
        qb = [p * scale for p in qb]
        vb = cols(off_b + 2 * B_W, B_W)

        def put(ref, parts):
            for j, p in enumerate(parts):
                ref[:, :, LANES * j:LANES * (j + 1)] = p.reshape(bb, tl, LANES).astype(ref.dtype)

        put(qa_ref, qa)
        put(ka_ref, ka)
        put(va_ref, va)
        put(qb_ref, qb)
        put(kb_ref, kb)
        put(vb_ref, vb)

        @pl.when(i == n_tiles - 1)
        def _():
            def put_state(ref, parts, rows):
                for j, p in enumerate(parts):
                    ref[:, :, LANES * j:LANES * (j + 1)] = p.reshape(bb, tl, LANES)[:, tl - rows:, :]

            put_state(ska_ref, ka, sa_rows)
            put_state(sva_ref, va, sa_rows)
            put_state(skb_ref, kb, sb_rows)
            put_state(svb_ref, vb, sb_rows)


def _rope_tables(pos):
    inv_freq = ROPE_THETA ** (-jnp.arange(0, ROT_DIM, 2, dtype=F32) / ROT_DIM)
    ang = pos.astype(F32)[:, None] * inv_freq[None, :]
    half = ROT_DIM // 2
    lane = jnp.arange(LANES) % HEAD_DIM
    ang_l = jnp.tile(ang, (1, LANES // half))
    rot = lane < ROT_DIM
    cos = jnp.where(rot[None, :], jnp.cos(ang_l), 1.0)
    sin_lo = jnp.where((lane < half)[None, :], -jnp.sin(ang_l), 0.0)
    sin_hi = jnp.where(((lane >= half) & rot)[None, :], jnp.sin(ang_l), 0.0)
    return cos, sin_lo, sin_hi


def _stage1(x, shift, scale, g_attn, w_qkv, gains, pos, *, bb, tl, pad):
    bsz, length, _ = x.shape
    n_tiles = length // tl
    pad_tiles = pad // tl
    sa_rows, sb_rows = min(WINDOW_A, length), min(WINDOW_B, length)
    assert bsz % bb == 0 and length % tl == 0 and pad % tl == 0 and sb_rows <= tl
    cos, sin_lo, sin_hi = _rope_tables(pos)
    seg_ids = jnp.arange(2 * LANES) // HEAD_DIM
    seg = ((seg_ids[:, None] == seg_ids[None, :]).astype(F32) / HEAD_DIM).astype(BF16)
    gq_a, gk_a, gq_b, gk_b = [jnp.tile(g, LANES // HEAD_DIM).reshape(1, LANES) for g in gains]

    def xmap(b, i):
        return (b, jnp.maximum(i - pad_tiles, 0), 0)

    def const2(b, i):
        return (0, 0)

    def tabmap(b, i):
        return (jnp.maximum(i - pad_tiles, 0), 0)

    def modmap(b, i):
        return (b, 0, 0)

    def kvmap(b, i):
        return (b, i, 0)

    lp = length + pad
    kern = functools.partial(_stage1_kernel, bb=bb, tl=tl, pad_tiles=pad_tiles,
                             n_tiles=n_tiles + pad_tiles, sa_rows=sa_rows, sb_rows=sb_rows)
    out_shapes = (
        jax.ShapeDtypeStruct((bsz, length, A_Q_W), BF16),
        jax.ShapeDtypeStruct((bsz, lp, A_KV_W), BF16),
        jax.ShapeDtypeStruct((bsz, lp, A_KV_W), BF16),
        jax.ShapeDtypeStruct((bsz, length, B_W), BF16),
        jax.ShapeDtypeStruct((bsz, lp, B_W), BF16),
        jax.ShapeDtypeStruct((bsz, lp, B_W), BF16),
        jax.ShapeDtypeStruct((bsz, sa_rows, A_KV_W), F32),
        jax.ShapeDtypeStruct((bsz, sa_rows, A_KV_W), F32),
        jax.ShapeDtypeStruct((bsz, sb_rows, B_W), F32),
        jax.ShapeDtypeStruct((bsz, sb_rows, B_W), F32),
    )
    out_specs = (
        pl.BlockSpec((bb, tl, A_Q_W), xmap),
        pl.BlockSpec((bb, tl, A_KV_W), kvmap),
        pl.BlockSpec((bb, tl, A_KV_W), kvmap),
        pl.BlockSpec((bb, tl, B_W), xmap),
        pl.BlockSpec((bb, tl, B_W), kvmap),
        pl.BlockSpec((bb, tl, B_W), kvmap),
        pl.BlockSpec((bb, sa_rows, A_KV_W), modmap),
        pl.BlockSpec((bb, sa_rows, A_KV_W), modmap),
        pl.BlockSpec((bb, sb_rows, B_W), modmap),
        pl.BlockSpec((bb, sb_rows, B_W), modmap),
    )
    in_specs = [
        pl.BlockSpec((bb, tl, D_MODEL), xmap),
        pl.BlockSpec((bb, 1, D_MODEL), modmap),
        pl.BlockSpec((bb, 1, D_MODEL), modmap),
        pl.BlockSpec((1, D_MODEL), const2),
        pl.BlockSpec((D_MODEL, QKV_W), const2),
        pl.BlockSpec((1, LANES), const2),
        pl.BlockSpec((1, LANES), const2),
        pl.BlockSpec((1, LANES), const2),
        pl.BlockSpec((1, LANES), const2),
        pl.BlockSpec((2 * LANES, 2 * LANES), const2),
        pl.BlockSpec((tl, LANES), tabmap),
        pl.BlockSpec((tl, LANES), tabmap),
        pl.BlockSpec((tl, LANES), tabmap),
    ]
    return pl.pallas_call(
        kern,
        grid=(bsz // bb, n_tiles + pad_tiles),
        in_specs=in_specs,
        out_specs=out_specs,
        out_shape=out_shapes,
        compiler_params=_cparams("arbitrary", "arbitrary"),
        name="stage1",
    )(x, shift, scale, g_attn.reshape(1, D_MODEL), w_qkv, gq_a, gk_a, gq_b, gk_b, seg, cos, sin_lo, sin_hi)


def _nt_dot(a, b):
    return lax.dot_general(a, b, (((1,), (1,)), ((), ())), preferred_element_type=F32)


def _attn_kernel(sink_ref, qa_ref, ka_ref, va_ref, qb_ref, kb_ref, vb_ref, bias_ref, *rest, n_c, cached):
    if cached:
        cka_ref, cva_ref, ckb_ref, cvb_ref, oa_ref, ob_ref = rest
    else:
        oa_ref, ob_ref = rest
    mask_prefix = not cached
    qi = pl.program_id(1)
    lane = lax.broadcasted_iota(I32, (CHUNK, LANES), 1)
    low = lane < HEAD_DIM
    half_mask = (jnp.where(low, 1.0, 0.0).astype(BF16), jnp.where(low, 0.0, 1.0).astype(BF16))
    n_pairs = B_W // LANES
    rows_b = n_pairs * 2 * CHUNK
    rows_a = A_Q_HEADS * CHUNK
    key_b = lax.broadcasted_iota(I32, (rows_b, BAND_B), 1)
    key_a = lax.broadcasted_iota(I32, (rows_a, BAND_A), 1)
    row_a = lax.broadcasted_iota(I32, (rows_a, 1), 0) // CHUNK

    sink_col = jnp.zeros((rows_a, 1), F32)
    for r in range(A_Q_HEADS):
        sink_col = jnp.where(row_a == r, sink_ref[r], sink_col)

    def chunk(c, masked):
        ci = qi * n_c + c
        row0 = pl.multiple_of(ci * CHUNK, CHUNK)
        rows = slice(c * CHUNK, (c + 1) * CHUNK)
        first_valid = PAD - ci * CHUNK

        def band(new_ref, cache_ref, start, size, cg):
            if cached:
                return jnp.concatenate([cache_ref[0, :, cg].astype(BF16), new_ref[0, :, cg]], axis=0)
            return new_ref[0, pl.ds(start, size), cg]

        s_parts = []
        for j in range(n_pairs):
            cg = slice(LANES * j, LANES * (j + 1))
            q = qb_ref[0, rows, cg]
            qs = jnp.concatenate([q * half_mask[0], q * half_mask[1]], axis=0)
            k = band(kb_ref, ckb_ref if cached else None, row0, BAND_B, cg)
            s_parts.append(_nt_dot(qs, k) + bias_ref[j])
        s = jnp.concatenate(s_parts, axis=0)
        if masked:
            s = jnp.where(key_b >= first_valid, s, NEG_INF)
        mx = jnp.max(s, axis=-1, keepdims=True)
        e = jnp.exp2(s - mx)
        inv = 1.0 / jnp.sum(e, axis=-1, keepdims=True)
        p_b = e.astype(BF16)
        for j in range(n_pairs):
            cg = slice(LANES * j, LANES * (j + 1))
            v = band(vb_ref, cvb_ref if cached else None, row0, BAND_B, cg)
            rj = slice(2 * CHUNK * j, 2 * CHUNK * (j + 1))
            o = jnp.dot(p_b[rj], v, preferred_element_type=F32) * inv[rj]
            ob_ref[0, rows, cg] = jnp.where(low, o[:CHUNK], o[CHUNK:]).astype(ob_ref.dtype)

        row_a0 = pl.multiple_of(row0 + (WINDOW_B - WINDOW_A), CHUNK)
        all_lanes = slice(0, LANES)
        k = band(ka_ref, cka_ref if cached else None, row_a0, BAND_A, all_lanes)
        v = band(va_ref, cva_ref if cached else None, row_a0, BAND_A, all_lanes)
        q = jnp.concatenate(
            [qa_ref[0, rows, LANES * j:LANES * (j + 1)] * half_mask[p]
             for p in range(A_KV_HEADS) for j in range(A_GROUP)], axis=0)
        s = _nt_dot(q, k)
        if masked:
            s = jnp.where(key_a >= first_valid - (WINDOW_B - WINDOW_A), s, NEG_INF)
        mx = jnp.maximum(jnp.max(s, axis=-1, keepdims=True), sink_col)
        e = jnp.exp2(s - mx)
        inv = 1.0 / (jnp.sum(e, axis=-1, keepdims=True) + jnp.exp2(sink_col - mx))
        o = jnp.dot(e.astype(BF16), v, preferred_element_type=F32) * inv
        for j in range(A_GROUP):
            lo = o[j * CHUNK:(j + 1) * CHUNK]
            hi = o[(A_GROUP + j) * CHUNK:(A_GROUP + j + 1) * CHUNK]
            oa_ref[0, rows, LANES * j:LANES * (j + 1)] = jnp.where(low, lo, hi).astype(oa_ref.dtype)

    steps_with_pad = pl.cdiv(PAD // CHUNK, n_c) if mask_prefix else 0
    if steps_with_pad:
        @pl.when(qi < steps_with_pad)
        def _():
            for c in range(n_c):
                chunk(c, True)

    @pl.when(qi >= steps_with_pad)
    def _():
        for c in range(n_c):
            chunk(c, False)


def _attention(qa, ka, va, qb, kb, vb, sinks, bias, *, n_c, caches=None):
    bsz, length, _ = qa.shape
    lp = ka.shape[1]
    qt = n_c * CHUNK
    cached = caches is not None
    assert length % qt == 0 and lp == (length if cached else length + PAD)
    assert not cached or (length == CHUNK and caches[0].shape[1] == WINDOW_A and caches[2].shape[1] == WINDOW_B)

    def qmap(b, i):
        return (b, i, 0)

    def kmap(b, i):
        return (b, 0, 0)

    in_specs = [
        pl.BlockSpec(memory_space=pltpu.SMEM),
        pl.BlockSpec((1, qt, A_Q_W), qmap),
        pl.BlockSpec((1, lp, A_KV_W), kmap),
        pl.BlockSpec((1, lp, A_KV_W), kmap),
        pl.BlockSpec((1, qt, B_W), qmap),
        pl.BlockSpec((1, lp, B_W), kmap),
        pl.BlockSpec((1, lp, B_W), kmap),
        pl.BlockSpec((B_HEADS // 2, 2 * CHUNK, BAND_B), lambda b, i: (0, 0, 0)),
    ]
    args = [sinks, qa, ka, va, qb, kb, vb, bias]
    if cached:
        in_specs += [pl.BlockSpec((1,) + c.shape[1:], kmap) for c in caches]
        args += list(caches)
    kern = functools.partial(_attn_kernel, n_c=n_c, cached=cached)
    return pl.pallas_call(
        kern,
        grid=(bsz, length // qt),
        in_specs=in_specs,
        out_specs=(pl.BlockSpec((1, qt, A_Q_W), qmap), pl.BlockSpec((1, qt, B_W), qmap)),
        out_shape=(jax.ShapeDtypeStruct((bsz, length, A_Q_W), BF16),
                   jax.ShapeDtypeStruct((bsz, length, B_W), BF16)),
        compiler_params=_cparams("arbitrary", "arbitrary"),
        name="attention",
    )(*args)


def _merge_kernel(x_ref, oa_ref, ob_ref, sha_ref, sca_ref, gta_ref, shm_ref, scm_ref, ga_ref, gm_ref,
                  wg_ref, wa_ref, wb_ref, wo_ref, wrh_ref, wrl_ref, br_ref,
                  x1_ref, h2_ref, ri_ref, rw_ref, cnt_ref, carry_ref, logit_ref, *, bb, tl, seg_tiles):
    m = bb * tl
    step = pl.program_id(0)

    @pl.when(step == 0)
    def _():
        carry_ref[...] = jnp.zeros_like(carry_ref)
        logit_ref[...] = jnp.zeros_like(logit_ref)

    logits = logit_ref[...]
    lane = lax.broadcasted_iota(I32, (m, LANES), 1)
    lane_f = lane.astype(F32)
    work = logits
    top_v, top_i = [], []
    sel_f = jnp.zeros((m, LANES), F32)
    for _ in range(TOP_K):
        mx = jnp.max(work, axis=-1, keepdims=True)
        idx = jnp.min(jnp.where(work == mx, lane_f, float(LANES)), axis=-1, keepdims=True)
        hit = lane_f == idx
        top_v.append(mx)
        top_i.append(idx)
        sel_f = jnp.where(hit, 1.0, sel_f)
        work = jnp.where(hit, -jnp.inf, work)
    ex = [jnp.exp(v - top_v[0]) for v in top_v]
    inv = 1.0 / (ex[0] + ex[1] + ex[2] + ex[3])

    r_i = lax.broadcasted_iota(I32, (m, m), 0)
    c_i = lax.broadcasted_iota(I32, (m, m), 1)
    before = jnp.where(r_i > c_i, 1.0, 0.0).astype(BF16)
    keep = jnp.where(jnp.maximum(step - 1, 0) % seg_tiles == 0, 0.0, 1.0)
    carry_in = carry_ref[...] * keep
    pos = jnp.dot(before, sel_f.astype(BF16), preferred_element_type=F32) + carry_in
    live = jnp.where(step > 0, 1.0, 0.0)
    carry = carry_in + live * jnp.sum(sel_f, axis=0, keepdims=True)
    carry_ref[...] = carry
    cnt_ref[0] = carry

    ri_f = jnp.zeros((m, LANES), F32)
    rw = jnp.zeros((m, LANES), F32)
    for k in range(TOP_K):
        rank = jnp.sum(jnp.where(lane_f == top_i[k], pos, 0.0), axis=-1, keepdims=True)
        ri_f = jnp.where(lane == k, top_i[k], ri_f)
        ri_f = jnp.where(lane == TOP_K + k, rank, ri_f)
        rw = jnp.where(lane == k, ex[k] * inv, rw)
    ri_ref[0] = jnp.transpose(ri_f)[:2 * TOP_K].astype(I32)
    rw_ref[...] = rw.reshape(bb, tl, LANES)

    x = x_ref[...]
    h = _modulated_norm(x, ga_ref[...], sca_ref[...], sha_ref[...])
    hb = h.reshape(m, D_MODEL).astype(BF16)
    gates = jnp.dot(hb, wg_ref[...], preferred_element_type=F32)
    ya = jnp.dot(oa_ref[...].reshape(m, A_Q_W), wa_ref[...], preferred_element_type=F32)
    yb = jnp.dot(ob_ref[...].reshape(m, B_W), wb_ref[...], preferred_element_type=F32)
    mix = jax.nn.sigmoid(gates[:, :D_MODEL]) * ya + jax.nn.sigmoid(gates[:, D_MODEL:]) * yb
    z = jnp.dot(mix.astype(BF16), wo_ref[...], preferred_element_type=F32)
    x1 = x + gta_ref[...] * z.reshape(bb, tl, D_MODEL)
    x1_ref[...] = x1

    h2 = _modulated_norm(x1, gm_ref[...], scm_ref[...], shm_ref[...]).reshape(m, D_MODEL)
    _pack_rows(h2_ref, h2)

    h2_hi = h2.astype(BF16)
    h2_lo = (h2 - h2_hi.astype(F32)).astype(BF16)
    wrh = wrh_ref[...]
    logit_ref[...] = (jnp.dot(h2_hi, wrh, preferred_element_type=F32)
                      + jnp.dot(h2_lo, wrh, preferred_element_type=F32)
                      + jnp.dot(h2_hi, wrl_ref[...], preferred_element_type=F32)) + br_ref[...]


def _merge(x, oa, ob, mods, g_attn, g_mlp, wg, wa, wb, wo, wr, br, *, bb, tl, b0, nb):
    _, length, _ = x.shape
    n_tok = nb * length
    m = bb * tl
    assert bb == 1 or tl == length
    sh_a, sc_a, gt_a, sh_m, sc_m = mods
    n_l = length // tl
    n_tiles = (nb // bb) * n_l
    blk0 = b0 // bb
    wr_hi = wr.astype(BF16)
    wr_lo = (wr - wr_hi.astype(F32)).astype(BF16)

    def cur(s):
        return jnp.minimum(s, n_tiles - 1)

    def prev(s):
        return jnp.maximum(s - 1, 0)

    def xmap(s):
        return (blk0 + cur(s) // n_l, cur(s) % n_l, 0)

    def modmap(s):
        return (blk0 + cur(s) // n_l, 0, 0)

    def omap(s):
        return (cur(s) // n_l, cur(s) % n_l, 0)

    def const2(s):
        return (0, 0)

    def flatmap(s):
        return (cur(s), 0)

    def routemap(s):
        return (prev(s) // n_l, prev(s) % n_l, 0)

    modspec = pl.BlockSpec((bb, 1, D_MODEL), modmap)
    kern = functools.partial(_merge_kernel, bb=bb, tl=tl, seg_tiles=n_tiles)
    return pl.pallas_call(
        kern,
        grid=(n_tiles + 1,),
        in_specs=[
            pl.BlockSpec((bb, tl, D_MODEL), xmap),
            pl.BlockSpec((bb, tl, A_Q_W), xmap),
            pl.BlockSpec((bb, tl, B_W), xmap),
            modspec, modspec, modspec, modspec, modspec,
            pl.BlockSpec((1, D_MODEL), const2),
            pl.BlockSpec((1, D_MODEL), const2),
            pl.BlockSpec((D_MODEL, 2 * D_MODEL), const2),
            pl.BlockSpec((A_Q_W, D_MODEL), const2),
            pl.BlockSpec((B_W, D_MODEL), const2),
            pl.BlockSpec((D_MODEL, D_MODEL), const2),
            pl.BlockSpec((D_MODEL, LANES), const2),
            pl.BlockSpec((D_MODEL, LANES), const2),
            pl.BlockSpec((1, LANES), const2),
        ],
        out_specs=(
            pl.BlockSpec((bb, tl, D_MODEL), omap),
            pl.BlockSpec((m * ROW_TILES, LANES), flatmap),
            pl.BlockSpec((1, 2 * TOP_K, m), lambda s: (prev(s), 0, 0)),
            pl.BlockSpec((bb, tl, LANES), routemap),
            pl.BlockSpec((1, 1, LANES), lambda s: (0, 0, 0)),
        ),
        out_shape=(
            jax.ShapeDtypeStruct((nb, length, D_MODEL), F32),
            jax.ShapeDtypeStruct((n_tok * ROW_TILES, LANES), I32),
            jax.ShapeDtypeStruct((n_tiles, 2 * TOP_K, m), I32),
            jax.ShapeDtypeStruct((nb, length, LANES), F32),
            jax.ShapeDtypeStruct((1, 1, LANES), F32),
        ),
        scratch_shapes=[pltpu.VMEM((1, LANES), F32), pltpu.VMEM((m, LANES), F32)],
        compiler_params=_cparams("arbitrary"),
        name="merge",
    )(x, oa, ob, sh_a, sc_a, gt_a, sh_m, sc_m, g_attn.reshape(1, D_MODEL), g_mlp.reshape(1, D_MODEL),
      wg, wa, wb, wo, wr_hi, wr_lo, br)


def _sc_mesh():
    return plsc.VectorSubcoreMesh(core_axis_name="c", subcore_axis_name="s",
                                  num_cores=SC_CORES, num_subcores=SC_SUBCORES)


def _sc_worker():
    return lax.axis_index("c") * SC_SUBCORES + lax.axis_index("s")


def _sc_params():
    return pltpu.CompilerParams(use_tc_tiling_on_sc=True)


def _part_chunks(parts):
    assert all(p % (2 * SC_WORKERS * SC_ROWS) == 0 for p in parts)
    chunks = [p // (SC_WORKERS * SC_ROWS) for p in parts]
    starts = [sum(chunks[:i]) for i in range(len(chunks))]
    return chunks, starts


def _dispatch(h2_parts, idx_flat, pad_flat, n_rows):
    n_parts = len(h2_parts)
    chunks, starts = _part_chunks([h.shape[0] for h in h2_parts])
    n_idx = sum(chunks) * TOP_K * SC_ROWS
    n_pad = pad_flat.shape[0] // SC_WORKERS
    pad_chunks = n_pad // SC_ROWS
    assert n_pad % SC_ROWS == 0
    zeros = jnp.zeros((SC_ROWS, ROW_TILES, LANES), I32)

    def body(*refs):
        h2_hbms = refs[:n_parts]
        idx_hbm, pad_hbm, zeros_hbm, xs_hbm, idx_v, pad_v, buf0, buf1, ls0, ls1, ss0, ss1 = refs[n_parts:]
        wid = _sc_worker()
        bufs, lsems, ssems = (buf0, buf1), (ls0, ls1), (ss0, ss1)
        pltpu.sync_copy(idx_hbm.at[pl.ds(pl.multiple_of(wid * n_idx, SC_ROWS), n_idx)], idx_v)
        pltpu.sync_copy(pad_hbm.at[pl.ds(pl.multiple_of(wid * n_pad, SC_ROWS), n_pad)], pad_v)

        pltpu.sync_copy(zeros_hbm, buf0)
        for c in range(pad_chunks):
            pltpu.async_copy(buf0, xs_hbm.at[pad_v.at[pl.ds(c * SC_ROWS, SC_ROWS)]], ss0)
        for c in range(pad_chunks):
            pltpu.make_async_copy(buf0, xs_hbm.at[pad_v.at[pl.ds(c * SC_ROWS, SC_ROWS)]], ss0).wait()

        def run_part(h2_hbm, n_chunks, c0):
            base = wid * (n_chunks * SC_ROWS)

            def load(g, b):
                src = h2_hbm.at[pl.ds(pl.multiple_of(base + g * SC_ROWS, SC_ROWS), SC_ROWS)]
                return pltpu.make_async_copy(src, bufs[b], lsems[b])

            def scatter(g, k, b):
                off = pl.multiple_of(((c0 + g) * TOP_K + k) * SC_ROWS, SC_ROWS)
                return pltpu.make_async_copy(bufs[b], xs_hbm.at[idx_v.at[pl.ds(off, SC_ROWS)]], ssems[b])

            load(0, 0).start()

            @pl.loop(0, n_chunks, step=2)
            def _(g0):
                for b in range(2):
                    g = g0 + b
                    load(g, b).wait()

                    @pl.when(g >= 1)
                    def _():
                        for k in range(TOP_K):
                            scatter(g - 1, k, 1 - b).wait()

                    @pl.when(g + 1 < n_chunks)
                    def _():
                        load(g + 1, 1 - b).start()

                    for k in range(TOP_K):
                        scatter(g, k, b).start()

            for k in range(TOP_K):
                scatter(n_chunks - 1, k, (n_chunks - 1) % 2).wait()

        for h2_hbm, n_chunks, c0 in zip(h2_hbms, chunks, starts):
            run_part(h2_hbm, n_chunks, c0)

    call = pl.kernel(
        body,
        out_type=jax.ShapeDtypeStruct((n_rows + TRASH_ROWS, ROW_TILES, LANES), I32),
        mesh=_sc_mesh(),
        scratch_types=[pltpu.VMEM((n_idx,), I32), pltpu.VMEM((n_pad,), I32),
                       pltpu.VMEM((SC_ROWS, ROW_TILES, LANES), I32), pltpu.VMEM((SC_ROWS, ROW_TILES, LANES), I32),
                       pltpu.SemaphoreType.DMA, pltpu.SemaphoreType.DMA,
                       pltpu.SemaphoreType.DMA, pltpu.SemaphoreType.DMA],
        compiler_params=_sc_params(),
        name="dispatch",
    )
    return call(*h2_parts, idx_flat, pad_flat, zeros)


def _gather_back(y_rows, idx_flat, parts):
    n_parts = len(parts)
    chunks, starts = _part_chunks(parts)
    n_idx = sum(chunks) * TOP_K * SC_ROWS

    def body(*refs):
        y_hbm, idx_hbm = refs[:2]
        yg_hbms = refs[2:2 + n_parts]
        idx_v, buf0, buf1, gs0, gs1, ws0, ws1 = refs[2 + n_parts:]
        wid = _sc_worker()
        bufs, gsems, wsems = (buf0, buf1), (gs0, gs1), (ws0, ws1)
        pltpu.sync_copy(idx_hbm.at[pl.ds(pl.multiple_of(wid * n_idx, SC_ROWS), n_idx)], idx_v)

        def run_part(yg_hbm, n_chunks, c0):
            base = wid * (n_chunks * SC_ROWS)
            n_items = n_chunks * TOP_K

            def gather(i, b):
                off = pl.multiple_of((c0 * TOP_K + i) * SC_ROWS, SC_ROWS)
                return pltpu.make_async_copy(y_hbm.at[idx_v.at[pl.ds(off, SC_ROWS)]], bufs[b], gsems[b])

            def write(i, b):
                g, k = i // TOP_K, i % TOP_K
                dst = yg_hbm.at[k, pl.ds(pl.multiple_of(base + g * SC_ROWS, SC_ROWS), SC_ROWS)]
                return pltpu.make_async_copy(bufs[b], dst, wsems[b])

            gather(0, 0).start()

            @pl.loop(0, n_items, step=2)
            def _(i0):
                for b in range(2):
                    i = i0 + b
                    gather(i, b).wait()

                    @pl.when(i >= 1)
                    def _():
                        write(i - 1, 1 - b).wait()

                    @pl.when(i + 1 < n_items)
                    def _():
                        gather(i + 1, 1 - b).start()

                    write(i, b).start()

            write(n_items - 1, (n_items - 1) % 2).wait()

        for yg_hbm, n_chunks, c0 in zip(yg_hbms, chunks, starts):
            run_part(yg_hbm, n_chunks, c0)

    call = pl.kernel(
        body,
        out_type=tuple(jax.ShapeDtypeStruct((TOP_K, p, ROW_TILES, LANES), I32) for p in parts),
        mesh=_sc_mesh(),
        scratch_types=[pltpu.VMEM((n_idx,), I32),
                       pltpu.VMEM((SC_ROWS, ROW_TILES, LANES), I32), pltpu.VMEM((SC_ROWS, ROW_TILES, LANES), I32),
                       pltpu.SemaphoreType.DMA, pltpu.SemaphoreType.DMA,
                       pltpu.SemaphoreType.DMA, pltpu.SemaphoreType.DMA],
        compiler_params=_sc_params(),
        name="gather_back",
    )
    return call(y_rows, idx_flat)


def _expert_kernel(blk_ref, used_ref, next_ref, slot_ref, xs_ref, wgu_hbm, bgu_ref, wd_hbm, bd_ref, y_ref,
                   wgu_buf, wd_buf, wgu_bf, wd_bf, sem_gu, sem_d):
    r = pl.program_id(0)
    e = blk_ref[r]
    prev = blk_ref[jnp.maximum(r - 1, 0)]
    active = r < used_ref[0]
    slot = slot_ref[r]

    def fetch(expert, s):
        return (pltpu.make_async_copy(wgu_hbm.at[expert], wgu_buf.at[s], sem_gu.at[s]),
                pltpu.make_async_copy(wd_hbm.at[expert], wd_buf.at[s], sem_d.at[s]))

    @pl.when(jnp.logical_and(active, r == 0))
    def _():
        for copy in fetch(e, slot):
            copy.start()

    @pl.when(jnp.logical_and(active, jnp.logical_or(r == 0, e != prev)))
    def _():
        for copy in fetch(e, slot):
            copy.wait()
        wgu_bf[...] = wgu_buf[slot].astype(BF16)
        wd_bf[...] = wd_buf[slot].astype(BF16)
        nxt = next_ref[r]

        @pl.when(nxt >= 0)
        def _():
            for copy in fetch(nxt, 1 - slot):
                copy.start()

    @pl.when(active)
    def _():
        lo, hi = _unpack_rows([xs_ref[pl.ds(j, ROUTE_BLOCK, stride=ROW_TILES), :] for j in range(ROW_TILES)])
        x = jnp.concatenate(lo + hi, axis=1).astype(BF16)
        acts = []
        for j in range(D_FF // FF_TILE):
            gc = slice(FF_TILE * j, FF_TILE * (j + 1))
            uc = slice(D_FF + FF_TILE * j, D_FF + FF_TILE * (j + 1))
            g = jnp.dot(x, wgu_bf[:, gc], preferred_element_type=F32) + bgu_ref[0, :, gc]
            u = jnp.dot(x, wgu_bf[:, uc], preferred_element_type=F32) + bgu_ref[0, :, uc]
            gate = jnp.minimum(g, SWIGLU_LIMIT)
            up = jnp.clip(u, -SWIGLU_LIMIT, SWIGLU_LIMIT)
            glu = gate * jax.nn.sigmoid(gate * SWIGLU_ALPHA)
            acts.append(((up + 1.0) * glu).astype(BF16))
        act = jnp.concatenate(acts, axis=1)
        y = jnp.dot(act, wd_bf[...], preferred_element_type=F32) + bd_ref[0]
        _pack_rows(y_ref, y)


def _experts(xs_rows, blk_expert, n_used, blk_next, blk_slot, w_gate_up, b_gate_up, w_down, b_down):
    n_blocks = blk_expert.shape[0]
    rb = ROUTE_BLOCK * ROW_TILES

    def rowmap(r, blk, used, nxt, slot):
        return (jnp.minimum(r, used[0] - 1), 0)

    def bmap(r, blk, used, nxt, slot):
        return (blk[jnp.minimum(r, used[0] - 1)], 0, 0)

    return pl.pallas_call(
        _expert_kernel,
        grid_spec=pltpu.PrefetchScalarGridSpec(
            num_scalar_prefetch=4,
            grid=(n_blocks,),
            in_specs=[
                pl.BlockSpec((rb, LANES), rowmap),
                pl.BlockSpec(memory_space=pl.ANY),
                pl.BlockSpec((1, 1, 2 * D_FF), bmap),
                pl.BlockSpec(memory_space=pl.ANY),
                pl.BlockSpec((1, 1, D_MODEL), bmap),
            ],
            out_specs=pl.BlockSpec((rb, LANES), rowmap),
            scratch_shapes=[pltpu.VMEM((2, D_MODEL, 2 * D_FF), F32), pltpu.VMEM((2, D_FF, D_MODEL), F32),
                            pltpu.VMEM((D_MODEL, 2 * D_FF), BF16), pltpu.VMEM((D_FF, D_MODEL), BF16),
                            pltpu.SemaphoreType.DMA((2,)), pltpu.SemaphoreType.DMA((2,))],
        ),
        out_shape=jax.ShapeDtypeStruct(xs_rows.shape, I32),
        compiler_params=_cparams("arbitrary"),
        name="experts",
    )(blk_expert, n_used, blk_next, blk_slot, xs_rows, w_gate_up, b_gate_up.reshape(N_EXPERTS, 1, 2 * D_FF),
      w_down, b_down.reshape(N_EXPERTS, 1, D_MODEL))


def _combine_kernel(x1_ref, gt_ref, rw_ref, yg_ref, *rest, bb, tl):
    o_ref = rest[-1]
    m = bb * tl
    rw = rw_ref[...].reshape(m, LANES)
    w = [rw[:, k:k + 1] for k in range(TOP_K)]
    gt = gt_ref[...]
    for j in range(ROW_TILES):
        acc_lo = jnp.zeros((m, LANES), F32)
        acc_hi = jnp.zeros((m, LANES), F32)
        for k in range(TOP_K):
            lo, hi = _unpack_rows([yg_ref[k, pl.ds(j, m, stride=ROW_TILES), :]])
            acc_lo = acc_lo + w[k] * lo[0]
            acc_hi = acc_hi + w[k] * hi[0]
        for acc, c0 in ((acc_lo, LANES * j), (acc_hi, HALF_W + LANES * j)):
            cg = slice(c0, c0 + LANES)
            o_ref[:, :, cg] = x1_ref[:, :, cg] + gt[:, :, cg] * acc.reshape(bb, tl, LANES)


def _combine(x1, gt_m, rw, yg, y_prev, *, bb, tl, b0, bsz):
    nb, length, _ = x1.shape
    m = bb * tl
    assert bb == 1 or tl == length
    n_l = length // tl
    blk0 = b0 // bb

    def smap(b, i):
        return (b, i, 0)

    def xmap(b, i):
        return (blk0 + b, i, 0)

    in_specs = [
        pl.BlockSpec((bb, tl, D_MODEL), smap),
        pl.BlockSpec((bb, 1, D_MODEL), lambda b, i: (blk0 + b, 0, 0)),
        pl.BlockSpec((bb, tl, LANES), smap),
        pl.BlockSpec((TOP_K, m * ROW_TILES, LANES), lambda b, i: (0, b * n_l + i, 0)),
    ]
    args = [x1, gt_m, rw, yg]
    aliases = {}
    if y_prev is not None:
        in_specs.append(pl.BlockSpec(memory_space=pl.ANY))
        args.append(y_prev)
        aliases = {len(args) - 1: 0}
    kern = functools.partial(_combine_kernel, bb=bb, tl=tl)
    return pl.pallas_call(
        kern,
        grid=(nb // bb, n_l),
        in_specs=in_specs,
        out_specs=pl.BlockSpec((bb, tl, D_MODEL), xmap),
        out_shape=jax.ShapeDtypeStruct((bsz, length, D_MODEL), F32),
        input_output_aliases=aliases,
        compiler_params=_cparams("arbitrary", "arbitrary"),
        name="combine",
    )(*args)


def _route_plan(routes):
    expert_ids = jnp.arange(N_EXPERTS, dtype=I32)
    part_counts = [c.reshape(-1)[:N_EXPERTS].astype(I32) for _, c in routes]
    counts = sum(part_counts)
    padded = (counts + ROUTE_BLOCK - 1) // ROUTE_BLOCK * ROUTE_BLOCK
    pends = jnp.cumsum(padded)
    pstarts = pends - padded
    n_assign = sum(ri.shape[0] * ri.shape[2] for ri, _ in routes) * TOP_K
    n_blocks = n_assign // ROUTE_BLOCK + N_EXPERTS
    n_rows = n_blocks * ROUTE_BLOCK
    lists = []
    first = jnp.zeros_like(counts)
    for (ri, _), cnt in zip(routes, part_counts):
        n_tiles, _, m = ri.shape
        idx, rank = ri[:, :TOP_K], ri[:, TOP_K:]
        dest = jnp.sum(jnp.where(idx[..., None] == expert_ids, pstarts + first, 0), axis=-1) + rank
        dest = dest.astype(I32).reshape(n_tiles, TOP_K, m // SC_ROWS, SC_ROWS).transpose(0, 2, 1, 3)
        lists.append(dest.reshape(SC_WORKERS, n_tiles * m // (SC_WORKERS * SC_ROWS), TOP_K, SC_ROWS))
        first = first + cnt
    idx_flat = jnp.concatenate(lists, axis=1).reshape(-1)
    blk_start = jnp.arange(n_blocks, dtype=I32) * ROUTE_BLOCK
    blk_expert = jnp.minimum(jnp.sum((pends[None, :] <= blk_start[:, None]).astype(I32), axis=1), N_EXPERTS - 1)
    n_used = (pends[-1] // ROUTE_BLOCK).reshape(1)
    nonempty = padded > 0
    later = nonempty[None, :] & (expert_ids[None, :] > expert_ids[:, None])
    next_e = jnp.min(jnp.where(later, expert_ids[None, :], N_EXPERTS), axis=1)
    next_e = jnp.where(next_e == N_EXPERTS, -1, next_e)
    slot_e = (jnp.cumsum(nonempty.astype(I32)) - 1) % 2
    own = blk_expert[:, None] == expert_ids[None, :]
    blk_next = jnp.sum(jnp.where(own, next_e[None, :], 0), axis=1)
    blk_slot = jnp.sum(jnp.where(own, slot_e[None, :], 0), axis=1)
    r = jnp.arange(ROUTE_BLOCK, dtype=I32)[None, :]
    pad_flat = jnp.where(r < (padded - counts)[:, None], (pstarts + counts)[:, None] + r, n_rows)
    blocks = (blk_expert.astype(I32), n_used.astype(I32), blk_next.astype(I32), blk_slot.astype(I32))
    return idx_flat, pad_flat.astype(I32).reshape(-1), blocks, n_rows


def _front(x, mods, pos, caches, weights, *, bb, tl, n_c):
    g_attn, w_qkv, gains, sinks, bias = weights[0], weights[2], weights[3], weights[4], weights[5]
    sh_a, sc_a = mods[0], mods[1]
    pad = 0 if caches is not None else PAD
    qa, ka, va, qb, kb, vb, ska, sva, skb, svb = _stage1(
        x, sh_a, sc_a, g_attn, w_qkv, gains, pos, bb=bb, tl=tl, pad=pad)
    oa, ob = _attention(qa, ka, va, qb, kb, vb, sinks, bias, n_c=n_c, caches=caches)
    return oa, ob, (ska, sva, skb, svb)


def _moe_segment(parts, weights):
    (g_attn, g_mlp, _, _, _, _, wg, wa, wb, wo, wr, br, w_gate_up, b_gate_up, w_down, b_down) = weights
    merged = []
    for p in parts:
        sh_a, sc_a, gt_a, sh_m, sc_m, _ = p["mods"]
        merged.append(_merge(p["x"], p["oa"], p["ob"], (sh_a, sc_a, gt_a, sh_m, sc_m), g_attn, g_mlp,
                             wg, wa, wb, wo, wr, br, bb=p["cfg"]["bb"], tl=p["cfg"]["tl"], b0=p["b0"], nb=p["nb"]))
    toks = [m[0].shape[0] * m[0].shape[1] for m in merged]
    idx_flat, pad_flat, blocks, n_rows = _route_plan([(m[2], m[4]) for m in merged])
    xs = _dispatch([m[1].reshape(t, ROW_TILES, LANES) for m, t in zip(merged, toks)], idx_flat, pad_flat, n_rows)
    y_rows = _experts(xs.reshape(-1, LANES), *blocks, w_gate_up, b_gate_up, w_down, b_down)
    ygs = _gather_back(y_rows.reshape(-1, ROW_TILES, LANES), idx_flat, toks)
    outs = []
    for p, m, yg, t in zip(parts, merged, ygs, toks):
        outs.append(_combine(m[0], p["mods"][5], m[3], yg.reshape(TOP_K, t * ROW_TILES, LANES), p["y"],
                             bb=p["cfg"]["bb_combine"], tl=p["cfg"]["tl_combine"], b0=p["b0"],
                             bsz=p["x"].shape[0]))
    return outs


def _rel_bias_table(rel_table):
    n_rel = rel_table.shape[1]
    ext = jnp.concatenate(
        [rel_table, jnp.broadcast_to(rel_table[:, -1:], (B_HEADS, CHUNK + BAND_B - n_rel))], axis=1)
    rows = jnp.stack([ext[:, i:i + BAND_B] for i in range(CHUNK)], axis=1)[..., ::-1]
    return rows.astype(F32).reshape(B_HEADS // 2, 2 * CHUNK, BAND_B)


def _prep_weights(g_attn, g_mlp, w_in, q_norm_a, k_norm_a, q_norm_b, k_norm_b, sinks_a, rel_table_b,
                  w_branch_a, w_branch_b, w_out, w_router, b_router, w_gate_up, b_gate_up, w_down, b_down):
    order = jnp.asarray(A_HEAD_ORDER)
    w_qa = w_in[:, :A_Q_W].reshape(D_MODEL, A_Q_HEADS, HEAD_DIM)[:, order].reshape(D_MODEL, A_Q_W)
    w_qkv = jnp.concatenate([w_qa, w_in[:, A_Q_W:QKV_W]], axis=1).astype(BF16)
    wg = w_in[:, QKV_W:].astype(BF16)
    wa = w_branch_a.reshape(A_Q_HEADS, HEAD_DIM, D_MODEL)[order].reshape(A_Q_W, D_MODEL).astype(BF16)
    wb = w_branch_b.astype(BF16)
    wo = w_out.astype(BF16)
    wr = jnp.zeros((D_MODEL, LANES), F32).at[:, :N_EXPERTS].set(w_router)
    br = jnp.full((1, LANES), NEG_INF, F32).at[0, :N_EXPERTS].set(b_router)
    bias = _rel_bias_table(rel_table_b) * LOG2_E
    gains = (q_norm_a, k_norm_a, q_norm_b, k_norm_b)
    return (g_attn, g_mlp, w_qkv, gains, sinks_a.astype(F32) * LOG2_E, bias, wg, wa, wb, wo, wr, br,
            w_gate_up, b_gate_up, w_down, b_down)


def _forward(x_prompt, x_sample, c_prompt, c_sample, cache_a_k, cache_a_v, cache_b_k, cache_b_v, params,
             past_len, cfg_prompt, cfg_sample):
    (g_attn, g_mlp, w_ada, b_ada, w_in, q_norm_a, k_norm_a, q_norm_b, k_norm_b, sinks_a, rel_table_b,
     w_branch_a, w_branch_b, w_out, w_router, b_router, w_gate_up, b_gate_up, w_down, b_down) = params
    depth = g_attn.shape[0]
    bp, lp_, _ = x_prompt.shape
    bs, ls, _ = x_sample.shape
    pos_p = jnp.arange(lp_, dtype=I32)
    pos_s = past_len + jnp.arange(ls, dtype=I32)
    yp, ys = x_prompt, x_sample
    st_p, st_s = [], []
    for l in range(depth):
        weights = _prep_weights(g_attn[l], g_mlp[l], w_in[l], q_norm_a[l], k_norm_a[l], q_norm_b[l],
                                k_norm_b[l], sinks_a[l], rel_table_b[l], w_branch_a[l], w_branch_b[l],
                                w_out[l], w_router[l], b_router[l], w_gate_up[l], b_gate_up[l],
                                w_down[l], b_down[l])
        mod = _adaln(jnp.concatenate([c_prompt, c_sample], axis=0), w_ada[l], b_ada[l])
        mod = mod.reshape(bp + bs, 6, 1, D_MODEL)
        mods_p = [mod[:bp, k] for k in range(6)]
        mods_s = [mod[bp:, k] for k in range(6)]
        caches = (cache_a_k[l].reshape(bs, -1, A_KV_W), cache_a_v[l].reshape(bs, -1, A_KV_W),
                  cache_b_k[l].reshape(bs, -1, B_W), cache_b_v[l].reshape(bs, -1, B_W))
        front = {k: cfg_prompt[k] for k in ("bb", "tl", "n_c")}
        oa_p, ob_p, sp = _front(yp, mods_p, pos_p, None, weights, **front)
        front = {k: cfg_sample[k] for k in ("bb", "tl", "n_c")}
        oa_s, ob_s, ss = _front(ys, mods_s, pos_s, caches, weights, **front)
        n_seg = cfg_prompt["n_seg"]
        nb = bp // n_seg
        xp, xs_in = yp, ys
        yp = None
        for s in range(n_seg):
            parts = [dict(x=xp, oa=oa_p, ob=ob_p, mods=mods_p, b0=s * nb, nb=nb, cfg=cfg_prompt, y=yp)]
            if s == n_seg - 1:
                parts.append(dict(x=xs_in, oa=oa_s, ob=ob_s, mods=mods_s, b0=0, nb=bs, cfg=cfg_sample, y=None))
            outs = _moe_segment(parts, weights)
            yp = outs[0]
            if s == n_seg - 1:
                ys = outs[1]
        st_p.append(sp)
        st_s.append(ss)

    def stack(states, k, heads):
        return jnp.stack([s[k].reshape(s[k].shape[0], s[k].shape[1], heads, HEAD_DIM) for s in states])

    return (yp, ys,
            stack(st_p, 0, A_KV_HEADS), stack(st_p, 1, A_KV_HEADS), stack(st_p, 2, B_HEADS), stack(st_p, 3, B_HEADS),
            stack(st_s, 0, A_KV_HEADS), stack(st_s, 1, A_KV_HEADS), stack(st_s, 2, B_HEADS), stack(st_s, 3, B_HEADS))


PAST_LEN = 2048
CFG_PROMPT = dict(bb=1, tl=512, n_c=8, tl_combine=512, bb_combine=1, n_seg=2)
CFG_SAMPLE = dict(bb=8, tl=64, n_c=1, tl_combine=64, bb_combine=8)


def kernel(x_prompt, x_sample, c_prompt, c_sample, cache_a_k, cache_a_v, cache_b_k, cache_b_v, g_attn, g_mlp,
           w_ada, b_ada, w_in, q_norm_a, k_norm_a, q_norm_b, k_norm_b, sinks_a, rel_table_b, w_branch_a,
           w_branch_b, w_out, w_router, b_router, w_gate_up, b_gate_up, w_down, b_down):
    params = (g_attn, g_mlp, w_ada, b_ada, w_in, q_norm_a, k_norm_a, q_norm_b, k_norm_b, sinks_a, rel_table_b,
              w_branch_a, w_branch_b, w_out, w_router, b_router, w_gate_up, b_gate_up, w_down, b_down)
    return _forward(x_prompt, x_sample, c_prompt, c_sample, cache_a_k, cache_a_v, cache_b_k, cache_b_v,
                    params, PAST_LEN, CFG_PROMPT, CFG_SAMPLE)
```

```python
import functools

import jax
import jax.numpy as jnp
from jax import lax
from jax.experimental import pallas as pl
from jax.experimental.pallas import tpu as pltpu
from jax.experimental.pallas import tpu_sc as plsc

F32 = jnp.float32
BF16 = jnp.bfloat16
I32 = jnp.int32

D_MODEL = 1024
CHUNK = 64
HEAD_DIM = 64
LANES = 128
A_Q_HEADS = 8
A_KV_HEADS = 2
A_GROUP = A_Q_HEADS // A_KV_HEADS
B_HEADS = 8
A_Q_W = A_Q_HEADS * HEAD_DIM
A_KV_W = A_KV_HEADS * HEAD_DIM
B_W = B_HEADS * HEAD_DIM
QKV_W = A_Q_W + 2 * A_KV_W + 3 * B_W
WINDOW_A = 128
WINDOW_B = 512
BAND_A = WINDOW_A + CHUNK
BAND_B = WINDOW_B + CHUNK
PAD = WINDOW_B
ROT_DIM = HEAD_DIM // 4
ROPE_THETA = 500000.0
REL_BACK = 4 * CHUNK
REL_FWD = CHUNK - 1
N_EXPERTS = 32
TOP_K = 4
D_FF = D_MODEL
SWIGLU_LIMIT = 7.0
SWIGLU_ALPHA = 1.702
ROUTE_BLOCK = 512
FF_TILE = 256
RMS_EPS = 1e-6
NEG_INF = -1e30
HIGH_HALF = -65536
LOG2_E = 1.4426950408889634
HALF_W = D_MODEL // 2
ROW_TILES = HALF_W // LANES
VMEM_LIMIT = 56 * 1024 * 1024
SC_CORES = 2
SC_SUBCORES = 16
SC_WORKERS = SC_CORES * SC_SUBCORES
SC_ROWS = 32
TRASH_ROWS = 8

A_HEAD_ORDER = (0, 4, 1, 5, 2, 6, 3, 7)


def _cparams(*sem):
    return pltpu.CompilerParams(dimension_semantics=sem, vmem_limit_bytes=VMEM_LIMIT)


def _adaln_kernel(c_ref, w_ref, b_ref, o_ref):
    c = c_ref[...]
    a = c * jax.nn.sigmoid(c)
    o_ref[...] = jnp.dot(a, w_ref[...], preferred_element_type=F32,
                         precision=lax.Precision.HIGHEST) + b_ref[...]


def _adaln(c, w_ada, b_ada):
    n_rows = c.shape[0]
    n_out = w_ada.shape[1]
    tn = 1024
    return pl.pallas_call(
        _adaln_kernel,
        grid=(n_out // tn,),
        in_specs=[pl.BlockSpec((n_rows, D_MODEL), lambda j: (0, 0)),
                  pl.BlockSpec((D_MODEL, tn), lambda j: (0, j)),
                  pl.BlockSpec((1, tn), lambda j: (0, j))],
        out_specs=pl.BlockSpec((n_rows, tn), lambda j: (0, j)),
        out_shape=jax.ShapeDtypeStruct((n_rows, n_out), F32),
        compiler_params=_cparams("arbitrary"),
        name="adaln",
    )(c, w_ada, b_ada.reshape(1, n_out))


def _pack_rows(ref, x):
    m = x.shape[0]
    for j in range(ROW_TILES):
        lo = x[:, LANES * j:LANES * (j + 1)].astype(BF16).astype(F32)
        hi = x[:, HALF_W + LANES * j:HALF_W + LANES * (j + 1)].astype(BF16).astype(F32)
        word = (lax.bitcast_convert_type(hi, I32) & HIGH_HALF) | lax.shift_right_logical(
            lax.bitcast_convert_type(lo, I32), 16)
        ref[pl.ds(j, m, stride=ROW_TILES), :] = word


def _unpack_rows(words):
    lo = [lax.bitcast_convert_type(lax.shift_left(w, 16), F32) for w in words]
    hi = [lax.bitcast_convert_type(w & HIGH_HALF, F32) for w in words]
    return lo, hi


def _modulated_norm(x, g, scale, shift):
    ms = jnp.mean(x * x, axis=-1, keepdims=True)
    return (x * lax.rsqrt(ms + RMS_EPS) * g) * (1.0 + scale) + shift


def _head_norms(groups, seg):
    flat = [p for parts, _ in groups for p in parts]
    m = flat[0].shape[0]
    sq = [p * p for p in flat]
    if len(sq) % 2:
        sq.append(sq[-1])
    pairs = [jnp.concatenate(sq[i:i + 2], axis=1) for i in range(0, len(sq), 2)]
    ms = jnp.dot(jnp.concatenate(pairs, axis=0).astype(BF16), seg, preferred_element_type=F32)
    r = lax.rsqrt(ms + RMS_EPS)
    out, n = [], 0
    for parts, gain in groups:
        normed = []
        for p in parts:
            rn = r[(n // 2) * m:(n // 2 + 1) * m, (n % 2) * LANES:(n % 2 + 1) * LANES]
            normed.append(p * rn * gain)
            n += 1
        out.append(normed)
    return out


def _rope(p, cos, sin_lo, sin_hi):
    return p * cos + pltpu.roll(p, LANES - ROT_DIM // 2, 1) * sin_lo + pltpu.roll(p, ROT_DIM // 2, 1) * sin_hi


def _stage1_kernel(x_ref, sh_ref, sc_ref, g_ref, w_ref, gqa_ref, gka_ref, gqb_ref, gkb_ref, seg_ref,
                   cos_ref, sinlo_ref, sinhi_ref,
                   qa_ref, ka_ref, va_ref, qb_ref, kb_ref, vb_ref, ska_ref, sva_ref, skb_ref, svb_ref,
                   *, bb, tl, pad_tiles, n_tiles, sa_rows, sb_rows):
    i = pl.program_id(1)
    m = bb * tl

    if pad_tiles:
        @pl.when(i < pad_tiles)
        def _():
            ka_ref[...] = jnp.zeros_like(ka_ref)
            va_ref[...] = jnp.zeros_like(va_ref)
            kb_ref[...] = jnp.zeros_like(kb_ref)
            vb_ref[...] = jnp.zeros_like(vb_ref)

    @pl.when(i >= pad_tiles)
    def _():
        h = _modulated_norm(x_ref[...], g_ref[...], sc_ref[...], sh_ref[...])
        hb = h.reshape(m, D_MODEL).astype(BF16)
        qkv = jnp.dot(hb, w_ref[...], preferred_element_type=F32)
        seg = seg_ref[...]

        def cols(lo, width):
            return [qkv[:, lo + LANES * j: lo + LANES * (j + 1)] for j in range(width // LANES)]

        cos = jnp.concatenate([cos_ref[...]] * bb, axis=0)
        sin_lo = jnp.concatenate([sinlo_ref[...]] * bb, axis=0)
        sin_hi = jnp.concatenate([sinhi_ref[...]] * bb, axis=0)
        scale = HEAD_DIM ** -0.5 * LOG2_E

        off_b = A_Q_W + 2 * A_KV_W
        qa, ka, qb, kb = _head_norms(
            [(cols(0, A_Q_W), gqa_ref[...]), (cols(A_Q_W, A_KV_W), gka_ref[...]),
             (cols(off_b, B_W), gqb_ref[...]), (cols(off_b + B_W, B_W), gkb_ref[...])], seg)
        qa = [_rope(p, cos, sin_lo, sin_hi) * scale for p in qa]
        ka = [_rope(p, cos, sin_lo, sin_hi) for p in ka]
        va = cols(A_Q_W + A_KV_W, A_KV_W)
        qb = [p * scale for p in qb]
        vb = cols(off_b + 2 * B_W, B_W)

        def put(ref, parts):
            for j, p in enumerate(parts):
                ref[:, :, LANES * j:LANES * (j + 1)] = p.reshape(bb, tl, LANES).astype(ref.dtype)

        put(qa_ref, qa)
        put(ka_ref, ka)
        put(va_ref, va)
        put(qb_ref, qb)
        put(kb_ref, kb)
        put(vb_ref, vb)

        @pl.when(i == n_tiles - 1)
        def _():
            def put_state(ref, parts, rows):
                for j, p in enumerate(parts):
                    ref[:, :, LANES * j:LANES * (j + 1)] = p.reshape(bb, tl, LANES)[:, tl - rows:, :]

            put_state(ska_ref, ka, sa_rows)
            put_state(sva_ref, va, sa_rows)
            put_state(skb_ref, kb, sb_rows)
            put_state(svb_ref, vb, sb_rows)


def _rope_tables(pos):
    inv_freq = ROPE_THETA ** (-jnp.arange(0, ROT_DIM, 2, dtype=F32) / ROT_DIM)
    ang = pos.astype(F32)[:, None] * inv_freq[None, :]
    half = ROT_DIM // 2
    lane = jnp.arange(LANES) % HEAD_DIM
    ang_l = jnp.tile(ang, (1, LANES // half))
    rot = lane < ROT_DIM
    cos = jnp.where(rot[None, :], jnp.cos(ang_l), 1.0)
    sin_lo = jnp.where((lane < half)[None, :], -jnp.sin(ang_l), 0.0)
    sin_hi = jnp.where(((lane >= half) & rot)[None, :], jnp.sin(ang_l), 0.0)
    return cos, sin_lo, sin_hi


def _stage1(x, shift, scale, g_attn, w_qkv, gains, pos, *, bb, tl, pad):
    bsz, length, _ = x.shape
    n_tiles = length // tl
    pad_tiles = pad // tl
    sa_rows, sb_rows = min(WINDOW_A, length), min(WINDOW_B, length)
    assert bsz % bb == 0 and length % tl == 0 and pad % tl == 0 and sb_rows <= tl
    cos, sin_lo, sin_hi = _rope_tables(pos)
    seg_ids = jnp.arange(2 * LANES) // HEAD_DIM
    seg = ((seg_ids[:, None] == seg_ids[None, :]).astype(F32) / HEAD_DIM).astype(BF16)
    gq_a, gk_a, gq_b, gk_b = [jnp.tile(g, LANES // HEAD_DIM).reshape(1, LANES) for g in gains]

    def xmap(b, i):
        return (b, jnp.maximum(i - pad_tiles, 0), 0)

    def const2(b, i):
        return (0, 0)

    def tabmap(b, i):
        return (jnp.maximum(i - pad_tiles, 0), 0)

    def modmap(b, i):
        return (b, 0, 0)

    def kvmap(b, i):
        return (b, i, 0)

    lp = length + pad
    kern = functools.partial(_stage1_kernel, bb=bb, tl=tl, pad_tiles=pad_tiles,
                             n_tiles=n_tiles + pad_tiles, sa_rows=sa_rows, sb_rows=sb_rows)
    out_shapes = (
        jax.ShapeDtypeStruct((bsz, length, A_Q_W), BF16),
        jax.ShapeDtypeStruct((bsz, lp, A_KV_W), BF16),
        jax.ShapeDtypeStruct((bsz, lp, A_KV_W), BF16),
        jax.ShapeDtypeStruct((bsz, length, B_W), BF16),
        jax.ShapeDtypeStruct((bsz, lp, B_W), BF16),
        jax.ShapeDtypeStruct((bsz, lp, B_W), BF16),
        jax.ShapeDtypeStruct((bsz, sa_rows, A_KV_W), F32),
        jax.ShapeDtypeStruct((bsz, sa_rows, A_KV_W), F32),
        jax.ShapeDtypeStruct((bsz, sb_rows, B_W), F32),
        jax.ShapeDtypeStruct((bsz, sb_rows, B_W), F32),
    )
    out_specs = (
        pl.BlockSpec((bb, tl, A_Q_W), xmap),
        pl.BlockSpec((bb, tl, A_KV_W), kvmap),
        pl.BlockSpec((bb, tl, A_KV_W), kvmap),
        pl.BlockSpec((bb, tl, B_W), xmap),
        pl.BlockSpec((bb, tl, B_W), kvmap),
        pl.BlockSpec((bb, tl, B_W), kvmap),
        pl.BlockSpec((bb, sa_rows, A_KV_W), modmap),
        pl.BlockSpec((bb, sa_rows, A_KV_W), modmap),
        pl.BlockSpec((bb, sb_rows, B_W), modmap),
        pl.BlockSpec((bb, sb_rows, B_W), modmap),
    )
    in_specs = [
        pl.BlockSpec((bb, tl, D_MODEL), xmap),
        pl.BlockSpec((bb, 1, D_MODEL), modmap),
        pl.BlockSpec((bb, 1, D_MODEL), modmap),
        pl.BlockSpec((1, D_MODEL), const2),
        pl.BlockSpec((D_MODEL, QKV_W), const2),
        pl.BlockSpec((1, LANES), const2),
        pl.BlockSpec((1, LANES), const2),
        pl.BlockSpec((1, LANES), const2),
        pl.BlockSpec((1, LANES), const2),
        pl.BlockSpec((2 * LANES, 2 * LANES), const2),
        pl.BlockSpec((tl, LANES), tabmap),
        pl.BlockSpec((tl, LANES), tabmap),
        pl.BlockSpec((tl, LANES), tabmap),
    ]
    return pl.pallas_call(
        kern,
        grid=(bsz // bb, n_tiles + pad_tiles),
        in_specs=in_specs,
        out_specs=out_specs,
        out_shape=out_shapes,
        compiler_params=_cparams("arbitrary", "arbitrary"),
        name="stage1",
    )(x, shift, scale, g_attn.reshape(1, D_MODEL), w_qkv, gq_a, gk_a, gq_b, gk_b, seg, cos, sin_lo, sin_hi)


def _nt_dot(a, b):
    return lax.dot_general(a, b, (((1,), (1,)), ((), ())), preferred_element_type=F32)


def _attn_kernel(sink_ref, qa_ref, ka_ref, va_ref, qb_ref, kb_ref, vb_ref, bias_ref, *rest, n_c, cached):
    if cached:
        cka_ref, cva_ref, ckb_ref, cvb_ref, oa_ref, ob_ref = rest
    else:
        oa_ref, ob_ref = rest
    mask_prefix = not cached
    qi = pl.program_id(1)
    lane = lax.broadcasted_iota(I32, (CHUNK, LANES), 1)
    low = lane < HEAD_DIM
    half_mask = (jnp.where(low, 1.0, 0.0).astype(BF16), jnp.where(low, 0.0, 1.0).astype(BF16))
    n_pairs = B_W // LANES
    rows_b = n_pairs * 2 * CHUNK
    rows_a = A_Q_HEADS * CHUNK
    key_b = lax.broadcasted_iota(I32, (rows_b, BAND_B), 1)
    key_a = lax.broadcasted_iota(I32, (rows_a, BAND_A), 1)
    row_a = lax.broadcasted_iota(I32, (rows_a, 1), 0) // CHUNK

    sink_col = jnp.zeros((rows_a, 1), F32)
    for r in range(A_Q_HEADS):
        sink_col = jnp.where(row_a == r, sink_ref[r], sink_col)

    def chunk(c, masked):
        ci = qi * n_c + c
        row0 = pl.multiple_of(ci * CHUNK, CHUNK)
        rows = slice(c * CHUNK, (c + 1) * CHUNK)
        first_valid = PAD - ci * CHUNK

        def band(new_ref, cache_ref, start, size, cg):
            if cached:
                return jnp.concatenate([cache_ref[0, :, cg].astype(BF16), new_ref[0, :, cg]], axis=0)
            return new_ref[0, pl.ds(start, size), cg]

        s_parts = []
        for j in range(n_pairs):
            cg = slice(LANES * j, LANES * (j + 1))
            q = qb_ref[0, rows, cg]
            qs = jnp.concatenate([q * half_mask[0], q * half_mask[1]], axis=0)
            k = band(kb_ref, ckb_ref if cached else None, row0, BAND_B, cg)
            s_parts.append(_nt_dot(qs, k) + bias_ref[j])
        s = jnp.concatenate(s_parts, axis=0)
        if masked:
            s = jnp.where(key_b >= first_valid, s, NEG_INF)
        mx = jnp.max(s, axis=-1, keepdims=True)
        e = jnp.exp2(s - mx)
        inv = 1.0 / jnp.sum(e, axis=-1, keepdims=True)
        p_b = e.astype(BF16)
        for j in range(n_pairs):
            cg = slice(LANES * j, LANES * (j + 1))
            v = band(vb_ref, cvb_ref if cached else None, row0, BAND_B, cg)
            rj = slice(2 * CHUNK * j, 2 * CHUNK * (j + 1))
            o = jnp.dot(p_b[rj], v, preferred_element_type=F32) * inv[rj]
            ob_ref[0, rows, cg] = jnp.where(low, o[:CHUNK], o[CHUNK:]).astype(ob_ref.dtype)

        row_a0 = pl.multiple_of(row0 + (WINDOW_B - WINDOW_A), CHUNK)
        all_lanes = slice(0, LANES)
        k = band(ka_ref, cka_ref if cached else None, row_a0, BAND_A, all_lanes)
        v = band(va_ref, cva_ref if cached else None, row_a0, BAND_A, all_lanes)
        q = jnp.concatenate(
            [qa_ref[0, rows, LANES * j:LANES * (j + 1)] * half_mask[p]
             for p in range(A_KV_HEADS) for j in range(A_GROUP)], axis=0)
        s = _nt_dot(q, k)
        if masked:
            s = jnp.where(key_a >= first_valid - (WINDOW_B - WINDOW_A), s, NEG_INF)
        mx = jnp.maximum(jnp.max(s, axis=-1, keepdims=True), sink_col)
        e = jnp.exp2(s - mx)
        inv = 1.0 / (jnp.sum(e, axis=-1, keepdims=True) + jnp.exp2(sink_col - mx))
        o = jnp.dot(e.astype(BF16), v, preferred_element_type=F32) * inv
        for j in range(A_GROUP):
            lo = o[j * CHUNK:(j + 1) * CHUNK]
            hi = o[(A_GROUP + j) * CHUNK:(A_GROUP + j + 1) * CHUNK]
            oa_ref[0, rows, LANES * j:LANES * (j + 1)] = jnp.where(low, lo, hi).astype(oa_ref.dtype)

    steps_with_pad = pl.cdiv(PAD // CHUNK, n_c) if mask_prefix else 0
    if steps_with_pad:
        @pl.when(qi < steps_with_pad)
        def _():
            for c in range(n_c):
                chunk(c, True)

    @pl.when(qi >= steps_with_pad)
    def _():
        for c in range(n_c):
            chunk(c, False)


def _attention(qa, ka, va, qb, kb, vb, sinks, bias, *, n_c, caches=None):
    bsz, length, _ = qa.shape
    lp = ka.shape[1]
    qt = n_c * CHUNK
    cached = caches is not None
    assert length % qt == 0 and lp == (length if cached else length + PAD)
    assert not cached or (length == CHUNK and caches[0].shape[1] == WINDOW_A and caches[2].shape[1] == WINDOW_B)

    def qmap(b, i):
        return (b, i, 0)

    def kmap(b, i):
        return (b, 0, 0)

    in_specs = [
        pl.BlockSpec(memory_space=pltpu.SMEM),
        pl.BlockSpec((1, qt, A_Q_W), qmap),
        pl.BlockSpec((1, lp, A_KV_W), kmap),
        pl.BlockSpec((1, lp, A_KV_W), kmap),
        pl.BlockSpec((1, qt, B_W), qmap),
        pl.BlockSpec((1, lp, B_W), kmap),
        pl.BlockSpec((1, lp, B_W), kmap),
        pl.BlockSpec((B_HEADS // 2, 2 * CHUNK, BAND_B), lambda b, i: (0, 0, 0)),
    ]
    args = [sinks, qa, ka, va, qb, kb, vb, bias]
    if cached:
        in_specs += [pl.BlockSpec((1,) + c.shape[1:], kmap) for c in caches]
        args += list(caches)
    kern = functools.partial(_attn_kernel, n_c=n_c, cached=cached)
    return pl.pallas_call(
        kern,
        grid=(bsz, length // qt),
        in_specs=in_specs,
        out_specs=(pl.BlockSpec((1, qt, A_Q_W), qmap), pl.BlockSpec((1, qt, B_W), qmap)),
        out_shape=(jax.ShapeDtypeStruct((bsz, length, A_Q_W), BF16),
                   jax.ShapeDtypeStruct((bsz, length, B_W), BF16)),
        compiler_params=_cparams("arbitrary", "arbitrary"),
        name="attention",
    )(*args)


def _merge_kernel(x_ref, oa_ref, ob_ref, sha_ref, sca_ref, gta_ref, shm_ref, scm_ref, ga_ref, gm_ref,
                  wg_ref, wa_ref, wb_ref, wo_ref, wrh_ref, wrl_ref, br_ref,
                  x1_ref, h2_ref, ri_ref, rw_ref, cnt_ref, carry_ref, logit_ref, before_ref, *, bb, tl):
    m = bb * tl
    step = pl.program_id(0)

    @pl.when(step == 0)
    def _():
        carry_ref[...] = jnp.zeros_like(carry_ref)
        logit_ref[...] = jnp.zeros_like(logit_ref)
        r_i = lax.broadcasted_iota(I32, (m, m), 0)
        c_i = lax.broadcasted_iota(I32, (m, m), 1)
        before_ref[...] = jnp.where(r_i > c_i, 1.0, 0.0).astype(BF16)

    logits = logit_ref[...]
    lane = lax.broadcasted_iota(I32, (m, LANES), 1)
    lane_f = lane.astype(F32)
    work = logits
    top_v, top_i = [], []
    sel_f = jnp.zeros((m, LANES), F32)
    for _ in range(TOP_K):
        mx = jnp.max(work, axis=-1, keepdims=True)
        idx = jnp.min(jnp.where(work == mx, lane_f, float(LANES)), axis=-1, keepdims=True)
        hit = lane_f == idx
        top_v.append(mx)
        top_i.append(idx)
        sel_f = jnp.where(hit, 1.0, sel_f)
        work = jnp.where(hit, -jnp.inf, work)
    ex = [jnp.exp(v - top_v[0]) for v in top_v]
    inv = 1.0 / (ex[0] + ex[1] + ex[2] + ex[3])

    carry_in = carry_ref[...]
    pos = jnp.dot(before_ref[...], sel_f.astype(BF16), preferred_element_type=F32) + carry_in
    live = jnp.where(step > 0, 1.0, 0.0)
    carry = carry_in + live * jnp.sum(sel_f, axis=0, keepdims=True)
    carry_ref[...] = carry
    cnt_ref[0] = carry

    ri_f = jnp.zeros((m, LANES), F32)
    rw = jnp.zeros((m, LANES), F32)
    for k in range(TOP_K):
        rank = jnp.sum(jnp.where(lane_f == top_i[k], pos, 0.0), axis=-1, keepdims=True)
        ri_f = jnp.where(lane == k, top_i[k], ri_f)
        ri_f = jnp.where(lane == TOP_K + k, rank, ri_f)
        rw = jnp.where(lane == k, ex[k] * inv, rw)
    ri_ref[0] = jnp.transpose(ri_f)[:2 * TOP_K].astype(I32)
    rw_ref[...] = rw.reshape(bb, tl, LANES)

    x = x_ref[...]
    h = _modulated_norm(x, ga_ref[...], sca_ref[...], sha_ref[...])
    hb = h.reshape(m, D_MODEL).astype(BF16)
    gates = jnp.dot(hb, wg_ref[...], preferred_element_type=F32)
    ya = jnp.dot(oa_ref[...].reshape(m, A_Q_W), wa_ref[...], preferred_element_type=F32)
    yb = jnp.dot(ob_ref[...].reshape(m, B_W), wb_ref[...], preferred_element_type=F32)
    mix = jax.nn.sigmoid(gates[:, :D_MODEL]) * ya + jax.nn.sigmoid(gates[:, D_MODEL:]) * yb
    z = jnp.dot(mix.astype(BF16), wo_ref[...], preferred_element_type=F32)
    x1 = x + gta_ref[...] * z.reshape(bb, tl, D_MODEL)
    x1_ref[...] = x1

    h2 = _modulated_norm(x1, gm_ref[...], scm_ref[...], shm_ref[...]).reshape(m, D_MODEL)
    _pack_rows(h2_ref, h2)

    h2_hi = h2.astype(BF16)
    h2_lo = (h2 - h2_hi.astype(F32)).astype(BF16)
    wrh = wrh_ref[...]
    logit_ref[...] = (jnp.dot(h2_hi, wrh, preferred_element_type=F32)
                      + jnp.dot(h2_lo, wrh, preferred_element_type=F32)
                      + jnp.dot(h2_hi, wrl_ref[...], preferred_element_type=F32)) + br_ref[...]


def _merge(x, oa, ob, mods, g_attn, g_mlp, wg, wa, wb, wo, wr, br, *, bb, tl, b0, nb):
    _, length, _ = x.shape
    n_tok = nb * length
    m = bb * tl
    assert bb == 1 or tl == length
    sh_a, sc_a, gt_a, sh_m, sc_m = mods
    n_l = length // tl
    n_tiles = (nb // bb) * n_l
    blk0 = b0 // bb
    wr_hi = wr.astype(BF16)
    wr_lo = (wr - wr_hi.astype(F32)).astype(BF16)

    def cur(s):
        return jnp.minimum(s, n_tiles - 1)

    def prev(s):
        return jnp.maximum(s - 1, 0)

    def xmap(s):
        return (blk0 + cur(s) // n_l, cur(s) % n_l, 0)

    def modmap(s):
        return (blk0 + cur(s) // n_l, 0, 0)

    def omap(s):
        return (cur(s) // n_l, cur(s) % n_l, 0)

    def const2(s):
        return (0, 0)

    def flatmap(s):
        return (cur(s), 0)

    def routemap(s):
        return (prev(s) // n_l, prev(s) % n_l, 0)

    modspec = pl.BlockSpec((bb, 1, D_MODEL), modmap)
    kern = functools.partial(_merge_kernel, bb=bb, tl=tl)
    return pl.pallas_call(
        kern,
        grid=(n_tiles + 1,),
        in_specs=[
            pl.BlockSpec((bb, tl, D_MODEL), xmap),
            pl.BlockSpec((bb, tl, A_Q_W), xmap),
            pl.BlockSpec((bb, tl, B_W), xmap),
            modspec, modspec, modspec, modspec, modspec,
            pl.BlockSpec((1, D_MODEL), const2),
            pl.BlockSpec((1, D_MODEL), const2),
            pl.BlockSpec((D_MODEL, 2 * D_MODEL), const2),
            pl.BlockSpec((A_Q_W, D_MODEL), const2),
            pl.BlockSpec((B_W, D_MODEL), const2),
            pl.BlockSpec((D_MODEL, D_MODEL), const2),
            pl.BlockSpec((D_MODEL, LANES), const2),
            pl.BlockSpec((D_MODEL, LANES), const2),
            pl.BlockSpec((1, LANES), const2),
        ],
        out_specs=(
            pl.BlockSpec((bb, tl, D_MODEL), omap),
            pl.BlockSpec((m * ROW_TILES, LANES), flatmap),
            pl.BlockSpec((1, 2 * TOP_K, m), lambda s: (prev(s), 0, 0)),
            pl.BlockSpec((bb, tl, LANES), routemap),
            pl.BlockSpec((1, 1, LANES), lambda s: (0, 0, 0)),
        ),
        out_shape=(
            jax.ShapeDtypeStruct((nb, length, D_MODEL), F32),
            jax.ShapeDtypeStruct((n_tok * ROW_TILES, LANES), I32),
            jax.ShapeDtypeStruct((n_tiles, 2 * TOP_K, m), I32),
            jax.ShapeDtypeStruct((nb, length, LANES), F32),
            jax.ShapeDtypeStruct((1, 1, LANES), F32),
        ),
        scratch_shapes=[pltpu.VMEM((1, LANES), F32), pltpu.VMEM((m, LANES), F32), pltpu.VMEM((m, m), BF16)],
        compiler_params=_cparams("arbitrary"),
        name="merge",
    )(x, oa, ob, sh_a, sc_a, gt_a, sh_m, sc_m, g_attn.reshape(1, D_MODEL), g_mlp.reshape(1, D_MODEL),
      wg, wa, wb, wo, wr_hi, wr_lo, br)


def _sc_mesh():
    return plsc.VectorSubcoreMesh(core_axis_name="c", subcore_axis_name="s",
                                  num_cores=SC_CORES, num_subcores=SC_SUBCORES)


def _sc_worker():
    return lax.axis_index("c") * SC_SUBCORES + lax.axis_index("s")


def _sc_params():
    return pltpu.CompilerParams(use_tc_tiling_on_sc=True)


def _part_chunks(parts):
    assert all(p % (2 * SC_WORKERS * SC_ROWS) == 0 for p in parts)
    chunks = [p // (SC_WORKERS * SC_ROWS) for p in parts]
    starts = [sum(chunks[:i]) for i in range(len(chunks))]
    return chunks, starts


def _dispatch(h2_parts, idx_flat, pad_flat, n_rows):
    n_parts = len(h2_parts)
    chunks, starts = _part_chunks([h.shape[0] for h in h2_parts])
    n_idx = sum(chunks) * TOP_K * SC_ROWS
    n_pad = pad_flat.shape[0] // SC_WORKERS
    pad_chunks = n_pad // SC_ROWS
    assert n_pad % SC_ROWS == 0
    zeros = jnp.zeros((SC_ROWS, ROW_TILES, LANES), I32)

    def body(*refs):
        h2_hbms = refs[:n_parts]
        idx_hbm, pad_hbm, zeros_hbm, xs_hbm, idx_v, pad_v, buf0, buf1, ls0, ls1, ss0, ss1 = refs[n_parts:]
        wid = _sc_worker()
        bufs, lsems, ssems = (buf0, buf1), (ls0, ls1), (ss0, ss1)
        pltpu.sync_copy(idx_hbm.at[pl.ds(pl.multiple_of(wid * n_idx, SC_ROWS), n_idx)], idx_v)
        pltpu.sync_copy(pad_hbm.at[pl.ds(pl.multiple_of(wid * n_pad, SC_ROWS), n_pad)], pad_v)

        pltpu.sync_copy(zeros_hbm, buf0)
        for c in range(pad_chunks):
            pltpu.async_copy(buf0, xs_hbm.at[pad_v.at[pl.ds(c * SC_ROWS, SC_ROWS)]], ss0)
        for c in range(pad_chunks):
            pltpu.make_async_copy(buf0, xs_hbm.at[pad_v.at[pl.ds(c * SC_ROWS, SC_ROWS)]], ss0).wait()

        def run_part(h2_hbm, n_chunks, c0):
            base = wid * (n_chunks * SC_ROWS)

            def load(g, b):
                src = h2_hbm.at[pl.ds(pl.multiple_of(base + g * SC_ROWS, SC_ROWS), SC_ROWS)]
                return pltpu.make_async_copy(src, bufs[b], lsems[b])

            def scatter(g, k, b):
                off = pl.multiple_of(((c0 + g) * TOP_K + k) * SC_ROWS, SC_ROWS)
                return pltpu.make_async_copy(bufs[b], xs_hbm.at[idx_v.at[pl.ds(off, SC_ROWS)]], ssems[b])

            load(0, 0).start()

            @pl.loop(0, n_chunks, step=2)
            def _(g0):
                for b in range(2):
                    g = g0 + b
                    load(g, b).wait()

                    @pl.when(g >= 1)
                    def _():
                        for k in range(TOP_K):
                            scatter(g - 1, k, 1 - b).wait()

                    @pl.when(g + 1 < n_chunks)
                    def _():
                        load(g + 1, 1 - b).start()

                    for k in range(TOP_K):
                        scatter(g, k, b).start()

            for k in range(TOP_K):
                scatter(n_chunks - 1, k, (n_chunks - 1) % 2).wait()

        for h2_hbm, n_chunks, c0 in zip(h2_hbms, chunks, starts):
            run_part(h2_hbm, n_chunks, c0)

    call = pl.kernel(
        body,
        out_type=jax.ShapeDtypeStruct((n_rows + TRASH_ROWS, ROW_TILES, LANES), I32),
        mesh=_sc_mesh(),
        scratch_types=[pltpu.VMEM((n_idx,), I32), pltpu.VMEM((n_pad,), I32),
                       pltpu.VMEM((SC_ROWS, ROW_TILES, LANES), I32), pltpu.VMEM((SC_ROWS, ROW_TILES, LANES), I32),
                       pltpu.SemaphoreType.DMA, pltpu.SemaphoreType.DMA,
                       pltpu.SemaphoreType.DMA, pltpu.SemaphoreType.DMA],
        compiler_params=_sc_params(),
        name="dispatch",
    )
    return call(*h2_parts, idx_flat, pad_flat, zeros)


def _gather_back(y_rows, idx_flat, parts):
    n_parts = len(parts)
    chunks, starts = _part_chunks(parts)
    n_idx = sum(chunks) * TOP_K * SC_ROWS

    def body(*refs):
        y_hbm, idx_hbm = refs[:2]
        yg_hbms = refs[2:2 + n_parts]
        idx_v, buf0, buf1, gs0, gs1, ws0, ws1 = refs[2 + n_parts:]
        wid = _sc_worker()
        bufs, gsems, wsems = (buf0, buf1), (gs0, gs1), (ws0, ws1)
        pltpu.sync_copy(idx_hbm.at[pl.ds(pl.multiple_of(wid * n_idx, SC_ROWS), n_idx)], idx_v)

        def run_part(yg_hbm, n_chunks, c0):
            base = wid * (n_chunks * SC_ROWS)
            n_items = n_chunks * TOP_K

            def gather(i, b):
                off = pl.multiple_of((c0 * TOP_K + i) * SC_ROWS, SC_ROWS)
                return pltpu.make_async_copy(y_hbm.at[idx_v.at[pl.ds(off, SC_ROWS)]], bufs[b], gsems[b])

            def write(i, b):
                g, k = i // TOP_K, i % TOP_K
                dst = yg_hbm.at[k, pl.ds(pl.multiple_of(base + g * SC_ROWS, SC_ROWS), SC_ROWS)]
                return pltpu.make_async_copy(bufs[b], dst, wsems[b])

            gather(0, 0).start()

            @pl.loop(0, n_items, step=2)
            def _(i0):
                for b in range(2):
                    i = i0 + b
                    gather(i, b).wait()

                    @pl.when(i >= 1)
                    def _():
                        write(i - 1, 1 - b).wait()

                    @pl.when(i + 1 < n_items)
                    def _():
                        gather(i + 1, 1 - b).start()

                    write(i, b).start()

            write(n_items - 1, (n_items - 1) % 2).wait()

        for yg_hbm, n_chunks, c0 in zip(yg_hbms, chunks, starts):
            run_part(yg_hbm, n_chunks, c0)

    call = pl.kernel(
        body,
        out_type=tuple(jax.ShapeDtypeStruct((TOP_K, p, ROW_TILES, LANES), I32) for p in parts),
        mesh=_sc_mesh(),
        scratch_types=[pltpu.VMEM((n_idx,), I32),
                       pltpu.VMEM((SC_ROWS, ROW_TILES, LANES), I32), pltpu.VMEM((SC_ROWS, ROW_TILES, LANES), I32),
                       pltpu.SemaphoreType.DMA, pltpu.SemaphoreType.DMA,
                       pltpu.SemaphoreType.DMA, pltpu.SemaphoreType.DMA],
        compiler_params=_sc_params(),
        name="gather_back",
    )
    return call(y_rows, idx_flat)


def _expert_kernel(blk_ref, used_ref, next_ref, slot_ref, xs_ref, wgu_hbm, bgu_ref, wd_hbm, bd_ref, y_ref,
                   wgu_buf, wd_buf, wgu_bf, wd_bf, sem_gu, sem_d):
    r = pl.program_id(0)
    e = blk_ref[r]
    prev = blk_ref[jnp.maximum(r - 1, 0)]
    active = r < used_ref[0]
    slot = slot_ref[r]

    def fetch(expert, s):
        return (pltpu.make_async_copy(wgu_hbm.at[expert], wgu_buf.at[s], sem_gu.at[s]),
                pltpu.make_async_copy(wd_hbm.at[expert], wd_buf.at[s], sem_d.at[s]))

    @pl.when(jnp.logical_and(active, r == 0))
    def _():
        for copy in fetch(e, slot):
            copy.start()

    @pl.when(jnp.logical_and(active, jnp.logical_or(r == 0, e != prev)))
    def _():
        for copy in fetch(e, slot):
            copy.wait()
        wgu_bf[...] = wgu_buf[slot].astype(BF16)
        wd_bf[...] = wd_buf[slot].astype(BF16)
        nxt = next_ref[r]

        @pl.when(nxt >= 0)
        def _():
            for copy in fetch(nxt, 1 - slot):
                copy.start()

    @pl.when(active)
    def _():
        lo, hi = _unpack_rows([xs_ref[pl.ds(j, ROUTE_BLOCK, stride=ROW_TILES), :] for j in range(ROW_TILES)])
        x = jnp.concatenate(lo + hi, axis=1).astype(BF16)
        acts = []
        for j in range(D_FF // FF_TILE):
            gc = slice(FF_TILE * j, FF_TILE * (j + 1))
            uc = slice(D_FF + FF_TILE * j, D_FF + FF_TILE * (j + 1))
            g = jnp.dot(x, wgu_bf[:, gc], preferred_element_type=F32) + bgu_ref[0, :, gc]
            u = jnp.dot(x, wgu_bf[:, uc], preferred_element_type=F32) + bgu_ref[0, :, uc]
            gate = jnp.minimum(g, SWIGLU_LIMIT)
            up = jnp.clip(u, -SWIGLU_LIMIT, SWIGLU_LIMIT)
            glu = gate * jax.nn.sigmoid(gate * SWIGLU_ALPHA)
            acts.append(((up + 1.0) * glu).astype(BF16))
        act = jnp.concatenate(acts, axis=1)
        y = jnp.dot(act, wd_bf[...], preferred_element_type=F32) + bd_ref[0]
        _pack_rows(y_ref, y)


def _experts(xs_rows, blk_expert, n_used, blk_next, blk_slot, w_gate_up, b_gate_up, w_down, b_down):
    n_blocks = blk_expert.shape[0]
    rb = ROUTE_BLOCK * ROW_TILES

    def rowmap(r, blk, used, nxt, slot):
        return (jnp.minimum(r, used[0] - 1), 0)

    def bmap(r, blk, used, nxt, slot):
        return (blk[jnp.minimum(r, used[0] - 1)], 0, 0)

    return pl.pallas_call(
        _expert_kernel,
        grid_spec=pltpu.PrefetchScalarGridSpec(
            num_scalar_prefetch=4,
            grid=(n_blocks,),
            in_specs=[
                pl.BlockSpec((rb, LANES), rowmap),
                pl.BlockSpec(memory_space=pl.ANY),
                pl.BlockSpec((1, 1, 2 * D_FF), bmap),
                pl.BlockSpec(memory_space=pl.ANY),
                pl.BlockSpec((1, 1, D_MODEL), bmap),
            ],
            out_specs=pl.BlockSpec((rb, LANES), rowmap),
            scratch_shapes=[pltpu.VMEM((2, D_MODEL, 2 * D_FF), F32), pltpu.VMEM((2, D_FF, D_MODEL), F32),
                            pltpu.VMEM((D_MODEL, 2 * D_FF), BF16), pltpu.VMEM((D_FF, D_MODEL), BF16),
                            pltpu.SemaphoreType.DMA((2,)), pltpu.SemaphoreType.DMA((2,))],
        ),
        out_shape=jax.ShapeDtypeStruct(xs_rows.shape, I32),
        compiler_params=_cparams("arbitrary"),
        name="experts",
    )(blk_expert, n_used, blk_next, blk_slot, xs_rows, w_gate_up, b_gate_up.reshape(N_EXPERTS, 1, 2 * D_FF),
      w_down, b_down.reshape(N_EXPERTS, 1, D_MODEL))


def _combine_kernel(x1_ref, gt_ref, rw_ref, yg_ref, *rest, bb, tl):
    o_ref = rest[-1]
    m = bb * tl
    rw = rw_ref[...].reshape(m, LANES)
    w = [rw[:, k:k + 1] for k in range(TOP_K)]
    gt = gt_ref[...]
    for j in range(ROW_TILES):
        acc_lo = jnp.zeros((m, LANES), F32)
        acc_hi = jnp.zeros((m, LANES), F32)
        for k in range(TOP_K):
            lo, hi = _unpack_rows([yg_ref[k, pl.ds(j, m, stride=ROW_TILES), :]])
            acc_lo = acc_lo + w[k] * lo[0]
            acc_hi = acc_hi + w[k] * hi[0]
        for acc, c0 in ((acc_lo, LANES * j), (acc_hi, HALF_W + LANES * j)):
            cg = slice(c0, c0 + LANES)
            o_ref[:, :, cg] = x1_ref[:, :, cg] + gt[:, :, cg] * acc.reshape(bb, tl, LANES)


def _combine(x1, gt_m, rw, yg, y_prev, *, bb, tl, b0, bsz):
    nb, length, _ = x1.shape
    m = bb * tl
    assert bb == 1 or tl == length
    n_l = length // tl
    blk0 = b0 // bb

    def smap(b, i):
        return (b, i, 0)

    def xmap(b, i):
        return (blk0 + b, i, 0)

    in_specs = [
        pl.BlockSpec((bb, tl, D_MODEL), smap),
        pl.BlockSpec((bb, 1, D_MODEL), lambda b, i: (blk0 + b, 0, 0)),
        pl.BlockSpec((bb, tl, LANES), smap),
        pl.BlockSpec((TOP_K, m * ROW_TILES, LANES), lambda b, i: (0, b * n_l + i, 0)),
    ]
    args = [x1, gt_m, rw, yg]
    aliases = {}
    if y_prev is not None:
        in_specs.append(pl.BlockSpec(memory_space=pl.ANY))
        args.append(y_prev)
        aliases = {len(args) - 1: 0}
    kern = functools.partial(_combine_kernel, bb=bb, tl=tl)
    return pl.pallas_call(
        kern,
        grid=(nb // bb, n_l),
        in_specs=in_specs,
        out_specs=pl.BlockSpec((bb, tl, D_MODEL), xmap),
        out_shape=jax.ShapeDtypeStruct((bsz, length, D_MODEL), F32),
        input_output_aliases=aliases,
        compiler_params=_cparams("arbitrary", "arbitrary"),
        name="combine",
    )(*args)


def _route_plan(routes):
    expert_ids = jnp.arange(N_EXPERTS, dtype=I32)
    part_counts = [c.reshape(-1)[:N_EXPERTS].astype(I32) for _, c in routes]
    counts = sum(part_counts)
    padded = (counts + ROUTE_BLOCK - 1) // ROUTE_BLOCK * ROUTE_BLOCK
    pends = jnp.cumsum(padded)
    pstarts = pends - padded
    n_assign = sum(ri.shape[0] * ri.shape[2] for ri, _ in routes) * TOP_K
    n_blocks = n_assign // ROUTE_BLOCK + N_EXPERTS
    n_rows = n_blocks * ROUTE_BLOCK
    lists = []
    first = jnp.zeros_like(counts)
    for (ri, _), cnt in zip(routes, part_counts):
        n_tiles, _, m = ri.shape
        idx, rank = ri[:, :TOP_K], ri[:, TOP_K:]
        dest = jnp.sum(jnp.where(idx[..., None] == expert_ids, pstarts + first, 0), axis=-1) + rank
        dest = dest.astype(I32).reshape(n_tiles, TOP_K, m // SC_ROWS, SC_ROWS).transpose(0, 2, 1, 3)
        lists.append(dest.reshape(SC_WORKERS, n_tiles * m // (SC_WORKERS * SC_ROWS), TOP_K, SC_ROWS))
        first = first + cnt
    idx_flat = jnp.concatenate(lists, axis=1).reshape(-1)
    blk_start = jnp.arange(n_blocks, dtype=I32) * ROUTE_BLOCK
    blk_expert = jnp.minimum(jnp.sum((pends[None, :] <= blk_start[:, None]).astype(I32), axis=1), N_EXPERTS - 1)
    n_used = (pends[-1] // ROUTE_BLOCK).reshape(1)
    nonempty = padded > 0
    later = nonempty[None, :] & (expert_ids[None, :] > expert_ids[:, None])
    next_e = jnp.min(jnp.where(later, expert_ids[None, :], N_EXPERTS), axis=1)
    next_e = jnp.where(next_e == N_EXPERTS, -1, next_e)
    slot_e = (jnp.cumsum(nonempty.astype(I32)) - 1) % 2
    own = blk_expert[:, None] == expert_ids[None, :]
    blk_next = jnp.sum(jnp.where(own, next_e[None, :], 0), axis=1)
    blk_slot = jnp.sum(jnp.where(own, slot_e[None, :], 0), axis=1)
    r = jnp.arange(ROUTE_BLOCK, dtype=I32)[None, :]
    pad_flat = jnp.where(r < (padded - counts)[:, None], (pstarts + counts)[:, None] + r, n_rows)
    blocks = (blk_expert.astype(I32), n_used.astype(I32), blk_next.astype(I32), blk_slot.astype(I32))
    return idx_flat, pad_flat.astype(I32).reshape(-1), blocks, n_rows


def _front(x, mods, pos, caches, weights, *, bb, tl, n_c):
    g_attn, w_qkv, gains, sinks, bias = weights[0], weights[2], weights[3], weights[4], weights[5]
    sh_a, sc_a = mods[0], mods[1]
    pad = 0 if caches is not None else PAD
    qa, ka, va, qb, kb, vb, ska, sva, skb, svb = _stage1(
        x, sh_a, sc_a, g_attn, w_qkv, gains, pos, bb=bb, tl=tl, pad=pad)
    oa, ob = _attention(qa, ka, va, qb, kb, vb, sinks, bias, n_c=n_c, caches=caches)
    return oa, ob, (ska, sva, skb, svb)


def _moe_segment(parts, weights):
    (g_attn, g_mlp, _, _, _, _, wg, wa, wb, wo, wr, br, w_gate_up, b_gate_up, w_down, b_down) = weights
    merged = []
    for p in parts:
        sh_a, sc_a, gt_a, sh_m, sc_m, _ = p["mods"]
        merged.append(_merge(p["x"], p["oa"], p["ob"], (sh_a, sc_a, gt_a, sh_m, sc_m), g_attn, g_mlp,
                             wg, wa, wb, wo, wr, br, bb=p["cfg"]["bb"], tl=p["cfg"]["tl"], b0=p["b0"], nb=p["nb"]))
    toks = [m[0].shape[0] * m[0].shape[1] for m in merged]
    idx_flat, pad_flat, blocks, n_rows = _route_plan([(m[2], m[4]) for m in merged])
    xs = _dispatch([m[1].reshape(t, ROW_TILES, LANES) for m, t in zip(merged, toks)], idx_flat, pad_flat, n_rows)
    y_rows = _experts(xs.reshape(-1, LANES), *blocks, w_gate_up, b_gate_up, w_down, b_down)
    ygs = _gather_back(y_rows.reshape(-1, ROW_TILES, LANES), idx_flat, toks)
    outs = []
    for p, m, yg, t in zip(parts, merged, ygs, toks):
        outs.append(_combine(m[0], p["mods"][5], m[3], yg.reshape(TOP_K, t * ROW_TILES, LANES), p["y"],
                             bb=p["cfg"]["bb_combine"], tl=p["cfg"]["tl_combine"], b0=p["b0"],
                             bsz=p["x"].shape[0]))
    return outs


def _rel_bias_table(rel_table):
    n_rel = rel_table.shape[1]
    assert n_rel == REL_BACK + REL_FWD + 1 and REL_FWD == CHUNK - 1 and REL_BACK <= WINDOW_B
    ext = jnp.concatenate(
        [rel_table, jnp.broadcast_to(rel_table[:, -1:], (B_HEADS, CHUNK + BAND_B - n_rel))], axis=1)
    rows = jnp.stack([ext[:, i:i + BAND_B] for i in range(CHUNK)], axis=1)[..., ::-1]
    return rows.astype(F32).reshape(B_HEADS // 2, 2 * CHUNK, BAND_B)


def _prep_weights(g_attn, g_mlp, w_in, q_norm_a, k_norm_a, q_norm_b, k_norm_b, sinks_a, rel_table_b,
                  w_branch_a, w_branch_b, w_out, w_router, b_router, w_gate_up, b_gate_up, w_down, b_down):
    order = jnp.asarray(A_HEAD_ORDER)
    w_qa = w_in[:, :A_Q_W].reshape(D_MODEL, A_Q_HEADS, HEAD_DIM)[:, order].reshape(D_MODEL, A_Q_W)
    w_qkv = jnp.concatenate([w_qa, w_in[:, A_Q_W:QKV_W]], axis=1).astype(BF16)
    wg = w_in[:, QKV_W:].astype(BF16)
    wa = w_branch_a.reshape(A_Q_HEADS, HEAD_DIM, D_MODEL)[order].reshape(A_Q_W, D_MODEL).astype(BF16)
    wb = w_branch_b.astype(BF16)
    wo = w_out.astype(BF16)
    wr = jnp.zeros((D_MODEL, LANES), F32).at[:, :N_EXPERTS].set(w_router)
    br = jnp.full((1, LANES), NEG_INF, F32).at[0, :N_EXPERTS].set(b_router)
    bias = _rel_bias_table(rel_table_b) * LOG2_E
    gains = (q_norm_a, k_norm_a, q_norm_b, k_norm_b)
    return (g_attn, g_mlp, w_qkv, gains, sinks_a.astype(F32) * LOG2_E, bias, wg, wa, wb, wo, wr, br,
            w_gate_up, b_gate_up, w_down, b_down)


def _forward(x_prompt, x_sample, c_prompt, c_sample, cache_a_k, cache_a_v, cache_b_k, cache_b_v, params,
             past_len, cfg_prompt, cfg_sample):
    (g_attn, g_mlp, w_ada, b_ada, w_in, q_norm_a, k_norm_a, q_norm_b, k_norm_b, sinks_a, rel_table_b,
     w_branch_a, w_branch_b, w_out, w_router, b_router, w_gate_up, b_gate_up, w_down, b_down) = params
    depth = g_attn.shape[0]
    bp, lp_, _ = x_prompt.shape
    bs, ls, _ = x_sample.shape
    pos_p = jnp.arange(lp_, dtype=I32)
    pos_s = past_len + jnp.arange(ls, dtype=I32)
    yp, ys = x_prompt, x_sample
    st_p, st_s = [], []
    for l in range(depth):
        weights = _prep_weights(g_attn[l], g_mlp[l], w_in[l], q_norm_a[l], k_norm_a[l], q_norm_b[l],
                                k_norm_b[l], sinks_a[l], rel_table_b[l], w_branch_a[l], w_branch_b[l],
                                w_out[l], w_router[l], b_router[l], w_gate_up[l], b_gate_up[l],
                                w_down[l], b_down[l])
        mod = _adaln(jnp.concatenate([c_prompt, c_sample], axis=0), w_ada[l], b_ada[l])
        mod = mod.reshape(bp + bs, 6, 1, D_MODEL)
        mods_p = [mod[:bp, k] for k in range(6)]
        mods_s = [mod[bp:, k] for k in range(6)]
        caches = (cache_a_k[l].reshape(bs, -1, A_KV_W), cache_a_v[l].reshape(bs, -1, A_KV_W),
                  cache_b_k[l].reshape(bs, -1, B_W), cache_b_v[l].reshape(bs, -1, B_W))
        front = {k: cfg_prompt[k] for k in ("bb", "tl", "n_c")}
        oa_p, ob_p, sp = _front(yp, mods_p, pos_p, None, weights, **front)
        front = {k: cfg_sample[k] for k in ("bb", "tl", "n_c")}
        oa_s, ob_s, ss = _front(ys, mods_s, pos_s, caches, weights, **front)
        n_seg = cfg_prompt["n_seg"]
        nb = bp // n_seg
        xp, xs_in = yp, ys
        yp = None
        for s in range(n_seg):
            parts = [dict(x=xp, oa=oa_p, ob=ob_p, mods=mods_p, b0=s * nb, nb=nb, cfg=cfg_prompt, y=yp)]
            if s == n_seg - 1:
                parts.append(dict(x=xs_in, oa=oa_s, ob=ob_s, mods=mods_s, b0=0, nb=bs, cfg=cfg_sample, y=None))
            outs = _moe_segment(parts, weights)
            yp = outs[0]
            if s == n_seg - 1:
                ys = outs[1]
        st_p.append(sp)
        st_s.append(ss)

    def stack(states, k, heads):
        return jnp.stack([s[k].reshape(s[k].shape[0], s[k].shape[1], heads, HEAD_DIM) for s in states])

    return (yp, ys,
            stack(st_p, 0, A_KV_HEADS), stack(st_p, 1, A_KV_HEADS), stack(st_p, 2, B_HEADS), stack(st_p, 3, B_HEADS),
            stack(st_s, 0, A_KV_HEADS), stack(st_s, 1, A_KV_HEADS), stack(st_s, 2, B_HEADS), stack(st_s, 3, B_HEADS))


PAST_LEN = 2048
CFG_PROMPT = dict(bb=1, tl=512, n_c=8, tl_combine=512, bb_combine=1, n_seg=2)
CFG_SAMPLE = dict(bb=8, tl=64, n_c=1, tl_combine=64, bb_combine=8)


def kernel(x_prompt, x_sample, c_prompt, c_sample, cache_a_k, cache_a_v, cache_b_k, cache_b_v, g_attn, g_mlp,
           w_ada, b_ada, w_in, q_norm_a, k_norm_a, q_norm_b, k_norm_b, sinks_a, rel_table_b, w_branch_a,
           w_branch_b, w_out, w_router, b_router, w_gate_up, b_gate_up, w_down, b_down):
    params = (g_attn, g_mlp, w_ada, b_ada, w_in, q_norm_a, k_norm_a, q_norm_b, k_norm_b, sinks_a, rel_table_b,
              w_branch_a, w_branch_b, w_out, w_router, b_router, w_gate_up, b_gate_up, w_down, b_down)
    return _forward(x_prompt, x_sample, c_prompt, c_sample, cache_a_k, cache_a_v, cache_b_k, cache_b_v,
                    params, PAST_LEN, CFG_PROMPT, CFG_SAMPLE)
```

```python
import functools

import jax
import jax.numpy as jnp
from jax import lax
from jax.experimental import pallas as pl
from jax.experimental.pallas import tpu as pltpu
from jax.experimental.pallas import tpu_sc as plsc

F32 = jnp.float32
BF16 = jnp.bfloat16
I32 = jnp.int32

D_MODEL = 1024
CHUNK = 64
HEAD_DIM = 64
LANES = 128
A_Q_HEADS = 8
A_KV_HEADS = 2
A_GROUP = A_Q_HEADS // A_KV_HEADS
B_HEADS = 8
A_Q_W = A_Q_HEADS * HEAD_DIM
A_KV_W = A_KV_HEADS * HEAD_DIM
B_W = B_HEADS * HEAD_DIM
QKV_W = A_Q_W + 2 * A_KV_W + 3 * B_W
WINDOW_A = 128
WINDOW_B = 512
BAND_A = WINDOW_A + CHUNK
BAND_B = WINDOW_B + CHUNK
PAD = WINDOW_B
ROT_DIM = HEAD_DIM // 4
ROPE_THETA = 500000.0
REL_BACK = 4 * CHUNK
REL_FWD = CHUNK - 1
N_EXPERTS = 32
TOP_K = 4
D_FF = D_MODEL
SWIGLU_LIMIT = 7.0
SWIGLU_ALPHA = 1.702
ROUTE_BLOCK = 512
FF_TILE = 256
RMS_EPS = 1e-6
NEG_INF = -1e30
HIGH_HALF = -65536
LOG2_E = 1.4426950408889634
HALF_W = D_MODEL // 2
ROW_TILES = HALF_W // LANES
VMEM_LIMIT = 56 * 1024 * 1024
SC_CORES = 2
SC_SUBCORES = 16
SC_WORKERS = SC_CORES * SC_SUBCORES
SC_ROWS = 32
TRASH_ROWS = 8

A_HEAD_ORDER = (0, 4, 1, 5, 2, 6, 3, 7)


def _cparams(*sem):
    return pltpu.CompilerParams(dimension_semantics=sem, vmem_limit_bytes=VMEM_LIMIT)


def _adaln_kernel(c_ref, w_ref, b_ref, o_ref):
    c = c_ref[...]
    a = c * jax.nn.sigmoid(c)
    o_ref[...] = jnp.dot(a, w_ref[...], preferred_element_type=F32,
                         precision=lax.Precision.HIGHEST) + b_ref[...]


def _adaln(c, w_ada, b_ada):
    n_rows = c.shape[0]
    n_out = w_ada.shape[1]
    tn = 1024
    return pl.pallas_call(
        _adaln_kernel,
        grid=(n_out // tn,),
        in_specs=[pl.BlockSpec((n_rows, D_MODEL), lambda j: (0, 0)),
                  pl.BlockSpec((D_MODEL, tn), lambda j: (0, j)),
                  pl.BlockSpec((1, tn), lambda j: (0, j))],
        out_specs=pl.BlockSpec((n_rows, tn), lambda j: (0, j)),
        out_shape=jax.ShapeDtypeStruct((n_rows, n_out), F32),
        compiler_params=_cparams("arbitrary"),
        name="adaln",
    )(c, w_ada, b_ada.reshape(1, n_out))


def _pack_rows(ref, x):
    m = x.shape[0]
    for j in range(ROW_TILES):
        lo = x[:, LANES * j:LANES * (j + 1)].astype(BF16).astype(F32)
        hi = x[:, HALF_W + LANES * j:HALF_W + LANES * (j + 1)].astype(BF16).astype(F32)
        word = (lax.bitcast_convert_type(hi, I32) & HIGH_HALF) | lax.shift_right_logical(
            lax.bitcast_convert_type(lo, I32), 16)
        ref[pl.ds(j, m, stride=ROW_TILES), :] = word


def _unpack_rows(words):
    lo = [lax.bitcast_convert_type(lax.shift_left(w, 16), F32) for w in words]
    hi = [lax.bitcast_convert_type(w & HIGH_HALF, F32) for w in words]
    return lo, hi


def _modulated_norm(x, g, scale, shift):
    ms = jnp.mean(x * x, axis=-1, keepdims=True)
    return (x * lax.rsqrt(ms + RMS_EPS) * g) * (1.0 + scale) + shift


def _head_norms(groups, seg):
    flat = [p for parts, _ in groups for p in parts]
    m = flat[0].shape[0]
    sq = [p * p for p in flat]
    if len(sq) % 2:
        sq.append(sq[-1])
    pairs = [jnp.concatenate(sq[i:i + 2], axis=1) for i in range(0, len(sq), 2)]
    ms = jnp.dot(jnp.concatenate(pairs, axis=0).astype(BF16), seg, preferred_element_type=F32)
    r = lax.rsqrt(ms + RMS_EPS)
    out, n = [], 0
    for parts, gain in groups:
        normed = []
        for p in parts:
            rn = r[(n // 2) * m:(n // 2 + 1) * m, (n % 2) * LANES:(n % 2 + 1) * LANES]
            normed.append(p * rn * gain)
            n += 1
        out.append(normed)
    return out


def _rope(p, cos, sin_lo, sin_hi):
    return p * cos + pltpu.roll(p, LANES - ROT_DIM // 2, 1) * sin_lo + pltpu.roll(p, ROT_DIM // 2, 1) * sin_hi


def _stage1_kernel(x_ref, sh_ref, sc_ref, g_ref, w_ref, gqa_ref, gka_ref, gqb_ref, gkb_ref, seg_ref,
                   cos_ref, sinlo_ref, sinhi_ref,
                   qa_ref, ka_ref, va_ref, qb_ref, kb_ref, vb_ref, ska_ref, sva_ref, skb_ref, svb_ref,
                   *, bb, tl, pad_tiles, n_tiles, sa_rows, sb_rows):
    i = pl.program_id(1)
    m = bb * tl

    if pad_tiles:
        @pl.when(i < pad_tiles)
        def _():
            ka_ref[...] = jnp.zeros_like(ka_ref)
            va_ref[...] = jnp.zeros_like(va_ref)
            kb_ref[...] = jnp.zeros_like(kb_ref)
            vb_ref[...] = jnp.zeros_like(vb_ref)

    @pl.when(i >= pad_tiles)
    def _():
        h = _modulated_norm(x_ref[...], g_ref[...], sc_ref[...], sh_ref[...])
        hb = h.reshape(m, D_MODEL).astype(BF16)
        qkv = jnp.dot(hb, w_ref[...], preferred_element_type=F32)
        seg = seg_ref[...]

        def cols(lo, width):
            return [qkv[:, lo + LANES * j: lo + LANES * (j + 1)] for j in range(width // LANES)]

        cos = jnp.concatenate([cos_ref[...]] * bb, axis=0)
        sin_lo = jnp.concatenate([sinlo_ref[...]] * bb, axis=0)
        sin_hi = jnp.concatenate([sinhi_ref[...]] * bb, axis=0)
        scale = HEAD_DIM ** -0.5 * LOG2_E

        off_b = A_Q_W + 2 * A_KV_W
        qa, ka, qb, kb = _head_norms(
            [(cols(0, A_Q_W), gqa_ref[...]), (cols(A_Q_W, A_KV_W), gka_ref[...]),
             (cols(off_b, B_W), gqb_ref[...]), (cols(off_b + B_W, B_W), gkb_ref[...])], seg)
        qa = [_rope(p, cos, sin_lo, sin_hi) * scale for p in qa]
        ka = [_rope(p, cos, sin_lo, sin_hi) for p in ka]
        va = cols(A_Q_W + A_KV_W, A_KV_W)
        qb = [p * scale for p in qb]
        vb = cols(off_b + 2 * B_W, B_W)

        def put(ref, parts):
            for j, p in enumerate(parts):
                ref[:, :, LANES * j:LANES * (j + 1)] = p.reshape(bb, tl, LANES).astype(ref.dtype)

        put(qa_ref, qa)
        put(ka_ref, ka)
        put(va_ref, va)
        put(qb_ref, qb)
        put(kb_ref, kb)
        put(vb_ref, vb)

        @pl.when(i == n_tiles - 1)
        def _():
            def put_state(ref, parts, rows):
                for j, p in enumerate(parts):
                    ref[:, :, LANES * j:LANES * (j + 1)] = p.reshape(bb, tl, LANES)[:, tl - rows:, :]

            put_state(ska_ref, ka, sa_rows)
            put_state(sva_ref, va, sa_rows)
            put_state(skb_ref, kb, sb_rows)
            put_state(svb_ref, vb, sb_rows)


def _rope_tables(pos):
    inv_freq = ROPE_THETA ** (-jnp.arange(0, ROT_DIM, 2, dtype=F32) / ROT_DIM)
    ang = pos.astype(F32)[:, None] * inv_freq[None, :]
    half = ROT_DIM // 2
    lane = jnp.arange(LANES) % HEAD_DIM
    ang_l = jnp.tile(ang, (1, LANES // half))
    rot = lane < ROT_DIM
    cos = jnp.where(rot[None, :], jnp.cos(ang_l), 1.0)
    sin_lo = jnp.where((lane < half)[None, :], -jnp.sin(ang_l), 0.0)
    sin_hi = jnp.where(((lane >= half) & rot)[None, :], jnp.sin(ang_l), 0.0)
    return cos, sin_lo, sin_hi


def _stage1(x, shift, scale, g_attn, w_qkv, gains, pos, *, bb, tl, pad):
    bsz, length, _ = x.shape
    n_tiles = length // tl
    pad_tiles = pad // tl
    sa_rows, sb_rows = min(WINDOW_A, length), min(WINDOW_B, length)
    assert bsz % bb == 0 and length % tl == 0 and pad % tl == 0 and sb_rows <= tl
    cos, sin_lo, sin_hi = _rope_tables(pos)
    seg_ids = jnp.arange(2 * LANES) // HEAD_DIM
    seg = ((seg_ids[:, None] == seg_ids[None, :]).astype(F32) / HEAD_DIM).astype(BF16)
    gq_a, gk_a, gq_b, gk_b = [jnp.tile(g, LANES // HEAD_DIM).reshape(1, LANES) for g in gains]

    def xmap(b, i):
        return (b, jnp.maximum(i - pad_tiles, 0), 0)

    def const2(b, i):
        return (0, 0)

    def tabmap(b, i):
        return (jnp.maximum(i - pad_tiles, 0), 0)

    def modmap(b, i):
        return (b, 0, 0)

    def kvmap(b, i):
        return (b, i, 0)

    lp = length + pad
    kern = functools.partial(_stage1_kernel, bb=bb, tl=tl, pad_tiles=pad_tiles,
                             n_tiles=n_tiles + pad_tiles, sa_rows=sa_rows, sb_rows=sb_rows)
    out_shapes = (
        jax.ShapeDtypeStruct((bsz, length, A_Q_W), BF16),
        jax.ShapeDtypeStruct((bsz, lp, A_KV_W), BF16),
        jax.ShapeDtypeStruct((bsz, lp, A_KV_W), BF16),
        jax.ShapeDtypeStruct((bsz, length, B_W), BF16),
        jax.ShapeDtypeStruct((bsz, lp, B_W), BF16),
        jax.ShapeDtypeStruct((bsz, lp, B_W), BF16),
        jax.ShapeDtypeStruct((bsz, sa_rows, A_KV_W), F32),
        jax.ShapeDtypeStruct((bsz, sa_rows, A_KV_W), F32),
        jax.ShapeDtypeStruct((bsz, sb_rows, B_W), F32),
        jax.ShapeDtypeStruct((bsz, sb_rows, B_W), F32),
    )
    out_specs = (
        pl.BlockSpec((bb, tl, A_Q_W), xmap),
        pl.BlockSpec((bb, tl, A_KV_W), kvmap),
        pl.BlockSpec((bb, tl, A_KV_W), kvmap),
        pl.BlockSpec((bb, tl, B_W), xmap),
        pl.BlockSpec((bb, tl, B_W), kvmap),
        pl.BlockSpec((bb, tl, B_W), kvmap),
        pl.BlockSpec((bb, sa_rows, A_KV_W), modmap),
        pl.BlockSpec((bb, sa_rows, A_KV_W), modmap),
        pl.BlockSpec((bb, sb_rows, B_W), modmap),
        pl.BlockSpec((bb, sb_rows, B_W), modmap),
    )
    in_specs = [
        pl.BlockSpec((bb, tl, D_MODEL), xmap),
        pl.BlockSpec((bb, 1, D_MODEL), modmap),
        pl.BlockSpec((bb, 1, D_MODEL), modmap),
        pl.BlockSpec((1, D_MODEL), const2),
        pl.BlockSpec((D_MODEL, QKV_W), const2),
        pl.BlockSpec((1, LANES), const2),
        pl.BlockSpec((1, LANES), const2),
        pl.BlockSpec((1, LANES), const2),
        pl.BlockSpec((1, LANES), const2),
        pl.BlockSpec((2 * LANES, 2 * LANES), const2),
        pl.BlockSpec((tl, LANES), tabmap),
        pl.BlockSpec((tl, LANES), tabmap),
        pl.BlockSpec((tl, LANES), tabmap),
    ]
    return pl.pallas_call(
        kern,
        grid=(bsz // bb, n_tiles + pad_tiles),
        in_specs=in_specs,
        out_specs=out_specs,
        out_shape=out_shapes,
        compiler_params=_cparams("arbitrary", "arbitrary"),
        name="stage1",
    )(x, shift, scale, g_attn.reshape(1, D_MODEL), w_qkv, gq_a, gk_a, gq_b, gk_b, seg, cos, sin_lo, sin_hi)


def _nt_dot(a, b):
    return lax.dot_general(a, b, (((1,), (1,)), ((), ())), preferred_element_type=F32)


def _attn_kernel(sink_ref, qa_ref, ka_ref, va_ref, qb_ref, kb_ref, vb_ref, bias_ref, *rest, n_c, cached):
    if cached:
        cka_ref, cva_ref, ckb_ref, cvb_ref, oa_ref, ob_ref = rest
    else:
        oa_ref, ob_ref = rest
    mask_prefix = not cached
    qi = pl.program_id(1)
    lane = lax.broadcasted_iota(I32, (CHUNK, LANES), 1)
    low = lane < HEAD_DIM
    half_mask = (jnp.where(low, 1.0, 0.0).astype(BF16), jnp.where(low, 0.0, 1.0).astype(BF16))
    n_pairs = B_W // LANES
    rows_b = n_pairs * 2 * CHUNK
    rows_a = A_Q_HEADS * CHUNK
    key_b = lax.broadcasted_iota(I32, (rows_b, BAND_B), 1)
    key_a = lax.broadcasted_iota(I32, (rows_a, BAND_A), 1)
    row_a = lax.broadcasted_iota(I32, (rows_a, 1), 0) // CHUNK

    sink_col = jnp.zeros((rows_a, 1), F32)
    for r in range(A_Q_HEADS):
        sink_col = jnp.where(row_a == r, sink_ref[r], sink_col)

    def chunk(c, masked):
        ci = qi * n_c + c
        row0 = pl.multiple_of(ci * CHUNK, CHUNK)
        rows = slice(c * CHUNK, (c + 1) * CHUNK)
        first_valid = PAD - ci * CHUNK

        def band(new_ref, cache_ref, start, size, cg):
            if cached:
                h0 = cg.start // HEAD_DIM
                old = jnp.concatenate([cache_ref[0, :, h0, :], cache_ref[0, :, h0 + 1, :]], axis=1)
                return jnp.concatenate([old.astype(BF16), new_ref[0, :, cg]], axis=0)
            return new_ref[0, pl.ds(start, size), cg]

        s_parts = []
        for j in range(n_pairs):
            cg = slice(LANES * j, LANES * (j + 1))
            q = qb_ref[0, rows, cg]
            qs = jnp.concatenate([q * half_mask[0], q * half_mask[1]], axis=0)
            k = band(kb_ref, ckb_ref if cached else None, row0, BAND_B, cg)
            s_parts.append(_nt_dot(qs, k) + bias_ref[j])
        s = jnp.concatenate(s_parts, axis=0)
        if masked:
            s = jnp.where(key_b >= first_valid, s, NEG_INF)
        mx = jnp.max(s, axis=-1, keepdims=True)
        e = jnp.exp2(s - mx)
        inv = 1.0 / jnp.sum(e, axis=-1, keepdims=True)
        p_b = e.astype(BF16)
        for j in range(n_pairs):
            cg = slice(LANES * j, LANES * (j + 1))
            v = band(vb_ref, cvb_ref if cached else None, row0, BAND_B, cg)
            rj = slice(2 * CHUNK * j, 2 * CHUNK * (j + 1))
            o = jnp.dot(p_b[rj], v, preferred_element_type=F32) * inv[rj]
            ob_ref[0, rows, cg] = jnp.where(low, o[:CHUNK], o[CHUNK:]).astype(ob_ref.dtype)

        row_a0 = pl.multiple_of(row0 + (WINDOW_B - WINDOW_A), CHUNK)
        all_lanes = slice(0, LANES)
        k = band(ka_ref, cka_ref if cached else None, row_a0, BAND_A, all_lanes)
        v = band(va_ref, cva_ref if cached else None, row_a0, BAND_A, all_lanes)
        q = jnp.concatenate(
            [qa_ref[0, rows, LANES * j:LANES * (j + 1)] * half_mask[p]
             for p in range(A_KV_HEADS) for j in range(A_GROUP)], axis=0)
        s = _nt_dot(q, k)
        if masked:
            s = jnp.where(key_a >= first_valid - (WINDOW_B - WINDOW_A), s, NEG_INF)
        mx = jnp.maximum(jnp.max(s, axis=-1, keepdims=True), sink_col)
        e = jnp.exp2(s - mx)
        inv = 1.0 / (jnp.sum(e, axis=-1, keepdims=True) + jnp.exp2(sink_col - mx))
        o = jnp.dot(e.astype(BF16), v, preferred_element_type=F32) * inv
        for j in range(A_GROUP):
            lo = o[j * CHUNK:(j + 1) * CHUNK]
            hi = o[(A_GROUP + j) * CHUNK:(A_GROUP + j + 1) * CHUNK]
            oa_ref[0, rows, LANES * j:LANES * (j + 1)] = jnp.where(low, lo, hi).astype(oa_ref.dtype)

    steps_with_pad = pl.cdiv(PAD // CHUNK, n_c) if mask_prefix else 0
    if steps_with_pad:
        @pl.when(qi < steps_with_pad)
        def _():
            for c in range(n_c):
                chunk(c, True)

    @pl.when(qi >= steps_with_pad)
    def _():
        for c in range(n_c):
            chunk(c, False)


def _attention(qa, ka, va, qb, kb, vb, sinks, bias, *, n_c, caches=None):
    bsz, length, _ = qa.shape
    lp = ka.shape[1]
    qt = n_c * CHUNK
    cached = caches is not None
    assert length % qt == 0 and lp == (length if cached else length + PAD)
    assert not cached or (length == CHUNK and caches[0].shape[1] == WINDOW_A and caches[2].shape[1] == WINDOW_B)

    def qmap(b, i):
        return (b, i, 0)

    def kmap(b, i):
        return (b, 0, 0)

    in_specs = [
        pl.BlockSpec(memory_space=pltpu.SMEM),
        pl.BlockSpec((1, qt, A_Q_W), qmap),
        pl.BlockSpec((1, lp, A_KV_W), kmap),
        pl.BlockSpec((1, lp, A_KV_W), kmap),
        pl.BlockSpec((1, qt, B_W), qmap),
        pl.BlockSpec((1, lp, B_W), kmap),
        pl.BlockSpec((1, lp, B_W), kmap),
        pl.BlockSpec((B_HEADS // 2, 2 * CHUNK, BAND_B), lambda b, i: (0, 0, 0)),
    ]
    args = [sinks, qa, ka, va, qb, kb, vb, bias]
    if cached:
        in_specs += [pl.BlockSpec((1,) + c.shape[1:], lambda b, i: (b, 0, 0, 0)) for c in caches]
        args += list(caches)
    kern = functools.partial(_attn_kernel, n_c=n_c, cached=cached)
    return pl.pallas_call(
        kern,
        grid=(bsz, length // qt),
        in_specs=in_specs,
        out_specs=(pl.BlockSpec((1, qt, A_Q_W), qmap), pl.BlockSpec((1, qt, B_W), qmap)),
        out_shape=(jax.ShapeDtypeStruct((bsz, length, A_Q_W), BF16),
                   jax.ShapeDtypeStruct((bsz, length, B_W), BF16)),
        compiler_params=_cparams("arbitrary", "arbitrary"),
        name="attention",
    )(*args)


def _merge_kernel(x_ref, oa_ref, ob_ref, sha_ref, sca_ref, gta_ref, shm_ref, scm_ref, ga_ref, gm_ref,
                  wg_ref, wa_ref, wb_ref, wo_ref, wrh_ref, wrl_ref, br_ref,
                  x1_ref, h2_ref, ri_ref, rw_ref, cnt_ref, carry_ref, logit_ref, before_ref, *, bb, tl):
    m = bb * tl
    step = pl.program_id(0)

    @pl.when(step == 0)
    def _():
        carry_ref[...] = jnp.zeros_like(carry_ref)
        logit_ref[...] = jnp.zeros_like(logit_ref)
        r_i = lax.broadcasted_iota(I32, (m, m), 0)
        c_i = lax.broadcasted_iota(I32, (m, m), 1)
        before_ref[...] = jnp.where(r_i > c_i, 1.0, 0.0).astype(BF16)

    logits = logit_ref[...]
    lane = lax.broadcasted_iota(I32, (m, LANES), 1)
    lane_f = lane.astype(F32)
    work = logits
    top_v, top_i = [], []
    sel_f = jnp.zeros((m, LANES), F32)
    for _ in range(TOP_K):
        mx = jnp.max(work, axis=-1, keepdims=True)
        idx = jnp.min(jnp.where(work == mx, lane_f, float(LANES)), axis=-1, keepdims=True)
        hit = lane_f == idx
        top_v.append(mx)
        top_i.append(idx)
        sel_f = jnp.where(hit, 1.0, sel_f)
        work = jnp.where(hit, -jnp.inf, work)
    ex = [jnp.exp(v - top_v[0]) for v in top_v]
    inv = 1.0 / (ex[0] + ex[1] + ex[2] + ex[3])

    carry_in = carry_ref[...]
    pos = jnp.dot(before_ref[...], sel_f.astype(BF16), preferred_element_type=F32) + carry_in
    live = jnp.where(step > 0, 1.0, 0.0)
    carry = carry_in + live * jnp.sum(sel_f, axis=0, keepdims=True)
    carry_ref[...] = carry
    cnt_ref[0] = carry

    ri_f = jnp.zeros((m, LANES), F32)
    rw = jnp.zeros((m, LANES), F32)
    for k in range(TOP_K):
        rank = jnp.sum(jnp.where(lane_f == top_i[k], pos, 0.0), axis=-1, keepdims=True)
        ri_f = jnp.where(lane == k, top_i[k], ri_f)
        ri_f = jnp.where(lane == TOP_K + k, rank, ri_f)
        rw = jnp.where(lane == k, ex[k] * inv, rw)
    ri_ref[0] = jnp.transpose(ri_f)[:2 * TOP_K].astype(I32)
    rw_ref[...] = rw.reshape(bb, tl, LANES)

    x = x_ref[...]
    h = _modulated_norm(x, ga_ref[...], sca_ref[...], sha_ref[...])
    hb = h.reshape(m, D_MODEL).astype(BF16)
    gates = jnp.dot(hb, wg_ref[...], preferred_element_type=F32)
    ya = jnp.dot(oa_ref[...].reshape(m, A_Q_W), wa_ref[...], preferred_element_type=F32)
    yb = jnp.dot(ob_ref[...].reshape(m, B_W), wb_ref[...], preferred_element_type=F32)
    mix = jax.nn.sigmoid(gates[:, :D_MODEL]) * ya + jax.nn.sigmoid(gates[:, D_MODEL:]) * yb
    z = jnp.dot(mix.astype(BF16), wo_ref[...], preferred_element_type=F32)
    x1 = x + gta_ref[...] * z.reshape(bb, tl, D_MODEL)
    x1_ref[...] = x1

    h2 = _modulated_norm(x1, gm_ref[...], scm_ref[...], shm_ref[...]).reshape(m, D_MODEL)
    _pack_rows(h2_ref, h2)

    h2_hi = h2.astype(BF16)
    h2_lo = (h2 - h2_hi.astype(F32)).astype(BF16)
    wrh = wrh_ref[...]
    logit_ref[...] = (jnp.dot(h2_hi, wrh, preferred_element_type=F32)
                      + jnp.dot(h2_lo, wrh, preferred_element_type=F32)
                      + jnp.dot(h2_hi, wrl_ref[...], preferred_element_type=F32)) + br_ref[...]


def _merge(x, oa, ob, mods, g_attn, g_mlp, wg, wa, wb, wo, wr, br, *, bb, tl, b0, nb):
    _, length, _ = x.shape
    n_tok = nb * length
    m = bb * tl
    assert bb == 1 or tl == length
    sh_a, sc_a, gt_a, sh_m, sc_m = mods
    n_l = length // tl
    n_tiles = (nb // bb) * n_l
    blk0 = b0 // bb
    wr_hi = wr.astype(BF16)
    wr_lo = (wr - wr_hi.astype(F32)).astype(BF16)

    def cur(s):
        return jnp.minimum(s, n_tiles - 1)

    def prev(s):
        return jnp.maximum(s - 1, 0)

    def xmap(s):
        return (blk0 + cur(s) // n_l, cur(s) % n_l, 0)

    def modmap(s):
        return (blk0 + cur(s) // n_l, 0, 0)

    def omap(s):
        return (cur(s) // n_l, cur(s) % n_l, 0)

    def const2(s):
        return (0, 0)

    def flatmap(s):
        return (cur(s), 0)

    def routemap(s):
        return (prev(s) // n_l, prev(s) % n_l, 0)

    modspec = pl.BlockSpec((bb, 1, D_MODEL), modmap)
    kern = functools.partial(_merge_kernel, bb=bb, tl=tl)
    return pl.pallas_call(
        kern,
        grid=(n_tiles + 1,),
        in_specs=[
            pl.BlockSpec((bb, tl, D_MODEL), xmap),
            pl.BlockSpec((bb, tl, A_Q_W), xmap),
            pl.BlockSpec((bb, tl, B_W), xmap),
            modspec, modspec, modspec, modspec, modspec,
            pl.BlockSpec((1, D_MODEL), const2),
            pl.BlockSpec((1, D_MODEL), const2),
            pl.BlockSpec((D_MODEL, 2 * D_MODEL), const2),
            pl.BlockSpec((A_Q_W, D_MODEL), const2),
            pl.BlockSpec((B_W, D_MODEL), const2),
            pl.BlockSpec((D_MODEL, D_MODEL), const2),
            pl.BlockSpec((D_MODEL, LANES), const2),
            pl.BlockSpec((D_MODEL, LANES), const2),
            pl.BlockSpec((1, LANES), const2),
        ],
        out_specs=(
            pl.BlockSpec((bb, tl, D_MODEL), omap),
            pl.BlockSpec((m * ROW_TILES, LANES), flatmap),
            pl.BlockSpec((1, 2 * TOP_K, m), lambda s: (prev(s), 0, 0)),
            pl.BlockSpec((bb, tl, LANES), routemap),
            pl.BlockSpec((1, 1, LANES), lambda s: (0, 0, 0)),
        ),
        out_shape=(
            jax.ShapeDtypeStruct((nb, length, D_MODEL), F32),
            jax.ShapeDtypeStruct((n_tok * ROW_TILES, LANES), I32),
            jax.ShapeDtypeStruct((n_tiles, 2 * TOP_K, m), I32),
            jax.ShapeDtypeStruct((nb, length, LANES), F32),
            jax.ShapeDtypeStruct((1, 1, LANES), F32),
        ),
        scratch_shapes=[pltpu.VMEM((1, LANES), F32), pltpu.VMEM((m, LANES), F32), pltpu.VMEM((m, m), BF16)],
        compiler_params=_cparams("arbitrary"),
        name="merge",
    )(x, oa, ob, sh_a, sc_a, gt_a, sh_m, sc_m, g_attn.reshape(1, D_MODEL), g_mlp.reshape(1, D_MODEL),
      wg, wa, wb, wo, wr_hi, wr_lo, br)


def _sc_mesh():
    return plsc.VectorSubcoreMesh(core_axis_name="c", subcore_axis_name="s",
                                  num_cores=SC_CORES, num_subcores=SC_SUBCORES)


def _sc_worker():
    return lax.axis_index("c") * SC_SUBCORES + lax.axis_index("s")


def _sc_params():
    return pltpu.CompilerParams(use_tc_tiling_on_sc=True)


def _part_chunks(parts):
    assert all(p % (2 * SC_WORKERS * SC_ROWS) == 0 for p in parts)
    chunks = [p // (SC_WORKERS * SC_ROWS) for p in parts]
    starts = [sum(chunks[:i]) for i in range(len(chunks))]
    return chunks, starts


def _dispatch(h2_parts, idx_flat, pad_flat, n_rows):
    n_parts = len(h2_parts)
    chunks, starts = _part_chunks([h.shape[0] for h in h2_parts])
    n_idx = sum(chunks) * TOP_K * SC_ROWS
    n_pad = pad_flat.shape[0] // SC_WORKERS
    pad_chunks = n_pad // SC_ROWS
    assert n_pad % SC_ROWS == 0
    zeros = jnp.zeros((SC_ROWS, ROW_TILES, LANES), I32)

    def body(*refs):
        h2_hbms = refs[:n_parts]
        idx_hbm, pad_hbm, zeros_hbm, xs_hbm, idx_v, pad_v, buf0, buf1, ls0, ls1, ss0, ss1 = refs[n_parts:]
        wid = _sc_worker()
        bufs, lsems, ssems = (buf0, buf1), (ls0, ls1), (ss0, ss1)
        pltpu.sync_copy(idx_hbm.at[pl.ds(pl.multiple_of(wid * n_idx, SC_ROWS), n_idx)], idx_v)
        pltpu.sync_copy(pad_hbm.at[pl.ds(pl.multiple_of(wid * n_pad, SC_ROWS), n_pad)], pad_v)

        pltpu.sync_copy(zeros_hbm, buf0)
        for c in range(pad_chunks):
            pltpu.async_copy(buf0, xs_hbm.at[pad_v.at[pl.ds(c * SC_ROWS, SC_ROWS)]], ss0)
        for c in range(pad_chunks):
            pltpu.make_async_copy(buf0, xs_hbm.at[pad_v.at[pl.ds(c * SC_ROWS, SC_ROWS)]], ss0).wait()

        def run_part(h2_hbm, n_chunks, c0):
            base = wid * (n_chunks * SC_ROWS)

            def load(g, b):
                src = h2_hbm.at[pl.ds(pl.multiple_of(base + g * SC_ROWS, SC_ROWS), SC_ROWS)]
                return pltpu.make_async_copy(src, bufs[b], lsems[b])

            def scatter(g, k, b):
                off = pl.multiple_of(((c0 + g) * TOP_K + k) * SC_ROWS, SC_ROWS)
                return pltpu.make_async_copy(bufs[b], xs_hbm.at[idx_v.at[pl.ds(off, SC_ROWS)]], ssems[b])

            load(0, 0).start()

            @pl.loop(0, n_chunks, step=2)
            def _(g0):
                for b in range(2):
                    g = g0 + b
                    load(g, b).wait()

                    @pl.when(g >= 1)
                    def _():
                        for k in range(TOP_K):
                            scatter(g - 1, k, 1 - b).wait()

                    @pl.when(g + 1 < n_chunks)
                    def _():
                        load(g + 1, 1 - b).start()

                    for k in range(TOP_K):
                        scatter(g, k, b).start()

            for k in range(TOP_K):
                scatter(n_chunks - 1, k, (n_chunks - 1) % 2).wait()

        for h2_hbm, n_chunks, c0 in zip(h2_hbms, chunks, starts):
            run_part(h2_hbm, n_chunks, c0)

    call = pl.kernel(
        body,
        out_type=jax.ShapeDtypeStruct((n_rows + TRASH_ROWS, ROW_TILES, LANES), I32),
        mesh=_sc_mesh(),
        scratch_types=[pltpu.VMEM((n_idx,), I32), pltpu.VMEM((n_pad,), I32),
                       pltpu.VMEM((SC_ROWS, ROW_TILES, LANES), I32), pltpu.VMEM((SC_ROWS, ROW_TILES, LANES), I32),
                       pltpu.SemaphoreType.DMA, pltpu.SemaphoreType.DMA,
                       pltpu.SemaphoreType.DMA, pltpu.SemaphoreType.DMA],
        compiler_params=_sc_params(),
        name="dispatch",
    )
    return call(*h2_parts, idx_flat, pad_flat, zeros)


def _gather_back(y_rows, idx_flat, parts):
    n_parts = len(parts)
    chunks, starts = _part_chunks(parts)
    n_idx = sum(chunks) * TOP_K * SC_ROWS

    def body(*refs):
        y_hbm, idx_hbm = refs[:2]
        yg_hbms = refs[2:2 + n_parts]
        idx_v, buf0, buf1, gs0, gs1, ws0, ws1 = refs[2 + n_parts:]
        wid = _sc_worker()
        bufs, gsems, wsems = (buf0, buf1), (gs0, gs1), (ws0, ws1)
        pltpu.sync_copy(idx_hbm.at[pl.ds(pl.multiple_of(wid * n_idx, SC_ROWS), n_idx)], idx_v)

        def run_part(yg_hbm, n_chunks, c0):
            base = wid * (n_chunks * SC_ROWS)
            n_items = n_chunks * TOP_K

            def gather(i, b):
                off = pl.multiple_of((c0 * TOP_K + i) * SC_ROWS, SC_ROWS)
                return pltpu.make_async_copy(y_hbm.at[idx_v.at[pl.ds(off, SC_ROWS)]], bufs[b], gsems[b])

            def write(i, b):
                g, k = i // TOP_K, i % TOP_K
                dst = yg_hbm.at[k, pl.ds(pl.multiple_of(base + g * SC_ROWS, SC_ROWS), SC_ROWS)]
                return pltpu.make_async_copy(bufs[b], dst, wsems[b])

            gather(0, 0).start()

            @pl.loop(0, n_items, step=2)
            def _(i0):
                for b in range(2):
                    i = i0 + b
                    gather(i, b).wait()

                    @pl.when(i >= 1)
                    def _():
                        write(i - 1, 1 - b).wait()

                    @pl.when(i + 1 < n_items)
                    def _():
                        gather(i + 1, 1 - b).start()

                    write(i, b).start()

            write(n_items - 1, (n_items - 1) % 2).wait()

        for yg_hbm, n_chunks, c0 in zip(yg_hbms, chunks, starts):
            run_part(yg_hbm, n_chunks, c0)

    call = pl.kernel(
        body,
        out_type=tuple(jax.ShapeDtypeStruct((TOP_K, p, ROW_TILES, LANES), I32) for p in parts),
        mesh=_sc_mesh(),
        scratch_types=[pltpu.VMEM((n_idx,), I32),
                       pltpu.VMEM((SC_ROWS, ROW_TILES, LANES), I32), pltpu.VMEM((SC_ROWS, ROW_TILES, LANES), I32),
                       pltpu.SemaphoreType.DMA, pltpu.SemaphoreType.DMA,
                       pltpu.SemaphoreType.DMA, pltpu.SemaphoreType.DMA],
        compiler_params=_sc_params(),
        name="gather_back",
    )
    return call(y_rows, idx_flat)


def _expert_kernel(blk_ref, used_ref, next_ref, slot_ref, xs_ref, wgu_hbm, bgu_ref, wd_hbm, bd_ref, y_ref,
                   wgu_buf, wd_buf, wgu_bf, wd_bf, sem_gu, sem_d):
    r = pl.program_id(0)
    e = blk_ref[r]
    prev = blk_ref[jnp.maximum(r - 1, 0)]
    active = r < used_ref[0]
    slot = slot_ref[r]

    def fetch(expert, s):
        return (pltpu.make_async_copy(wgu_hbm.at[expert], wgu_buf.at[s], sem_gu.at[s]),
                pltpu.make_async_copy(wd_hbm.at[expert], wd_buf.at[s], sem_d.at[s]))

    @pl.when(jnp.logical_and(active, r == 0))
    def _():
        for copy in fetch(e, slot):
            copy.start()

    @pl.when(jnp.logical_and(active, jnp.logical_or(r == 0, e != prev)))
    def _():
        for copy in fetch(e, slot):
            copy.wait()
        wgu_bf[...] = wgu_buf[slot].astype(BF16)
        wd_bf[...] = wd_buf[slot].astype(BF16)
        nxt = next_ref[r]

        @pl.when(nxt >= 0)
        def _():
            for copy in fetch(nxt, 1 - slot):
                copy.start()

    @pl.when(active)
    def _():
        lo, hi = _unpack_rows([xs_ref[pl.ds(j, ROUTE_BLOCK, stride=ROW_TILES), :] for j in range(ROW_TILES)])
        x = jnp.concatenate(lo + hi, axis=1).astype(BF16)
        acts = []
        for j in range(D_FF // FF_TILE):
            gc = slice(FF_TILE * j, FF_TILE * (j + 1))
            uc = slice(D_FF + FF_TILE * j, D_FF + FF_TILE * (j + 1))
            g = jnp.dot(x, wgu_bf[:, gc], preferred_element_type=F32) + bgu_ref[0, :, gc]
            u = jnp.dot(x, wgu_bf[:, uc], preferred_element_type=F32) + bgu_ref[0, :, uc]
            gate = jnp.minimum(g, SWIGLU_LIMIT)
            up = jnp.clip(u, -SWIGLU_LIMIT, SWIGLU_LIMIT)
            glu = gate * jax.nn.sigmoid(gate * SWIGLU_ALPHA)
            acts.append(((up + 1.0) * glu).astype(BF16))
        act = jnp.concatenate(acts, axis=1)
        y = jnp.dot(act, wd_bf[...], preferred_element_type=F32) + bd_ref[0]
        _pack_rows(y_ref, y)


def _experts(xs_rows, blk_expert, n_used, blk_next, blk_slot, w_gate_up, b_gate_up, w_down, b_down):
    n_blocks = blk_expert.shape[0]
    rb = ROUTE_BLOCK * ROW_TILES

    def rowmap(r, blk, used, nxt, slot):
        return (jnp.minimum(r, used[0] - 1), 0)

    def bmap(r, blk, used, nxt, slot):
        return (blk[jnp.minimum(r, used[0] - 1)], 0, 0)

    return pl.pallas_call(
        _expert_kernel,
        grid_spec=pltpu.PrefetchScalarGridSpec(
            num_scalar_prefetch=4,
            grid=(n_blocks,),
            in_specs=[
                pl.BlockSpec((rb, LANES), rowmap),
                pl.BlockSpec(memory_space=pl.ANY),
                pl.BlockSpec((1, 1, 2 * D_FF), bmap),
                pl.BlockSpec(memory_space=pl.ANY),
                pl.BlockSpec((1, 1, D_MODEL), bmap),
            ],
            out_specs=pl.BlockSpec((rb, LANES), rowmap),
            scratch_shapes=[pltpu.VMEM((2, D_MODEL, 2 * D_FF), F32), pltpu.VMEM((2, D_FF, D_MODEL), F32),
                            pltpu.VMEM((D_MODEL, 2 * D_FF), BF16), pltpu.VMEM((D_FF, D_MODEL), BF16),
                            pltpu.SemaphoreType.DMA((2,)), pltpu.SemaphoreType.DMA((2,))],
        ),
        out_shape=jax.ShapeDtypeStruct(xs_rows.shape, I32),
        compiler_params=_cparams("arbitrary"),
        name="experts",
    )(blk_expert, n_used, blk_next, blk_slot, xs_rows, w_gate_up, b_gate_up.reshape(N_EXPERTS, 1, 2 * D_FF),
      w_down, b_down.reshape(N_EXPERTS, 1, D_MODEL))


def _combine_kernel(x1_ref, gt_ref, rw_ref, yg_ref, *rest, bb, tl):
    o_ref = rest[-1]
    m = bb * tl
    rw = rw_ref[...].reshape(m, LANES)
    w = [rw[:, k:k + 1] for k in range(TOP_K)]
    gt = gt_ref[...]
    for j in range(ROW_TILES):
        acc_lo = jnp.zeros((m, LANES), F32)
        acc_hi = jnp.zeros((m, LANES), F32)
        for k in range(TOP_K):
            lo, hi = _unpack_rows([yg_ref[k, pl.ds(j, m, stride=ROW_TILES), :]])
            acc_lo = acc_lo + w[k] * lo[0]
            acc_hi = acc_hi + w[k] * hi[0]
        for acc, c0 in ((acc_lo, LANES * j), (acc_hi, HALF_W + LANES * j)):
            cg = slice(c0, c0 + LANES)
            o_ref[:, :, cg] = x1_ref[:, :, cg] + gt[:, :, cg] * acc.reshape(bb, tl, LANES)


def _combine(x1, gt_m, rw, yg, y_prev, *, bb, tl, b0, bsz):
    nb, length, _ = x1.shape
    m = bb * tl
    assert bb == 1 or tl == length
    n_l = length // tl
    blk0 = b0 // bb

    def smap(b, i):
        return (b, i, 0)

    def xmap(b, i):
        return (blk0 + b, i, 0)

    in_specs = [
        pl.BlockSpec((bb, tl, D_MODEL), smap),
        pl.BlockSpec((bb, 1, D_MODEL), lambda b, i: (blk0 + b, 0, 0)),
        pl.BlockSpec((bb, tl, LANES), smap),
        pl.BlockSpec((TOP_K, m * ROW_TILES, LANES), lambda b, i: (0, b * n_l + i, 0)),
    ]
    args = [x1, gt_m, rw, yg]
    aliases = {}
    if y_prev is not None:
        in_specs.append(pl.BlockSpec(memory_space=pl.ANY))
        args.append(y_prev)
        aliases = {len(args) - 1: 0}
    kern = functools.partial(_combine_kernel, bb=bb, tl=tl)
    return pl.pallas_call(
        kern,
        grid=(nb // bb, n_l),
        in_specs=in_specs,
        out_specs=pl.BlockSpec((bb, tl, D_MODEL), xmap),
        out_shape=jax.ShapeDtypeStruct((bsz, length, D_MODEL), F32),
        input_output_aliases=aliases,
        compiler_params=_cparams("arbitrary", "arbitrary"),
        name="combine",
    )(*args)


def _route_plan(routes):
    expert_ids = jnp.arange(N_EXPERTS, dtype=I32)
    part_counts = [c.reshape(-1)[:N_EXPERTS].astype(I32) for _, c in routes]
    counts = sum(part_counts)
    padded = (counts + ROUTE_BLOCK - 1) // ROUTE_BLOCK * ROUTE_BLOCK
    pends = jnp.cumsum(padded)
    pstarts = pends - padded
    n_assign = sum(ri.shape[0] * ri.shape[2] for ri, _ in routes) * TOP_K
    n_blocks = n_assign // ROUTE_BLOCK + N_EXPERTS
    n_rows = n_blocks * ROUTE_BLOCK
    lists = []
    first = jnp.zeros_like(counts)
    for (ri, _), cnt in zip(routes, part_counts):
        n_tiles, _, m = ri.shape
        idx, rank = ri[:, :TOP_K], ri[:, TOP_K:]
        dest = jnp.sum(jnp.where(idx[..., None] == expert_ids, pstarts + first, 0), axis=-1) + rank
        dest = dest.astype(I32).reshape(n_tiles, TOP_K, m // SC_ROWS, SC_ROWS).transpose(0, 2, 1, 3)
        lists.append(dest.reshape(SC_WORKERS, n_tiles * m // (SC_WORKERS * SC_ROWS), TOP_K, SC_ROWS))
        first = first + cnt
    idx_flat = jnp.concatenate(lists, axis=1).reshape(-1)
    blk_start = jnp.arange(n_blocks, dtype=I32) * ROUTE_BLOCK
    blk_expert = jnp.minimum(jnp.sum((pends[None, :] <= blk_start[:, None]).astype(I32), axis=1), N_EXPERTS - 1)
    n_used = (pends[-1] // ROUTE_BLOCK).reshape(1)
    nonempty = padded > 0
    later = nonempty[None, :] & (expert_ids[None, :] > expert_ids[:, None])
    next_e = jnp.min(jnp.where(later, expert_ids[None, :], N_EXPERTS), axis=1)
    next_e = jnp.where(next_e == N_EXPERTS, -1, next_e)
    slot_e = (jnp.cumsum(nonempty.astype(I32)) - 1) % 2
    own = blk_expert[:, None] == expert_ids[None, :]
    blk_next = jnp.sum(jnp.where(own, next_e[None, :], 0), axis=1)
    blk_slot = jnp.sum(jnp.where(own, slot_e[None, :], 0), axis=1)
    r = jnp.arange(ROUTE_BLOCK, dtype=I32)[None, :]
    pad_flat = jnp.where(r < (padded - counts)[:, None], (pstarts + counts)[:, None] + r, n_rows)
    blocks = (blk_expert.astype(I32), n_used.astype(I32), blk_next.astype(I32), blk_slot.astype(I32))
    return idx_flat, pad_flat.astype(I32).reshape(-1), blocks, n_rows


def _front(x, mods, pos, caches, weights, *, bb, tl, n_c):
    g_attn, w_qkv, gains, sinks, bias = weights[0], weights[2], weights[3], weights[4], weights[5]
    sh_a, sc_a = mods[0], mods[1]
    pad = 0 if caches is not None else PAD
    qa, ka, va, qb, kb, vb, ska, sva, skb, svb = _stage1(
        x, sh_a, sc_a, g_attn, w_qkv, gains, pos, bb=bb, tl=tl, pad=pad)
    oa, ob = _attention(qa, ka, va, qb, kb, vb, sinks, bias, n_c=n_c, caches=caches)
    return oa, ob, (ska, sva, skb, svb)


def _moe_segment(parts, weights):
    (g_attn, g_mlp, _, _, _, _, wg, wa, wb, wo, wr, br, w_gate_up, b_gate_up, w_down, b_down) = weights
    merged = []
    for p in parts:
        sh_a, sc_a, gt_a, sh_m, sc_m, _ = p["mods"]
        merged.append(_merge(p["x"], p["oa"], p["ob"], (sh_a, sc_a, gt_a, sh_m, sc_m), g_attn, g_mlp,
                             wg, wa, wb, wo, wr, br, bb=p["cfg"]["bb"], tl=p["cfg"]["tl"], b0=p["b0"], nb=p["nb"]))
    toks = [m[0].shape[0] * m[0].shape[1] for m in merged]
    idx_flat, pad_flat, blocks, n_rows = _route_plan([(m[2], m[4]) for m in merged])
    xs = _dispatch([m[1].reshape(t, ROW_TILES, LANES) for m, t in zip(merged, toks)], idx_flat, pad_flat, n_rows)
    y_rows = _experts(xs.reshape(-1, LANES), *blocks, w_gate_up, b_gate_up, w_down, b_down)
    ygs = _gather_back(y_rows.reshape(-1, ROW_TILES, LANES), idx_flat, toks)
    outs = []
    for p, m, yg, t in zip(parts, merged, ygs, toks):
        outs.append(_combine(m[0], p["mods"][5], m[3], yg.reshape(TOP_K, t * ROW_TILES, LANES), p["y"],
                             bb=p["cfg"]["bb_combine"], tl=p["cfg"]["tl_combine"], b0=p["b0"],
                             bsz=p["x"].shape[0]))
    return outs


def _rel_bias_table(rel_table):
    n_rel = rel_table.shape[1]
    assert n_rel == REL_BACK + REL_FWD + 1 and REL_FWD == CHUNK - 1 and REL_BACK <= WINDOW_B
    ext = jnp.concatenate(
        [rel_table, jnp.broadcast_to(rel_table[:, -1:], (B_HEADS, CHUNK + BAND_B - n_rel))], axis=1)
    rows = jnp.stack([ext[:, i:i + BAND_B] for i in range(CHUNK)], axis=1)[..., ::-1]
    return rows.astype(F32).reshape(B_HEADS // 2, 2 * CHUNK, BAND_B)


def _prep_weights(g_attn, g_mlp, w_in, q_norm_a, k_norm_a, q_norm_b, k_norm_b, sinks_a, rel_table_b,
                  w_branch_a, w_branch_b, w_out, w_router, b_router, w_gate_up, b_gate_up, w_down, b_down):
    order = jnp.asarray(A_HEAD_ORDER)
    w_qa = w_in[:, :A_Q_W].reshape(D_MODEL, A_Q_HEADS, HEAD_DIM)[:, order].reshape(D_MODEL, A_Q_W)
    w_qkv = jnp.concatenate([w_qa, w_in[:, A_Q_W:QKV_W]], axis=1).astype(BF16)
    wg = w_in[:, QKV_W:].astype(BF16)
    wa = w_branch_a.reshape(A_Q_HEADS, HEAD_DIM, D_MODEL)[order].reshape(A_Q_W, D_MODEL).astype(BF16)
    wb = w_branch_b.astype(BF16)
    wo = w_out.astype(BF16)
    wr = jnp.zeros((D_MODEL, LANES), F32).at[:, :N_EXPERTS].set(w_router)
    br = jnp.full((1, LANES), NEG_INF, F32).at[0, :N_EXPERTS].set(b_router)
    bias = _rel_bias_table(rel_table_b) * LOG2_E
    gains = (q_norm_a, k_norm_a, q_norm_b, k_norm_b)
    return (g_attn, g_mlp, w_qkv, gains, sinks_a.astype(F32) * LOG2_E, bias, wg, wa, wb, wo, wr, br,
            w_gate_up, b_gate_up, w_down, b_down)


def _forward(x_prompt, x_sample, c_prompt, c_sample, cache_a_k, cache_a_v, cache_b_k, cache_b_v, params,
             past_len, cfg_prompt, cfg_sample):
    (g_attn, g_mlp, w_ada, b_ada, w_in, q_norm_a, k_norm_a, q_norm_b, k_norm_b, sinks_a, rel_table_b,
     w_branch_a, w_branch_b, w_out, w_router, b_router, w_gate_up, b_gate_up, w_down, b_down) = params
    depth = g_attn.shape[0]
    bp, lp_, _ = x_prompt.shape
    bs, ls, _ = x_sample.shape
    pos_p = jnp.arange(lp_, dtype=I32)
    pos_s = past_len + jnp.arange(ls, dtype=I32)
    yp, ys = x_prompt, x_sample
    st_p, st_s = [], []
    for l in range(depth):
        weights = _prep_weights(g_attn[l], g_mlp[l], w_in[l], q_norm_a[l], k_norm_a[l], q_norm_b[l],
                                k_norm_b[l], sinks_a[l], rel_table_b[l], w_branch_a[l], w_branch_b[l],
                                w_out[l], w_router[l], b_router[l], w_gate_up[l], b_gate_up[l],
                                w_down[l], b_down[l])
        mod = _adaln(jnp.concatenate([c_prompt, c_sample], axis=0), w_ada[l], b_ada[l])
        mod = mod.reshape(bp + bs, 6, 1, D_MODEL)
        mods_p = [mod[:bp, k] for k in range(6)]
        mods_s = [mod[bp:, k] for k in range(6)]
        caches = (cache_a_k[l], cache_a_v[l], cache_b_k[l], cache_b_v[l])
        front = {k: cfg_prompt[k] for k in ("bb", "tl", "n_c")}
        oa_p, ob_p, sp = _front(yp, mods_p, pos_p, None, weights, **front)
        front = {k: cfg_sample[k] for k in ("bb", "tl", "n_c")}
        oa_s, ob_s, ss = _front(ys, mods_s, pos_s, caches, weights, **front)
        n_seg = cfg_prompt["n_seg"]
        nb = bp // n_seg
        xp, xs_in = yp, ys
        yp = None
        for s in range(n_seg):
            parts = [dict(x=xp, oa=oa_p, ob=ob_p, mods=mods_p, b0=s * nb, nb=nb, cfg=cfg_prompt, y=yp)]
            if s == n_seg - 1:
                parts.append(dict(x=xs_in, oa=oa_s, ob=ob_s, mods=mods_s, b0=0, nb=bs, cfg=cfg_sample, y=None))
            outs = _moe_segment(parts, weights)
            yp = outs[0]
            if s == n_seg - 1:
                ys = outs[1]
        st_p.append(sp)
        st_s.append(ss)

    def stack(states, k, heads):
        return jnp.stack([s[k].reshape(s[k].shape[0], s[k].shape[1], heads, HEAD_DIM) for s in states])

    return (yp, ys,
            stack(st_p, 0, A_KV_HEADS), stack(st_p, 1, A_KV_HEADS), stack(st_p, 2, B_HEADS), stack(st_p, 3, B_HEADS),
            stack(st_s, 0, A_KV_HEADS), stack(st_s, 1, A_KV_HEADS), stack(st_s, 2, B_HEADS), stack(st_s, 3, B_HEADS))


PAST_LEN = 2048
CFG_PROMPT = dict(bb=1, tl=512, n_c=8, tl_combine=512, bb_combine=1, n_seg=2)
CFG_SAMPLE = dict(bb=8, tl=64, n_c=1, tl_combine=64, bb_combine=8)


def kernel(x_prompt, x_sample, c_prompt, c_sample, cache_a_k, cache_a_v, cache_b_k, cache_b_v, g_attn, g_mlp,
           w_ada, b_ada, w_in, q_norm_a, k_norm_a, q_norm_b, k_norm_b, sinks_a, rel_table_b, w_branch_a,
           w_branch_b, w_out, w_router, b_router, w_gate_up, b_gate_up, w_down, b_down):
    params = (g_attn, g_mlp, w_ada, b_ada, w_in, q_norm_a, k_norm_a, q_norm_b, k_norm_b, sinks_a, rel_table_b,
              w_branch_a, w_branch_b, w_out, w_router, b_router, w_gate_up, b_gate_up, w_down, b_down)
    return _forward(x_prompt, x_sample, c_prompt, c_sample, cache_a_k, cache_a_v, cache_b_k, cache_b_v,
                    params, PAST_LEN, CFG_PROMPT, CFG_SAMPLE)
```

```python
import functools

import jax
import jax.numpy as jnp
from jax import lax
from jax.experimental import pallas as pl
from jax.experimental.pallas import tpu as pltpu
from jax.experimental.pallas import tpu_sc as plsc

F32 = jnp.float32
BF16 = jnp.bfloat16
I32 = jnp.int32

D_MODEL = 1024
CHUNK = 64
HEAD_DIM = 64
LANES = 128
SUBLANES = 8
A_Q_HEADS = 8
A_KV_HEADS = 2
A_GROUP = A_Q_HEADS // A_KV_HEADS
B_HEADS = 8
A_Q_W = A_Q_HEADS * HEAD_DIM
A_KV_W = A_KV_HEADS * HEAD_DIM
B_W = B_HEADS * HEAD_DIM
QKV_W = A_Q_W + 2 * A_KV_W + 3 * B_W
WINDOW_A = 128
WINDOW_B = 512
BAND_A = WINDOW_A + CHUNK
BAND_B = WINDOW_B + CHUNK
PAD = WINDOW_B
ROT_DIM = HEAD_DIM // 4
ROPE_THETA = 500000.0
REL_BACK = 4 * CHUNK
REL_FWD = CHUNK - 1
N_EXPERTS = 32
TOP_K = 4
D_FF = D_MODEL
SWIGLU_LIMIT = 7.0
SWIGLU_ALPHA = 1.702
ROUTE_BLOCK = 512
FF_TILE = 256
RMS_EPS = 1e-6
NEG_INF = -1e30
HIGH_HALF = -65536
LOG2_E = 1.4426950408889634
HALF_W = D_MODEL // 2
ROW_TILES = HALF_W // LANES
VMEM_LIMIT = 56 * 1024 * 1024
SC_CORES = 2
SC_SUBCORES = 16
SC_WORKERS = SC_CORES * SC_SUBCORES
SC_ROWS = 32
TRASH_ROWS = 8

A_HEAD_ORDER = (0, 4, 1, 5, 2, 6, 3, 7)


def _cparams(*sem):
    return pltpu.CompilerParams(dimension_semantics=sem, vmem_limit_bytes=VMEM_LIMIT)


def _adaln_kernel(c_ref, w_ref, b_ref, o_ref):
    c = c_ref[...]
    a = c * jax.nn.sigmoid(c)
    o_ref[...] = jnp.dot(a, w_ref[...], preferred_element_type=F32,
                         precision=lax.Precision.HIGHEST) + b_ref[...]


def _adaln(c, w_ada, b_ada):
    n_rows = c.shape[0]
    n_out = w_ada.shape[1]
    tn = 1024
    return pl.pallas_call(
        _adaln_kernel,
        grid=(n_out // tn,),
        in_specs=[pl.BlockSpec((n_rows, D_MODEL), lambda j: (0, 0)),
                  pl.BlockSpec((D_MODEL, tn), lambda j: (0, j)),
                  pl.BlockSpec((1, tn), lambda j: (0, j))],
        out_specs=pl.BlockSpec((n_rows, tn), lambda j: (0, j)),
        out_shape=jax.ShapeDtypeStruct((n_rows, n_out), F32),
        compiler_params=_cparams("arbitrary"),
        name="adaln",
    )(c, w_ada, b_ada.reshape(1, n_out))


def _pack_rows(ref, x):
    m = x.shape[0]
    for j in range(ROW_TILES):
        lo = x[:, LANES * j:LANES * (j + 1)].astype(BF16).astype(F32)
        hi = x[:, HALF_W + LANES * j:HALF_W + LANES * (j + 1)].astype(BF16).astype(F32)
        word = (lax.bitcast_convert_type(hi, I32) & HIGH_HALF) | lax.shift_right_logical(
            lax.bitcast_convert_type(lo, I32), 16)
        ref[pl.ds(j, m, stride=ROW_TILES), :] = word


def _unpack_rows(words):
    lo = [lax.bitcast_convert_type(lax.shift_left(w, 16), F32) for w in words]
    hi = [lax.bitcast_convert_type(w & HIGH_HALF, F32) for w in words]
    return lo, hi


def _modulated_norm(x, g, scale, shift):
    ms = jnp.mean(x * x, axis=-1, keepdims=True)
    return (x * lax.rsqrt(ms + RMS_EPS) * g) * (1.0 + scale) + shift


def _head_norms(groups, seg):
    flat = [p for parts, _ in groups for p in parts]
    m = flat[0].shape[0]
    sq = [p * p for p in flat]
    if len(sq) % 2:
        sq.append(sq[-1])
    pairs = [jnp.concatenate(sq[i:i + 2], axis=1) for i in range(0, len(sq), 2)]
    ms = jnp.dot(jnp.concatenate(pairs, axis=0).astype(BF16), seg, preferred_element_type=F32)
    r = lax.rsqrt(ms + RMS_EPS)
    out, n = [], 0
    for parts, gain in groups:
        normed = []
        for p in parts:
            rn = r[(n // 2) * m:(n // 2 + 1) * m, (n % 2) * LANES:(n % 2 + 1) * LANES]
            normed.append(p * rn * gain)
            n += 1
        out.append(normed)
    return out


def _rope(p, cos, sin_lo, sin_hi):
    return p * cos + pltpu.roll(p, LANES - ROT_DIM // 2, 1) * sin_lo + pltpu.roll(p, ROT_DIM // 2, 1) * sin_hi


def _stage1_kernel(x_ref, sh_ref, sc_ref, g_ref, w_ref, gqa_ref, gka_ref, gqb_ref, gkb_ref, seg_ref,
                   cos_ref, sinlo_ref, sinhi_ref,
                   qa_ref, ka_ref, va_ref, qb_ref, kb_ref, vb_ref, ska_ref, sva_ref, skb_ref, svb_ref,
                   *, bb, tl, pad_tiles, n_tiles, sa_rows, sb_rows):
    i = pl.program_id(1)
    m = bb * tl

    if pad_tiles:
        @pl.when(i < pad_tiles)
        def _():
            ka_ref[...] = jnp.zeros_like(ka_ref)
            va_ref[...] = jnp.zeros_like(va_ref)
            kb_ref[...] = jnp.zeros_like(kb_ref)
            vb_ref[...] = jnp.zeros_like(vb_ref)

    @pl.when(i >= pad_tiles)
    def _():
        h = _modulated_norm(x_ref[...], g_ref[...], sc_ref[...], sh_ref[...])
        hb = h.reshape(m, D_MODEL).astype(BF16)
        qkv = jnp.dot(hb, w_ref[...], preferred_element_type=F32)
        seg = seg_ref[...]

        def cols(lo, width):
            return [qkv[:, lo + LANES * j: lo + LANES * (j + 1)] for j in range(width // LANES)]

        cos = jnp.concatenate([cos_ref[...]] * bb, axis=0)
        sin_lo = jnp.concatenate([sinlo_ref[...]] * bb, axis=0)
        sin_hi = jnp.concatenate([sinhi_ref[...]] * bb, axis=0)
        scale = HEAD_DIM ** -0.5 * LOG2_E

        off_b = A_Q_W + 2 * A_KV_W
        qa, ka, qb, kb = _head_norms(
            [(cols(0, A_Q_W), gqa_ref[...]), (cols(A_Q_W, A_KV_W), gka_ref[...]),
             (cols(off_b, B_W), gqb_ref[...]), (cols(off_b + B_W, B_W), gkb_ref[...])], seg)
        qa = [_rope(p, cos, sin_lo, sin_hi) * scale for p in qa]
        ka = [_rope(p, cos, sin_lo, sin_hi) for p in ka]
        va = cols(A_Q_W + A_KV_W, A_KV_W)
        qb = [p * scale for p in qb]
        vb = cols(off_b + 2 * B_W, B_W)

        def put(ref, parts):
            for j, p in enumerate(parts):
                ref[:, :, LANES * j:LANES * (j + 1)] = p.reshape(bb, tl, LANES).astype(ref.dtype)

        put(qa_ref, qa)
        put(ka_ref, ka)
        put(va_ref, va)
        put(qb_ref, qb)
        put(kb_ref, kb)
        put(vb_ref, vb)

        @pl.when(i == n_tiles - 1)
        def _():
            def put_state(ref, parts, rows):
                for j, p in enumerate(parts):
                    ref[:, :, LANES * j:LANES * (j + 1)] = p.reshape(bb, tl, LANES)[:, tl - rows:, :]

            put_state(ska_ref, ka, sa_rows)
            put_state(sva_ref, va, sa_rows)
            put_state(skb_ref, kb, sb_rows)
            put_state(svb_ref, vb, sb_rows)


def _rope_tables(pos):
    inv_freq = ROPE_THETA ** (-jnp.arange(0, ROT_DIM, 2, dtype=F32) / ROT_DIM)
    ang = pos.astype(F32)[:, None] * inv_freq[None, :]
    half = ROT_DIM // 2
    lane = jnp.arange(LANES) % HEAD_DIM
    ang_l = jnp.tile(ang, (1, LANES // half))
    rot = lane < ROT_DIM
    cos = jnp.where(rot[None, :], jnp.cos(ang_l), 1.0)
    sin_lo = jnp.where((lane < half)[None, :], -jnp.sin(ang_l), 0.0)
    sin_hi = jnp.where(((lane >= half) & rot)[None, :], jnp.sin(ang_l), 0.0)
    return cos, sin_lo, sin_hi


def _stage1(x, shift, scale, g_attn, w_qkv, gains, pos, *, bb, tl, pad):
    bsz, length, _ = x.shape
    n_tiles = length // tl
    pad_tiles = pad // tl
    sa_rows, sb_rows = min(WINDOW_A, length), min(WINDOW_B, length)
    assert bsz % bb == 0 and length % tl == 0 and pad % tl == 0 and sb_rows <= tl
    cos, sin_lo, sin_hi = _rope_tables(pos)
    seg_ids = jnp.arange(2 * LANES) // HEAD_DIM
    seg = ((seg_ids[:, None] == seg_ids[None, :]).astype(F32) / HEAD_DIM).astype(BF16)
    gq_a, gk_a, gq_b, gk_b = [jnp.tile(g, LANES // HEAD_DIM).reshape(1, LANES) for g in gains]

    def xmap(b, i):
        return (b, jnp.maximum(i - pad_tiles, 0), 0)

    def const2(b, i):
        return (0, 0)

    def tabmap(b, i):
        return (jnp.maximum(i - pad_tiles, 0), 0)

    def modmap(b, i):
        return (b, 0, 0)

    def kvmap(b, i):
        return (b, i, 0)

    lp = length + pad
    kern = functools.partial(_stage1_kernel, bb=bb, tl=tl, pad_tiles=pad_tiles,
                             n_tiles=n_tiles + pad_tiles, sa_rows=sa_rows, sb_rows=sb_rows)
    out_shapes = (
        jax.ShapeDtypeStruct((bsz, length, A_Q_W), BF16),
        jax.ShapeDtypeStruct((bsz, lp, A_KV_W), BF16),
        jax.ShapeDtypeStruct((bsz, lp, A_KV_W), BF16),
        jax.ShapeDtypeStruct((bsz, length, B_W), BF16),
        jax.ShapeDtypeStruct((bsz, lp, B_W), BF16),
        jax.ShapeDtypeStruct((bsz, lp, B_W), BF16),
        jax.ShapeDtypeStruct((bsz, sa_rows, A_KV_W), F32),
        jax.ShapeDtypeStruct((bsz, sa_rows, A_KV_W), F32),
        jax.ShapeDtypeStruct((bsz, sb_rows, B_W), F32),
        jax.ShapeDtypeStruct((bsz, sb_rows, B_W), F32),
    )
    out_specs = (
        pl.BlockSpec((bb, tl, A_Q_W), xmap),
        pl.BlockSpec((bb, tl, A_KV_W), kvmap),
        pl.BlockSpec((bb, tl, A_KV_W), kvmap),
        pl.BlockSpec((bb, tl, B_W), xmap),
        pl.BlockSpec((bb, tl, B_W), kvmap),
        pl.BlockSpec((bb, tl, B_W), kvmap),
        pl.BlockSpec((bb, sa_rows, A_KV_W), modmap),
        pl.BlockSpec((bb, sa_rows, A_KV_W), modmap),
        pl.BlockSpec((bb, sb_rows, B_W), modmap),
        pl.BlockSpec((bb, sb_rows, B_W), modmap),
    )
    in_specs = [
        pl.BlockSpec((bb, tl, D_MODEL), xmap),
        pl.BlockSpec((bb, 1, D_MODEL), modmap),
        pl.BlockSpec((bb, 1, D_MODEL), modmap),
        pl.BlockSpec((1, D_MODEL), const2),
        pl.BlockSpec((D_MODEL, QKV_W), const2),
        pl.BlockSpec((1, LANES), const2),
        pl.BlockSpec((1, LANES), const2),
        pl.BlockSpec((1, LANES), const2),
        pl.BlockSpec((1, LANES), const2),
        pl.BlockSpec((2 * LANES, 2 * LANES), const2),
        pl.BlockSpec((tl, LANES), tabmap),
        pl.BlockSpec((tl, LANES), tabmap),
        pl.BlockSpec((tl, LANES), tabmap),
    ]
    return pl.pallas_call(
        kern,
        grid=(bsz // bb, n_tiles + pad_tiles),
        in_specs=in_specs,
        out_specs=out_specs,
        out_shape=out_shapes,
        compiler_params=_cparams("arbitrary", "arbitrary"),
        name="stage1",
    )(x, shift, scale, g_attn.reshape(1, D_MODEL), w_qkv, gq_a, gk_a, gq_b, gk_b, seg, cos, sin_lo, sin_hi)


def _nt_dot(a, b):
    return lax.dot_general(a, b, (((1,), (1,)), ((), ())), preferred_element_type=F32)


def _attn_kernel(sink_ref, qa_ref, ka_ref, va_ref, qb_ref, kb_ref, vb_ref, bias_ref, *rest, n_c, cached):
    if cached:
        cka_ref, cva_ref, ckb_ref, cvb_ref, oa_ref, ob_ref = rest
    else:
        oa_ref, ob_ref = rest
    mask_prefix = not cached
    qi = pl.program_id(1)
    lane = lax.broadcasted_iota(I32, (CHUNK, LANES), 1)
    low = lane < HEAD_DIM
    half_mask = (jnp.where(low, 1.0, 0.0).astype(BF16), jnp.where(low, 0.0, 1.0).astype(BF16))
    n_pairs = B_W // LANES
    rows_b = n_pairs * 2 * CHUNK
    rows_a = A_Q_HEADS * CHUNK
    key_b = lax.broadcasted_iota(I32, (rows_b, BAND_B), 1)
    key_a = lax.broadcasted_iota(I32, (rows_a, BAND_A), 1)
    row_a = lax.broadcasted_iota(I32, (rows_a, 1), 0) // CHUNK

    sink_col = jnp.zeros((rows_a, 1), F32)
    for r in range(A_Q_HEADS):
        sink_col = jnp.where(row_a == r, sink_ref[r], sink_col)

    def chunk(c, masked):
        ci = qi * n_c + c
        row0 = pl.multiple_of(ci * CHUNK, CHUNK)
        rows = slice(c * CHUNK, (c + 1) * CHUNK)
        first_valid = PAD - ci * CHUNK

        def band(new_ref, cache_ref, start, size, cg):
            if cached:
                return jnp.concatenate([cache_ref[0, :, cg].astype(BF16), new_ref[0, :, cg]], axis=0)
            return new_ref[0, pl.ds(start, size), cg]

        s_parts = []
        for j in range(n_pairs):
            cg = slice(LANES * j, LANES * (j + 1))
            q = qb_ref[0, rows, cg]
            qs = jnp.concatenate([q * half_mask[0], q * half_mask[1]], axis=0)
            k = band(kb_ref, ckb_ref if cached else None, row0, BAND_B, cg)
            s_parts.append(_nt_dot(qs, k) + bias_ref[j])
        s = jnp.concatenate(s_parts, axis=0)
        if masked:
            s = jnp.where(key_b >= first_valid, s, NEG_INF)
        mx = jnp.max(s, axis=-1, keepdims=True)
        e = jnp.exp2(s - mx)
        inv = 1.0 / jnp.sum(e, axis=-1, keepdims=True)
        p_b = e.astype(BF16)
        for j in range(n_pairs):
            cg = slice(LANES * j, LANES * (j + 1))
            v = band(vb_ref, cvb_ref if cached else None, row0, BAND_B, cg)
            rj = slice(2 * CHUNK * j, 2 * CHUNK * (j + 1))
            o = jnp.dot(p_b[rj], v, preferred_element_type=F32) * inv[rj]
            ob_ref[0, rows, cg] = jnp.where(low, o[:CHUNK], o[CHUNK:]).astype(ob_ref.dtype)

        row_a0 = pl.multiple_of(row0 + (WINDOW_B - WINDOW_A), CHUNK)
        all_lanes = slice(0, LANES)
        k = band(ka_ref, cka_ref if cached else None, row_a0, BAND_A, all_lanes)
        v = band(va_ref, cva_ref if cached else None, row_a0, BAND_A, all_lanes)
        q = jnp.concatenate(
            [qa_ref[0, rows, LANES * j:LANES * (j + 1)] * half_mask[p]
             for p in range(A_KV_HEADS) for j in range(A_GROUP)], axis=0)
        s = _nt_dot(q, k)
        if masked:
            s = jnp.where(key_a >= first_valid - (WINDOW_B - WINDOW_A), s, NEG_INF)
        mx = jnp.maximum(jnp.max(s, axis=-1, keepdims=True), sink_col)
        e = jnp.exp2(s - mx)
        inv = 1.0 / (jnp.sum(e, axis=-1, keepdims=True) + jnp.exp2(sink_col - mx))
        o = jnp.dot(e.astype(BF16), v, preferred_element_type=F32) * inv
        for j in range(A_GROUP):
            lo = o[j * CHUNK:(j + 1) * CHUNK]
            hi = o[(A_GROUP + j) * CHUNK:(A_GROUP + j + 1) * CHUNK]
            oa_ref[0, rows, LANES * j:LANES * (j + 1)] = jnp.where(low, lo, hi).astype(oa_ref.dtype)

    steps_with_pad = pl.cdiv(PAD // CHUNK, n_c) if mask_prefix else 0
    if steps_with_pad:
        @pl.when(qi < steps_with_pad)
        def _():
            for c in range(n_c):
                chunk(c, True)

    @pl.when(qi >= steps_with_pad)
    def _():
        for c in range(n_c):
            chunk(c, False)


def _attention(qa, ka, va, qb, kb, vb, sinks, bias, *, n_c, caches=None):
    bsz, length, _ = qa.shape
    lp = ka.shape[1]
    qt = n_c * CHUNK
    cached = caches is not None
    assert length % qt == 0 and lp == (length if cached else length + PAD)
    assert not cached or (length == CHUNK and caches[0].shape[1] == WINDOW_A and caches[2].shape[1] == WINDOW_B)

    def qmap(b, i):
        return (b, i, 0)

    def kmap(b, i):
        return (b, 0, 0)

    in_specs = [
        pl.BlockSpec(memory_space=pltpu.SMEM),
        pl.BlockSpec((1, qt, A_Q_W), qmap),
        pl.BlockSpec((1, lp, A_KV_W), kmap),
        pl.BlockSpec((1, lp, A_KV_W), kmap),
        pl.BlockSpec((1, qt, B_W), qmap),
        pl.BlockSpec((1, lp, B_W), kmap),
        pl.BlockSpec((1, lp, B_W), kmap),
        pl.BlockSpec((B_HEADS // 2, 2 * CHUNK, BAND_B), lambda b, i: (0, 0, 0)),
    ]
    args = [sinks, qa, ka, va, qb, kb, vb, bias]
    if cached:
        in_specs += [pl.BlockSpec((1,) + c.shape[1:], kmap) for c in caches]
        args += list(caches)
    kern = functools.partial(_attn_kernel, n_c=n_c, cached=cached)
    return pl.pallas_call(
        kern,
        grid=(bsz, length // qt),
        in_specs=in_specs,
        out_specs=(pl.BlockSpec((1, qt, A_Q_W), qmap), pl.BlockSpec((1, qt, B_W), qmap)),
        out_shape=(jax.ShapeDtypeStruct((bsz, length, A_Q_W), BF16),
                   jax.ShapeDtypeStruct((bsz, length, B_W), BF16)),
        compiler_params=_cparams("arbitrary", "arbitrary"),
        name="attention",
    )(*args)


def _merge_kernel(x_ref, oa_ref, ob_ref, sha_ref, sca_ref, gta_ref, shm_ref, scm_ref, ga_ref, gm_ref,
                  wg_ref, wa_ref, wb_ref, wo_ref, wrh_ref, wrl_ref, br_ref,
                  x1_ref, h2_ref, ri_ref, rw_ref, cnt_ref, carry_ref, logit_ref, *, bb, tl, seg_tiles):
    m = bb * tl
    step = pl.program_id(0)

    @pl.when(step == 0)
    def _():
        carry_ref[...] = jnp.zeros_like(carry_ref)
        logit_ref[...] = jnp.zeros_like(logit_ref)

    logits = logit_ref[...]
    lane = lax.broadcasted_iota(I32, (m, LANES), 1)
    lane_f = lane.astype(F32)
    work = logits
    top_v, top_i = [], []
    sel_f = jnp.zeros((m, LANES), F32)
    for _ in range(TOP_K):
        mx = jnp.max(work, axis=-1, keepdims=True)
        idx = jnp.min(jnp.where(work == mx, lane_f, float(LANES)), axis=-1, keepdims=True)
        hit = lane_f == idx
        top_v.append(mx)
        top_i.append(idx)
        sel_f = jnp.where(hit, 1.0, sel_f)
        work = jnp.where(hit, -jnp.inf, work)
    ex = [jnp.exp(v - top_v[0]) for v in top_v]
    inv = 1.0 / (ex[0] + ex[1] + ex[2] + ex[3])

    r_i = lax.broadcasted_iota(I32, (m, m), 0)
    c_i = lax.broadcasted_iota(I32, (m, m), 1)
    before = jnp.where(r_i > c_i, 1.0, 0.0).astype(BF16)
    keep = jnp.where(jnp.maximum(step - 1, 0) % seg_tiles == 0, 0.0, 1.0)
    carry_in = carry_ref[...] * keep
    pos = jnp.dot(before, sel_f.astype(BF16), preferred_element_type=F32) + carry_in
    live = jnp.where(step > 0, 1.0, 0.0)
    carry = carry_in + live * jnp.sum(sel_f, axis=0, keepdims=True)
    carry_ref[...] = carry
    cnt_ref[0] = carry

    ri_f = jnp.zeros((m, LANES), F32)
    rw = jnp.zeros((m, LANES), F32)
    for k in range(TOP_K):
        rank = jnp.sum(jnp.where(lane_f == top_i[k], pos, 0.0), axis=-1, keepdims=True)
        ri_f = jnp.where(lane == k, top_i[k], ri_f)
        ri_f = jnp.where(lane == TOP_K + k, rank, ri_f)
        rw = jnp.where(lane == k, ex[k] * inv, rw)
    ri_ref[0] = jnp.transpose(ri_f)[:2 * TOP_K].astype(I32)
    rw_ref[...] = rw.reshape(bb, tl, LANES)

    x = x_ref[...]
    h = _modulated_norm(x, ga_ref[...], sca_ref[...], sha_ref[...])
    hb = h.reshape(m, D_MODEL).astype(BF16)
    gates = jnp.dot(hb, wg_ref[...], preferred_element_type=F32)
    ya = jnp.dot(oa_ref[...].reshape(m, A_Q_W), wa_ref[...], preferred_element_type=F32)
    yb = jnp.dot(ob_ref[...].reshape(m, B_W), wb_ref[...], preferred_element_type=F32)
    mix = jax.nn.sigmoid(gates[:, :D_MODEL]) * ya + jax.nn.sigmoid(gates[:, D_MODEL:]) * yb
    z = jnp.dot(mix.astype(BF16), wo_ref[...], preferred_element_type=F32)
    x1 = x + gta_ref[...] * z.reshape(bb, tl, D_MODEL)
    x1_ref[...] = x1

    h2 = _modulated_norm(x1, gm_ref[...], scm_ref[...], shm_ref[...]).reshape(m, D_MODEL)
    _pack_rows(h2_ref, h2)

    h2_hi = h2.astype(BF16)
    h2_lo = (h2 - h2_hi.astype(F32)).astype(BF16)
    wrh = wrh_ref[...]
    logit_ref[...] = (jnp.dot(h2_hi, wrh, preferred_element_type=F32)
                      + jnp.dot(h2_lo, wrh, preferred_element_type=F32)
                      + jnp.dot(h2_hi, wrl_ref[...], preferred_element_type=F32)) + br_ref[...]


def _merge(x, oa, ob, mods, g_attn, g_mlp, wg, wa, wb, wo, wr, br, *, bb, tl, b0, nb):
    _, length, _ = x.shape
    n_tok = nb * length
    m = bb * tl
    assert bb == 1 or tl == length
    sh_a, sc_a, gt_a, sh_m, sc_m = mods
    n_l = length // tl
    n_tiles = (nb // bb) * n_l
    blk0 = b0 // bb
    wr_hi = wr.astype(BF16)
    wr_lo = (wr - wr_hi.astype(F32)).astype(BF16)

    def cur(s):
        return jnp.minimum(s, n_tiles - 1)

    def prev(s):
        return jnp.maximum(s - 1, 0)

    def xmap(s):
        return (blk0 + cur(s) // n_l, cur(s) % n_l, 0)

    def modmap(s):
        return (blk0 + cur(s) // n_l, 0, 0)

    def omap(s):
        return (cur(s) // n_l, cur(s) % n_l, 0)

    def const2(s):
        return (0, 0)

    def flatmap(s):
        return (cur(s), 0)

    def routemap(s):
        return (prev(s) // n_l, prev(s) % n_l, 0)

    modspec = pl.BlockSpec((bb, 1, D_MODEL), modmap)
    kern = functools.partial(_merge_kernel, bb=bb, tl=tl, seg_tiles=n_tiles)
    return pl.pallas_call(
        kern,
        grid=(n_tiles + 1,),
        in_specs=[
            pl.BlockSpec((bb, tl, D_MODEL), xmap),
            pl.BlockSpec((bb, tl, A_Q_W), xmap),
            pl.BlockSpec((bb, tl, B_W), xmap),
            modspec, modspec, modspec, modspec, modspec,
            pl.BlockSpec((1, D_MODEL), const2),
            pl.BlockSpec((1, D_MODEL), const2),
            pl.BlockSpec((D_MODEL, 2 * D_MODEL), const2),
            pl.BlockSpec((A_Q_W, D_MODEL), const2),
            pl.BlockSpec((B_W, D_MODEL), const2),
            pl.BlockSpec((D_MODEL, D_MODEL), const2),
            pl.BlockSpec((D_MODEL, LANES), const2),
            pl.BlockSpec((D_MODEL, LANES), const2),
            pl.BlockSpec((1, LANES), const2),
        ],
        out_specs=(
            pl.BlockSpec((bb, tl, D_MODEL), omap),
            pl.BlockSpec((m * ROW_TILES, LANES), flatmap),
            pl.BlockSpec((1, 2 * TOP_K, m), lambda s: (prev(s), 0, 0)),
            pl.BlockSpec((bb, tl, LANES), routemap),
            pl.BlockSpec((1, 1, LANES), lambda s: (0, 0, 0)),
        ),
        out_shape=(
            jax.ShapeDtypeStruct((nb, length, D_MODEL), F32),
            jax.ShapeDtypeStruct((n_tok * ROW_TILES, LANES), I32),
            jax.ShapeDtypeStruct((n_tiles, 2 * TOP_K, m), I32),
            jax.ShapeDtypeStruct((nb, length, LANES), F32),
            jax.ShapeDtypeStruct((1, 1, LANES), F32),
        ),
        scratch_shapes=[pltpu.VMEM((1, LANES), F32), pltpu.VMEM((m, LANES), F32)],
        compiler_params=_cparams("arbitrary"),
        name="merge",
    )(x, oa, ob, sh_a, sc_a, gt_a, sh_m, sc_m, g_attn.reshape(1, D_MODEL), g_mlp.reshape(1, D_MODEL),
      wg, wa, wb, wo, wr_hi, wr_lo, br)


def _sc_mesh():
    return plsc.VectorSubcoreMesh(core_axis_name="c", subcore_axis_name="s",
                                  num_cores=SC_CORES, num_subcores=SC_SUBCORES)


def _sc_worker():
    return lax.axis_index("c") * SC_SUBCORES + lax.axis_index("s")


def _sc_params():
    return pltpu.CompilerParams(use_tc_tiling_on_sc=True)


def _part_chunks(parts):
    assert all(p % (2 * SC_WORKERS * SC_ROWS) == 0 for p in parts)
    chunks = [p // (SC_WORKERS * SC_ROWS) for p in parts]
    starts = [sum(chunks[:i]) for i in range(len(chunks))]
    return chunks, starts


def _dispatch(h2_parts, idx_flat, pad_flat, n_rows):
    n_parts = len(h2_parts)
    chunks, starts = _part_chunks([h.shape[0] for h in h2_parts])
    n_idx = sum(chunks) * TOP_K * SC_ROWS
    n_pad = pad_flat.shape[0] // SC_WORKERS
    pad_chunks = n_pad // SC_ROWS
    assert n_pad % SC_ROWS == 0
    zeros = jnp.zeros((SC_ROWS, ROW_TILES, LANES), I32)

    def body(*refs):
        h2_hbms = refs[:n_parts]
        idx_hbm, pad_hbm, zeros_hbm, xs_hbm, idx_v, pad_v, buf0, buf1, ls0, ls1, ss0, ss1 = refs[n_parts:]
        wid = _sc_worker()
        bufs, lsems, ssems = (buf0, buf1), (ls0, ls1), (ss0, ss1)
        pltpu.sync_copy(idx_hbm.at[pl.ds(pl.multiple_of(wid * n_idx, SC_ROWS), n_idx)], idx_v)
        pltpu.sync_copy(pad_hbm.at[pl.ds(pl.multiple_of(wid * n_pad, SC_ROWS), n_pad)], pad_v)

        pltpu.sync_copy(zeros_hbm, buf0)
        for c in range(pad_chunks):
            pltpu.async_copy(buf0, xs_hbm.at[pad_v.at[pl.ds(c * SC_ROWS, SC_ROWS)]], ss0)
        for c in range(pad_chunks):
            pltpu.make_async_copy(buf0, xs_hbm.at[pad_v.at[pl.ds(c * SC_ROWS, SC_ROWS)]], ss0).wait()

        def run_part(h2_hbm, n_chunks, c0):
            base = wid * (n_chunks * SC_ROWS)

            def load(g, b):
                src = h2_hbm.at[pl.ds(pl.multiple_of(base + g * SC_ROWS, SC_ROWS), SC_ROWS)]
                return pltpu.make_async_copy(src, bufs[b], lsems[b])

            def scatter(g, k, b):
                off = pl.multiple_of(((c0 + g) * TOP_K + k) * SC_ROWS, SC_ROWS)
                return pltpu.make_async_copy(bufs[b], xs_hbm.at[idx_v.at[pl.ds(off, SC_ROWS)]], ssems[b])

            load(0, 0).start()

            @pl.loop(0, n_chunks, step=2)
            def _(g0):
                for b in range(2):
                    g = g0 + b
                    load(g, b).wait()

                    @pl.when(g >= 1)
                    def _():
                        for k in range(TOP_K):
                            scatter(g - 1, k, 1 - b).wait()

                    @pl.when(g + 1 < n_chunks)
                    def _():
                        load(g + 1, 1 - b).start()

                    for k in range(TOP_K):
                        scatter(g, k, b).start()

            for k in range(TOP_K):
                scatter(n_chunks - 1, k, (n_chunks - 1) % 2).wait()

        for h2_hbm, n_chunks, c0 in zip(h2_hbms, chunks, starts):
            run_part(h2_hbm, n_chunks, c0)

    call = pl.kernel(
        body,
        out_type=jax.ShapeDtypeStruct((n_rows + TRASH_ROWS, ROW_TILES, LANES), I32),
        mesh=_sc_mesh(),
        scratch_types=[pltpu.VMEM((n_idx,), I32), pltpu.VMEM((n_pad,), I32),
                       pltpu.VMEM((SC_ROWS, ROW_TILES, LANES), I32), pltpu.VMEM((SC_ROWS, ROW_TILES, LANES), I32),
                       pltpu.SemaphoreType.DMA, pltpu.SemaphoreType.DMA,
                       pltpu.SemaphoreType.DMA, pltpu.SemaphoreType.DMA],
        compiler_params=_sc_params(),
        name="dispatch",
    )
    return call(*h2_parts, idx_flat, pad_flat, zeros)


def _gather_back(y_rows, idx_flat, parts):
    n_parts = len(parts)
    chunks, starts = _part_chunks(parts)
    n_idx = sum(chunks) * TOP_K * SC_ROWS

    def body(*refs):
        y_hbm, idx_hbm = refs[:2]
        yg_hbms = refs[2:2 + n_parts]
        idx_v, buf0, buf1, gs0, gs1, ws0, ws1 = refs[2 + n_parts:]
        wid = _sc_worker()
        bufs, gsems, wsems = (buf0, buf1), (gs0, gs1), (ws0, ws1)
        pltpu.sync_copy(idx_hbm.at[pl.ds(pl.multiple_of(wid * n_idx, SC_ROWS), n_idx)], idx_v)

        def run_part(yg_hbm, n_chunks, c0):
            base = wid * (n_chunks * SC_ROWS)
            n_items = n_chunks * TOP_K

            def gather(i, b):
                off = pl.multiple_of((c0 * TOP_K + i) * SC_ROWS, SC_ROWS)
                return pltpu.make_async_copy(y_hbm.at[idx_v.at[pl.ds(off, SC_ROWS)]], bufs[b], gsems[b])

            def write(i, b):
                g, k = i // TOP_K, i % TOP_K
                dst = yg_hbm.at[k, pl.ds(pl.multiple_of(base + g * SC_ROWS, SC_ROWS), SC_ROWS)]
                return pltpu.make_async_copy(bufs[b], dst, wsems[b])

            gather(0, 0).start()

            @pl.loop(0, n_items, step=2)
            def _(i0):
                for b in range(2):
                    i = i0 + b
                    gather(i, b).wait()

                    @pl.when(i >= 1)
                    def _():
                        write(i - 1, 1 - b).wait()

                    @pl.when(i + 1 < n_items)
                    def _():
                        gather(i + 1, 1 - b).start()

                    write(i, b).start()

            write(n_items - 1, (n_items - 1) % 2).wait()

        for yg_hbm, n_chunks, c0 in zip(yg_hbms, chunks, starts):
            run_part(yg_hbm, n_chunks, c0)

    call = pl.kernel(
        body,
        out_type=tuple(jax.ShapeDtypeStruct((TOP_K, p, ROW_TILES, LANES), I32) for p in parts),
        mesh=_sc_mesh(),
        scratch_types=[pltpu.VMEM((n_idx,), I32),
                       pltpu.VMEM((SC_ROWS, ROW_TILES, LANES), I32), pltpu.VMEM((SC_ROWS, ROW_TILES, LANES), I32),
                       pltpu.SemaphoreType.DMA, pltpu.SemaphoreType.DMA,
                       pltpu.SemaphoreType.DMA, pltpu.SemaphoreType.DMA],
        compiler_params=_sc_params(),
        name="gather_back",
    )
    return call(y_rows, idx_flat)


def _expert_kernel(blk_ref, used_ref, next_ref, slot_ref, xs_ref, wgu_hbm, bgu_ref, wd_hbm, bd_ref, y_ref,
                   wgu_buf, wd_buf, wgu_bf, wd_bf, sem_gu, sem_d):
    r = pl.program_id(0)
    e = blk_ref[r]
    prev = blk_ref[jnp.maximum(r - 1, 0)]
    active = r < used_ref[0]
    slot = slot_ref[r]

    def fetch(expert, s):
        return (pltpu.make_async_copy(wgu_hbm.at[expert], wgu_buf.at[s], sem_gu.at[s]),
                pltpu.make_async_copy(wd_hbm.at[expert], wd_buf.at[s], sem_d.at[s]))

    @pl.when(jnp.logical_and(active, r == 0))
    def _():
        for copy in fetch(e, slot):
            copy.start()

    @pl.when(jnp.logical_and(active, jnp.logical_or(r == 0, e != prev)))
    def _():
        for copy in fetch(e, slot):
            copy.wait()
        wgu_bf[...] = wgu_buf[slot].astype(BF16)
        wd_bf[...] = wd_buf[slot].astype(BF16)
        nxt = next_ref[r]

        @pl.when(nxt >= 0)
        def _():
            for copy in fetch(nxt, 1 - slot):
                copy.start()

    @pl.when(active)
    def _():
        lo, hi = _unpack_rows([xs_ref[pl.ds(j, ROUTE_BLOCK, stride=ROW_TILES), :] for j in range(ROW_TILES)])
        x = jnp.concatenate(lo + hi, axis=1).astype(BF16)
        acts = []
        for j in range(D_FF // FF_TILE):
            gc = slice(FF_TILE * j, FF_TILE * (j + 1))
            uc = slice(D_FF + FF_TILE * j, D_FF + FF_TILE * (j + 1))
            g = jnp.dot(x, wgu_bf[:, gc], preferred_element_type=F32) + bgu_ref[0, :, gc]
            u = jnp.dot(x, wgu_bf[:, uc], preferred_element_type=F32) + bgu_ref[0, :, uc]
            gate = jnp.minimum(g, SWIGLU_LIMIT)
            up = jnp.clip(u, -SWIGLU_LIMIT, SWIGLU_LIMIT)
            glu = gate * jax.nn.sigmoid(gate * SWIGLU_ALPHA)
            acts.append(((up + 1.0) * glu).astype(BF16))
        act = jnp.concatenate(acts, axis=1)
        y = jnp.dot(act, wd_bf[...], preferred_element_type=F32) + bd_ref[0]
        _pack_rows(y_ref, y)


def _experts(xs_rows, blk_expert, n_used, blk_next, blk_slot, w_gate_up, b_gate_up, w_down, b_down):
    n_blocks = blk_expert.shape[0]
    rb = ROUTE_BLOCK * ROW_TILES

    def rowmap(r, blk, used, nxt, slot):
        return (jnp.minimum(r, used[0] - 1), 0)

    def bmap(r, blk, used, nxt, slot):
        return (blk[jnp.minimum(r, used[0] - 1)], 0, 0)

    return pl.pallas_call(
        _expert_kernel,
        grid_spec=pltpu.PrefetchScalarGridSpec(
            num_scalar_prefetch=4,
            grid=(n_blocks,),
            in_specs=[
                pl.BlockSpec((rb, LANES), rowmap),
                pl.BlockSpec(memory_space=pl.ANY),
                pl.BlockSpec((1, 1, 2 * D_FF), bmap),
                pl.BlockSpec(memory_space=pl.ANY),
                pl.BlockSpec((1, 1, D_MODEL), bmap),
            ],
            out_specs=pl.BlockSpec((rb, LANES), rowmap),
            scratch_shapes=[pltpu.VMEM((2, D_MODEL, 2 * D_FF), F32), pltpu.VMEM((2, D_FF, D_MODEL), F32),
                            pltpu.VMEM((D_MODEL, 2 * D_FF), BF16), pltpu.VMEM((D_FF, D_MODEL), BF16),
                            pltpu.SemaphoreType.DMA((2,)), pltpu.SemaphoreType.DMA((2,))],
        ),
        out_shape=jax.ShapeDtypeStruct(xs_rows.shape, I32),
        compiler_params=_cparams("arbitrary"),
        name="experts",
    )(blk_expert, n_used, blk_next, blk_slot, xs_rows, w_gate_up, b_gate_up.reshape(N_EXPERTS, 1, 2 * D_FF),
      w_down, b_down.reshape(N_EXPERTS, 1, D_MODEL))


def _combine_kernel(x1_ref, gt_ref, rw_ref, yg_ref, *rest, bb, tl):
    o_ref = rest[-1]
    m = bb * tl
    rw = rw_ref[...].reshape(m, LANES)
    w = [rw[:, k:k + 1] for k in range(TOP_K)]
    gt = gt_ref[...]
    for j in range(ROW_TILES):
        acc_lo = jnp.zeros((m, LANES), F32)
        acc_hi = jnp.zeros((m, LANES), F32)
        for k in range(TOP_K):
            lo, hi = _unpack_rows([yg_ref[k, pl.ds(j, m, stride=ROW_TILES), :]])
            acc_lo = acc_lo + w[k] * lo[0]
            acc_hi = acc_hi + w[k] * hi[0]
        for acc, c0 in ((acc_lo, LANES * j), (acc_hi, HALF_W + LANES * j)):
            cg = slice(c0, c0 + LANES)
            o_ref[:, :, cg] = x1_ref[:, :, cg] + gt[:, :, cg] * acc.reshape(bb, tl, LANES)


def _combine(x1, gt_m, rw, yg, y_prev, *, bb, tl, b0, bsz):
    nb, length, _ = x1.shape
    m = bb * tl
    assert bb == 1 or tl == length
    n_l = length // tl
    blk0 = b0 // bb

    def smap(b, i):
        return (b, i, 0)

    def xmap(b, i):
        return (blk0 + b, i, 0)

    in_specs = [
        pl.BlockSpec((bb, tl, D_MODEL), smap),
        pl.BlockSpec((bb, 1, D_MODEL), lambda b, i: (blk0 + b, 0, 0)),
        pl.BlockSpec((bb, tl, LANES), smap),
        pl.BlockSpec((TOP_K, m * ROW_TILES, LANES), lambda b, i: (0, b * n_l + i, 0)),
    ]
    args = [x1, gt_m, rw, yg]
    aliases = {}
    if y_prev is not None:
        in_specs.append(pl.BlockSpec(memory_space=pl.ANY))
        args.append(y_prev)
        aliases = {len(args) - 1: 0}
    kern = functools.partial(_combine_kernel, bb=bb, tl=tl)
    return pl.pallas_call(
        kern,
        grid=(nb // bb, n_l),
        in_specs=in_specs,
        out_specs=pl.BlockSpec((bb, tl, D_MODEL), xmap),
        out_shape=jax.ShapeDtypeStruct((bsz, length, D_MODEL), F32),
        input_output_aliases=aliases,
        compiler_params=_cparams("arbitrary", "arbitrary"),
        name="combine",
    )(*args)


def _route_plan(routes):
    expert_ids = jnp.arange(N_EXPERTS, dtype=I32)
    part_counts = [c.reshape(-1)[:N_EXPERTS].astype(I32) for _, c in routes]
    counts = sum(part_counts)
    padded = (counts + ROUTE_BLOCK - 1) // ROUTE_BLOCK * ROUTE_BLOCK
    pends = jnp.cumsum(padded)
    pstarts = pends - padded
    n_assign = sum(ri.shape[0] * ri.shape[2] for ri, _ in routes) * TOP_K
    n_blocks = n_assign // ROUTE_BLOCK + N_EXPERTS
    n_rows = n_blocks * ROUTE_BLOCK
    lists = []
    first = jnp.zeros_like(counts)
    for (ri, _), cnt in zip(routes, part_counts):
        n_tiles, _, m = ri.shape
        idx, rank = ri[:, :TOP_K], ri[:, TOP_K:]
        dest = jnp.sum(jnp.where(idx[..., None] == expert_ids, pstarts + first, 0), axis=-1) + rank
        dest = dest.astype(I32).reshape(n_tiles, TOP_K, m // SC_ROWS, SC_ROWS).transpose(0, 2, 1, 3)
        lists.append(dest.reshape(SC_WORKERS, n_tiles * m // (SC_WORKERS * SC_ROWS), TOP_K, SC_ROWS))
        first = first + cnt
    idx_flat = jnp.concatenate(lists, axis=1).reshape(-1)
    blk_start = jnp.arange(n_blocks, dtype=I32) * ROUTE_BLOCK
    blk_expert = jnp.minimum(jnp.sum((pends[None, :] <= blk_start[:, None]).astype(I32), axis=1), N_EXPERTS - 1)
    n_used = (pends[-1] // ROUTE_BLOCK).reshape(1)
    nonempty = padded > 0
    later = nonempty[None, :] & (expert_ids[None, :] > expert_ids[:, None])
    next_e = jnp.min(jnp.where(later, expert_ids[None, :], N_EXPERTS), axis=1)
    next_e = jnp.where(next_e == N_EXPERTS, -1, next_e)
    slot_e = (jnp.cumsum(nonempty.astype(I32)) - 1) % 2
    own = blk_expert[:, None] == expert_ids[None, :]
    blk_next = jnp.sum(jnp.where(own, next_e[None, :], 0), axis=1)
    blk_slot = jnp.sum(jnp.where(own, slot_e[None, :], 0), axis=1)
    r = jnp.arange(ROUTE_BLOCK, dtype=I32)[None, :]
    pad_flat = jnp.where(r < (padded - counts)[:, None], (pstarts + counts)[:, None] + r, n_rows)
    blocks = (blk_expert.astype(I32), n_used.astype(I32), blk_next.astype(I32), blk_slot.astype(I32))
    return idx_flat, pad_flat.astype(I32).reshape(-1), blocks, n_rows


def _front(x, mods, pos, caches, weights, *, bb, tl, n_c):
    g_attn, w_qkv, gains, sinks, bias = weights[0], weights[2], weights[3], weights[4], weights[5]
    sh_a, sc_a = mods[0], mods[1]
    pad = 0 if caches is not None else PAD
    qa, ka, va, qb, kb, vb, ska, sva, skb, svb = _stage1(
        x, sh_a, sc_a, g_attn, w_qkv, gains, pos, bb=bb, tl=tl, pad=pad)
    oa, ob = _attention(qa, ka, va, qb, kb, vb, sinks, bias, n_c=n_c, caches=caches)
    return oa, ob, (ska, sva, skb, svb)


def _moe_segment(parts, weights):
    (g_attn, g_mlp, _, _, _, _, wg, wa, wb, wo, wr, br, w_gate_up, b_gate_up, w_down, b_down) = weights
    merged = []
    for p in parts:
        sh_a, sc_a, gt_a, sh_m, sc_m, _ = p["mods"]
        merged.append(_merge(p["x"], p["oa"], p["ob"], (sh_a, sc_a, gt_a, sh_m, sc_m), g_attn, g_mlp,
                             wg, wa, wb, wo, wr, br, bb=p["cfg"]["bb"], tl=p["cfg"]["tl"], b0=p["b0"], nb=p["nb"]))
    toks = [m[0].shape[0] * m[0].shape[1] for m in merged]
    idx_flat, pad_flat, blocks, n_rows = _route_plan([(m[2], m[4]) for m in merged])
    xs = _dispatch([m[1].reshape(t, ROW_TILES, LANES) for m, t in zip(merged, toks)], idx_flat, pad_flat, n_rows)
    y_rows = _experts(xs.reshape(-1, LANES), *blocks, w_gate_up, b_gate_up, w_down, b_down)
    ygs = _gather_back(y_rows.reshape(-1, ROW_TILES, LANES), idx_flat, toks)
    outs = []
    for p, m, yg, t in zip(parts, merged, ygs, toks):
        outs.append(_combine(m[0], p["mods"][5], m[3], yg.reshape(TOP_K, t * ROW_TILES, LANES), p["y"],
                             bb=p["cfg"]["bb_combine"], tl=p["cfg"]["tl_combine"], b0=p["b0"],
                             bsz=p["x"].shape[0]))
    return outs


def _rel_bias_table(rel_table):
    n_rel = rel_table.shape[1]
    ext = jnp.concatenate(
        [rel_table, jnp.broadcast_to(rel_table[:, -1:], (B_HEADS, CHUNK + BAND_B - n_rel))], axis=1)
    rows = jnp.stack([ext[:, i:i + BAND_B] for i in range(CHUNK)], axis=1)[..., ::-1]
    return rows.astype(F32).reshape(B_HEADS // 2, 2 * CHUNK, BAND_B)


def _prep_weights(g_attn, g_mlp, w_in, q_norm_a, k_norm_a, q_norm_b, k_norm_b, sinks_a, rel_table_b,
                  w_branch_a, w_branch_b, w_out, w_router, b_router, w_gate_up, b_gate_up, w_down, b_down):
    order = jnp.asarray(A_HEAD_ORDER)
    w_qa = w_in[:, :A_Q_W].reshape(D_MODEL, A_Q_HEADS, HEAD_DIM)[:, order].reshape(D_MODEL, A_Q_W)
    w_qkv = jnp.concatenate([w_qa, w_in[:, A_Q_W:QKV_W]], axis=1).astype(BF16)
    wg = w_in[:, QKV_W:].astype(BF16)
    wa = w_branch_a.reshape(A_Q_HEADS, HEAD_DIM, D_MODEL)[order].reshape(A_Q_W, D_MODEL).astype(BF16)
    wb = w_branch_b.astype(BF16)
    wo = w_out.astype(BF16)
    wr = jnp.zeros((D_MODEL, LANES), F32).at[:, :N_EXPERTS].set(w_router)
    br = jnp.full((1, LANES), NEG_INF, F32).at[0, :N_EXPERTS].set(b_router)
    bias = _rel_bias_table(rel_table_b) * LOG2_E
    gains = (q_norm_a, k_norm_a, q_norm_b, k_norm_b)
    return (g_attn, g_mlp, w_qkv, gains, sinks_a.astype(F32) * LOG2_E, bias, wg, wa, wb, wo, wr, br,
            w_gate_up, b_gate_up, w_down, b_down)


def _forward(x_prompt, x_sample, c_prompt, c_sample, cache_a_k, cache_a_v, cache_b_k, cache_b_v, params,
             past_len, cfg_prompt, cfg_sample):
    (g_attn, g_mlp, w_ada, b_ada, w_in, q_norm_a, k_norm_a, q_norm_b, k_norm_b, sinks_a, rel_table_b,
     w_branch_a, w_branch_b, w_out, w_router, b_router, w_gate_up, b_gate_up, w_down, b_down) = params
    depth = g_attn.shape[0]
    bp, lp_, _ = x_prompt.shape
    bs, ls, _ = x_sample.shape
    pos_p = jnp.arange(lp_, dtype=I32)
    pos_s = past_len + jnp.arange(ls, dtype=I32)
    yp, ys = x_prompt, x_sample
    st_p, st_s = [], []
    for l in range(depth):
        weights = _prep_weights(g_attn[l], g_mlp[l], w_in[l], q_norm_a[l], k_norm_a[l], q_norm_b[l],
                                k_norm_b[l], sinks_a[l], rel_table_b[l], w_branch_a[l], w_branch_b[l],
                                w_out[l], w_router[l], b_router[l], w_gate_up[l], b_gate_up[l],
                                w_down[l], b_down[l])
        mod = _adaln(jnp.concatenate([c_prompt, c_sample], axis=0), w_ada[l], b_ada[l])
        mod = mod.reshape(bp + bs, 6, 1, D_MODEL)
        mods_p = [mod[:bp, k] for k in range(6)]
        mods_s = [mod[bp:, k] for k in range(6)]
        caches = (cache_a_k[l].reshape(bs, -1, A_KV_W), cache_a_v[l].reshape(bs, -1, A_KV_W),
                  cache_b_k[l].reshape(bs, -1, B_W), cache_b_v[l].reshape(bs, -1, B_W))
        front = {k: cfg_prompt[k] for k in ("bb", "tl", "n_c")}
        oa_p, ob_p, sp = _front(yp, mods_p, pos_p, None, weights, **front)
        front = {k: cfg_sample[k] for k in ("bb", "tl", "n_c")}
        oa_s, ob_s, ss = _front(ys, mods_s, pos_s, caches, weights, **front)
        n_seg = cfg_prompt["n_seg"]
        nb = bp // n_seg
        xp, xs_in = yp, ys
        yp = None
        for s in range(n_seg):
            parts = [dict(x=xp, oa=oa_p, ob=ob_p, mods=mods_p, b0=s * nb, nb=nb, cfg=cfg_prompt, y=yp)]
            if s == n_seg - 1:
                parts.append(dict(x=xs_in, oa=oa_s, ob=ob_s, mods=mods_s, b0=0, nb=bs, cfg=cfg_sample, y=None))
            outs = _moe_segment(parts, weights)
            yp = outs[0]
            if s == n_seg - 1:
                ys = outs[1]
        st_p.append(sp)
        st_s.append(ss)

    def stack(states, k, heads):
        return jnp.stack([s[k].reshape(s[k].shape[0], s[k].shape[1], heads, HEAD_DIM) for s in states])

    return (yp, ys,
            stack(st_p, 0, A_KV_HEADS), stack(st_p, 1, A_KV_HEADS), stack(st_p, 2, B_HEADS), stack(st_p, 3, B_HEADS),
            stack(st_s, 0, A_KV_HEADS), stack(st_s, 1, A_KV_HEADS), stack(st_s, 2, B_HEADS), stack(st_s, 3, B_HEADS))


PAST_LEN = 2048
CFG_PROMPT = dict(bb=1, tl=512, n_c=8, tl_combine=1024, bb_combine=1, n_seg=2)
CFG_SAMPLE = dict(bb=8, tl=64, n_c=1, tl_combine=64, bb_combine=8)


def kernel(x_prompt, x_sample, c_prompt, c_sample, cache_a_k, cache_a_v, cache_b_k, cache_b_v, g_attn, g_mlp,
           w_ada, b_ada, w_in, q_norm_a, k_norm_a, q_norm_b, k_norm_b, sinks_a, rel_table_b, w_branch_a,
           w_branch_b, w_out, w_router, b_router, w_gate_up, b_gate_up, w_down, b_down):
    params = (g_attn, g_mlp, w_ada, b_ada, w_in, q_norm_a, k_norm_a, q_norm_b, k_norm_b, sinks_a, rel_table_b,
              w_branch_a, w_branch_b, w_out, w_router, b_router, w_gate_up, b_gate_up, w_down, b_down)
    return _forward(x_prompt, x_sample, c_prompt, c_sample, cache_a_k, cache_a_v, cache_b_k, cache_b_v,
                    params, PAST_LEN, CFG_PROMPT, CFG_SAMPLE)
```
